```python
import math
import jax, jax.numpy as jnp
from jax import lax
import numpy as np

D_MODEL = 1024
BATCH = 16
SEQ = 2048
DEPTH = 1

PLE_DIM = 256
A_HEADS = 8
A_HEAD_DIM = 64
A_WIDTH = A_HEADS * A_HEAD_DIM
DILATED_PATTERNS = ((128, 1), (512, 4), (2048, 16))
ATT_BLOCK = 128
B_HEADS = 4
B_HEAD_DIM = 128
B_WIDTH = B_HEADS * B_HEAD_DIM
CONV_WIDTH = 4
DELTA_CHUNK = 64
MIX_WIDTH = A_WIDTH + B_WIDTH
IN_SIZES = (A_WIDTH, A_WIDTH, A_WIDTH, B_WIDTH, B_WIDTH, B_WIDTH, B_WIDTH, B_HEADS, B_HEADS)
IN_COLS = sum(IN_SIZES)
N_EXPERTS = 32
TOP_K = 4
D_EXPERT = D_MODEL
SWIGLU_LIMIT = 7.0
SWIGLU_ALPHA = 1.702
MOE_BLOCK = 128
LN_EPS = 1e-5
RMS_EPS = 1e-6
L2_EPS = 1e-6
DEEPNORM_ALPHA = (2.0 * DEPTH) ** 0.25
DEEPNORM_BETA = (8.0 * DEPTH) ** -0.25

kernel_name = "hymba_dilated_deltanet_moe_deepnorm"


def layer_norm(x, g, b):
    xf = x.astype(jnp.float32)
    mu = jnp.mean(xf, axis=-1, keepdims=True)
    xc = xf - mu
    var = jnp.mean(xc * xc, axis=-1, keepdims=True)
    y = xc * lax.rsqrt(var + LN_EPS) * g.astype(jnp.float32) + b.astype(jnp.float32)
    return y.astype(x.dtype)


def _dilated_branch(q, k, v, window, dilation):
    b, s, h, e = q.shape
    n_back = window // dilation
    L = s // dilation
    nb = -(-L // ATT_BLOCK)
    Lp = nb * ATT_BLOCK

    def split(t):
        t = t.reshape(b, L, dilation, h, e).transpose(0, 2, 3, 1, 4)
        return jnp.pad(t, ((0, 0), (0, 0), (0, 0), (0, Lp - L), (0, 0)))

    def band(t):
        t = jnp.pad(t, ((0, 0), (0, 0), (0, 0), (ATT_BLOCK, 0), (0, 0)))
        t = t.reshape(b, dilation, h, nb + 1, ATT_BLOCK, e)
        return jnp.concatenate([t[:, :, :, :-1], t[:, :, :, 1:]], axis=4)

    qb = split(q).reshape(b, dilation, h, nb, ATT_BLOCK, e)
    kb = band(split(k))
    vb = band(split(v))

    qi = jnp.arange(ATT_BLOCK)[:, None]
    kc = jnp.arange(2 * ATT_BLOCK)[None, :]
    dist = ATT_BLOCK + qi - kc
    key_pos = jnp.arange(nb)[:, None, None] * ATT_BLOCK + kc[None] - ATT_BLOCK
    valid = (dist >= 0) & (dist <= n_back) & (key_pos >= 0)

    scores = jnp.einsum('bdhnqe,bdhnke->bdhnqk', qb, kb,
                        preferred_element_type=jnp.float32) * (1.0 / math.sqrt(e))
    scores = jnp.where(valid, scores, -jnp.inf)
    m = jnp.max(scores, axis=-1, keepdims=True)
    pr = jnp.exp(scores - m)
    den = jnp.sum(pr, axis=-1, keepdims=True)
    o = jnp.einsum('bdhnqk,bdhnke->bdhnqe', pr.astype(vb.dtype), vb,
                   preferred_element_type=jnp.float32) / den
    lse = (m + jnp.log(den))[..., 0]

    o = o.reshape(b, dilation, h, Lp, e)[:, :, :, :L].transpose(0, 3, 1, 2, 4).reshape(b, s, h, e)
    lse = lse.reshape(b, dilation, h, Lp)[..., :L].transpose(0, 3, 1, 2).reshape(b, s, h)
    return o, lse


def dilated_attention(q, k, v):
    outs, lses = [], []
    for window, dilation in DILATED_PATTERNS:
        o, lse = _dilated_branch(q, k, v, window, dilation)
        outs.append(o)
        lses.append(lse)
    wts = jax.nn.softmax(jnp.stack(lses, 0), axis=0)
    return jnp.einsum('pbsh,pbshe->bshe', wts, jnp.stack(outs, 0))


def _causal_depthwise_conv(x, w):
    c = x.shape[-1]
    return lax.conv_general_dilated(x, w[:, None, :].astype(x.dtype), window_strides=(1,),
                                    padding=[(CONV_WIDTH - 1, 0)],
                                    dimension_numbers=('NWC', 'WIO', 'NWC'),
                                    feature_group_count=c)


def _l2norm(t):
    return t * lax.rsqrt(jnp.sum(t * t, axis=-1, keepdims=True) + L2_EPS)


def _gated_delta_rule(q, k, v, g, beta):
    b, s, h, dk = q.shape
    dv = v.shape[-1]
    c = DELTA_CHUNK
    n = s // c

    def chunk(t):
        t = t.reshape((b, n, c, h) + t.shape[3:])
        return jnp.moveaxis(t, 3, 1)

    q, k, v, g, beta = [chunk(t) for t in (q, k, v, g, beta)]
    gc = jnp.cumsum(g, axis=-1)
    diff = gc[..., :, None] - gc[..., None, :]
    strict = jnp.tril(jnp.ones((c, c), bool), -1)
    causal = jnp.tril(jnp.ones((c, c), bool))
    decay_strict = jnp.where(strict, jnp.exp(jnp.where(strict, diff, 0.0)), 0.0)
    decay_causal = jnp.where(causal, jnp.exp(jnp.where(causal, diff, 0.0)), 0.0)

    kbeta = k * beta[..., None]
    a_mat = jnp.eye(c, dtype=jnp.float32) + jnp.einsum('bhnid,bhnjd->bhnij', kbeta, k) * decay_strict
    rhs = jnp.concatenate([v * beta[..., None], kbeta * jnp.exp(gc)[..., None]], axis=-1)
    sol = lax.linalg.triangular_solve(a_mat, rhs, left_side=True, lower=True, unit_diagonal=True)
    u, w = sol[..., :dv], sol[..., dv:]

    attn = jnp.einsum('bhnid,bhnjd->bhnij', q, k) * decay_causal
    q_dec = q * jnp.exp(gc)[..., None]
    g_last = gc[..., -1]
    k_dec = k * jnp.exp(g_last[..., None] - gc)[..., None]
    xs = [jnp.moveaxis(t, 2, 0) for t in (u, w, attn, q_dec, k_dec, g_last)]

    def step(state, inp):
        u_n, w_n, attn_n, qd_n, kd_n, gl_n = inp
        v_new = u_n - jnp.einsum('bhck,bhkv->bhcv', w_n, state)
        out = (jnp.einsum('bhck,bhkv->bhcv', qd_n, state)
               + jnp.einsum('bhij,bhjv->bhiv', attn_n, v_new))
        state = state * jnp.exp(gl_n)[..., None, None] + jnp.einsum('bhck,bhcv->bhkv', kd_n, v_new)
        return state, out

    s0 = jnp.zeros((b, h, dk, dv), jnp.float32)
    _, out = lax.scan(step, s0, xs)
    return out.transpose(1, 0, 3, 2, 4).reshape(b, s, h, dv)


def gated_deltanet(qkv, z, b_logit, a_logit, conv_w, a_log, dt_bias, norm_w):
    bsz, s, _ = qkv.shape
    qkv = jax.nn.silu(_causal_depthwise_conv(qkv, conv_w)).astype(jnp.float32)
    q, k, v = jnp.split(qkv, 3, axis=-1)
    q = _l2norm(q.reshape(bsz, s, B_HEADS, B_HEAD_DIM)) * (B_HEAD_DIM ** -0.5)
    k = _l2norm(k.reshape(bsz, s, B_HEADS, B_HEAD_DIM))
    v = v.reshape(bsz, s, B_HEADS, B_HEAD_DIM)
    beta = jax.nn.sigmoid(b_logit.astype(jnp.float32))
    g = -jnp.exp(a_log.astype(jnp.float32)) * jax.nn.softplus(
        a_logit.astype(jnp.float32) + dt_bias.astype(jnp.float32))
    o = _gated_delta_rule(q, k, v, g, beta)
    o = o * lax.rsqrt(jnp.mean(o * o, axis=-1, keepdims=True) + RMS_EPS) * norm_w.astype(jnp.float32)
    o = o * jax.nn.silu(z.astype(jnp.float32).reshape(bsz, s, B_HEADS, B_HEAD_DIM))
    return o.reshape(bsz, s, B_WIDTH)


def moe(x, w_router, b_router, w_gate_up, b_gate_up, w_down, b_down):
    bsz, s, d = x.shape
    t = bsz * s
    xt = x.reshape(t, d)
    logits = jnp.dot(xt, w_router, preferred_element_type=jnp.float32) + b_router.astype(jnp.float32)
    top_val, top_idx = lax.top_k(logits, TOP_K)
    gates = jax.nn.softmax(top_val, axis=-1)

    flat_e = top_idx.reshape(-1).astype(jnp.int32)
    flat_tok = jnp.repeat(jnp.arange(t, dtype=jnp.int32), TOP_K)
    flat_g = gates.reshape(-1)
    order = jnp.argsort(flat_e)
    sorted_e = flat_e[order]
    counts = jnp.bincount(flat_e, length=N_EXPERTS)
    padded = (counts + MOE_BLOCK - 1) // MOE_BLOCK * MOE_BLOCK
    start = jnp.cumsum(counts) - counts
    pad_end = jnp.cumsum(padded)
    pad_start = pad_end - padded
    dest = pad_start[sorted_e] + (jnp.arange(t * TOP_K) - start[sorted_e])

    n_blocks = -(-(t * TOP_K) // MOE_BLOCK) + N_EXPERTS
    n_rows = n_blocks * MOE_BLOCK
    buf_tok = jnp.full((n_rows,), t, jnp.int32).at[dest].set(flat_tok[order])
    buf_gate = jnp.zeros((n_rows,), jnp.float32).at[dest].set(flat_g[order])
    block_e = jnp.minimum(jnp.searchsorted(pad_end, jnp.arange(n_blocks) * MOE_BLOCK, side='right'),
                          N_EXPERTS - 1)

    x_pad = jnp.concatenate([xt, jnp.zeros((1, d), xt.dtype)], axis=0)
    xb = x_pad[buf_tok].reshape(n_blocks, MOE_BLOCK, d)

    def expert_block(args):
        xblk, e = args
        hgu = jnp.dot(xblk, w_gate_up[e]) + b_gate_up[e]
        gate = jnp.minimum(hgu[:, :D_EXPERT], SWIGLU_LIMIT)
        up = jnp.clip(hgu[:, D_EXPERT:], -SWIGLU_LIMIT, SWIGLU_LIMIT)
        act = (up + 1.0) * (gate * jax.nn.sigmoid(SWIGLU_ALPHA * gate))
        return jnp.dot(act, w_down[e]) + b_down[e]

    yb = lax.map(expert_block, (xb, block_e)).reshape(n_rows, d)
    y = jax.ops.segment_sum(yb * buf_gate[:, None].astype(yb.dtype), buf_tok, num_segments=t + 1)[:t]
    return y.reshape(bsz, s, d).astype(x.dtype)


def setup_inputs(seed: int = 0) -> dict:
    key = jax.random.key(seed)
    ks = jax.random.split(key, 24)
    f32 = jnp.float32
    nrm = lambda k, shape, scale: jax.random.normal(k, shape, f32) * scale

    x = jax.random.normal(ks[0], (BATCH, SEQ, D_MODEL), f32)
    p = jax.random.normal(ks[1], (DEPTH, BATCH, SEQ, PLE_DIM), f32)

    w_in = nrm(ks[2], (DEPTH, D_MODEL, IN_COLS), D_MODEL ** -0.5)
    offs = np.cumsum((0,) + IN_SIZES)
    col_scale = np.ones((IN_COLS,), np.float32)
    col_scale[offs[2]:offs[3]] = DEEPNORM_BETA
    col_scale[offs[5]:offs[6]] = DEEPNORM_BETA
    w_in = w_in * jnp.asarray(col_scale)

    conv_w = nrm(ks[3], (DEPTH, CONV_WIDTH, 3 * B_WIDTH), CONV_WIDTH ** -0.5)
    a_log = jnp.log(jax.random.uniform(ks[4], (DEPTH, B_HEADS), f32, 1.0, 16.0))
    dt = jnp.exp(jax.random.uniform(ks[5], (DEPTH, B_HEADS), f32, math.log(1e-3), math.log(1e-1)))
    dt_bias = dt + jnp.log(-jnp.expm1(-dt))
    gnorm_w = 1.0 + nrm(ks[6], (DEPTH, B_HEAD_DIM), 0.02)
    w_out = nrm(ks[7], (DEPTH, MIX_WIDTH, D_MODEL), MIX_WIDTH ** -0.5 * DEEPNORM_BETA)
    ln1_g = 1.0 + nrm(ks[8], (DEPTH, D_MODEL), 0.02)
    ln1_b = nrm(ks[9], (DEPTH, D_MODEL), 0.02)

    w_router = nrm(ks[10], (DEPTH, D_MODEL, N_EXPERTS), D_MODEL ** -0.5)
    b_router = nrm(ks[11], (DEPTH, N_EXPERTS), 0.01)
    w_gate_up = nrm(ks[12], (DEPTH, N_EXPERTS, D_MODEL, 2 * D_EXPERT), D_MODEL ** -0.5)
    b_gate_up = nrm(ks[13], (DEPTH, N_EXPERTS, 2 * D_EXPERT), 0.01)
    w_down = nrm(ks[14], (DEPTH, N_EXPERTS, D_EXPERT, D_MODEL), D_EXPERT ** -0.5 * DEEPNORM_BETA)
    b_down = nrm(ks[15], (DEPTH, N_EXPERTS, D_MODEL), 0.01)

    w_ple_gate = nrm(ks[16], (DEPTH, D_MODEL, D_MODEL), D_MODEL ** -0.5)
    b_ple_gate = nrm(ks[17], (DEPTH, D_MODEL), 0.01)
    w_ple_proj = nrm(ks[18], (DEPTH, PLE_DIM, D_MODEL), PLE_DIM ** -0.5 * DEEPNORM_BETA)
    ln2_g = 1.0 + nrm(ks[19], (DEPTH, D_MODEL), 0.02)
    ln2_b = nrm(ks[20], (DEPTH, D_MODEL), 0.02)

    return {"x": x, "p": p, "w_in": w_in, "conv_w": conv_w, "a_log": a_log, "dt_bias": dt_bias,
            "gnorm_w": gnorm_w, "w_out": w_out, "ln1_g": ln1_g, "ln1_b": ln1_b,
            "w_router": w_router, "b_router": b_router, "w_gate_up": w_gate_up,
            "b_gate_up": b_gate_up, "w_down": w_down, "b_down": b_down,
            "w_ple_gate": w_ple_gate, "b_ple_gate": b_ple_gate, "w_ple_proj": w_ple_proj,
            "ln2_g": ln2_g, "ln2_b": ln2_b}


def reference(x, p, w_in, conv_w, a_log, dt_bias, gnorm_w, w_out, ln1_g, ln1_b,
              w_router, b_router, w_gate_up, b_gate_up, w_down, b_down,
              w_ple_gate, b_ple_gate, w_ple_proj, ln2_g, ln2_b):
    bsz, s, _ = x.shape
    split_at = [int(o) for o in np.cumsum(IN_SIZES)[:-1]]
    for i in range(DEPTH):
        hproj = jnp.dot(x, w_in[i])
        qa, ka, va, qb, kb, vb, zb, bb, ab = jnp.split(hproj, split_at, axis=-1)
        shp = (bsz, s, A_HEADS, A_HEAD_DIM)
        ya = dilated_attention(qa.reshape(shp), ka.reshape(shp), va.reshape(shp))
        ya = ya.reshape(bsz, s, A_WIDTH).astype(x.dtype)
        yb = gated_deltanet(jnp.concatenate([qb, kb, vb], axis=-1), zb, bb, ab,
                            conv_w[i], a_log[i], dt_bias[i], gnorm_w[i]).astype(x.dtype)
        mix = jnp.dot(jnp.concatenate([ya, yb], axis=-1), w_out[i])
        x = layer_norm(DEEPNORM_ALPHA * x + mix, ln1_g[i], ln1_b[i])
        ffn = moe(x, w_router[i], b_router[i], w_gate_up[i], b_gate_up[i], w_down[i], b_down[i])
        ple = jax.nn.sigmoid(jnp.dot(x, w_ple_gate[i]) + b_ple_gate[i]) * jnp.dot(p[i], w_ple_proj[i])
        x = layer_norm(DEEPNORM_ALPHA * x + ffn + ple.astype(x.dtype), ln2_g[i], ln2_b[i])
    return x
```

```python
import functools
import math

import jax
import jax.numpy as jnp
from jax import lax
from jax.experimental import pallas as pl
from jax.experimental.pallas import tpu as pltpu

LANES = 128
VMEM_LIMIT_BYTES = 56 * 1024 * 1024

D_MODEL = 1024
PLE_DIM = 256
A_HEADS = 8
A_HEAD_DIM = 64
A_WIDTH = A_HEADS * A_HEAD_DIM
DILATIONS = (1, 4, 16)
ATT_BLOCK = 128
B_HEADS = 4
B_HEAD_DIM = 128
B_WIDTH = B_HEADS * B_HEAD_DIM
CONV_WIDTH = 4
DELTA_CHUNK = 64
N_EXPERTS = 32
TOP_K = 4
D_EXPERT = D_MODEL
SWIGLU_LIMIT = 7.0
SWIGLU_ALPHA = 1.702
MOE_BLOCK = 128
LN_EPS = 1e-5
RMS_EPS = 1e-6
L2_EPS = 1e-6
DEPTH = 1
DEEPNORM_ALPHA = (2.0 * DEPTH) ** 0.25

F32 = jnp.float32
BF16 = jnp.bfloat16


def _params(n_parallel_axes=1):
    return pltpu.CompilerParams(
        dimension_semantics=("arbitrary",) * n_parallel_axes,
        vmem_limit_bytes=VMEM_LIMIT_BYTES)


def _in_proj_kernel(x_ref, wa_ref, wb_ref, ws_ref, wst_ref, a_ref, b_ref, s_ref, st_ref):
    xb = x_ref[...].astype(BF16)
    a_ref[...] = jnp.dot(xb, wa_ref[...], preferred_element_type=F32)
    b_ref[...] = jnp.dot(xb, wb_ref[...], preferred_element_type=F32)
    s_ref[...] = jnp.dot(xb, ws_ref[...], preferred_element_type=F32)
    st_ref[...] = lax.dot_general(wst_ref[...], xb, (((1,), (1,)), ((), ())),
                                  preferred_element_type=F32)


def in_proj(x2d, w_in, tm=256):
    t, d = x2d.shape
    na, nb = 3 * A_WIDTH, 4 * B_WIDTH
    wa = w_in[:, :na].astype(BF16)
    wb = w_in[:, na:na + nb].astype(BF16)
    ws = jnp.pad(w_in[:, na + nb:], ((0, 0), (0, LANES - 2 * B_HEADS))).astype(BF16)
    wst = ws[:, :16].T
    return pl.pallas_call(
        _in_proj_kernel,
        grid=(t // tm,),
        in_specs=[pl.BlockSpec((tm, d), lambda i: (i, 0)),
                  pl.BlockSpec((d, na), lambda i: (0, 0)),
                  pl.BlockSpec((d, nb), lambda i: (0, 0)),
                  pl.BlockSpec((d, LANES), lambda i: (0, 0)),
                  pl.BlockSpec((16, d), lambda i: (0, 0))],
        out_specs=[pl.BlockSpec((tm, na), lambda i: (i, 0)),
                   pl.BlockSpec((tm, nb), lambda i: (i, 0)),
                   pl.BlockSpec((tm, LANES), lambda i: (i, 0)),
                   pl.BlockSpec((16, tm), lambda i: (0, i))],
        out_shape=[jax.ShapeDtypeStruct((t, na), F32),
                   jax.ShapeDtypeStruct((t, nb), F32),
                   jax.ShapeDtypeStruct((t, LANES), F32),
                   jax.ShapeDtypeStruct((16, t), F32)],
        compiler_params=_params(),
        name="in_proj",
    )(x2d, wa, wb, ws, wst)


def _attn_kernel(q_ref, k_ref, v_ref, o_ref, m_sc, l_sc, acc_sc, *, seq):
    blk = ATT_BLOCK
    lane = lax.broadcasted_iota(jnp.int32, (blk, LANES), 1)
    head0 = lane < A_HEAD_DIM
    row = lax.broadcasted_iota(jnp.int32, (blk, blk), 0)
    col = lax.broadcasted_iota(jnp.int32, (blk, blk), 1)
    prev_ok = col >= row
    cur_ok = col <= row
    neg_inf = jnp.float32(-jnp.inf)
    scale = 1.0 / math.sqrt(A_HEAD_DIM)

    def block(start, stride, has_prev, first_branch, last_branch):
        def rows(ref, s0):
            if stride == 1:
                return ref[pl.ds(s0, blk), :]
            return ref[pl.ds(s0, blk, stride=stride), :]

        def put(ref, val):
            if stride == 1:
                ref[pl.ds(start, blk), :] = val
            else:
                ref[pl.ds(start, blk, stride=stride), :] = val

        q = rows(q_ref, start) * scale
        kc = rows(k_ref, start).astype(BF16)
        vc = rows(v_ref, start)
        if has_prev:
            kp = rows(k_ref, start - blk * stride).astype(BF16)
            vp = rows(v_ref, start - blk * stride)
        if not first_branch:
            m_old = rows(m_sc, start)
            l_old = rows(l_sc, start)
            acc_old = rows(acc_sc, start)

        m_new_h, sum_h, pv_h = [], [], []
        for h in range(2):
            hm = head0 if h == 0 else jnp.logical_not(head0)
            qh = jnp.where(hm, q, 0.0).astype(BF16)
            s_c = lax.dot_general(qh, kc, (((1,), (1,)), ((), ())), preferred_element_type=F32)
            s_c = jnp.where(cur_ok, s_c, neg_inf)
            t = s_c
            if has_prev:
                s_p = lax.dot_general(qh, kp, (((1,), (1,)), ((), ())), preferred_element_type=F32)
                s_p = jnp.where(prev_ok, s_p, neg_inf)
                t = jnp.maximum(t, s_p)
            if not first_branch:
                t = jnp.maximum(t, jnp.where(hm, m_old, neg_inf))
            m_h = jnp.max(t, axis=1, keepdims=True)
            p_c = jnp.exp(s_c - m_h)
            psum = p_c
            pv = jnp.dot(p_c.astype(BF16), jnp.where(hm, vc, 0.0).astype(BF16),
                         preferred_element_type=F32)
            if has_prev:
                p_p = jnp.exp(s_p - m_h)
                psum = psum + p_p
                pv = pv + jnp.dot(p_p.astype(BF16), jnp.where(hm, vp, 0.0).astype(BF16),
                                  preferred_element_type=F32)
            m_new_h.append(m_h)
            sum_h.append(jnp.sum(psum, axis=1, keepdims=True))
            pv_h.append(pv)

        m_new = jnp.where(head0, m_new_h[0], m_new_h[1])
        l_new = jnp.where(head0, sum_h[0], sum_h[1])
        acc_new = pv_h[0] + pv_h[1]
        if not first_branch:
            alpha = jnp.exp(m_old - m_new)
            l_new = l_new + alpha * l_old
            acc_new = acc_new + alpha * acc_old
        if last_branch:
            put(o_ref, acc_new / l_new)
        else:
            put(m_sc, m_new)
            put(l_sc, l_new)
            put(acc_sc, acc_new)

    n_br = len(DILATIONS)
    for bi, dil in enumerate(DILATIONS):
        first, last = bi == 0, bi == n_br - 1
        sub_len = seq // dil
        nblk = sub_len // blk

        def sub(r, carry, dil=dil, nblk=nblk, first=first, last=last):
            block(r, dil, False, first, last)
            if nblk > 1:
                def body(n, c):
                    block(r + n * blk * dil, dil, True, first, last)
                    return c
                lax.fori_loop(1, nblk, body, 0)
            return carry

        if dil == 1:
            sub(0, 0)
        else:
            lax.fori_loop(0, dil, sub, 0)


def dilated_attn(qkv_a, bsz, seq):
    t = bsz * seq
    n_pairs = A_WIDTH // LANES
    blk = lambda off: pl.BlockSpec((seq, LANES), lambda b, p: (b, off + p))
    return pl.pallas_call(
        functools.partial(_attn_kernel, seq=seq),
        grid=(bsz, n_pairs),
        in_specs=[blk(0), blk(n_pairs), blk(2 * n_pairs)],
        out_specs=pl.BlockSpec((seq, LANES), lambda b, p: (b, p)),
        out_shape=jax.ShapeDtypeStruct((t, A_WIDTH), F32),
        scratch_shapes=[pltpu.VMEM((seq, LANES), F32)] * 3,
        compiler_params=_params(2),
        name="dilated_attn",
    )(qkv_a, qkv_a, qkv_a)


DELTA_HEADS_PER_STEP = 2
CONV_TILE = 256
_HI = lax.Precision.HIGHEST


def _dot_hi(a, b):
    return jnp.dot(a, b, precision=_HI, preferred_element_type=F32)


def _softplus(x):
    return jnp.maximum(x, 0.0) + jnp.log(1.0 + jnp.exp(-jnp.abs(x)))


def _sigmoid(x):
    return 1.0 / (1.0 + jnp.exp(-x))


def _delta_kernel(q_ref, k_ref, v_ref, z_ref, sc_ref, st_ref, par_ref, part_ref,
                  wq_ref, wk_ref, wv_ref, nw_ref, o_ref,
                  xpad_sc, qn_sc, kn_sc, vn_sc, u_sc, w_sc, attn_sc, kdt_sc, gcr_sc, egl_sc,
                  state_sc, *, seq, hps):
    c = DELTA_CHUNK
    dk = B_HEAD_DIM
    n_chunks = seq // c
    group = pl.program_id(1)
    lane = lax.broadcasted_iota(jnp.int32, (2 * c, LANES), 1)
    r2 = lax.broadcasted_iota(jnp.int32, (2 * c, 2 * c), 0)
    c2 = lax.broadcasted_iota(jnp.int32, (2 * c, 2 * c), 1)
    same_chunk = (r2 // c) == (c2 // c)
    cum_mat = jnp.where(same_chunk & (c2 <= r2), 1.0, 0.0).astype(F32)
    tot_mat = jnp.where(same_chunk, 1.0, 0.0).astype(F32)
    ri = lax.broadcasted_iota(jnp.int32, (c, c), 0)
    ci = lax.broadcasted_iota(jnp.int32, (c, c), 1)
    causal = ci <= ri
    strict = ci < ri
    eye = jnp.where(ci == ri, 1.0, 0.0).astype(F32)
    upper = jnp.where(ri <= ci, 1.0, 0.0).astype(F32)

    xpad_sc[0:8, :] = jnp.zeros((8, dk), F32)
    for hh in range(hps):
        cols = slice(hh * dk, (hh + 1) * dk)
        for src, w_ref, dst, kind in ((q_ref, wq_ref, qn_sc, "q"), (k_ref, wk_ref, kn_sc, "k"),
                                      (v_ref, wv_ref, vn_sc, "v")):
            xpad_sc[8:8 + seq, :] = src[:, cols]
            w = w_ref[:, cols]
            for c0 in range(0, seq, CONV_TILE):
                y = w[0:1, :] * xpad_sc[c0 + 5:c0 + 5 + CONV_TILE, :]
                for j in range(1, CONV_WIDTH):
                    y = y + w[j:j + 1, :] * xpad_sc[c0 + 5 + j:c0 + 5 + j + CONV_TILE, :]
                y = y * _sigmoid(y)
                if kind != "v":
                    y = y * lax.rsqrt(jnp.sum(y * y, axis=1, keepdims=True) + L2_EPS)
                if kind == "q":
                    y = y * (B_HEAD_DIM ** -0.5)
                dst[hh, c0:c0 + CONV_TILE, :] = y

    for hh in range(hps):
        h = group * hps + hh
        a_row = st_ref[B_HEADS + h]
        g_row = -jnp.exp(part_ref[0, h]) * _softplus(a_row + part_ref[1, h])
        gcr_sc[hh] = _dot_hi(g_row, upper)

    alog_l = par_ref[0:1, :]
    dtb_l = par_ref[1:2, :]

    def phase1(i, carry):
        r0 = pl.multiple_of(i * 2 * c, 2 * c)
        rows = pl.ds(r0, 2 * c)
        s = sc_ref[rows, :]
        beta_all = _sigmoid(s)
        g_all = -jnp.exp(alog_l) * _softplus(s + dtb_l)
        gc_all = _dot_hi(cum_mat, g_all)
        gl_all = _dot_hi(tot_mat, g_all)
        for hh in range(hps):
            h = group * hps + hh
            pick = lambda a, l: jnp.sum(jnp.where(lane == l, a, 0.0), axis=1, keepdims=True)
            beta = pick(beta_all, h)
            gc = pick(gc_all, B_HEADS + h)
            gl = pick(gl_all, B_HEADS + h)
            egc = jnp.exp(gc)
            q = qn_sc[hh, rows, :]
            k = kn_sc[hh, rows, :]
            v = vn_sc[hh, rows, :]
            kbeta = k * beta
            vbeta = v * beta
            qn_sc[hh, rows, :] = q * egc
            kdec = k * jnp.exp(gl - gc)
            kb16 = k.astype(BF16)
            for cc in range(2):
                sl = slice(cc * c, (cc + 1) * c)
                n = 2 * i + cc
                diff = gc[sl] - gcr_sc[hh, pl.ds(n, 1), :]
                e = jnp.exp(jnp.where(causal, diff, 0.0))
                kk = lax.dot_general(kbeta[sl].astype(BF16), kb16[sl], (((1,), (1,)), ((), ())),
                                     preferred_element_type=F32)
                a = jnp.where(strict, kk * e, 0.0)
                inv = eye - a
                pw = a
                for _ in range(5):
                    pw = _dot_hi(pw, pw)
                    inv = inv + _dot_hi(inv, pw)
                rhs = jnp.concatenate([vbeta[sl], kbeta[sl] * egc[sl]], axis=1)
                sol = _dot_hi(inv, rhs)
                u_sc[hh, pl.ds(r0 + cc * c, c), :] = sol[:, :dk]
                w_sc[hh, pl.ds(r0 + cc * c, c), :] = sol[:, dk:].astype(BF16)
                qk = lax.dot_general(q[sl].astype(BF16), kb16[sl], (((1,), (1,)), ((), ())),
                                     preferred_element_type=F32)
                attn_sc[hh, pl.ds(r0 + cc * c, c), :] = jnp.where(causal, qk * e, 0.0).astype(BF16)
                kdt_sc[hh, n] = kdec[sl].T.astype(BF16)
                egl_sc[hh, pl.ds(n, 1), :] = jnp.broadcast_to(jnp.exp(gl[cc * c:cc * c + 1]), (1, dk))
        return carry

    lax.fori_loop(0, n_chunks // 2, phase1, 0)

    state_sc[...] = jnp.zeros(state_sc.shape, F32)
    nw = nw_ref[...]

    def phase2(n, carry):
        r0 = pl.multiple_of(n * c, c)
        rows = pl.ds(r0, c)
        for hh in range(hps):
            st = state_sc[hh]
            st16 = st.astype(BF16)
            v_new = u_sc[hh, rows, :] - jnp.dot(w_sc[hh, rows, :], st16, preferred_element_type=F32)
            vn16 = v_new.astype(BF16)
            out = (jnp.dot(qn_sc[hh, rows, :].astype(BF16), st16, preferred_element_type=F32)
                   + jnp.dot(attn_sc[hh, rows, :], vn16, preferred_element_type=F32))
            state_sc[hh] = (st * egl_sc[hh, pl.ds(n, 1), :]
                            + jnp.dot(kdt_sc[hh, n], vn16, preferred_element_type=F32))
            o = out * lax.rsqrt(jnp.mean(out * out, axis=1, keepdims=True) + RMS_EPS) * nw
            z = z_ref[rows, hh * dk:(hh + 1) * dk]
            o_ref[rows, hh * dk:(hh + 1) * dk] = o * (z * _sigmoid(z))
        return carry

    lax.fori_loop(0, n_chunks, phase2, 0)


def deltanet(qkvz_b, scal, scal_t, conv_w, a_log, dt_bias, gnorm_w, bsz, seq):
    t = bsz * seq
    hps = DELTA_HEADS_PER_STEP
    n_groups = B_HEADS // hps
    wg = hps * B_HEAD_DIM
    c = DELTA_CHUNK
    n_chunks = seq // c
    st3 = scal_t.reshape(16, t // c, c)
    par = jnp.zeros((8, LANES), F32)
    par = par.at[0, B_HEADS:2 * B_HEADS].set(a_log).at[1, B_HEADS:2 * B_HEADS].set(dt_bias)
    part = jnp.stack([a_log, dt_bias]).astype(F32)
    blk = lambda off: pl.BlockSpec((seq, wg), lambda b, g: (b, off + g))
    wblk = lambda off: pl.BlockSpec((CONV_WIDTH, wg), lambda b, g: (0, off + g))
    sc = lambda shape, dt: pltpu.VMEM(shape, dt)
    return pl.pallas_call(
        functools.partial(_delta_kernel, seq=seq, hps=hps),
        grid=(bsz, n_groups),
        in_specs=[blk(0), blk(n_groups), blk(2 * n_groups), blk(3 * n_groups),
                  pl.BlockSpec((seq, LANES), lambda b, g: (b, 0)),
                  pl.BlockSpec((16, n_chunks, c), lambda b, g: (0, b, 0)),
                  pl.BlockSpec((8, LANES), lambda b, g: (0, 0)),
                  pl.BlockSpec(memory_space=pltpu.SMEM),
                  wblk(0), wblk(n_groups), wblk(2 * n_groups),
                  pl.BlockSpec((1, B_HEAD_DIM), lambda b, g: (0, 0))],
        out_specs=pl.BlockSpec((seq, wg), lambda b, g: (b, g)),
        out_shape=jax.ShapeDtypeStruct((t, B_WIDTH), F32),
        scratch_shapes=[sc((seq + 8, B_HEAD_DIM), F32),
                        sc((hps, seq, B_HEAD_DIM), F32), sc((hps, seq, B_HEAD_DIM), F32),
                        sc((hps, seq, B_HEAD_DIM), F32),
                        sc((hps, seq, B_HEAD_DIM), F32), sc((hps, seq, B_HEAD_DIM), BF16),
                        sc((hps, seq, c), BF16), sc((hps, n_chunks, B_HEAD_DIM, c), BF16),
                        sc((hps, n_chunks, c), F32), sc((hps, n_chunks, B_HEAD_DIM), F32),
                        sc((hps, B_HEAD_DIM, B_HEAD_DIM), F32)],
        compiler_params=_params(2),
        name="deltanet",
    )(qkvz_b, qkvz_b, qkvz_b, qkvz_b, scal, st3, par, part,
      conv_w, conv_w, conv_w, gnorm_w.reshape(1, B_HEAD_DIM))


def _layer_norm(h, g, b):
    mu = jnp.mean(h, axis=1, keepdims=True)
    hc = h - mu
    var = jnp.mean(hc * hc, axis=1, keepdims=True)
    return hc * lax.rsqrt(var + LN_EPS) * g + b


def _mix_route_kernel(ya_ref, yb_ref, x_ref, p_ref, woa_ref, wob_ref, g1_ref, b1_ref,
                      wr_ref, br_ref, wpg_ref, bpg_ref, wpp_ref,
                      x1_ref, res_ref, route_ref, gate_ref, cnt_ref, run_sc, *, tm):
    i = pl.program_id(0)

    @pl.when(i == 0)
    def _():
        run_sc[...] = jnp.zeros(run_sc.shape, F32)

    mix = (jnp.dot(ya_ref[...].astype(BF16), woa_ref[...], preferred_element_type=F32)
           + jnp.dot(yb_ref[...].astype(BF16), wob_ref[...], preferred_element_type=F32))
    x1 = _layer_norm(DEEPNORM_ALPHA * x_ref[...] + mix, g1_ref[...], b1_ref[...])
    x1_ref[...] = x1
    x1b = x1.astype(BF16)

    lane = lax.broadcasted_iota(jnp.int32, (tm, LANES), 1)
    logits = _dot_hi(x1, wr_ref[...]) + br_ref[...]
    cur = jnp.where(lane < N_EXPERTS, logits, -jnp.inf)
    vals, hots = [], []
    for _ in range(TOP_K):
        m = jnp.max(cur, axis=1, keepdims=True)
        idx = jnp.min(jnp.where(cur == m, lane, LANES), axis=1, keepdims=True)
        hot = lane == idx
        cur = jnp.where(hot, -jnp.inf, cur)
        vals.append(m)
        hots.append((hot, idx))
    exps = [jnp.exp(v - vals[0]) for v in vals]
    den = exps[0] + exps[1] + exps[2] + exps[3]

    member = jnp.zeros((tm, LANES), F32)
    for hot, _ in hots:
        member = member + jnp.where(hot, 1.0, 0.0)
    rr = lax.broadcasted_iota(jnp.int32, (tm, tm), 0)
    cc = lax.broadcasted_iota(jnp.int32, (tm, tm), 1)
    before = jnp.where(cc < rr, 1.0, 0.0).astype(BF16)
    prior = jnp.dot(before, member.astype(BF16), preferred_element_type=F32) + run_sc[0:1, :]
    route = jnp.zeros((tm, LANES), jnp.int32)
    gate = jnp.zeros((tm, LANES), F32)
    for kk, (hot, idx) in enumerate(hots):
        rank = jnp.sum(jnp.where(hot, prior, 0.0), axis=1, keepdims=True).astype(jnp.int32)
        route = jnp.where(lane == kk, idx, route)
        route = jnp.where(lane == TOP_K + kk, rank, route)
        gate = jnp.where(lane == kk, exps[kk] / den, gate)
    route_ref[...] = route
    gate_ref[...] = gate
    run_sc[...] = run_sc[...] + jnp.sum(member, axis=0, keepdims=True)
    cnt_ref[...] = run_sc[...]

    pgate = _sigmoid(jnp.dot(x1b, wpg_ref[...], preferred_element_type=F32) + bpg_ref[...])
    proj = jnp.dot(p_ref[...].astype(BF16), wpp_ref[...], preferred_element_type=F32)
    res_ref[...] = DEEPNORM_ALPHA * x1 + pgate * proj


def mix_route(ya, yb, x2d, p2d, w_out, ln1_g, ln1_b, w_router, b_router, w_ple_gate, b_ple_gate,
              w_ple_proj, tm=256):
    t, d = x2d.shape
    woa = w_out[:A_WIDTH].astype(BF16)
    wob = w_out[A_WIDTH:].astype(BF16)
    wr = jnp.pad(w_router, ((0, 0), (0, LANES - N_EXPERTS)))
    br = jnp.pad(b_router, (0, LANES - N_EXPERTS)).reshape(1, LANES)
    row = lambda w: pl.BlockSpec((tm, w), lambda i: (i, 0))
    full = lambda a: pl.BlockSpec(a.shape, lambda i: (0,) * a.ndim)
    ops = [woa, wob, ln1_g.reshape(1, d), ln1_b.reshape(1, d), wr, br,
           w_ple_gate.astype(BF16), b_ple_gate.reshape(1, d), w_ple_proj.astype(BF16)]
    return pl.pallas_call(
        functools.partial(_mix_route_kernel, tm=tm),
        grid=(t // tm,),
        in_specs=[row(A_WIDTH), row(B_WIDTH), row(d), row(PLE_DIM)] + [full(a) for a in ops],
        out_specs=[row(d), row(d), row(LANES), row(LANES),
                   pl.BlockSpec((8, LANES), lambda i: (0, 0))],
        out_shape=[jax.ShapeDtypeStruct((t, d), F32), jax.ShapeDtypeStruct((t, d), F32),
                   jax.ShapeDtypeStruct((t, LANES), jnp.int32),
                   jax.ShapeDtypeStruct((t, LANES), F32),
                   jax.ShapeDtypeStruct((8, LANES), F32)],
        scratch_shapes=[pltpu.VMEM((8, LANES), F32)],
        compiler_params=_params(),
        name="mix_route",
    )(ya, yb, x2d, p2d, *ops)


MOE_TILE = 512
DISPATCH_TM = 256
COMBINE_TM = 128


def _dispatch_kernel(pstart_ref, pend_ref, route_ref, x_ref, xb_ref, zero_sc, sem, zsem, *, tm):
    i = pl.program_id(0)

    def zero_copy(e):
        off = pl.multiple_of(pend_ref[e] - MOE_TILE, MOE_TILE)
        return pltpu.make_async_copy(zero_sc, xb_ref.at[pl.ds(off, MOE_TILE), :], zsem)

    @pl.when(i == 0)
    def _():
        zero_sc[...] = jnp.zeros(zero_sc.shape, zero_sc.dtype)
        for e in range(N_EXPERTS):
            @pl.when(pend_ref[e] > pstart_ref[e])
            def _():
                zero_copy(e).start()
        for e in range(N_EXPERTS):
            @pl.when(pend_ref[e] > pstart_ref[e])
            def _():
                zero_copy(e).wait()

        n_tiles = xb_ref.shape[0] // MOE_TILE
        first_unused = pend_ref[N_EXPERTS - 1] // MOE_TILE

        def tail_copy(j):
            off = pl.multiple_of(j * MOE_TILE, MOE_TILE)
            return pltpu.make_async_copy(zero_sc, xb_ref.at[pl.ds(off, MOE_TILE), :], zsem)

        def tail_start(j, c):
            tail_copy(j).start()
            return c

        def tail_wait(j, c):
            tail_copy(j).wait()
            return c

        lax.fori_loop(first_unused, n_tiles, tail_start, 0)
        lax.fori_loop(first_unused, n_tiles, tail_wait, 0)

    def row_copy(r, dest):
        return pltpu.make_async_copy(x_ref.at[pl.ds(r, 1), :], xb_ref.at[pl.ds(dest, 1), :], sem)

    def start(r, c):
        for kk in range(TOP_K):
            e = route_ref[r * 2 * TOP_K + kk]
            rank = route_ref[r * 2 * TOP_K + TOP_K + kk]
            row_copy(r, pstart_ref[e] + rank).start()
        return c

    lax.fori_loop(0, tm, start, 0)

    def wait(r, c):
        for kk in range(TOP_K):
            row_copy(0, 0).wait()
        return c

    lax.fori_loop(0, tm, wait, 0)


def dispatch(x1, route_flat, pad_start, pad_end, n_rows, tm=DISPATCH_TM):
    t, d = x1.shape
    return pl.pallas_call(
        functools.partial(_dispatch_kernel, tm=tm),
        grid_spec=pltpu.PrefetchScalarGridSpec(
            num_scalar_prefetch=2,
            grid=(t // tm,),
            in_specs=[pl.BlockSpec((tm * 2 * TOP_K,), lambda i, ps, pe: (i,), memory_space=pltpu.SMEM),
                      pl.BlockSpec((tm, d), lambda i, ps, pe: (i, 0))],
            out_specs=pl.BlockSpec(memory_space=pl.ANY),
            scratch_shapes=[pltpu.VMEM((MOE_TILE, d), F32),
                            pltpu.SemaphoreType.DMA(()), pltpu.SemaphoreType.DMA(())]),
        out_shape=jax.ShapeDtypeStruct((n_rows, d), F32),
        compiler_params=_params(),
        name="dispatch",
    )(pad_start, pad_end, route_flat, x1)


def _expert_kernel(be_ref, nu_ref, x_ref, wgu_ref, bgu_ref, wd_ref, bd_ref, y_ref):
    i = pl.program_id(0)

    @pl.when(i < nu_ref[0])
    def _():
        xb = x_ref[...].astype(BF16)
        acc = jnp.zeros(y_ref.shape, F32) + bd_ref[0]
        nchunk = 256
        for c0 in range(0, D_EXPERT, nchunk):
            g = jnp.dot(xb, wgu_ref[0, :, c0:c0 + nchunk], preferred_element_type=F32)
            g = g + bgu_ref[0, :, c0:c0 + nchunk]
            u = jnp.dot(xb, wgu_ref[0, :, D_EXPERT + c0:D_EXPERT + c0 + nchunk],
                        preferred_element_type=F32)
            u = u + bgu_ref[0, :, D_EXPERT + c0:D_EXPERT + c0 + nchunk]
            g = jnp.minimum(g, SWIGLU_LIMIT)
            u = jnp.clip(u, -SWIGLU_LIMIT, SWIGLU_LIMIT)
            act = (u + 1.0) * (g * _sigmoid(SWIGLU_ALPHA * g))
            acc = acc + jnp.dot(act.astype(BF16), wd_ref[0, c0:c0 + nchunk, :],
                                preferred_element_type=F32)
        y_ref[...] = acc

    @pl.when(i >= nu_ref[0])
    def _():
        y_ref[...] = jnp.zeros(y_ref.shape, F32)


def experts(xb, block_e, n_used, w_gate_up, b_gate_up, w_down, b_down):
    n_rows, d = xb.shape
    n_blocks = n_rows // MOE_TILE
    clamp = lambda i, be, nu: jnp.minimum(i, nu[0] - 1)
    return pl.pallas_call(
        _expert_kernel,
        grid_spec=pltpu.PrefetchScalarGridSpec(
            num_scalar_prefetch=2,
            grid=(n_blocks,),
            in_specs=[pl.BlockSpec((MOE_TILE, d), lambda i, be, nu: (clamp(i, be, nu), 0)),
                      pl.BlockSpec((1, d, 2 * D_EXPERT), lambda i, be, nu: (be[i], 0, 0)),
                      pl.BlockSpec((1, 1, 2 * D_EXPERT), lambda i, be, nu: (be[i], 0, 0)),
                      pl.BlockSpec((1, D_EXPERT, d), lambda i, be, nu: (be[i], 0, 0)),
                      pl.BlockSpec((1, 1, d), lambda i, be, nu: (be[i], 0, 0))],
            out_specs=pl.BlockSpec((MOE_TILE, d), lambda i, be, nu: (i, 0))),
        out_shape=jax.ShapeDtypeStruct((n_rows, d), F32),
        compiler_params=_params(),
        name="experts",
    )(block_e, n_used, xb, w_gate_up.astype(BF16), b_gate_up.reshape(N_EXPERTS, 1, -1),
      w_down.astype(BF16), b_down.reshape(N_EXPERTS, 1, -1))


def _combine_kernel(pstart_ref, route_ref, gate_ref, res_ref, g2_ref, b2_ref, yb_ref, o_ref,
                    buf_sc, sem, *, tm):
    def row_copy(kk, r, src):
        return pltpu.make_async_copy(yb_ref.at[pl.ds(src, 1), :], buf_sc.at[kk, pl.ds(r, 1), :], sem)

    def start(r, c):
        for kk in range(TOP_K):
            e = route_ref[r * 2 * TOP_K + kk]
            rank = route_ref[r * 2 * TOP_K + TOP_K + kk]
            row_copy(kk, r, pstart_ref[e] + rank).start()
        return c

    lax.fori_loop(0, tm, start, 0)

    def wait(r, c):
        for kk in range(TOP_K):
            row_copy(0, 0, 0).wait()
        return c

    lax.fori_loop(0, tm, wait, 0)

    h = res_ref[...]
    gate = gate_ref[...]
    for kk in range(TOP_K):
        h = h + gate[:, kk:kk + 1] * buf_sc[kk]
    o_ref[...] = _layer_norm(h, g2_ref[...], b2_ref[...])


def combine(yb, route_flat, gates, res, pad_start, ln2_g, ln2_b, tm=COMBINE_TM):
    t, d = res.shape
    return pl.pallas_call(
        functools.partial(_combine_kernel, tm=tm),
        grid_spec=pltpu.PrefetchScalarGridSpec(
            num_scalar_prefetch=1,
            grid=(t // tm,),
            in_specs=[pl.BlockSpec((tm * 2 * TOP_K,), lambda i, ps: (i,), memory_space=pltpu.SMEM),
                      pl.BlockSpec((tm, LANES), lambda i, ps: (i, 0)),
                      pl.BlockSpec((tm, d), lambda i, ps: (i, 0)),
                      pl.BlockSpec((1, d), lambda i, ps: (0, 0)),
                      pl.BlockSpec((1, d), lambda i, ps: (0, 0)),
                      pl.BlockSpec(memory_space=pl.ANY)],
            out_specs=pl.BlockSpec((tm, d), lambda i, ps: (i, 0)),
            scratch_shapes=[pltpu.VMEM((TOP_K, tm, d), F32), pltpu.SemaphoreType.DMA(())]),
        out_shape=jax.ShapeDtypeStruct((t, d), F32),
        compiler_params=_params(),
        name="combine",
    )(pad_start, route_flat, gates, res, ln2_g.reshape(1, d), ln2_b.reshape(1, d), yb)


def kernel(x, p, w_in, conv_w, a_log, dt_bias, gnorm_w, w_out, ln1_g, ln1_b, w_router, b_router,
           w_gate_up, b_gate_up, w_down, b_down, w_ple_gate, b_ple_gate, w_ple_proj, ln2_g, ln2_b):
    bsz, seq, d = x.shape
    t = bsz * seq
    x2d = x.reshape(t, d)
    qkv_a, qkvz_b, scal, scal_t = in_proj(x2d, w_in[0])
    ya = dilated_attn(qkv_a, bsz, seq)
    yb = deltanet(qkvz_b, scal, scal_t, conv_w[0], a_log[0], dt_bias[0], gnorm_w[0], bsz, seq)
    x1, res, route, gates, counts = mix_route(
        ya, yb, x2d, p[0].reshape(t, PLE_DIM), w_out[0], ln1_g[0], ln1_b[0], w_router[0],
        b_router[0], w_ple_gate[0], b_ple_gate[0], w_ple_proj[0])

    cnt = counts[0, :N_EXPERTS].astype(jnp.int32)
    padded = (cnt + MOE_TILE - 1) // MOE_TILE * MOE_TILE
    pad_end = jnp.cumsum(padded).astype(jnp.int32)
    pad_start = pad_end - padded
    n_blocks = (t * TOP_K) // MOE_TILE + N_EXPERTS
    n_used = (pad_end[-1:] // MOE_TILE).astype(jnp.int32)
    block_e = jnp.minimum(
        jnp.searchsorted(pad_end, jnp.arange(n_blocks, dtype=jnp.int32) * MOE_TILE, side="right"),
        N_EXPERTS - 1).astype(jnp.int32)
    route_flat = route[:, :2 * TOP_K].reshape(-1)

    xb = dispatch(x1, route_flat, pad_start, pad_end, n_blocks * MOE_TILE)
    yexp = experts(xb, block_e, n_used, w_gate_up[0], b_gate_up[0], w_down[0], b_down[0])
    out = combine(yexp, route_flat, gates, res, pad_start, ln2_g[0], ln2_b[0])
    return out.reshape(bsz, seq, d)
```

```python
import functools
import math

import jax
import jax.numpy as jnp
from jax import lax
from jax.experimental import pallas as pl
from jax.experimental.pallas import tpu as pltpu

LANES = 128
VMEM_LIMIT_BYTES = 56 * 1024 * 1024

D_MODEL = 1024
PLE_DIM = 256
A_HEADS = 8
A_HEAD_DIM = 64
A_WIDTH = A_HEADS * A_HEAD_DIM
DILATIONS = (1, 4, 16)
ATT_BLOCK = 128
B_HEADS = 4
B_HEAD_DIM = 128
B_WIDTH = B_HEADS * B_HEAD_DIM
CONV_WIDTH = 4
DELTA_CHUNK = 64
N_EXPERTS = 32
TOP_K = 4
D_EXPERT = D_MODEL
SWIGLU_LIMIT = 7.0
SWIGLU_ALPHA = 1.702
MOE_BLOCK = 128
LN_EPS = 1e-5
RMS_EPS = 1e-6
L2_EPS = 1e-6
DEPTH = 1
DEEPNORM_ALPHA = (2.0 * DEPTH) ** 0.25

F32 = jnp.float32
BF16 = jnp.bfloat16


def _params(n_parallel_axes=1):
    return pltpu.CompilerParams(
        dimension_semantics=("arbitrary",) * n_parallel_axes,
        vmem_limit_bytes=VMEM_LIMIT_BYTES)


def _in_proj_kernel(x_ref, wa_ref, wb_ref, wz_ref, ws_ref, wst_ref, a_ref, b_ref, z_ref, s_ref, st_ref):
    xb = x_ref[...].astype(BF16)
    a_ref[...] = jnp.dot(xb, wa_ref[...], preferred_element_type=F32)
    b_ref[...] = jnp.dot(xb, wb_ref[...], preferred_element_type=F32).astype(BF16)
    z_ref[...] = jnp.dot(xb, wz_ref[...], preferred_element_type=F32)
    s_ref[...] = jnp.dot(xb, ws_ref[...], preferred_element_type=F32)
    st_ref[...] = lax.dot_general(wst_ref[...], xb, (((1,), (1,)), ((), ())),
                                  preferred_element_type=F32)


def in_proj(x2d, w_in, tm=256):
    t, d = x2d.shape
    na, nb = 3 * A_WIDTH, 3 * B_WIDTH
    wa = w_in[:, :na].astype(BF16)
    wb = w_in[:, na:na + nb].astype(BF16)
    wz = w_in[:, na + nb:na + nb + B_WIDTH].astype(BF16)
    ws = jnp.pad(w_in[:, na + nb + B_WIDTH:], ((0, 0), (0, LANES - 2 * B_HEADS))).astype(BF16)
    wst = ws[:, :16].T
    full = lambda a: pl.BlockSpec(a.shape, lambda i: (0, 0))
    return pl.pallas_call(
        _in_proj_kernel,
        grid=(t // tm,),
        in_specs=[pl.BlockSpec((tm, d), lambda i: (i, 0)),
                  full(wa), full(wb), full(wz), full(ws), full(wst)],
        out_specs=[pl.BlockSpec((tm, na), lambda i: (i, 0)),
                   pl.BlockSpec((tm, nb), lambda i: (i, 0)),
                   pl.BlockSpec((tm, B_WIDTH), lambda i: (i, 0)),
                   pl.BlockSpec((tm, LANES), lambda i: (i, 0)),
                   pl.BlockSpec((16, tm), lambda i: (0, i))],
        out_shape=[jax.ShapeDtypeStruct((t, na), F32),
                   jax.ShapeDtypeStruct((t, nb), BF16),
                   jax.ShapeDtypeStruct((t, B_WIDTH), F32),
                   jax.ShapeDtypeStruct((t, LANES), F32),
                   jax.ShapeDtypeStruct((16, t), F32)],
        compiler_params=_params(),
        name="in_proj",
    )(x2d, wa, wb, wz, ws, wst)


ATTN_UNROLL_DENSE = 3
ATTN_UNROLL_SINGLE = 4


def _attn_kernel(q_ref, k_ref, v_ref, o_ref, m_sc, l_sc, acc_sc, *, seq):
    blk = ATT_BLOCK
    lane = lax.broadcasted_iota(jnp.int32, (blk, LANES), 1)
    head0 = lane < A_HEAD_DIM
    head0_kv = lax.broadcasted_iota(jnp.int32, (2 * blk, LANES), 1) < A_HEAD_DIM
    row = lax.broadcasted_iota(jnp.int32, (blk, blk), 0)
    col = lax.broadcasted_iota(jnp.int32, (blk, blk), 1)
    prev_ok = col >= row
    cur_ok = col <= row
    neg_inf = jnp.float32(-jnp.inf)
    scale = 1.0 / math.sqrt(A_HEAD_DIM)

    def blocks(starts, stride, has_prev, first_branch, last_branch):
        def rows(ref, s0):
            if stride == 1:
                return ref[pl.ds(s0, blk), :]
            return ref[pl.ds(s0, blk, stride=stride), :]

        def put(ref, s0, val):
            if stride == 1:
                ref[pl.ds(s0, blk), :] = val
            else:
                ref[pl.ds(s0, blk, stride=stride), :] = val

        hmask = (head0, jnp.logical_not(head0))
        items = []
        for start, hp in zip(starts, has_prev):
            q = rows(q_ref, start) * scale
            keys = rows(k_ref, start)
            vals = rows(v_ref, start)
            if hp:
                keys = jnp.concatenate([rows(k_ref, start - blk * stride), keys], axis=0)
                vals = jnp.concatenate([rows(v_ref, start - blk * stride), vals], axis=0)
            it = dict(start=start, hp=hp, keys=keys.astype(BF16), vals=vals,
                      qh=[jnp.where(hm, q, 0.0).astype(BF16) for hm in hmask])
            if not first_branch:
                it["m_old"] = rows(m_sc, start)
                it["l_old"] = rows(l_sc, start)
                it["acc_old"] = rows(acc_sc, start)
            items.append(it)

        for it in items:
            it["s"] = [lax.dot_general(qh, it["keys"], (((1,), (1,)), ((), ())),
                                       preferred_element_type=F32) for qh in it["qh"]]
        for it in items:
            ok = jnp.concatenate([prev_ok, cur_ok], axis=1) if it["hp"] else cur_ok
            it["m"], it["sum"], it["p"] = [], [], []
            for h in range(2):
                s = jnp.where(ok, it["s"][h], neg_inf)
                t = jnp.maximum(s[:, :blk], s[:, blk:]) if it["hp"] else s
                if not first_branch:
                    t = jnp.maximum(t, jnp.where(hmask[h], it["m_old"], neg_inf))
                m_h = jnp.max(t, axis=1, keepdims=True)
                p = jnp.exp(s - m_h)
                psum = p[:, :blk] + p[:, blk:] if it["hp"] else p
                it["m"].append(m_h)
                it["sum"].append(jnp.sum(psum, axis=1, keepdims=True))
                it["p"].append(p.astype(BF16))
        for it in items:
            kv0 = head0_kv if it["hp"] else head0
            kvmask = (kv0, jnp.logical_not(kv0))
            it["pv"] = [jnp.dot(it["p"][h], jnp.where(kvmask[h], it["vals"], 0.0).astype(BF16),
                                preferred_element_type=F32) for h in range(2)]
        for it in items:
            m_new = jnp.where(head0, it["m"][0], it["m"][1])
            l_new = jnp.where(head0, it["sum"][0], it["sum"][1])
            acc_new = it["pv"][0] + it["pv"][1]
            if not first_branch:
                alpha = jnp.exp(it["m_old"] - m_new)
                l_new = l_new + alpha * it["l_old"]
                acc_new = acc_new + alpha * it["acc_old"]
            if last_branch:
                put(o_ref, it["start"], acc_new / l_new)
            else:
                put(m_sc, it["start"], m_new)
                put(l_sc, it["start"], l_new)
                put(acc_sc, it["start"], acc_new)

    n_br = len(DILATIONS)
    for bi, dil in enumerate(DILATIONS):
        first, last = bi == 0, bi == n_br - 1
        nblk = seq // dil // blk
        if dil == 1:
            blocks([0], 1, [False], first, last)
            per = ATTN_UNROLL_DENSE
            assert (nblk - 1) % per == 0

            def body(g, c, first=first, last=last, per=per):
                starts = [pl.multiple_of((1 + g * per + j) * blk, blk) for j in range(per)]
                blocks(starts, 1, [True] * per, first, last)
                return c
            lax.fori_loop(0, (nblk - 1) // per, body, 0)
        elif nblk > 1:
            def body(r, c, dil=dil, nblk=nblk, first=first, last=last):
                blocks([r + n * blk * dil for n in range(nblk)], dil,
                       [n > 0 for n in range(nblk)], first, last)
                return c
            lax.fori_loop(0, dil, body, 0)
        else:
            per = ATTN_UNROLL_SINGLE
            assert dil % per == 0

            def body(g, c, dil=dil, first=first, last=last, per=per):
                blocks([g * per + j for j in range(per)], dil, [False] * per, first, last)
                return c
            lax.fori_loop(0, dil // per, body, 0)


def dilated_attn(qkv_a, bsz, seq):
    t = bsz * seq
    n_pairs = A_WIDTH // LANES
    blk = lambda off: pl.BlockSpec((seq, LANES), lambda b, p: (b, off + p))
    return pl.pallas_call(
        functools.partial(_attn_kernel, seq=seq),
        grid=(bsz, n_pairs),
        in_specs=[blk(0), blk(n_pairs), blk(2 * n_pairs)],
        out_specs=pl.BlockSpec((seq, LANES), lambda b, p: (b, p)),
        out_shape=jax.ShapeDtypeStruct((t, A_WIDTH), F32),
        scratch_shapes=[pltpu.VMEM((seq, LANES), F32)] * 3,
        compiler_params=_params(2),
        name="dilated_attn",
    )(qkv_a, qkv_a, qkv_a)


DELTA_HEADS_PER_STEP = 4
DELTA_P1_CHUNKS = 4
CONV_TILE = 256
CONV_HISTORY_ROWS = 16
_HI = lax.Precision.HIGHEST


def _dot_hi(a, b):
    return jnp.dot(a, b, precision=_HI, preferred_element_type=F32)


def _dot16(a, b):
    return jnp.dot(a, b, preferred_element_type=F32)


def _split2(x):
    hi = x.astype(BF16)
    return hi, (x - hi.astype(F32)).astype(BF16)


def _dot3(a, b):
    return _dot16(a[0], b[0]) + (_dot16(a[1], b[0]) + _dot16(a[0], b[1]))


def _split3(x):
    hi = x.astype(BF16)
    r = x - hi.astype(F32)
    mid = r.astype(BF16)
    return hi, mid, (r - mid.astype(F32)).astype(BF16)


def _dot_sel(m01, x):
    hi, mid, lo = _split3(x)
    return _dot16(m01, hi) + (_dot16(m01, mid) + _dot16(m01, lo))


def _dot_sel_r(x, m01):
    hi, mid, lo = _split3(x)
    return _dot16(hi, m01) + (_dot16(mid, m01) + _dot16(lo, m01))


def _stack_parts(x, lo_half, want_l=True, want_r=True):
    xh = x.astype(BF16)
    xh_f = xh.astype(F32)
    xl_f = x - xh_f
    left = right = None
    if want_l:
        mix = jnp.where(lo_half, xh_f, xl_f).astype(BF16)
        left = jnp.concatenate([mix, mix], axis=1)
    if want_r:
        xl = xl_f.astype(BF16)
        right = jnp.concatenate([xh, xh, xl, xl], axis=0)
    return left, right


def _softplus(x):
    return jnp.maximum(x, 0.0) + jnp.log(1.0 + jnp.exp(-jnp.abs(x)))


def _sigmoid(x):
    return 1.0 / (1.0 + jnp.exp(-x))


def _delta_kernel(q_ref, k_ref, v_ref, z_ref, sc_ref, st_ref, par_ref, part_ref,
                  wq_ref, wk_ref, wv_ref, nw_ref, o_ref,
                  qn_sc, kn_sc, vn_sc, w_sc, attn_sc, kdt_sc, gcr_sc, egl_sc,
                  state_sc, *, seq, hps):
    c = DELTA_CHUNK
    dk = B_HEAD_DIM
    n_chunks = seq // c
    tile = DELTA_P1_CHUNKS * c
    group = pl.program_id(1)
    lane = lax.broadcasted_iota(jnp.int32, (tile, LANES), 1)
    r2 = lax.broadcasted_iota(jnp.int32, (tile, tile), 0)
    c2 = lax.broadcasted_iota(jnp.int32, (tile, tile), 1)
    same_chunk = (r2 // c) == (c2 // c)
    cum_mat = jnp.where(same_chunk & (c2 <= r2), 1.0, 0.0).astype(BF16)
    tot_mat = jnp.where(same_chunk, 1.0, 0.0).astype(BF16)
    ri = lax.broadcasted_iota(jnp.int32, (c, 2 * c), 0)
    li = lax.broadcasted_iota(jnp.int32, (c, 2 * c), 1)
    lo_half = li < c
    ci = jnp.where(lo_half, li, li - c)
    causal = ci <= ri
    strict = ci < ri
    eye = jnp.where(ci == ri, 1.0, 0.0).astype(F32)
    upper = jnp.where(ri <= ci, 1.0, 0.0).astype(BF16)

    hist = CONV_HISTORY_ROWS
    for hh in range(hps):
        cols = slice(hh * dk, (hh + 1) * dk)
        for src, w_ref, dst, kind in ((q_ref, wq_ref, qn_sc, "q"), (k_ref, wk_ref, kn_sc, "k"),
                                      (v_ref, wv_ref, vn_sc, "v")):
            w = w_ref[:, cols]

            def conv_tile(xe, c0, w=w, dst=dst, kind=kind, hh=hh):
                off = hist - (CONV_WIDTH - 1)
                y = w[0:1, :] * xe[off:off + CONV_TILE]
                for j in range(1, CONV_WIDTH):
                    y = y + w[j:j + 1, :] * xe[off + j:off + j + CONV_TILE]
                y = y * _sigmoid(y)
                if kind != "v":
                    y = y * lax.rsqrt(jnp.sum(y * y, axis=1, keepdims=True) + L2_EPS)
                if kind == "q":
                    y = y * (B_HEAD_DIM ** -0.5)
                dst[hh, pl.ds(c0, CONV_TILE), :] = y

            conv_tile(jnp.concatenate([jnp.zeros((hist, dk), F32),
                                       src[0:CONV_TILE, cols].astype(F32)], axis=0), 0)

            def conv_body(ti, carry, src=src, cols=cols, conv_tile=conv_tile):
                c0 = pl.multiple_of(ti * CONV_TILE, CONV_TILE)
                conv_tile(src[pl.ds(c0 - hist, hist + CONV_TILE), cols].astype(F32), c0)
                return carry

            lax.fori_loop(1, seq // CONV_TILE, conv_body, 0)

    for hh in range(hps):
        h = group * hps + hh
        a_row = st_ref[B_HEADS + h]
        g_row = -jnp.exp(part_ref[0, h]) * _softplus(a_row + part_ref[1, h])
        gcr_sc[hh] = _dot_sel_r(g_row, upper)

    alog_l = par_ref[0:1, :]
    dtb_l = par_ref[1:2, :]

    def phase1(i, carry):
        r0 = pl.multiple_of(i * tile, tile)
        rows = pl.ds(r0, tile)
        s = sc_ref[rows, :]
        beta_all = _sigmoid(s)
        g_all = -jnp.exp(alog_l) * _softplus(s + dtb_l)
        gc_all = _dot_sel(cum_mat, g_all)
        gl_all = _dot_sel(tot_mat, g_all)
        chains = []
        for hh in range(hps):
            h = group * hps + hh
            pick = lambda a, l: jnp.sum(jnp.where(lane == l, a, 0.0), axis=1, keepdims=True)
            beta = pick(beta_all, h)
            gc = pick(gc_all, B_HEADS + h)
            gl = pick(gl_all, B_HEADS + h)
            egc = jnp.exp(gc)
            q = qn_sc[hh, rows, :]
            k = kn_sc[hh, rows, :]
            v = vn_sc[hh, rows, :]
            kbeta = k * beta
            vbeta = v * beta
            qn_sc[hh, rows, :] = q * egc
            kdec = k * jnp.exp(gl - gc)
            kb16 = k.astype(BF16)
            for cc in range(DELTA_P1_CHUNKS):
                sl = slice(cc * c, (cc + 1) * c)
                n = DELTA_P1_CHUNKS * i + cc
                kdt_sc[hh, n] = kdec[sl].T.astype(BF16)
                egl_sc[hh, pl.ds(n, 1), :] = jnp.broadcast_to(jnp.exp(gl[cc * c:cc * c + 1]), (1, dk))
                diff = gc[sl] - gcr_sc[hh, pl.ds(n, 1), :]
                e = jnp.exp(jnp.where(causal, diff, 0.0))
                k2 = jnp.concatenate([kb16[sl], kb16[sl]], axis=0)
                lhs = jnp.concatenate([kbeta[sl], q[sl]], axis=0).astype(BF16)
                rhs = jnp.concatenate([vbeta[sl], kbeta[sl] * egc[sl]], axis=1)
                chains.append(dict(hh=hh, row=r0 + cc * c, e=e, k2=k2, lhs=lhs, rhs=rhs))

        for ch in chains:
            kq = lax.dot_general(ch["lhs"], ch["k2"], (((1,), (1,)), ((), ())),
                                 preferred_element_type=F32)
            ch["a"] = jnp.where(strict, kq[:c] * ch["e"], 0.0)
            attn_sc[ch["hh"], pl.ds(ch["row"], c), :] = (
                jnp.where(causal, kq[c:] * ch["e"], 0.0)[:, :c].astype(BF16))
        for ch in chains:
            ch["inv"] = eye - ch["a"]
            p_l, p_r = _stack_parts(ch["a"], lo_half)
            ch["p"] = _dot16(p_l, p_r)
        for _ in range(4):
            for ch in chains:
                p_l, p_r = _stack_parts(ch["p"], lo_half)
                inv_l, _ = _stack_parts(ch["inv"], lo_half, want_r=False)
                res = _dot16(jnp.concatenate([p_l, inv_l], axis=0), p_r)
                ch["p"] = res[:c]
                ch["inv"] = ch["inv"] + res[c:]
        for ch in chains:
            inv_l, _ = _stack_parts(ch["inv"], lo_half, want_r=False)
            _, p_r = _stack_parts(ch["p"], lo_half, want_l=False)
            ch["inv"] = ch["inv"] + _dot16(inv_l, p_r)
        for ch in chains:
            inv_l, _ = _stack_parts(ch["inv"], lo_half, want_r=False)
            _, rhs_r = _stack_parts(ch["rhs"], None, want_l=False)
            sol = _dot16(inv_l, rhs_r)
            vn_sc[ch["hh"], pl.ds(ch["row"], c), :] = sol[:, :dk]
            w_sc[ch["hh"], pl.ds(ch["row"], c), :] = sol[:, dk:].astype(BF16)
        return carry

    lax.fori_loop(0, n_chunks // DELTA_P1_CHUNKS, phase1, 0)

    state_sc[...] = jnp.zeros(state_sc.shape, F32)
    nw = nw_ref[...]

    def phase2(n, carry):
        r0 = pl.multiple_of(n * c, c)
        rows = pl.ds(r0, c)
        heads = range(hps)
        st = [state_sc[hh] for hh in heads]
        st16 = [s_.astype(BF16) for s_ in st]
        lhs = [jnp.concatenate([w_sc[hh, rows, :], qn_sc[hh, rows, :].astype(BF16)], axis=0)
               for hh in heads]
        ws = [_dot16(lhs[hh], st16[hh]) for hh in heads]
        vn16 = [(vn_sc[hh, rows, :] - ws[hh][:c]).astype(BF16) for hh in heads]
        av = [_dot16(attn_sc[hh, rows, :], vn16[hh]) for hh in heads]
        kv = [_dot16(kdt_sc[hh, n], vn16[hh]) for hh in heads]
        for hh in heads:
            state_sc[hh] = st[hh] * egl_sc[hh, pl.ds(n, 1), :] + kv[hh]
            out = ws[hh][c:] + av[hh]
            o = out * lax.rsqrt(jnp.mean(out * out, axis=1, keepdims=True) + RMS_EPS) * nw
            z = z_ref[rows, hh * dk:(hh + 1) * dk]
            o_ref[rows, hh * dk:(hh + 1) * dk] = (o * (z * _sigmoid(z))).astype(o_ref.dtype)
        return carry

    lax.fori_loop(0, n_chunks, phase2, 0)


def deltanet(qkv_b, z_b, scal, scal_t, conv_w, a_log, dt_bias, gnorm_w, bsz, seq):
    t = bsz * seq
    hps = DELTA_HEADS_PER_STEP
    n_groups = B_HEADS // hps
    wg = hps * B_HEAD_DIM
    c = DELTA_CHUNK
    n_chunks = seq // c
    st3 = scal_t.reshape(16, t // c, c)
    par = jnp.zeros((8, LANES), F32)
    par = par.at[0, B_HEADS:2 * B_HEADS].set(a_log).at[1, B_HEADS:2 * B_HEADS].set(dt_bias)
    part = jnp.stack([a_log, dt_bias]).astype(F32)
    blk = lambda off: pl.BlockSpec((seq, wg), lambda b, g: (b, off + g))
    wblk = lambda off: pl.BlockSpec((CONV_WIDTH, wg), lambda b, g: (0, off + g))
    sc = lambda shape, dt: pltpu.VMEM(shape, dt)
    return pl.pallas_call(
        functools.partial(_delta_kernel, seq=seq, hps=hps),
        grid=(bsz, n_groups),
        in_specs=[blk(0), blk(n_groups), blk(2 * n_groups), blk(0),
                  pl.BlockSpec((seq, LANES), lambda b, g: (b, 0)),
                  pl.BlockSpec((16, n_chunks, c), lambda b, g: (0, b, 0)),
                  pl.BlockSpec((8, LANES), lambda b, g: (0, 0)),
                  pl.BlockSpec(memory_space=pltpu.SMEM),
                  wblk(0), wblk(n_groups), wblk(2 * n_groups),
                  pl.BlockSpec((1, B_HEAD_DIM), lambda b, g: (0, 0))],
        out_specs=pl.BlockSpec((seq, wg), lambda b, g: (b, g)),
        out_shape=jax.ShapeDtypeStruct((t, B_WIDTH), BF16),
        scratch_shapes=[sc((hps, seq, B_HEAD_DIM), F32), sc((hps, seq, B_HEAD_DIM), F32),
                        sc((hps, seq, B_HEAD_DIM), F32),
                        sc((hps, seq, B_HEAD_DIM), BF16),
                        sc((hps, seq, c), BF16), sc((hps, n_chunks, B_HEAD_DIM, c), BF16),
                        sc((hps, n_chunks, 2 * c), F32), sc((hps, n_chunks, B_HEAD_DIM), F32),
                        sc((hps, B_HEAD_DIM, B_HEAD_DIM), F32)],
        compiler_params=_params(2),
        name="deltanet",
    )(qkv_b, qkv_b, qkv_b, z_b, scal, st3, par, part,
      conv_w, conv_w, conv_w, gnorm_w.reshape(1, B_HEAD_DIM))


def _layer_norm(h, g, b):
    mu = jnp.mean(h, axis=1, keepdims=True)
    hc = h - mu
    var = jnp.mean(hc * hc, axis=1, keepdims=True)
    return hc * lax.rsqrt(var + LN_EPS) * g + b


def _mix_route_kernel(ya_ref, yb_ref, x_ref, p_ref, woa_ref, wob_ref, g1_ref, b1_ref,
                      wr_ref, br_ref, wpg_ref, bpg_ref, wpp_ref,
                      x1_ref, res_ref, route_ref, gate_ref, cnt_ref, run_sc, *, tm):
    i = pl.program_id(0)

    @pl.when(i == 0)
    def _():
        run_sc[...] = jnp.zeros(run_sc.shape, F32)

    mix = (jnp.dot(ya_ref[...].astype(BF16), woa_ref[...], preferred_element_type=F32)
           + jnp.dot(yb_ref[...].astype(BF16), wob_ref[...], preferred_element_type=F32))
    x1 = _layer_norm(DEEPNORM_ALPHA * x_ref[...] + mix, g1_ref[...], b1_ref[...])
    x1_ref[...] = x1
    x1b = x1.astype(BF16)

    lane = lax.broadcasted_iota(jnp.int32, (tm, LANES), 1)
    logits = _dot_hi(x1, wr_ref[...]) + br_ref[...]
    cur = jnp.where(lane < N_EXPERTS, logits, -jnp.inf)
    vals, hots = [], []
    for _ in range(TOP_K):
        m = jnp.max(cur, axis=1, keepdims=True)
        idx = jnp.min(jnp.where(cur == m, lane, LANES), axis=1, keepdims=True)
        hot = lane == idx
        cur = jnp.where(hot, -jnp.inf, cur)
        vals.append(m)
        hots.append((hot, idx))
    exps = [jnp.exp(v - vals[0]) for v in vals]
    den = exps[0] + exps[1] + exps[2] + exps[3]

    member = jnp.zeros((tm, LANES), F32)
    for hot, _ in hots:
        member = member + jnp.where(hot, 1.0, 0.0)
    rr = lax.broadcasted_iota(jnp.int32, (tm, tm), 0)
    cc = lax.broadcasted_iota(jnp.int32, (tm, tm), 1)
    before = jnp.where(cc < rr, 1.0, 0.0).astype(BF16)
    prior = jnp.dot(before, member.astype(BF16), preferred_element_type=F32) + run_sc[0:1, :]
    route = jnp.zeros((tm, LANES), jnp.int32)
    gate = jnp.zeros((tm, LANES), F32)
    for kk, (hot, idx) in enumerate(hots):
        rank = jnp.sum(jnp.where(hot, prior, 0.0), axis=1, keepdims=True).astype(jnp.int32)
        route = jnp.where(lane == kk, idx, route)
        route = jnp.where(lane == TOP_K + kk, rank, route)
        gate = jnp.where(lane == kk, exps[kk] / den, gate)
    route_ref[...] = route
    gate_ref[...] = gate
    run_sc[...] = run_sc[...] + jnp.sum(member, axis=0, keepdims=True)
    cnt_ref[...] = run_sc[...]

    pgate = _sigmoid(jnp.dot(x1b, wpg_ref[...], preferred_element_type=F32) + bpg_ref[...])
    proj = jnp.dot(p_ref[...].astype(BF16), wpp_ref[...], preferred_element_type=F32)
    res_ref[...] = DEEPNORM_ALPHA * x1 + pgate * proj


def mix_route(ya, yb, x2d, p2d, w_out, ln1_g, ln1_b, w_router, b_router, w_ple_gate, b_ple_gate,
              w_ple_proj, tm=256):
    t, d = x2d.shape
    woa = w_out[:A_WIDTH].astype(BF16)
    wob = w_out[A_WIDTH:].astype(BF16)
    wr = jnp.pad(w_router, ((0, 0), (0, LANES - N_EXPERTS)))
    br = jnp.pad(b_router, (0, LANES - N_EXPERTS)).reshape(1, LANES)
    row = lambda w: pl.BlockSpec((tm, w), lambda i: (i, 0))
    full = lambda a: pl.BlockSpec(a.shape, lambda i: (0,) * a.ndim)
    ops = [woa, wob, ln1_g.reshape(1, d), ln1_b.reshape(1, d), wr, br,
           w_ple_gate.astype(BF16), b_ple_gate.reshape(1, d), w_ple_proj.astype(BF16)]
    return pl.pallas_call(
        functools.partial(_mix_route_kernel, tm=tm),
        grid=(t // tm,),
        in_specs=[row(A_WIDTH), row(B_WIDTH), row(d), row(PLE_DIM)] + [full(a) for a in ops],
        out_specs=[row(d), row(d), row(LANES), row(LANES),
                   pl.BlockSpec((8, LANES), lambda i: (0, 0))],
        out_shape=[jax.ShapeDtypeStruct((t, d), F32), jax.ShapeDtypeStruct((t, d), F32),
                   jax.ShapeDtypeStruct((t, LANES), jnp.int32),
                   jax.ShapeDtypeStruct((t, LANES), F32),
                   jax.ShapeDtypeStruct((8, LANES), F32)],
        scratch_shapes=[pltpu.VMEM((8, LANES), F32)],
        compiler_params=_params(),
        name="mix_route",
    )(ya, yb, x2d, p2d, *ops)


MOE_TILE = 512
DISPATCH_TM = 256
COMBINE_TM = 128


def _dispatch_kernel(pstart_ref, pend_ref, route_ref, x_ref, xb_ref, zero_sc, sem, zsem, *, tm):
    i = pl.program_id(0)

    def zero_copy(e):
        off = pl.multiple_of(pend_ref[e] - MOE_TILE, MOE_TILE)
        return pltpu.make_async_copy(zero_sc, xb_ref.at[pl.ds(off, MOE_TILE), :], zsem)

    @pl.when(i == 0)
    def _():
        zero_sc[...] = jnp.zeros(zero_sc.shape, zero_sc.dtype)
        for e in range(N_EXPERTS):
            @pl.when(pend_ref[e] > pstart_ref[e])
            def _():
                zero_copy(e).start()
        for e in range(N_EXPERTS):
            @pl.when(pend_ref[e] > pstart_ref[e])
            def _():
                zero_copy(e).wait()

        n_tiles = xb_ref.shape[0] // MOE_TILE
        first_unused = pend_ref[N_EXPERTS - 1] // MOE_TILE

        def tail_copy(j):
            off = pl.multiple_of(j * MOE_TILE, MOE_TILE)
            return pltpu.make_async_copy(zero_sc, xb_ref.at[pl.ds(off, MOE_TILE), :], zsem)

        def tail_start(j, c):
            tail_copy(j).start()
            return c

        def tail_wait(j, c):
            tail_copy(j).wait()
            return c

        lax.fori_loop(first_unused, n_tiles, tail_start, 0)
        lax.fori_loop(first_unused, n_tiles, tail_wait, 0)

    def row_copy(r, dest):
        return pltpu.make_async_copy(x_ref.at[pl.ds(r, 1), :], xb_ref.at[pl.ds(dest, 1), :], sem)

    def start(r, c):
        for kk in range(TOP_K):
            e = route_ref[r * 2 * TOP_K + kk]
            rank = route_ref[r * 2 * TOP_K + TOP_K + kk]
            row_copy(r, pstart_ref[e] + rank).start()
        return c

    lax.fori_loop(0, tm, start, 0)

    def wait(r, c):
        for kk in range(TOP_K):
            row_copy(0, 0).wait()
        return c

    lax.fori_loop(0, tm, wait, 0)


def dispatch(x1, route_flat, pad_start, pad_end, n_rows, tm=DISPATCH_TM):
    t, d = x1.shape
    return pl.pallas_call(
        functools.partial(_dispatch_kernel, tm=tm),
        grid_spec=pltpu.PrefetchScalarGridSpec(
            num_scalar_prefetch=2,
            grid=(t // tm,),
            in_specs=[pl.BlockSpec((tm * 2 * TOP_K,), lambda i, ps, pe: (i,), memory_space=pltpu.SMEM),
                      pl.BlockSpec((tm, d), lambda i, ps, pe: (i, 0))],
            out_specs=pl.BlockSpec(memory_space=pl.ANY),
            scratch_shapes=[pltpu.VMEM((MOE_TILE, d), F32),
                            pltpu.SemaphoreType.DMA(()), pltpu.SemaphoreType.DMA(())]),
        out_shape=jax.ShapeDtypeStruct((n_rows, d), F32),
        compiler_params=_params(),
        name="dispatch",
    )(pad_start, pad_end, route_flat, x1)


def _expert_kernel(be_ref, nu_ref, x_ref, wgu_ref, bgu_ref, wd_ref, bd_ref, y_ref):
    i = pl.program_id(0)

    @pl.when(i < nu_ref[0])
    def _():
        xb = x_ref[...].astype(BF16)
        acc = jnp.zeros(y_ref.shape, F32) + bd_ref[0]
        nchunk = 256
        for c0 in range(0, D_EXPERT, nchunk):
            g = jnp.dot(xb, wgu_ref[0, :, c0:c0 + nchunk], preferred_element_type=F32)
            g = g + bgu_ref[0, :, c0:c0 + nchunk]
            u = jnp.dot(xb, wgu_ref[0, :, D_EXPERT + c0:D_EXPERT + c0 + nchunk],
                        preferred_element_type=F32)
            u = u + bgu_ref[0, :, D_EXPERT + c0:D_EXPERT + c0 + nchunk]
            g = jnp.minimum(g, SWIGLU_LIMIT)
            u = jnp.clip(u, -SWIGLU_LIMIT, SWIGLU_LIMIT)
            act = (u + 1.0) * (g * _sigmoid(SWIGLU_ALPHA * g))
            acc = acc + jnp.dot(act.astype(BF16), wd_ref[0, c0:c0 + nchunk, :],
                                preferred_element_type=F32)
        y_ref[...] = acc

    @pl.when(i >= nu_ref[0])
    def _():
        y_ref[...] = jnp.zeros(y_ref.shape, F32)


def experts(xb, block_e, n_used, w_gate_up, b_gate_up, w_down, b_down):
    n_rows, d = xb.shape
    n_blocks = n_rows // MOE_TILE
    clamp = lambda i, be, nu: jnp.minimum(i, nu[0] - 1)
    return pl.pallas_call(
        _expert_kernel,
        grid_spec=pltpu.PrefetchScalarGridSpec(
            num_scalar_prefetch=2,
            grid=(n_blocks,),
            in_specs=[pl.BlockSpec((MOE_TILE, d), lambda i, be, nu: (clamp(i, be, nu), 0)),
                      pl.BlockSpec((1, d, 2 * D_EXPERT), lambda i, be, nu: (be[i], 0, 0)),
                      pl.BlockSpec((1, 1, 2 * D_EXPERT), lambda i, be, nu: (be[i], 0, 0)),
                      pl.BlockSpec((1, D_EXPERT, d), lambda i, be, nu: (be[i], 0, 0)),
                      pl.BlockSpec((1, 1, d), lambda i, be, nu: (be[i], 0, 0))],
            out_specs=pl.BlockSpec((MOE_TILE, d), lambda i, be, nu: (i, 0))),
        out_shape=jax.ShapeDtypeStruct((n_rows, d), F32),
        compiler_params=_params(),
        name="experts",
    )(block_e, n_used, xb, w_gate_up.astype(BF16), b_gate_up.reshape(N_EXPERTS, 1, -1),
      w_down.astype(BF16), b_down.reshape(N_EXPERTS, 1, -1))


def _combine_kernel(pstart_ref, route_ref, gate_ref, res_ref, g2_ref, b2_ref, yb_ref, o_ref,
                    buf_sc, sem, *, tm):
    def row_copy(kk, r, src):
        return pltpu.make_async_copy(yb_ref.at[pl.ds(src, 1), :], buf_sc.at[kk, pl.ds(r, 1), :], sem)

    def start(r, c):
        for kk in range(TOP_K):
            e = route_ref[r * 2 * TOP_K + kk]
            rank = route_ref[r * 2 * TOP_K + TOP_K + kk]
            row_copy(kk, r, pstart_ref[e] + rank).start()
        return c

    lax.fori_loop(0, tm, start, 0)

    def wait(r, c):
        for kk in range(TOP_K):
            row_copy(0, 0, 0).wait()
        return c

    lax.fori_loop(0, tm, wait, 0)

    h = res_ref[...]
    gate = gate_ref[...]
    for kk in range(TOP_K):
        h = h + gate[:, kk:kk + 1] * buf_sc[kk]
    o_ref[...] = _layer_norm(h, g2_ref[...], b2_ref[...])


def combine(yb, route_flat, gates, res, pad_start, ln2_g, ln2_b, tm=COMBINE_TM):
    t, d = res.shape
    return pl.pallas_call(
        functools.partial(_combine_kernel, tm=tm),
        grid_spec=pltpu.PrefetchScalarGridSpec(
            num_scalar_prefetch=1,
            grid=(t // tm,),
            in_specs=[pl.BlockSpec((tm * 2 * TOP_K,), lambda i, ps: (i,), memory_space=pltpu.SMEM),
                      pl.BlockSpec((tm, LANES), lambda i, ps: (i, 0)),
                      pl.BlockSpec((tm, d), lambda i, ps: (i, 0)),
                      pl.BlockSpec((1, d), lambda i, ps: (0, 0)),
                      pl.BlockSpec((1, d), lambda i, ps: (0, 0)),
                      pl.BlockSpec(memory_space=pl.ANY)],
            out_specs=pl.BlockSpec((tm, d), lambda i, ps: (i, 0)),
            scratch_shapes=[pltpu.VMEM((TOP_K, tm, d), F32), pltpu.SemaphoreType.DMA(())]),
        out_shape=jax.ShapeDtypeStruct((t, d), F32),
        compiler_params=_params(),
        name="combine",
    )(pad_start, route_flat, gates, res, ln2_g.reshape(1, d), ln2_b.reshape(1, d), yb)


def kernel(x, p, w_in, conv_w, a_log, dt_bias, gnorm_w, w_out, ln1_g, ln1_b, w_router, b_router,
           w_gate_up, b_gate_up, w_down, b_down, w_ple_gate, b_ple_gate, w_ple_proj, ln2_g, ln2_b):
    bsz, seq, d = x.shape
    t = bsz * seq
    x2d = x.reshape(t, d)
    qkv_a, qkv_b, z_b, scal, scal_t = in_proj(x2d, w_in[0])
    ya = dilated_attn(qkv_a, bsz, seq)
    yb = deltanet(qkv_b, z_b, scal, scal_t, conv_w[0], a_log[0], dt_bias[0], gnorm_w[0], bsz, seq)
    x1, res, route, gates, counts = mix_route(
        ya, yb, x2d, p[0].reshape(t, PLE_DIM), w_out[0], ln1_g[0], ln1_b[0], w_router[0],
        b_router[0], w_ple_gate[0], b_ple_gate[0], w_ple_proj[0])

    cnt = counts[0, :N_EXPERTS].astype(jnp.int32)
    padded = (cnt + MOE_TILE - 1) // MOE_TILE * MOE_TILE
    pad_end = jnp.cumsum(padded).astype(jnp.int32)
    pad_start = pad_end - padded
    n_blocks = (t * TOP_K) // MOE_TILE + N_EXPERTS
    n_used = (pad_end[-1:] // MOE_TILE).astype(jnp.int32)
    tile_start = jnp.arange(n_blocks, dtype=jnp.int32) * MOE_TILE
    block_e = jnp.minimum(jnp.sum(pad_end[None, :] <= tile_start[:, None], axis=1),
                          N_EXPERTS - 1).astype(jnp.int32)
    route_flat = route[:, :2 * TOP_K].reshape(-1)

    xb = dispatch(x1, route_flat, pad_start, pad_end, n_blocks * MOE_TILE)
    yexp = experts(xb, block_e, n_used, w_gate_up[0], b_gate_up[0], w_down[0], b_down[0])
    out = combine(yexp, route_flat, gates, res, pad_start, ln2_g[0], ln2_b[0])
    return out.reshape(bsz, seq, d)
```

```python
import functools
import math

import jax
import jax.numpy as jnp
from jax import lax
from jax.experimental import pallas as pl
from jax.experimental.pallas import tpu as pltpu

LANES = 128
VMEM_LIMIT_BYTES = 56 * 1024 * 1024

D_MODEL = 1024
PLE_DIM = 256
A_HEADS = 8
A_HEAD_DIM = 64
A_WIDTH = A_HEADS * A_HEAD_DIM
DILATIONS = (1, 4, 16)
ATT_BLOCK = 128
B_HEADS = 4
B_HEAD_DIM = 128
B_WIDTH = B_HEADS * B_HEAD_DIM
CONV_WIDTH = 4
DELTA_CHUNK = 64
N_EXPERTS = 32
TOP_K = 4
D_EXPERT = D_MODEL
SWIGLU_LIMIT = 7.0
SWIGLU_ALPHA = 1.702
MOE_BLOCK = 128
LN_EPS = 1e-5
RMS_EPS = 1e-6
L2_EPS = 1e-6
DEPTH = 1
DEEPNORM_ALPHA = (2.0 * DEPTH) ** 0.25

F32 = jnp.float32
BF16 = jnp.bfloat16


def _params(n_parallel_axes=1):
    return pltpu.CompilerParams(
        dimension_semantics=("arbitrary",) * n_parallel_axes,
        vmem_limit_bytes=VMEM_LIMIT_BYTES)


def _in_proj_kernel(x_ref, wa_ref, wb_ref, wz_ref, ws_ref, wst_ref, a_ref, b_ref, z_ref, s_ref, st_ref):
    xb = x_ref[...].astype(BF16)
    a_ref[...] = jnp.dot(xb, wa_ref[...], preferred_element_type=F32)
    b_ref[...] = jnp.dot(xb, wb_ref[...], preferred_element_type=F32).astype(BF16)
    z_ref[...] = jnp.dot(xb, wz_ref[...], preferred_element_type=F32)
    s_ref[...] = jnp.dot(xb, ws_ref[...], preferred_element_type=F32)
    st_ref[...] = lax.dot_general(wst_ref[...], xb, (((1,), (1,)), ((), ())),
                                  preferred_element_type=F32)


def in_proj(x2d, w_in, tm=256):
    t, d = x2d.shape
    na, nb = 3 * A_WIDTH, 3 * B_WIDTH
    wa = w_in[:, :na].astype(BF16)
    wb = w_in[:, na:na + nb].astype(BF16)
    wz = w_in[:, na + nb:na + nb + B_WIDTH].astype(BF16)
    ws = jnp.pad(w_in[:, na + nb + B_WIDTH:], ((0, 0), (0, LANES - 2 * B_HEADS))).astype(BF16)
    wst = ws[:, :16].T
    full = lambda a: pl.BlockSpec(a.shape, lambda i: (0, 0))
    return pl.pallas_call(
        _in_proj_kernel,
        grid=(t // tm,),
        in_specs=[pl.BlockSpec((tm, d), lambda i: (i, 0)),
                  full(wa), full(wb), full(wz), full(ws), full(wst)],
        out_specs=[pl.BlockSpec((tm, na), lambda i: (i, 0)),
                   pl.BlockSpec((tm, nb), lambda i: (i, 0)),
                   pl.BlockSpec((tm, B_WIDTH), lambda i: (i, 0)),
                   pl.BlockSpec((tm, LANES), lambda i: (i, 0)),
                   pl.BlockSpec((16, tm), lambda i: (0, i))],
        out_shape=[jax.ShapeDtypeStruct((t, na), F32),
                   jax.ShapeDtypeStruct((t, nb), BF16),
                   jax.ShapeDtypeStruct((t, B_WIDTH), F32),
                   jax.ShapeDtypeStruct((t, LANES), F32),
                   jax.ShapeDtypeStruct((16, t), F32)],
        compiler_params=_params(),
        name="in_proj",
    )(x2d, wa, wb, wz, ws, wst)


ATTN_UNROLL_DENSE = 3
ATTN_UNROLL_SINGLE = 4


def _attn_kernel(q_ref, k_ref, v_ref, o_ref, m_sc, l_sc, acc_sc, *, seq):
    blk = ATT_BLOCK
    lane = lax.broadcasted_iota(jnp.int32, (blk, LANES), 1)
    head0 = lane < A_HEAD_DIM
    head0_kv = lax.broadcasted_iota(jnp.int32, (2 * blk, LANES), 1) < A_HEAD_DIM
    row = lax.broadcasted_iota(jnp.int32, (blk, blk), 0)
    col = lax.broadcasted_iota(jnp.int32, (blk, blk), 1)
    prev_ok = col >= row
    cur_ok = col <= row
    neg_inf = jnp.float32(-jnp.inf)
    scale = 1.0 / math.sqrt(A_HEAD_DIM)

    def blocks(starts, stride, has_prev, first_branch, last_branch):
        def rows(ref, s0):
            if stride == 1:
                return ref[pl.ds(s0, blk), :]
            return ref[pl.ds(s0, blk, stride=stride), :]

        def put(ref, s0, val):
            if stride == 1:
                ref[pl.ds(s0, blk), :] = val
            else:
                ref[pl.ds(s0, blk, stride=stride), :] = val

        hmask = (head0, jnp.logical_not(head0))
        items = []
        for start, hp in zip(starts, has_prev):
            q = rows(q_ref, start) * scale
            keys = rows(k_ref, start)
            vals = rows(v_ref, start)
            if hp:
                keys = jnp.concatenate([rows(k_ref, start - blk * stride), keys], axis=0)
                vals = jnp.concatenate([rows(v_ref, start - blk * stride), vals], axis=0)
            it = dict(start=start, hp=hp, keys=keys.astype(BF16), vals=vals,
                      qh=[jnp.where(hm, q, 0.0).astype(BF16) for hm in hmask])
            if not first_branch:
                it["m_old"] = rows(m_sc, start)
                it["l_old"] = rows(l_sc, start)
                it["acc_old"] = rows(acc_sc, start)
            items.append(it)

        for it in items:
            it["s"] = [lax.dot_general(qh, it["keys"], (((1,), (1,)), ((), ())),
                                       preferred_element_type=F32) for qh in it["qh"]]
        for it in items:
            ok = jnp.concatenate([prev_ok, cur_ok], axis=1) if it["hp"] else cur_ok
            it["m"], it["sum"], it["p"] = [], [], []
            for h in range(2):
                s = jnp.where(ok, it["s"][h], neg_inf)
                t = jnp.maximum(s[:, :blk], s[:, blk:]) if it["hp"] else s
                if not first_branch:
                    t = jnp.maximum(t, jnp.where(hmask[h], it["m_old"], neg_inf))
                m_h = jnp.max(t, axis=1, keepdims=True)
                p = jnp.exp(s - m_h)
                psum = p[:, :blk] + p[:, blk:] if it["hp"] else p
                it["m"].append(m_h)
                it["sum"].append(jnp.sum(psum, axis=1, keepdims=True))
                it["p"].append(p.astype(BF16))
        for it in items:
            kv0 = head0_kv if it["hp"] else head0
            kvmask = (kv0, jnp.logical_not(kv0))
            it["pv"] = [jnp.dot(it["p"][h], jnp.where(kvmask[h], it["vals"], 0.0).astype(BF16),
                                preferred_element_type=F32) for h in range(2)]
        for it in items:
            m_new = jnp.where(head0, it["m"][0], it["m"][1])
            l_new = jnp.where(head0, it["sum"][0], it["sum"][1])
            acc_new = it["pv"][0] + it["pv"][1]
            if not first_branch:
                alpha = jnp.exp(it["m_old"] - m_new)
                l_new = l_new + alpha * it["l_old"]
                acc_new = acc_new + alpha * it["acc_old"]
            if last_branch:
                put(o_ref, it["start"], acc_new / l_new)
            else:
                put(m_sc, it["start"], m_new)
                put(l_sc, it["start"], l_new)
                put(acc_sc, it["start"], acc_new)

    n_br = len(DILATIONS)
    for bi, dil in enumerate(DILATIONS):
        first, last = bi == 0, bi == n_br - 1
        nblk = seq // dil // blk
        if dil == 1:
            blocks([0], 1, [False], first, last)
            per = ATTN_UNROLL_DENSE
            assert (nblk - 1) % per == 0

            def body(g, c, first=first, last=last, per=per):
                starts = [pl.multiple_of((1 + g * per + j) * blk, blk) for j in range(per)]
                blocks(starts, 1, [True] * per, first, last)
                return c
            lax.fori_loop(0, (nblk - 1) // per, body, 0)
        elif nblk > 1:
            def body(r, c, dil=dil, nblk=nblk, first=first, last=last):
                blocks([r + n * blk * dil for n in range(nblk)], dil,
                       [n > 0 for n in range(nblk)], first, last)
                return c
            lax.fori_loop(0, dil, body, 0)
        else:
            per = ATTN_UNROLL_SINGLE
            assert dil % per == 0

            def body(g, c, dil=dil, first=first, last=last, per=per):
                blocks([g * per + j for j in range(per)], dil, [False] * per, first, last)
                return c
            lax.fori_loop(0, dil // per, body, 0)


def dilated_attn(qkv_a, bsz, seq):
    t = bsz * seq
    n_pairs = A_WIDTH // LANES
    blk = lambda off: pl.BlockSpec((seq, LANES), lambda b, p: (b, off + p))
    return pl.pallas_call(
        functools.partial(_attn_kernel, seq=seq),
        grid=(bsz, n_pairs),
        in_specs=[blk(0), blk(n_pairs), blk(2 * n_pairs)],
        out_specs=pl.BlockSpec((seq, LANES), lambda b, p: (b, p)),
        out_shape=jax.ShapeDtypeStruct((t, A_WIDTH), F32),
        scratch_shapes=[pltpu.VMEM((seq, LANES), F32)] * 3,
        compiler_params=_params(2),
        name="dilated_attn",
    )(qkv_a, qkv_a, qkv_a)


DELTA_HEADS_PER_STEP = 4
DELTA_P1_CHUNKS = 4
CONV_TILE = 256
CONV_HISTORY_ROWS = 16
_HI = lax.Precision.HIGHEST


def _dot_hi(a, b):
    return jnp.dot(a, b, precision=_HI, preferred_element_type=F32)


def _dot16(a, b):
    return jnp.dot(a, b, preferred_element_type=F32)


def _split2(x):
    hi = x.astype(BF16)
    return hi, (x - hi.astype(F32)).astype(BF16)


def _dot3(a, b):
    return _dot16(a[0], b[0]) + (_dot16(a[1], b[0]) + _dot16(a[0], b[1]))


def _split3(x):
    hi = x.astype(BF16)
    r = x - hi.astype(F32)
    mid = r.astype(BF16)
    return hi, mid, (r - mid.astype(F32)).astype(BF16)


def _dot_sel(m01, x):
    hi, mid, lo = _split3(x)
    return _dot16(m01, hi) + (_dot16(m01, mid) + _dot16(m01, lo))


def _dot_sel_r(x, m01):
    hi, mid, lo = _split3(x)
    return _dot16(hi, m01) + (_dot16(mid, m01) + _dot16(lo, m01))


def _stack_parts(x, lo_half, want_l=True, want_r=True):
    xh = x.astype(BF16)
    xh_f = xh.astype(F32)
    xl_f = x - xh_f
    left = right = None
    if want_l:
        mix = jnp.where(lo_half, xh_f, xl_f).astype(BF16)
        left = jnp.concatenate([mix, mix], axis=1)
    if want_r:
        xl = xl_f.astype(BF16)
        right = jnp.concatenate([xh, xh, xl, xl], axis=0)
    return left, right


def _softplus(x):
    return jnp.maximum(x, 0.0) + jnp.log(1.0 + jnp.exp(-jnp.abs(x)))


def _sigmoid(x):
    return 1.0 / (1.0 + jnp.exp(-x))


def _delta_kernel(q_ref, k_ref, v_ref, z_ref, sc_ref, st_ref, par_ref, part_ref,
                  wq_ref, wk_ref, wv_ref, nw_ref, o_ref,
                  qn_sc, kn_sc, vn_sc, w_sc, attn_sc, kdt_sc, gcr_sc, egl_sc,
                  state_sc, *, seq, hps):
    c = DELTA_CHUNK
    dk = B_HEAD_DIM
    n_chunks = seq // c
    tile = DELTA_P1_CHUNKS * c
    group = pl.program_id(1)
    lane = lax.broadcasted_iota(jnp.int32, (tile, LANES), 1)
    r2 = lax.broadcasted_iota(jnp.int32, (tile, tile), 0)
    c2 = lax.broadcasted_iota(jnp.int32, (tile, tile), 1)
    same_chunk = (r2 // c) == (c2 // c)
    cum_mat = jnp.where(same_chunk & (c2 <= r2), 1.0, 0.0).astype(BF16)
    tot_mat = jnp.where(same_chunk, 1.0, 0.0).astype(BF16)
    ri = lax.broadcasted_iota(jnp.int32, (c, 2 * c), 0)
    li = lax.broadcasted_iota(jnp.int32, (c, 2 * c), 1)
    lo_half = li < c
    ci = jnp.where(lo_half, li, li - c)
    causal = ci <= ri
    strict = ci < ri
    eye = jnp.where(ci == ri, 1.0, 0.0).astype(F32)
    upper = jnp.where(ri <= ci, 1.0, 0.0).astype(BF16)

    hist = CONV_HISTORY_ROWS
    for hh in range(hps):
        cols = slice(hh * dk, (hh + 1) * dk)
        for src, w_ref, dst, kind in ((q_ref, wq_ref, qn_sc, "q"), (k_ref, wk_ref, kn_sc, "k"),
                                      (v_ref, wv_ref, vn_sc, "v")):
            w = w_ref[:, cols]

            def conv_tile(xe, c0, w=w, dst=dst, kind=kind, hh=hh):
                off = hist - (CONV_WIDTH - 1)
                y = w[0:1, :] * xe[off:off + CONV_TILE]
                for j in range(1, CONV_WIDTH):
                    y = y + w[j:j + 1, :] * xe[off + j:off + j + CONV_TILE]
                y = y * _sigmoid(y)
                if kind != "v":
                    y = y * lax.rsqrt(jnp.sum(y * y, axis=1, keepdims=True) + L2_EPS)
                if kind == "q":
                    y = y * (B_HEAD_DIM ** -0.5)
                dst[hh, pl.ds(c0, CONV_TILE), :] = y

            conv_tile(jnp.concatenate([jnp.zeros((hist, dk), F32),
                                       src[0:CONV_TILE, cols].astype(F32)], axis=0), 0)

            def conv_body(ti, carry, src=src, cols=cols, conv_tile=conv_tile):
                c0 = pl.multiple_of(ti * CONV_TILE, CONV_TILE)
                conv_tile(src[pl.ds(c0 - hist, hist + CONV_TILE), cols].astype(F32), c0)
                return carry

            lax.fori_loop(1, seq // CONV_TILE, conv_body, 0)

    for hh in range(hps):
        h = group * hps + hh
        a_row = st_ref[B_HEADS + h]
        g_row = -jnp.exp(part_ref[0, h]) * _softplus(a_row + part_ref[1, h])
        gcr_sc[hh] = _dot_sel_r(g_row, upper)

    alog_l = par_ref[0:1, :]
    dtb_l = par_ref[1:2, :]

    def phase1(i, carry):
        r0 = pl.multiple_of(i * tile, tile)
        rows = pl.ds(r0, tile)
        s = sc_ref[rows, :]
        beta_all = _sigmoid(s)
        g_all = -jnp.exp(alog_l) * _softplus(s + dtb_l)
        gc_all = _dot_sel(cum_mat, g_all)
        gl_all = _dot_sel(tot_mat, g_all)
        chains = []
        for hh in range(hps):
            h = group * hps + hh
            pick = lambda a, l: jnp.sum(jnp.where(lane == l, a, 0.0), axis=1, keepdims=True)
            beta = pick(beta_all, h)
            gc = pick(gc_all, B_HEADS + h)
            gl = pick(gl_all, B_HEADS + h)
            egc = jnp.exp(gc)
            q = qn_sc[hh, rows, :]
            k = kn_sc[hh, rows, :]
            v = vn_sc[hh, rows, :]
            kbeta = k * beta
            vbeta = v * beta
            qn_sc[hh, rows, :] = q * egc
            kdec = k * jnp.exp(gl - gc)
            kb16 = k.astype(BF16)
            for cc in range(DELTA_P1_CHUNKS):
                sl = slice(cc * c, (cc + 1) * c)
                n = DELTA_P1_CHUNKS * i + cc
                kdt_sc[hh, n] = kdec[sl].T.astype(BF16)
                egl_sc[hh, pl.ds(n, 1), :] = jnp.broadcast_to(jnp.exp(gl[cc * c:cc * c + 1]), (1, dk))
                diff = gc[sl] - gcr_sc[hh, pl.ds(n, 1), :]
                e = jnp.exp(jnp.where(causal, diff, 0.0))
                k2 = jnp.concatenate([kb16[sl], kb16[sl]], axis=0)
                lhs = jnp.concatenate([kbeta[sl], q[sl]], axis=0).astype(BF16)
                rhs = jnp.concatenate([vbeta[sl], kbeta[sl] * egc[sl]], axis=1)
                chains.append(dict(hh=hh, row=r0 + cc * c, e=e, k2=k2, lhs=lhs, rhs=rhs))

        for ch in chains:
            kq = lax.dot_general(ch["lhs"], ch["k2"], (((1,), (1,)), ((), ())),
                                 preferred_element_type=F32)
            ch["a"] = jnp.where(strict, kq[:c] * ch["e"], 0.0)
            attn_sc[ch["hh"], pl.ds(ch["row"], c), :] = (
                jnp.where(causal, kq[c:] * ch["e"], 0.0)[:, :c].astype(BF16))
        for ch in chains:
            ch["inv"] = eye - ch["a"]
            p_l, p_r = _stack_parts(ch["a"], lo_half)
            ch["p"] = _dot16(p_l, p_r)
        for _ in range(4):
            for ch in chains:
                p_l, p_r = _stack_parts(ch["p"], lo_half)
                inv_l, _ = _stack_parts(ch["inv"], lo_half, want_r=False)
                res = _dot16(jnp.concatenate([p_l, inv_l], axis=0), p_r)
                ch["p"] = res[:c]
                ch["inv"] = ch["inv"] + res[c:]
        for ch in chains:
            inv_l, _ = _stack_parts(ch["inv"], lo_half, want_r=False)
            _, p_r = _stack_parts(ch["p"], lo_half, want_l=False)
            ch["inv"] = ch["inv"] + _dot16(inv_l, p_r)
        for ch in chains:
            inv_l, _ = _stack_parts(ch["inv"], lo_half, want_r=False)
            _, rhs_r = _stack_parts(ch["rhs"], None, want_l=False)
            sol = _dot16(inv_l, rhs_r)
            vn_sc[ch["hh"], pl.ds(ch["row"], c), :] = sol[:, :dk]
            w_sc[ch["hh"], pl.ds(ch["row"], c), :] = sol[:, dk:].astype(BF16)
        return carry

    lax.fori_loop(0, n_chunks // DELTA_P1_CHUNKS, phase1, 0)

    state_sc[...] = jnp.zeros(state_sc.shape, F32)
    nw = nw_ref[...]

    def phase2(n, carry):
        r0 = pl.multiple_of(n * c, c)
        rows = pl.ds(r0, c)
        heads = range(hps)
        st = [state_sc[hh] for hh in heads]
        st16 = [s_.astype(BF16) for s_ in st]
        lhs = [jnp.concatenate([w_sc[hh, rows, :], qn_sc[hh, rows, :].astype(BF16)], axis=0)
               for hh in heads]
        ws = [_dot16(lhs[hh], st16[hh]) for hh in heads]
        vn16 = [(vn_sc[hh, rows, :] - ws[hh][:c]).astype(BF16) for hh in heads]
        av = [_dot16(attn_sc[hh, rows, :], vn16[hh]) for hh in heads]
        kv = [_dot16(kdt_sc[hh, n], vn16[hh]) for hh in heads]
        for hh in heads:
            state_sc[hh] = st[hh] * egl_sc[hh, pl.ds(n, 1), :] + kv[hh]
            out = ws[hh][c:] + av[hh]
            o = out * lax.rsqrt(jnp.mean(out * out, axis=1, keepdims=True) + RMS_EPS) * nw
            z = z_ref[rows, hh * dk:(hh + 1) * dk]
            o_ref[rows, hh * dk:(hh + 1) * dk] = (o * (z * _sigmoid(z))).astype(o_ref.dtype)
        return carry

    lax.fori_loop(0, n_chunks, phase2, 0)


def deltanet(qkv_b, z_b, scal, scal_t, conv_w, a_log, dt_bias, gnorm_w, bsz, seq):
    t = bsz * seq
    hps = DELTA_HEADS_PER_STEP
    n_groups = B_HEADS // hps
    wg = hps * B_HEAD_DIM
    c = DELTA_CHUNK
    n_chunks = seq // c
    st3 = scal_t.reshape(16, t // c, c)
    par = jnp.zeros((8, LANES), F32)
    par = par.at[0, B_HEADS:2 * B_HEADS].set(a_log).at[1, B_HEADS:2 * B_HEADS].set(dt_bias)
    part = jnp.stack([a_log, dt_bias]).astype(F32)
    blk = lambda off: pl.BlockSpec((seq, wg), lambda b, g: (b, off + g))
    wblk = lambda off: pl.BlockSpec((CONV_WIDTH, wg), lambda b, g: (0, off + g))
    sc = lambda shape, dt: pltpu.VMEM(shape, dt)
    return pl.pallas_call(
        functools.partial(_delta_kernel, seq=seq, hps=hps),
        grid=(bsz, n_groups),
        in_specs=[blk(0), blk(n_groups), blk(2 * n_groups), blk(0),
                  pl.BlockSpec((seq, LANES), lambda b, g: (b, 0)),
                  pl.BlockSpec((16, n_chunks, c), lambda b, g: (0, b, 0)),
                  pl.BlockSpec((8, LANES), lambda b, g: (0, 0)),
                  pl.BlockSpec(memory_space=pltpu.SMEM),
                  wblk(0), wblk(n_groups), wblk(2 * n_groups),
                  pl.BlockSpec((1, B_HEAD_DIM), lambda b, g: (0, 0))],
        out_specs=pl.BlockSpec((seq, wg), lambda b, g: (b, g)),
        out_shape=jax.ShapeDtypeStruct((t, B_WIDTH), BF16),
        scratch_shapes=[sc((hps, seq, B_HEAD_DIM), F32), sc((hps, seq, B_HEAD_DIM), F32),
                        sc((hps, seq, B_HEAD_DIM), F32),
                        sc((hps, seq, B_HEAD_DIM), BF16),
                        sc((hps, seq, c), BF16), sc((hps, n_chunks, B_HEAD_DIM, c), BF16),
                        sc((hps, n_chunks, 2 * c), F32), sc((hps, n_chunks, B_HEAD_DIM), F32),
                        sc((hps, B_HEAD_DIM, B_HEAD_DIM), F32)],
        compiler_params=_params(2),
        name="deltanet",
    )(qkv_b, qkv_b, qkv_b, z_b, scal, st3, par, part,
      conv_w, conv_w, conv_w, gnorm_w.reshape(1, B_HEAD_DIM))


SUBLANES = 8
TOKEN_TILE_ROWS = D_MODEL // LANES
assert TOKEN_TILE_ROWS == SUBLANES


def _store_token_tiles(ref, x, n_tok):
    for j in range(TOKEN_TILE_ROWS):
        ref[pl.ds(j, n_tok, stride=TOKEN_TILE_ROWS), :] = x[:, j * LANES:(j + 1) * LANES]


def _load_token_tiles(ref, n_tok, lead=None):
    idx = lambda j: (pl.ds(j, n_tok, stride=TOKEN_TILE_ROWS), slice(None))
    if lead is None:
        return [ref[idx(j)] for j in range(TOKEN_TILE_ROWS)]
    return [ref[(lead,) + idx(j)] for j in range(TOKEN_TILE_ROWS)]


def _layer_norm(h, g, b):
    mu = jnp.mean(h, axis=1, keepdims=True)
    hc = h - mu
    var = jnp.mean(hc * hc, axis=1, keepdims=True)
    return hc * lax.rsqrt(var + LN_EPS) * g + b


def _mix_route_kernel(ya_ref, yb_ref, x_ref, p_ref, woa_ref, wob_ref, g1_ref, b1_ref,
                      wr_ref, br_ref, wpg_ref, bpg_ref, wpp_ref,
                      x1_ref, res_ref, route_ref, gate_ref, cnt_ref, run_sc, wr2_sc, *, tm):
    i = pl.program_id(0)

    @pl.when(i == 0)
    def _():
        run_sc[...] = jnp.zeros(run_sc.shape, F32)
        w_hi, w_lo = _split2(wr_ref[...])
        wr2_sc[:, :LANES] = w_hi
        wr2_sc[:, LANES:] = w_lo

    mix = (jnp.dot(ya_ref[...].astype(BF16), woa_ref[...], preferred_element_type=F32)
           + jnp.dot(yb_ref[...].astype(BF16), wob_ref[...], preferred_element_type=F32))
    x1 = _layer_norm(DEEPNORM_ALPHA * x_ref[...] + mix, g1_ref[...], b1_ref[...])
    _store_token_tiles(x1_ref, x1, tm)
    x1b = x1.astype(BF16)

    x1l = (x1 - x1b.astype(F32)).astype(BF16)
    r_hi = jnp.dot(x1b, wr2_sc[...], preferred_element_type=F32)
    r_lo = jnp.dot(x1l, wr2_sc[:, :LANES], preferred_element_type=F32)
    logits = r_hi[:, :LANES] + (r_hi[:, LANES:] + r_lo) + br_ref[...]

    pgate = _sigmoid(jnp.dot(x1b, wpg_ref[...], preferred_element_type=F32) + bpg_ref[...])
    proj = jnp.dot(p_ref[...].astype(BF16), wpp_ref[...], preferred_element_type=F32)
    res_ref[...] = DEEPNORM_ALPHA * x1 + pgate * proj

    lane = lax.broadcasted_iota(jnp.int32, (tm, LANES), 1)
    cur = jnp.where(lane < N_EXPERTS, logits, -jnp.inf)
    vals, hots = [], []
    for _ in range(TOP_K):
        m = jnp.max(cur, axis=1, keepdims=True)
        idx = jnp.min(jnp.where(cur == m, lane, LANES), axis=1, keepdims=True)
        hot = lane == idx
        cur = jnp.where(hot, -jnp.inf, cur)
        vals.append(m)
        hots.append((hot, idx))
    exps = [jnp.exp(v - vals[0]) for v in vals]
    den = exps[0] + exps[1] + exps[2] + exps[3]

    member = jnp.zeros((tm, LANES), F32)
    for hot, _ in hots:
        member = member + jnp.where(hot, 1.0, 0.0)
    rr = lax.broadcasted_iota(jnp.int32, (tm, tm), 0)
    cc = lax.broadcasted_iota(jnp.int32, (tm, tm), 1)
    before = jnp.where(cc < rr, 1.0, 0.0).astype(BF16)
    prior = jnp.dot(before, member.astype(BF16), preferred_element_type=F32) + run_sc[0:1, :]
    route = jnp.zeros((tm, LANES), jnp.int32)
    gate = jnp.zeros((tm, LANES), F32)
    for kk, (hot, idx) in enumerate(hots):
        rank = jnp.sum(jnp.where(hot, prior, 0.0), axis=1, keepdims=True).astype(jnp.int32)
        route = jnp.where(lane == kk, idx, route)
        route = jnp.where(lane == TOP_K + kk, rank, route)
        gate = jnp.where(lane == kk, exps[kk] / den, gate)
    route_ref[...] = route
    gate_ref[...] = gate
    run_sc[...] = run_sc[...] + jnp.sum(member, axis=0, keepdims=True)
    cnt_ref[...] = run_sc[...]


def mix_route(ya, yb, x2d, p2d, w_out, ln1_g, ln1_b, w_router, b_router, w_ple_gate, b_ple_gate,
              w_ple_proj, tm=256):
    t, d = x2d.shape
    woa = w_out[:A_WIDTH].astype(BF16)
    wob = w_out[A_WIDTH:].astype(BF16)
    wr = jnp.pad(w_router, ((0, 0), (0, LANES - N_EXPERTS)))
    br = jnp.pad(b_router, (0, LANES - N_EXPERTS)).reshape(1, LANES)
    row = lambda w: pl.BlockSpec((tm, w), lambda i: (i, 0))
    full = lambda a: pl.BlockSpec(a.shape, lambda i: (0,) * a.ndim)
    ops = [woa, wob, ln1_g.reshape(1, d), ln1_b.reshape(1, d), wr, br,
           w_ple_gate.astype(BF16), b_ple_gate.reshape(1, d), w_ple_proj.astype(BF16)]
    return pl.pallas_call(
        functools.partial(_mix_route_kernel, tm=tm),
        grid=(t // tm,),
        in_specs=[row(A_WIDTH), row(B_WIDTH), row(d), row(PLE_DIM)] + [full(a) for a in ops],
        out_specs=[pl.BlockSpec((tm * TOKEN_TILE_ROWS, LANES), lambda i: (i, 0)),
                   row(d), row(LANES), row(LANES),
                   pl.BlockSpec((8, LANES), lambda i: (0, 0))],
        out_shape=[jax.ShapeDtypeStruct((t * TOKEN_TILE_ROWS, LANES), F32),
                   jax.ShapeDtypeStruct((t, d), F32),
                   jax.ShapeDtypeStruct((t, LANES), jnp.int32),
                   jax.ShapeDtypeStruct((t, LANES), F32),
                   jax.ShapeDtypeStruct((8, LANES), F32)],
        scratch_shapes=[pltpu.VMEM((8, LANES), F32), pltpu.VMEM((d, 2 * LANES), BF16)],
        compiler_params=_params(),
        name="mix_route",
    )(ya, yb, x2d, p2d, *ops)


MOE_TILE = 512
DISPATCH_TM = 256
COMBINE_TM = 256
DMA_ROWS_PER_ITER = 2


def _dispatch_kernel(pstart_ref, pend_ref, slot_ref, x_ref, xb_ref, zero_sc, sem, zsem, *, tm):
    i = pl.program_id(0)
    tr = TOKEN_TILE_ROWS
    tile_rows = MOE_TILE * tr

    def zero_copy(e):
        off = pl.multiple_of((pend_ref[e] - MOE_TILE) * tr, tile_rows)
        return pltpu.make_async_copy(zero_sc, xb_ref.at[pl.ds(off, tile_rows), :], zsem)

    @pl.when(i == 0)
    def _():
        zero_sc[...] = jnp.zeros(zero_sc.shape, zero_sc.dtype)
        for e in range(N_EXPERTS):
            @pl.when(pend_ref[e] > pstart_ref[e])
            def _():
                zero_copy(e).start()
        for e in range(N_EXPERTS):
            @pl.when(pend_ref[e] > pstart_ref[e])
            def _():
                zero_copy(e).wait()

        n_tiles = xb_ref.shape[0] // tile_rows
        first_unused = pend_ref[N_EXPERTS - 1] // MOE_TILE

        def tail_copy(j):
            off = pl.multiple_of(j * tile_rows, tile_rows)
            return pltpu.make_async_copy(zero_sc, xb_ref.at[pl.ds(off, tile_rows), :], zsem)

        def tail_start(j, c):
            tail_copy(j).start()
            return c

        def tail_wait(j, c):
            tail_copy(j).wait()
            return c

        lax.fori_loop(first_unused, n_tiles, tail_start, 0)
        lax.fori_loop(first_unused, n_tiles, tail_wait, 0)

    def token_copy(tok, slot):
        src = pl.multiple_of(tok * tr, tr)
        dst = pl.multiple_of(slot * tr, tr)
        return pltpu.make_async_copy(x_ref.at[pl.ds(src, tr), :], xb_ref.at[pl.ds(dst, tr), :], sem)

    def start(g, c):
        toks = [g * DMA_ROWS_PER_ITER + u for u in range(DMA_ROWS_PER_ITER)]
        slots = [[slot_ref[r * TOP_K + kk] for kk in range(TOP_K)] for r in toks]
        for r, row_slots in zip(toks, slots):
            for slot in row_slots:
                token_copy(i * tm + r, slot).start()
        return c

    def wait(r, c):
        for kk in range(TOP_K):
            token_copy(0, 0).wait()
        return c

    lax.fori_loop(0, tm // DMA_ROWS_PER_ITER, start, 0)

    @pl.when(i > 0)
    def _():
        lax.fori_loop(0, tm, wait, 0)

    @pl.when(i == pl.num_programs(0) - 1)
    def _():
        lax.fori_loop(0, tm, wait, 0)


def dispatch(x1_tiles, slot_flat, pad_start, pad_end, n_slots, tm=DISPATCH_TM):
    t = x1_tiles.shape[0] // TOKEN_TILE_ROWS
    return pl.pallas_call(
        functools.partial(_dispatch_kernel, tm=tm),
        grid_spec=pltpu.PrefetchScalarGridSpec(
            num_scalar_prefetch=2,
            grid=(t // tm,),
            in_specs=[pl.BlockSpec((tm * TOP_K,), lambda i, ps, pe: (i,), memory_space=pltpu.SMEM),
                      pl.BlockSpec(memory_space=pl.ANY)],
            out_specs=pl.BlockSpec(memory_space=pl.ANY),
            scratch_shapes=[pltpu.VMEM((MOE_TILE * TOKEN_TILE_ROWS, LANES), F32),
                            pltpu.SemaphoreType.DMA(()), pltpu.SemaphoreType.DMA(())]),
        out_shape=jax.ShapeDtypeStruct((n_slots * TOKEN_TILE_ROWS, LANES), F32),
        compiler_params=_params(),
        name="dispatch",
    )(pad_start, pad_end, slot_flat, x1_tiles)


EXPERT_CAST_ROWS = 256


def _expert_kernel(be_ref, nu_ref, x_ref, wgu_ref, bgu_ref, wd_ref, bd_ref, y_ref, wgu_sc, wd_sc):
    i = pl.program_id(0)
    used = i < nu_ref[0]
    new_expert = jnp.logical_or(i == 0, be_ref[i] != be_ref[jnp.maximum(i - 1, 0)])

    @pl.when(jnp.logical_and(used, new_expert))
    def _():
        def cast(j, c):
            r = pl.ds(pl.multiple_of(j * EXPERT_CAST_ROWS, EXPERT_CAST_ROWS), EXPERT_CAST_ROWS)
            wgu_sc[r, :] = wgu_ref[0, r, :].astype(BF16)
            wd_sc[r, :] = wd_ref[0, r, :].astype(BF16)
            return c
        lax.fori_loop(0, D_MODEL // EXPERT_CAST_ROWS, cast, 0)

    @pl.when(used)
    def _():
        xb = jnp.concatenate([c.astype(BF16) for c in _load_token_tiles(x_ref, MOE_TILE)], axis=1)
        acc = jnp.zeros((MOE_TILE, D_MODEL), F32) + bd_ref[0]
        nchunk = 256
        for c0 in range(0, D_EXPERT, nchunk):
            g = jnp.dot(xb, wgu_sc[:, c0:c0 + nchunk], preferred_element_type=F32)
            g = g + bgu_ref[0, :, c0:c0 + nchunk]
            u = jnp.dot(xb, wgu_sc[:, D_EXPERT + c0:D_EXPERT + c0 + nchunk],
                        preferred_element_type=F32)
            u = u + bgu_ref[0, :, D_EXPERT + c0:D_EXPERT + c0 + nchunk]
            g = jnp.minimum(g, SWIGLU_LIMIT)
            u = jnp.clip(u, -SWIGLU_LIMIT, SWIGLU_LIMIT)
            act = (u + 1.0) * (g * _sigmoid(SWIGLU_ALPHA * g))
            acc = acc + jnp.dot(act.astype(BF16), wd_sc[c0:c0 + nchunk, :],
                                preferred_element_type=F32)
        _store_token_tiles(y_ref, acc, MOE_TILE)

    @pl.when(jnp.logical_not(used))
    def _():
        y_ref[...] = jnp.zeros(y_ref.shape, F32)


def experts(xb_tiles, block_e, n_used, w_gate_up, b_gate_up, w_down, b_down):
    tile_rows = MOE_TILE * TOKEN_TILE_ROWS
    n_blocks = xb_tiles.shape[0] // tile_rows
    d = D_MODEL
    clamp = lambda i, be, nu: jnp.minimum(i, nu[0] - 1)
    return pl.pallas_call(
        _expert_kernel,
        grid_spec=pltpu.PrefetchScalarGridSpec(
            num_scalar_prefetch=2,
            grid=(n_blocks,),
            in_specs=[pl.BlockSpec((tile_rows, LANES), lambda i, be, nu: (clamp(i, be, nu), 0)),
                      pl.BlockSpec((1, d, 2 * D_EXPERT), lambda i, be, nu: (be[i], 0, 0)),
                      pl.BlockSpec((1, 1, 2 * D_EXPERT), lambda i, be, nu: (be[i], 0, 0)),
                      pl.BlockSpec((1, D_EXPERT, d), lambda i, be, nu: (be[i], 0, 0)),
                      pl.BlockSpec((1, 1, d), lambda i, be, nu: (be[i], 0, 0))],
            out_specs=pl.BlockSpec((tile_rows, LANES), lambda i, be, nu: (i, 0)),
            scratch_shapes=[pltpu.VMEM((d, 2 * D_EXPERT), BF16), pltpu.VMEM((D_EXPERT, d), BF16)]),
        out_shape=jax.ShapeDtypeStruct(xb_tiles.shape, F32),
        compiler_params=_params(),
        name="experts",
    )(block_e, n_used, xb_tiles, w_gate_up, b_gate_up.reshape(N_EXPERTS, 1, -1),
      w_down, b_down.reshape(N_EXPERTS, 1, -1))


def _combine_kernel(slot_ref, slot_next_ref, gate_ref, res_ref, g2_ref, b2_ref,
                    yb_ref, o_ref, buf_sc, sem, *, tm):
    i = pl.program_id(0)
    tr = TOKEN_TILE_ROWS
    cur = lax.rem(i, 2)

    def token_copy(buf, kk, r, slot):
        src = pl.multiple_of(slot * tr, tr)
        dst = pl.multiple_of(r * tr, tr)
        return pltpu.make_async_copy(yb_ref.at[pl.ds(src, tr), :],
                                     buf_sc.at[buf, kk, pl.ds(dst, tr), :], sem.at[buf])

    def issue(sref, buf):
        def start(g, c):
            toks = [g * DMA_ROWS_PER_ITER + u for u in range(DMA_ROWS_PER_ITER)]
            slots = [[sref[r * TOP_K + kk] for kk in range(TOP_K)] for r in toks]
            for r, row_slots in zip(toks, slots):
                for kk, slot in enumerate(row_slots):
                    token_copy(buf, kk, r, slot).start()
            return c
        lax.fori_loop(0, tm // DMA_ROWS_PER_ITER, start, 0)

    @pl.when(i == 0)
    def _():
        issue(slot_ref, 0)

    @pl.when(i + 1 < pl.num_programs(0))
    def _():
        issue(slot_next_ref, 1 - cur)

    def wait(r, c):
        for kk in range(TOP_K):
            token_copy(cur, 0, 0, 0).wait()
        return c

    lax.fori_loop(0, tm, wait, 0)

    gate = gate_ref[...]
    chunks = []
    for j in range(TOKEN_TILE_ROWS):
        hj = res_ref[:, j * LANES:(j + 1) * LANES]
        for kk in range(TOP_K):
            hj = hj + gate[:, kk:kk + 1] * buf_sc[cur, kk, pl.ds(j, tm, stride=tr), :]
        chunks.append(hj)
    mu = sum(jnp.sum(hj, axis=1, keepdims=True) for hj in chunks) * (1.0 / D_MODEL)
    cent = [hj - mu for hj in chunks]
    var = sum(jnp.sum(cj * cj, axis=1, keepdims=True) for cj in cent) * (1.0 / D_MODEL)
    inv = lax.rsqrt(var + LN_EPS)
    for j, cj in enumerate(cent):
        cols = slice(j * LANES, (j + 1) * LANES)
        o_ref[:, cols] = cj * inv * g2_ref[:, cols] + b2_ref[:, cols]


def combine(y_tiles, slot_flat, gates, res, ln2_g, ln2_b, tm=COMBINE_TM):
    t, d = res.shape
    n_steps = t // tm
    slot_spec = lambda f: pl.BlockSpec((tm * TOP_K,), f, memory_space=pltpu.SMEM)
    return pl.pallas_call(
        functools.partial(_combine_kernel, tm=tm),
        grid=(n_steps,),
        in_specs=[slot_spec(lambda i: (i,)),
                  slot_spec(lambda i: (jnp.minimum(i + 1, n_steps - 1),)),
                  pl.BlockSpec((tm, LANES), lambda i: (i, 0)),
                  pl.BlockSpec((tm, d), lambda i: (i, 0)),
                  pl.BlockSpec((1, d), lambda i: (0, 0)),
                  pl.BlockSpec((1, d), lambda i: (0, 0)),
                  pl.BlockSpec(memory_space=pl.ANY)],
        out_specs=pl.BlockSpec((tm, d), lambda i: (i, 0)),
        scratch_shapes=[pltpu.VMEM((2, TOP_K, tm * TOKEN_TILE_ROWS, LANES), F32),
                        pltpu.SemaphoreType.DMA((2,))],
        out_shape=jax.ShapeDtypeStruct((t, d), F32),
        compiler_params=_params(),
        name="combine",
    )(slot_flat, slot_flat, gates, res, ln2_g.reshape(1, d), ln2_b.reshape(1, d), y_tiles)


def kernel(x, p, w_in, conv_w, a_log, dt_bias, gnorm_w, w_out, ln1_g, ln1_b, w_router, b_router,
           w_gate_up, b_gate_up, w_down, b_down, w_ple_gate, b_ple_gate, w_ple_proj, ln2_g, ln2_b):
    bsz, seq, d = x.shape
    t = bsz * seq
    x2d = x.reshape(t, d)
    qkv_a, qkv_b, z_b, scal, scal_t = in_proj(x2d, w_in[0])
    ya = dilated_attn(qkv_a, bsz, seq)
    yb = deltanet(qkv_b, z_b, scal, scal_t, conv_w[0], a_log[0], dt_bias[0], gnorm_w[0], bsz, seq)
    x1, res, route, gates, counts = mix_route(
        ya, yb, x2d, p[0].reshape(t, PLE_DIM), w_out[0], ln1_g[0], ln1_b[0], w_router[0],
        b_router[0], w_ple_gate[0], b_ple_gate[0], w_ple_proj[0])

    cnt = counts[0, :N_EXPERTS].astype(jnp.int32)
    padded = (cnt + MOE_TILE - 1) // MOE_TILE * MOE_TILE
    pad_end = jnp.cumsum(padded).astype(jnp.int32)
    pad_start = pad_end - padded
    n_blocks = (t * TOP_K) // MOE_TILE + N_EXPERTS
    n_used = (pad_end[-1:] // MOE_TILE).astype(jnp.int32)
    tile_start = jnp.arange(n_blocks, dtype=jnp.int32) * MOE_TILE
    block_e = jnp.minimum(jnp.sum(pad_end[None, :] <= tile_start[:, None], axis=1),
                          N_EXPERTS - 1).astype(jnp.int32)
    experts_iota = jnp.arange(N_EXPERTS, dtype=jnp.int32)
    group_off = jnp.sum(jnp.where(route[:, :TOP_K, None] == experts_iota, pad_start, 0), axis=-1)
    slot_flat = (group_off + route[:, TOP_K:2 * TOP_K]).reshape(-1).astype(jnp.int32)

    xb = dispatch(x1, slot_flat, pad_start, pad_end, n_blocks * MOE_TILE)
    yexp = experts(xb, block_e, n_used, w_gate_up[0], b_gate_up[0], w_down[0], b_down[0])
    out = combine(yexp, slot_flat, gates, res, ln2_g[0], ln2_b[0])
    return out.reshape(bsz, seq, d)
```

```python
import functools
import math

import jax
import jax.numpy as jnp
from jax import lax
from jax.experimental import pallas as pl
from jax.experimental.pallas import tpu as pltpu

LANES = 128
VMEM_LIMIT_BYTES = 56 * 1024 * 1024

D_MODEL = 1024
PLE_DIM = 256
A_HEADS = 8
A_HEAD_DIM = 64
A_WIDTH = A_HEADS * A_HEAD_DIM
DILATIONS = (1, 4, 16)
ATT_BLOCK = 128
B_HEADS = 4
B_HEAD_DIM = 128
B_WIDTH = B_HEADS * B_HEAD_DIM
CONV_WIDTH = 4
DELTA_CHUNK = 64
N_EXPERTS = 32
TOP_K = 4
D_EXPERT = D_MODEL
SWIGLU_LIMIT = 7.0
SWIGLU_ALPHA = 1.702
MOE_BLOCK = 128
LN_EPS = 1e-5
RMS_EPS = 1e-6
L2_EPS = 1e-6
DEPTH = 1
DEEPNORM_ALPHA = (2.0 * DEPTH) ** 0.25

F32 = jnp.float32
BF16 = jnp.bfloat16


def _params(n_parallel_axes=1):
    return pltpu.CompilerParams(
        dimension_semantics=("arbitrary",) * n_parallel_axes,
        vmem_limit_bytes=VMEM_LIMIT_BYTES)


def _in_proj_kernel(x_ref, wa_ref, wb_ref, wz_ref, ws_ref, wst_ref, a_ref, b_ref, z_ref, s_ref, st_ref):
    xb = x_ref[...].astype(BF16)
    a_ref[...] = jnp.dot(xb, wa_ref[...], preferred_element_type=F32)
    b_ref[...] = jnp.dot(xb, wb_ref[...], preferred_element_type=F32).astype(BF16)
    z_ref[...] = jnp.dot(xb, wz_ref[...], preferred_element_type=F32)
    s_ref[...] = jnp.dot(xb, ws_ref[...], preferred_element_type=F32)
    st_ref[...] = lax.dot_general(wst_ref[...], xb, (((1,), (1,)), ((), ())),
                                  preferred_element_type=F32)


def in_proj(x2d, w_in, tm=256):
    t, d = x2d.shape
    na, nb = 3 * A_WIDTH, 3 * B_WIDTH
    wa = w_in[:, :na].astype(BF16)
    wb = w_in[:, na:na + nb].astype(BF16)
    wz = w_in[:, na + nb:na + nb + B_WIDTH].astype(BF16)
    ws = jnp.pad(w_in[:, na + nb + B_WIDTH:], ((0, 0), (0, LANES - 2 * B_HEADS))).astype(BF16)
    wst = ws[:, :16].T
    full = lambda a: pl.BlockSpec(a.shape, lambda i: (0, 0))
    return pl.pallas_call(
        _in_proj_kernel,
        grid=(t // tm,),
        in_specs=[pl.BlockSpec((tm, d), lambda i: (i, 0)),
                  full(wa), full(wb), full(wz), full(ws), full(wst)],
        out_specs=[pl.BlockSpec((tm, na), lambda i: (i, 0)),
                   pl.BlockSpec((tm, nb), lambda i: (i, 0)),
                   pl.BlockSpec((tm, B_WIDTH), lambda i: (i, 0)),
                   pl.BlockSpec((tm, LANES), lambda i: (i, 0)),
                   pl.BlockSpec((16, tm), lambda i: (0, i))],
        out_shape=[jax.ShapeDtypeStruct((t, na), F32),
                   jax.ShapeDtypeStruct((t, nb), BF16),
                   jax.ShapeDtypeStruct((t, B_WIDTH), F32),
                   jax.ShapeDtypeStruct((t, LANES), F32),
                   jax.ShapeDtypeStruct((16, t), F32)],
        compiler_params=_params(),
        name="in_proj",
    )(x2d, wa, wb, wz, ws, wst)


ATTN_UNROLL_DENSE = 3
ATTN_UNROLL_SINGLE = 4


def _attn_kernel(q_ref, k_ref, v_ref, o_ref, m_sc, l_sc, acc_sc, *, seq):
    blk = ATT_BLOCK
    lane = lax.broadcasted_iota(jnp.int32, (blk, LANES), 1)
    head0 = lane < A_HEAD_DIM
    head0_kv = lax.broadcasted_iota(jnp.int32, (2 * blk, LANES), 1) < A_HEAD_DIM
    row = lax.broadcasted_iota(jnp.int32, (blk, blk), 0)
    col = lax.broadcasted_iota(jnp.int32, (blk, blk), 1)
    prev_ok = col >= row
    cur_ok = col <= row
    neg_inf = jnp.float32(-jnp.inf)
    scale = 1.0 / math.sqrt(A_HEAD_DIM)

    def blocks(starts, stride, has_prev, first_branch, last_branch):
        def rows(ref, s0):
            if stride == 1:
                return ref[pl.ds(s0, blk), :]
            return ref[pl.ds(s0, blk, stride=stride), :]

        def put(ref, s0, val):
            if stride == 1:
                ref[pl.ds(s0, blk), :] = val
            else:
                ref[pl.ds(s0, blk, stride=stride), :] = val

        hmask = (head0, jnp.logical_not(head0))
        items = []
        for start, hp in zip(starts, has_prev):
            q = rows(q_ref, start) * scale
            keys = rows(k_ref, start)
            vals = rows(v_ref, start)
            if hp:
                keys = jnp.concatenate([rows(k_ref, start - blk * stride), keys], axis=0)
                vals = jnp.concatenate([rows(v_ref, start - blk * stride), vals], axis=0)
            it = dict(start=start, hp=hp, keys=keys.astype(BF16), vals=vals,
                      qh=[jnp.where(hm, q, 0.0).astype(BF16) for hm in hmask])
            if not first_branch:
                it["m_old"] = rows(m_sc, start)
                it["l_old"] = rows(l_sc, start)
                it["acc_old"] = rows(acc_sc, start)
            items.append(it)

        for it in items:
            it["s"] = [lax.dot_general(qh, it["keys"], (((1,), (1,)), ((), ())),
                                       preferred_element_type=F32) for qh in it["qh"]]
        for it in items:
            ok = jnp.concatenate([prev_ok, cur_ok], axis=1) if it["hp"] else cur_ok
            it["m"], it["sum"], it["p"] = [], [], []
            for h in range(2):
                s = jnp.where(ok, it["s"][h], neg_inf)
                t = jnp.maximum(s[:, :blk], s[:, blk:]) if it["hp"] else s
                if not first_branch:
                    t = jnp.maximum(t, jnp.where(hmask[h], it["m_old"], neg_inf))
                m_h = jnp.max(t, axis=1, keepdims=True)
                p = jnp.exp(s - m_h)
                psum = p[:, :blk] + p[:, blk:] if it["hp"] else p
                it["m"].append(m_h)
                it["sum"].append(jnp.sum(psum, axis=1, keepdims=True))
                it["p"].append(p.astype(BF16))
        for it in items:
            kv0 = head0_kv if it["hp"] else head0
            kvmask = (kv0, jnp.logical_not(kv0))
            it["pv"] = [jnp.dot(it["p"][h], jnp.where(kvmask[h], it["vals"], 0.0).astype(BF16),
                                preferred_element_type=F32) for h in range(2)]
        for it in items:
            m_new = jnp.where(head0, it["m"][0], it["m"][1])
            l_new = jnp.where(head0, it["sum"][0], it["sum"][1])
            acc_new = it["pv"][0] + it["pv"][1]
            if not first_branch:
                alpha = jnp.exp(it["m_old"] - m_new)
                l_new = l_new + alpha * it["l_old"]
                acc_new = acc_new + alpha * it["acc_old"]
            if last_branch:
                put(o_ref, it["start"], acc_new / l_new)
            else:
                put(m_sc, it["start"], m_new)
                put(l_sc, it["start"], l_new)
                put(acc_sc, it["start"], acc_new)

    n_br = len(DILATIONS)
    for bi, dil in enumerate(DILATIONS):
        first, last = bi == 0, bi == n_br - 1
        nblk = seq // dil // blk
        if dil == 1:
            blocks([0], 1, [False], first, last)
            per = ATTN_UNROLL_DENSE
            assert (nblk - 1) % per == 0

            def body(g, c, first=first, last=last, per=per):
                starts = [pl.multiple_of((1 + g * per + j) * blk, blk) for j in range(per)]
                blocks(starts, 1, [True] * per, first, last)
                return c
            lax.fori_loop(0, (nblk - 1) // per, body, 0)
        elif nblk > 1:
            def body(r, c, dil=dil, nblk=nblk, first=first, last=last):
                blocks([r + n * blk * dil for n in range(nblk)], dil,
                       [n > 0 for n in range(nblk)], first, last)
                return c
            lax.fori_loop(0, dil, body, 0)
        else:
            per = ATTN_UNROLL_SINGLE
            assert dil % per == 0

            def body(g, c, dil=dil, first=first, last=last, per=per):
                blocks([g * per + j for j in range(per)], dil, [False] * per, first, last)
                return c
            lax.fori_loop(0, dil // per, body, 0)


def dilated_attn(qkv_a, bsz, seq):
    t = bsz * seq
    n_pairs = A_WIDTH // LANES
    blk = lambda off: pl.BlockSpec((seq, LANES), lambda b, p: (b, off + p))
    return pl.pallas_call(
        functools.partial(_attn_kernel, seq=seq),
        grid=(bsz, n_pairs),
        in_specs=[blk(0), blk(n_pairs), blk(2 * n_pairs)],
        out_specs=pl.BlockSpec((seq, LANES), lambda b, p: (b, p)),
        out_shape=jax.ShapeDtypeStruct((t, A_WIDTH), F32),
        scratch_shapes=[pltpu.VMEM((seq, LANES), F32)] * 3,
        compiler_params=_params(2),
        name="dilated_attn",
    )(qkv_a, qkv_a, qkv_a)


DELTA_HEADS_PER_STEP = 4
DELTA_P1_CHUNKS = 4
CONV_TILE = 256
CONV_HISTORY_ROWS = 16
_HI = lax.Precision.HIGHEST


def _dot_hi(a, b):
    return jnp.dot(a, b, precision=_HI, preferred_element_type=F32)


def _dot16(a, b):
    return jnp.dot(a, b, preferred_element_type=F32)


def _split2(x):
    hi = x.astype(BF16)
    return hi, (x - hi.astype(F32)).astype(BF16)


def _dot3(a, b):
    return _dot16(a[0], b[0]) + (_dot16(a[1], b[0]) + _dot16(a[0], b[1]))


def _split3(x):
    hi = x.astype(BF16)
    r = x - hi.astype(F32)
    mid = r.astype(BF16)
    return hi, mid, (r - mid.astype(F32)).astype(BF16)


def _dot_sel(m01, x):
    hi, mid, lo = _split3(x)
    return _dot16(m01, hi) + (_dot16(m01, mid) + _dot16(m01, lo))


def _dot_sel_r(x, m01):
    hi, mid, lo = _split3(x)
    return _dot16(hi, m01) + (_dot16(mid, m01) + _dot16(lo, m01))


def _stack_parts(x, lo_half, want_l=True, want_r=True):
    xh = x.astype(BF16)
    xh_f = xh.astype(F32)
    xl_f = x - xh_f
    left = right = None
    if want_l:
        mix = jnp.where(lo_half, xh_f, xl_f).astype(BF16)
        left = jnp.concatenate([mix, mix], axis=1)
    if want_r:
        xl = xl_f.astype(BF16)
        right = jnp.concatenate([xh, xh, xl, xl], axis=0)
    return left, right


def _softplus(x):
    return jnp.maximum(x, 0.0) + jnp.log(1.0 + jnp.exp(-jnp.abs(x)))


def _sigmoid(x):
    return 1.0 / (1.0 + jnp.exp(-x))


def _delta_kernel(q_ref, k_ref, v_ref, z_ref, sc_ref, st_ref, par_ref, part_ref,
                  wq_ref, wk_ref, wv_ref, nw_ref, o_ref,
                  qn_sc, kn_sc, vn_sc, w_sc, attn_sc, kdt_sc, gcr_sc, egl_sc,
                  state_sc, *, seq, hps):
    c = DELTA_CHUNK
    dk = B_HEAD_DIM
    n_chunks = seq // c
    tile = DELTA_P1_CHUNKS * c
    group = pl.program_id(1)
    lane = lax.broadcasted_iota(jnp.int32, (tile, LANES), 1)
    r2 = lax.broadcasted_iota(jnp.int32, (tile, tile), 0)
    c2 = lax.broadcasted_iota(jnp.int32, (tile, tile), 1)
    same_chunk = (r2 // c) == (c2 // c)
    cum_mat = jnp.where(same_chunk & (c2 <= r2), 1.0, 0.0).astype(BF16)
    tot_mat = jnp.where(same_chunk, 1.0, 0.0).astype(BF16)
    ri = lax.broadcasted_iota(jnp.int32, (c, 2 * c), 0)
    li = lax.broadcasted_iota(jnp.int32, (c, 2 * c), 1)
    lo_half = li < c
    ci = jnp.where(lo_half, li, li - c)
    causal = ci <= ri
    strict = ci < ri
    eye = jnp.where(ci == ri, 1.0, 0.0).astype(F32)
    upper = jnp.where(ri <= ci, 1.0, 0.0).astype(BF16)

    hist = CONV_HISTORY_ROWS
    for hh in range(hps):
        cols = slice(hh * dk, (hh + 1) * dk)
        for src, w_ref, dst, kind in ((q_ref, wq_ref, qn_sc, "q"), (k_ref, wk_ref, kn_sc, "k"),
                                      (v_ref, wv_ref, vn_sc, "v")):
            w = w_ref[:, cols]

            def conv_tile(xe, c0, w=w, dst=dst, kind=kind, hh=hh):
                off = hist - (CONV_WIDTH - 1)
                y = w[0:1, :] * xe[off:off + CONV_TILE]
                for j in range(1, CONV_WIDTH):
                    y = y + w[j:j + 1, :] * xe[off + j:off + j + CONV_TILE]
                y = y * _sigmoid(y)
                if kind != "v":
                    y = y * lax.rsqrt(jnp.sum(y * y, axis=1, keepdims=True) + L2_EPS)
                if kind == "q":
                    y = y * (B_HEAD_DIM ** -0.5)
                dst[hh, pl.ds(c0, CONV_TILE), :] = y

            conv_tile(jnp.concatenate([jnp.zeros((hist, dk), F32),
                                       src[0:CONV_TILE, cols].astype(F32)], axis=0), 0)

            def conv_body(ti, carry, src=src, cols=cols, conv_tile=conv_tile):
                c0 = pl.multiple_of(ti * CONV_TILE, CONV_TILE)
                conv_tile(src[pl.ds(c0 - hist, hist + CONV_TILE), cols].astype(F32), c0)
                return carry

            lax.fori_loop(1, seq // CONV_TILE, conv_body, 0)

    for hh in range(hps):
        h = group * hps + hh
        a_row = st_ref[B_HEADS + h]
        g_row = -jnp.exp(part_ref[0, h]) * _softplus(a_row + part_ref[1, h])
        gcr_sc[hh] = _dot_sel_r(g_row, upper)

    alog_l = par_ref[0:1, :]
    dtb_l = par_ref[1:2, :]

    def phase1(i, carry):
        r0 = pl.multiple_of(i * tile, tile)
        rows = pl.ds(r0, tile)
        s = sc_ref[rows, :]
        beta_all = _sigmoid(s)
        g_all = -jnp.exp(alog_l) * _softplus(s + dtb_l)
        gc_all = _dot_sel(cum_mat, g_all)
        gl_all = _dot_sel(tot_mat, g_all)
        chains = []
        for hh in range(hps):
            h = group * hps + hh
            pick = lambda a, l: jnp.sum(jnp.where(lane == l, a, 0.0), axis=1, keepdims=True)
            beta = pick(beta_all, h)
            gc = pick(gc_all, B_HEADS + h)
            gl = pick(gl_all, B_HEADS + h)
            egc = jnp.exp(gc)
            q = qn_sc[hh, rows, :]
            k = kn_sc[hh, rows, :]
            v = vn_sc[hh, rows, :]
            kbeta = k * beta
            vbeta = v * beta
            qn_sc[hh, rows, :] = q * egc
            kdec = k * jnp.exp(gl - gc)
            kb16 = k.astype(BF16)
            for cc in range(DELTA_P1_CHUNKS):
                sl = slice(cc * c, (cc + 1) * c)
                n = DELTA_P1_CHUNKS * i + cc
                kdt_sc[hh, n] = kdec[sl].T.astype(BF16)
                egl_sc[hh, pl.ds(n, 1), :] = jnp.broadcast_to(jnp.exp(gl[cc * c:cc * c + 1]), (1, dk))
                diff = gc[sl] - gcr_sc[hh, pl.ds(n, 1), :]
                e = jnp.exp(jnp.where(causal, diff, 0.0))
                k2 = jnp.concatenate([kb16[sl], kb16[sl]], axis=0)
                lhs = jnp.concatenate([kbeta[sl], q[sl]], axis=0).astype(BF16)
                rhs = jnp.concatenate([vbeta[sl], kbeta[sl] * egc[sl]], axis=1)
                chains.append(dict(hh=hh, row=r0 + cc * c, e=e, k2=k2, lhs=lhs, rhs=rhs))

        for ch in chains:
            kq = lax.dot_general(ch["lhs"], ch["k2"], (((1,), (1,)), ((), ())),
                                 preferred_element_type=F32)
            ch["a"] = jnp.where(strict, kq[:c] * ch["e"], 0.0)
            attn_sc[ch["hh"], pl.ds(ch["row"], c), :] = (
                jnp.where(causal, kq[c:] * ch["e"], 0.0)[:, :c].astype(BF16))
        for ch in chains:
            ch["inv"] = eye - ch["a"]
            p_l, p_r = _stack_parts(ch["a"], lo_half)
            ch["p"] = _dot16(p_l, p_r)
        for _ in range(4):
            for ch in chains:
                p_l, p_r = _stack_parts(ch["p"], lo_half)
                inv_l, _ = _stack_parts(ch["inv"], lo_half, want_r=False)
                res = _dot16(jnp.concatenate([p_l, inv_l], axis=0), p_r)
                ch["p"] = res[:c]
                ch["inv"] = ch["inv"] + res[c:]
        for ch in chains:
            inv_l, _ = _stack_parts(ch["inv"], lo_half, want_r=False)
            _, p_r = _stack_parts(ch["p"], lo_half, want_l=False)
            ch["inv"] = ch["inv"] + _dot16(inv_l, p_r)
        for ch in chains:
            inv_l, _ = _stack_parts(ch["inv"], lo_half, want_r=False)
            _, rhs_r = _stack_parts(ch["rhs"], None, want_l=False)
            sol = _dot16(inv_l, rhs_r)
            vn_sc[ch["hh"], pl.ds(ch["row"], c), :] = sol[:, :dk]
            w_sc[ch["hh"], pl.ds(ch["row"], c), :] = sol[:, dk:].astype(BF16)
        return carry

    lax.fori_loop(0, n_chunks // DELTA_P1_CHUNKS, phase1, 0)

    state_sc[...] = jnp.zeros(state_sc.shape, F32)
    nw = nw_ref[...]

    def phase2(n, carry):
        r0 = pl.multiple_of(n * c, c)
        rows = pl.ds(r0, c)
        heads = range(hps)
        st = [state_sc[hh] for hh in heads]
        st16 = [s_.astype(BF16) for s_ in st]
        lhs = [jnp.concatenate([w_sc[hh, rows, :], qn_sc[hh, rows, :].astype(BF16)], axis=0)
               for hh in heads]
        ws = [_dot16(lhs[hh], st16[hh]) for hh in heads]
        vn16 = [(vn_sc[hh, rows, :] - ws[hh][:c]).astype(BF16) for hh in heads]
        av = [_dot16(attn_sc[hh, rows, :], vn16[hh]) for hh in heads]
        kv = [_dot16(kdt_sc[hh, n], vn16[hh]) for hh in heads]
        for hh in heads:
            state_sc[hh] = st[hh] * egl_sc[hh, pl.ds(n, 1), :] + kv[hh]
            out = ws[hh][c:] + av[hh]
            o = out * lax.rsqrt(jnp.mean(out * out, axis=1, keepdims=True) + RMS_EPS) * nw
            z = z_ref[rows, hh * dk:(hh + 1) * dk]
            o_ref[rows, hh * dk:(hh + 1) * dk] = (o * (z * _sigmoid(z))).astype(o_ref.dtype)
        return carry

    lax.fori_loop(0, n_chunks, phase2, 0)


def deltanet(qkv_b, z_b, scal, scal_t, conv_w, a_log, dt_bias, gnorm_w, bsz, seq):
    t = bsz * seq
    hps = DELTA_HEADS_PER_STEP
    n_groups = B_HEADS // hps
    wg = hps * B_HEAD_DIM
    c = DELTA_CHUNK
    n_chunks = seq // c
    st3 = scal_t.reshape(16, t // c, c)
    par = jnp.zeros((8, LANES), F32)
    par = par.at[0, B_HEADS:2 * B_HEADS].set(a_log).at[1, B_HEADS:2 * B_HEADS].set(dt_bias)
    part = jnp.stack([a_log, dt_bias]).astype(F32)
    blk = lambda off: pl.BlockSpec((seq, wg), lambda b, g: (b, off + g))
    wblk = lambda off: pl.BlockSpec((CONV_WIDTH, wg), lambda b, g: (0, off + g))
    sc = lambda shape, dt: pltpu.VMEM(shape, dt)
    return pl.pallas_call(
        functools.partial(_delta_kernel, seq=seq, hps=hps),
        grid=(bsz, n_groups),
        in_specs=[blk(0), blk(n_groups), blk(2 * n_groups), blk(0),
                  pl.BlockSpec((seq, LANES), lambda b, g: (b, 0)),
                  pl.BlockSpec((16, n_chunks, c), lambda b, g: (0, b, 0)),
                  pl.BlockSpec((8, LANES), lambda b, g: (0, 0)),
                  pl.BlockSpec(memory_space=pltpu.SMEM),
                  wblk(0), wblk(n_groups), wblk(2 * n_groups),
                  pl.BlockSpec((1, B_HEAD_DIM), lambda b, g: (0, 0))],
        out_specs=pl.BlockSpec((seq, wg), lambda b, g: (b, g)),
        out_shape=jax.ShapeDtypeStruct((t, B_WIDTH), BF16),
        scratch_shapes=[sc((hps, seq, B_HEAD_DIM), F32), sc((hps, seq, B_HEAD_DIM), F32),
                        sc((hps, seq, B_HEAD_DIM), F32),
                        sc((hps, seq, B_HEAD_DIM), BF16),
                        sc((hps, seq, c), BF16), sc((hps, n_chunks, B_HEAD_DIM, c), BF16),
                        sc((hps, n_chunks, 2 * c), F32), sc((hps, n_chunks, B_HEAD_DIM), F32),
                        sc((hps, B_HEAD_DIM, B_HEAD_DIM), F32)],
        compiler_params=_params(2),
        name="deltanet",
    )(qkv_b, qkv_b, qkv_b, z_b, scal, st3, par, part,
      conv_w, conv_w, conv_w, gnorm_w.reshape(1, B_HEAD_DIM))


SUBLANES = 8
TOKEN_TILE_ROWS = D_MODEL // LANES
assert TOKEN_TILE_ROWS == SUBLANES


def _store_token_tiles(ref, x, n_tok):
    for j in range(TOKEN_TILE_ROWS):
        ref[pl.ds(j, n_tok, stride=TOKEN_TILE_ROWS), :] = x[:, j * LANES:(j + 1) * LANES]


def _load_token_tiles(ref, n_tok, lead=None):
    idx = lambda j: (pl.ds(j, n_tok, stride=TOKEN_TILE_ROWS), slice(None))
    if lead is None:
        return [ref[idx(j)] for j in range(TOKEN_TILE_ROWS)]
    return [ref[(lead,) + idx(j)] for j in range(TOKEN_TILE_ROWS)]


def _layer_norm(h, g, b):
    mu = jnp.mean(h, axis=1, keepdims=True)
    hc = h - mu
    var = jnp.mean(hc * hc, axis=1, keepdims=True)
    return hc * lax.rsqrt(var + LN_EPS) * g + b


def _mix_route_kernel(ya_ref, yb_ref, x_ref, p_ref, woa_ref, wob_ref, g1_ref, b1_ref,
                      wr_ref, br_ref, wpg_ref, bpg_ref, wpp_ref,
                      x1_ref, res_ref, route_ref, gate_ref, cnt_ref, run_sc, wr2_sc, *, tm):
    i = pl.program_id(0)

    @pl.when(i == 0)
    def _():
        run_sc[...] = jnp.zeros(run_sc.shape, F32)
        w_hi, w_lo = _split2(wr_ref[...])
        wr2_sc[:, :LANES] = w_hi
        wr2_sc[:, LANES:] = w_lo

    mix = (jnp.dot(ya_ref[...].astype(BF16), woa_ref[...], preferred_element_type=F32)
           + jnp.dot(yb_ref[...].astype(BF16), wob_ref[...], preferred_element_type=F32))
    x1 = _layer_norm(DEEPNORM_ALPHA * x_ref[...] + mix, g1_ref[...], b1_ref[...])
    _store_token_tiles(x1_ref, x1, tm)
    x1b = x1.astype(BF16)

    x1l = (x1 - x1b.astype(F32)).astype(BF16)
    r_hi = jnp.dot(x1b, wr2_sc[...], preferred_element_type=F32)
    r_lo = jnp.dot(x1l, wr2_sc[:, :LANES], preferred_element_type=F32)
    logits = r_hi[:, :LANES] + (r_hi[:, LANES:] + r_lo) + br_ref[...]

    pgate = _sigmoid(jnp.dot(x1b, wpg_ref[...], preferred_element_type=F32) + bpg_ref[...])
    proj = jnp.dot(p_ref[...].astype(BF16), wpp_ref[...], preferred_element_type=F32)
    res_ref[...] = DEEPNORM_ALPHA * x1 + pgate * proj

    lane = lax.broadcasted_iota(jnp.int32, (tm, LANES), 1)
    cur = jnp.where(lane < N_EXPERTS, logits, -jnp.inf)
    vals, hots = [], []
    for _ in range(TOP_K):
        m = jnp.max(cur, axis=1, keepdims=True)
        idx = jnp.min(jnp.where(cur == m, lane, LANES), axis=1, keepdims=True)
        hot = lane == idx
        cur = jnp.where(hot, -jnp.inf, cur)
        vals.append(m)
        hots.append((hot, idx))
    exps = [jnp.exp(v - vals[0]) for v in vals]
    den = exps[0] + exps[1] + exps[2] + exps[3]

    member = jnp.zeros((tm, LANES), F32)
    for hot, _ in hots:
        member = member + jnp.where(hot, 1.0, 0.0)
    rr = lax.broadcasted_iota(jnp.int32, (tm, tm), 0)
    cc = lax.broadcasted_iota(jnp.int32, (tm, tm), 1)
    before = jnp.where(cc < rr, 1.0, 0.0).astype(BF16)
    prior = jnp.dot(before, member.astype(BF16), preferred_element_type=F32) + run_sc[0:1, :]
    route = jnp.zeros((tm, LANES), jnp.int32)
    gate = jnp.zeros((tm, LANES), F32)
    for kk, (hot, idx) in enumerate(hots):
        rank = jnp.sum(jnp.where(hot, prior, 0.0), axis=1, keepdims=True).astype(jnp.int32)
        route = jnp.where(lane == kk, idx, route)
        route = jnp.where(lane == TOP_K + kk, rank, route)
        gate = jnp.where(lane == kk, exps[kk] / den, gate)
    route_ref[...] = route
    gate_ref[...] = gate
    run_sc[...] = run_sc[...] + jnp.sum(member, axis=0, keepdims=True)
    cnt_ref[...] = run_sc[...]


def mix_route(ya, yb, x2d, p2d, w_out, ln1_g, ln1_b, w_router, b_router, w_ple_gate, b_ple_gate,
              w_ple_proj, tm=256):
    t, d = x2d.shape
    woa = w_out[:A_WIDTH].astype(BF16)
    wob = w_out[A_WIDTH:].astype(BF16)
    wr = jnp.pad(w_router, ((0, 0), (0, LANES - N_EXPERTS)))
    br = jnp.pad(b_router, (0, LANES - N_EXPERTS)).reshape(1, LANES)
    row = lambda w: pl.BlockSpec((tm, w), lambda i: (i, 0))
    full = lambda a: pl.BlockSpec(a.shape, lambda i: (0,) * a.ndim)
    ops = [woa, wob, ln1_g.reshape(1, d), ln1_b.reshape(1, d), wr, br,
           w_ple_gate.astype(BF16), b_ple_gate.reshape(1, d), w_ple_proj.astype(BF16)]
    return pl.pallas_call(
        functools.partial(_mix_route_kernel, tm=tm),
        grid=(t // tm,),
        in_specs=[row(A_WIDTH), row(B_WIDTH), row(d), row(PLE_DIM)] + [full(a) for a in ops],
        out_specs=[pl.BlockSpec((tm * TOKEN_TILE_ROWS, LANES), lambda i: (i, 0)),
                   row(d), row(LANES), row(LANES),
                   pl.BlockSpec((8, LANES), lambda i: (0, 0))],
        out_shape=[jax.ShapeDtypeStruct((t * TOKEN_TILE_ROWS, LANES), F32),
                   jax.ShapeDtypeStruct((t, d), F32),
                   jax.ShapeDtypeStruct((t, LANES), jnp.int32),
                   jax.ShapeDtypeStruct((t, LANES), F32),
                   jax.ShapeDtypeStruct((8, LANES), F32)],
        scratch_shapes=[pltpu.VMEM((8, LANES), F32), pltpu.VMEM((d, 2 * LANES), BF16)],
        compiler_params=_params(),
        name="mix_route",
    )(ya, yb, x2d, p2d, *ops)


MOE_TILE = 512
DISPATCH_TM = 512
COMBINE_TM = 256
DMA_ROWS_PER_ITER = 4


def _dispatch_kernel(pstart_ref, pend_ref, slot_ref, x_ref, xb_ref, zero_sc, sem, zsem, *, tm):
    i = pl.program_id(0)
    tr = TOKEN_TILE_ROWS
    tile_rows = MOE_TILE * tr

    def zero_copy(e):
        off = pl.multiple_of((pend_ref[e] - MOE_TILE) * tr, tile_rows)
        return pltpu.make_async_copy(zero_sc, xb_ref.at[pl.ds(off, tile_rows), :], zsem)

    @pl.when(i == 0)
    def _():
        zero_sc[...] = jnp.zeros(zero_sc.shape, zero_sc.dtype)
        for e in range(N_EXPERTS):
            @pl.when(pend_ref[e] > pstart_ref[e])
            def _():
                zero_copy(e).start()
        for e in range(N_EXPERTS):
            @pl.when(pend_ref[e] > pstart_ref[e])
            def _():
                zero_copy(e).wait()

        n_tiles = xb_ref.shape[0] // tile_rows
        first_unused = pend_ref[N_EXPERTS - 1] // MOE_TILE

        def tail_copy(j):
            off = pl.multiple_of(j * tile_rows, tile_rows)
            return pltpu.make_async_copy(zero_sc, xb_ref.at[pl.ds(off, tile_rows), :], zsem)

        def tail_start(j, c):
            tail_copy(j).start()
            return c

        def tail_wait(j, c):
            tail_copy(j).wait()
            return c

        lax.fori_loop(first_unused, n_tiles, tail_start, 0)
        lax.fori_loop(first_unused, n_tiles, tail_wait, 0)

    def token_copy(tok, slot):
        src = pl.multiple_of(tok * tr, tr)
        dst = pl.multiple_of(slot * tr, tr)
        return pltpu.make_async_copy(x_ref.at[pl.ds(src, tr), :], xb_ref.at[pl.ds(dst, tr), :], sem)

    def start(g, c):
        toks = [g * DMA_ROWS_PER_ITER + u for u in range(DMA_ROWS_PER_ITER)]
        slots = [[slot_ref[r * TOP_K + kk] for kk in range(TOP_K)] for r in toks]
        for r, row_slots in zip(toks, slots):
            for slot in row_slots:
                token_copy(r, slot).start()
        return c

    def wait(r, c):
        for kk in range(TOP_K):
            token_copy(0, 0).wait()
        return c

    lax.fori_loop(0, tm // DMA_ROWS_PER_ITER, start, 0)
    lax.fori_loop(0, tm, wait, 0)


def dispatch(x1_tiles, slot_flat, pad_start, pad_end, n_slots, tm=DISPATCH_TM):
    t = x1_tiles.shape[0] // TOKEN_TILE_ROWS
    return pl.pallas_call(
        functools.partial(_dispatch_kernel, tm=tm),
        grid_spec=pltpu.PrefetchScalarGridSpec(
            num_scalar_prefetch=2,
            grid=(t // tm,),
            in_specs=[pl.BlockSpec((tm * TOP_K,), lambda i, ps, pe: (i,), memory_space=pltpu.SMEM),
                      pl.BlockSpec((tm * TOKEN_TILE_ROWS, LANES), lambda i, ps, pe: (i, 0))],
            out_specs=pl.BlockSpec(memory_space=pl.ANY),
            scratch_shapes=[pltpu.VMEM((MOE_TILE * TOKEN_TILE_ROWS, LANES), F32),
                            pltpu.SemaphoreType.DMA(()), pltpu.SemaphoreType.DMA(())]),
        out_shape=jax.ShapeDtypeStruct((n_slots * TOKEN_TILE_ROWS, LANES), F32),
        compiler_params=_params(),
        name="dispatch",
    )(pad_start, pad_end, slot_flat, x1_tiles)


EXPERT_CAST_ROWS = 256
EXPERT_SUBTILES = 1


def _expert_kernel(be_ref, nu_ref, x_ref, wgu_ref, bgu_ref, wd_ref, bd_ref, y_ref, wgu_sc, wd_sc):
    i = pl.program_id(0)
    used = i < nu_ref[0]
    new_expert = jnp.logical_or(i == 0, be_ref[i] != be_ref[jnp.maximum(i - 1, 0)])

    @pl.when(jnp.logical_and(used, new_expert))
    def _():
        def cast(j, c):
            r = pl.ds(pl.multiple_of(j * EXPERT_CAST_ROWS, EXPERT_CAST_ROWS), EXPERT_CAST_ROWS)
            wgu_sc[r, :] = wgu_ref[0, r, :].astype(BF16)
            wd_sc[r, :] = wd_ref[0, r, :].astype(BF16)
            return c
        lax.fori_loop(0, D_MODEL // EXPERT_CAST_ROWS, cast, 0)

    @pl.when(used)
    def _():
        sub = MOE_TILE // EXPERT_SUBTILES
        tr = TOKEN_TILE_ROWS
        for s in range(EXPERT_SUBTILES):
            base = s * sub * tr
            xb = jnp.concatenate(
                [x_ref[pl.ds(base + j, sub, stride=tr), :].astype(BF16) for j in range(tr)], axis=1)
            acc = jnp.zeros((sub, D_MODEL), F32) + bd_ref[0]
            nchunk = 256
            for c0 in range(0, D_EXPERT, nchunk):
                g = jnp.dot(xb, wgu_sc[:, c0:c0 + nchunk], preferred_element_type=F32)
                g = g + bgu_ref[0, :, c0:c0 + nchunk]
                u = jnp.dot(xb, wgu_sc[:, D_EXPERT + c0:D_EXPERT + c0 + nchunk],
                            preferred_element_type=F32)
                u = u + bgu_ref[0, :, D_EXPERT + c0:D_EXPERT + c0 + nchunk]
                g = jnp.minimum(g, SWIGLU_LIMIT)
                u = jnp.clip(u, -SWIGLU_LIMIT, SWIGLU_LIMIT)
                act = (u + 1.0) * (g * _sigmoid(SWIGLU_ALPHA * g))
                acc = acc + jnp.dot(act.astype(BF16), wd_sc[c0:c0 + nchunk, :],
                                    preferred_element_type=F32)
            for j in range(tr):
                y_ref[pl.ds(base + j, sub, stride=tr), :] = acc[:, j * LANES:(j + 1) * LANES]

    @pl.when(jnp.logical_not(used))
    def _():
        y_ref[...] = jnp.zeros(y_ref.shape, F32)


def experts(xb_tiles, block_e, n_used, w_gate_up, b_gate_up, w_down, b_down):
    tile_rows = MOE_TILE * TOKEN_TILE_ROWS
    n_blocks = xb_tiles.shape[0] // tile_rows
    d = D_MODEL
    clamp = lambda i, be, nu: jnp.minimum(i, nu[0] - 1)
    return pl.pallas_call(
        _expert_kernel,
        grid_spec=pltpu.PrefetchScalarGridSpec(
            num_scalar_prefetch=2,
            grid=(n_blocks,),
            in_specs=[pl.BlockSpec((tile_rows, LANES), lambda i, be, nu: (clamp(i, be, nu), 0)),
                      pl.BlockSpec((1, d, 2 * D_EXPERT), lambda i, be, nu: (be[i], 0, 0)),
                      pl.BlockSpec((1, 1, 2 * D_EXPERT), lambda i, be, nu: (be[i], 0, 0)),
                      pl.BlockSpec((1, D_EXPERT, d), lambda i, be, nu: (be[i], 0, 0)),
                      pl.BlockSpec((1, 1, d), lambda i, be, nu: (be[i], 0, 0))],
            out_specs=pl.BlockSpec((tile_rows, LANES), lambda i, be, nu: (i, 0)),
            scratch_shapes=[pltpu.VMEM((d, 2 * D_EXPERT), BF16), pltpu.VMEM((D_EXPERT, d), BF16)]),
        out_shape=jax.ShapeDtypeStruct(xb_tiles.shape, F32),
        compiler_params=_params(),
        name="experts",
    )(block_e, n_used, xb_tiles, w_gate_up, b_gate_up.reshape(N_EXPERTS, 1, -1),
      w_down, b_down.reshape(N_EXPERTS, 1, -1))


def _combine_kernel(slot_ref, slot_next_ref, gate_ref, res_ref, g2_ref, b2_ref,
                    yb_ref, o_ref, buf_sc, sem, *, tm):
    i = pl.program_id(0)
    tr = TOKEN_TILE_ROWS
    cur = lax.rem(i, 2)

    def token_copy(buf, kk, r, slot):
        src = pl.multiple_of(slot * tr, tr)
        dst = pl.multiple_of(r * tr, tr)
        return pltpu.make_async_copy(yb_ref.at[pl.ds(src, tr), :],
                                     buf_sc.at[buf, kk, pl.ds(dst, tr), :], sem.at[buf])

    def issue(sref, buf):
        def start(g, c):
            toks = [g * DMA_ROWS_PER_ITER + u for u in range(DMA_ROWS_PER_ITER)]
            slots = [[sref[r * TOP_K + kk] for kk in range(TOP_K)] for r in toks]
            for r, row_slots in zip(toks, slots):
                for kk, slot in enumerate(row_slots):
                    token_copy(buf, kk, r, slot).start()
            return c
        lax.fori_loop(0, tm // DMA_ROWS_PER_ITER, start, 0)

    @pl.when(i == 0)
    def _():
        issue(slot_ref, 0)

    @pl.when(i + 1 < pl.num_programs(0))
    def _():
        issue(slot_next_ref, 1 - cur)

    def wait(r, c):
        for kk in range(TOP_K):
            token_copy(cur, 0, 0, 0).wait()
        return c

    lax.fori_loop(0, tm, wait, 0)

    gate = gate_ref[...]
    chunks = []
    for j in range(TOKEN_TILE_ROWS):
        hj = res_ref[:, j * LANES:(j + 1) * LANES]
        for kk in range(TOP_K):
            hj = hj + gate[:, kk:kk + 1] * buf_sc[cur, kk, pl.ds(j, tm, stride=tr), :]
        chunks.append(hj)
    mu = sum(jnp.sum(hj, axis=1, keepdims=True) for hj in chunks) * (1.0 / D_MODEL)
    cent = [hj - mu for hj in chunks]
    var = sum(jnp.sum(cj * cj, axis=1, keepdims=True) for cj in cent) * (1.0 / D_MODEL)
    inv = lax.rsqrt(var + LN_EPS)
    for j, cj in enumerate(cent):
        cols = slice(j * LANES, (j + 1) * LANES)
        o_ref[:, cols] = cj * inv * g2_ref[:, cols] + b2_ref[:, cols]


def combine(y_tiles, slot_flat, gates, res, ln2_g, ln2_b, tm=COMBINE_TM):
    t, d = res.shape
    n_steps = t // tm
    slot_spec = lambda f: pl.BlockSpec((tm * TOP_K,), f, memory_space=pltpu.SMEM)
    return pl.pallas_call(
        functools.partial(_combine_kernel, tm=tm),
        grid=(n_steps,),
        in_specs=[slot_spec(lambda i: (i,)),
                  slot_spec(lambda i: (jnp.minimum(i + 1, n_steps - 1),)),
                  pl.BlockSpec((tm, LANES), lambda i: (i, 0)),
                  pl.BlockSpec((tm, d), lambda i: (i, 0)),
                  pl.BlockSpec((1, d), lambda i: (0, 0)),
                  pl.BlockSpec((1, d), lambda i: (0, 0)),
                  pl.BlockSpec(memory_space=pl.ANY)],
        out_specs=pl.BlockSpec((tm, d), lambda i: (i, 0)),
        scratch_shapes=[pltpu.VMEM((2, TOP_K, tm * TOKEN_TILE_ROWS, LANES), F32),
                        pltpu.SemaphoreType.DMA((2,))],
        out_shape=jax.ShapeDtypeStruct((t, d), F32),
        compiler_params=_params(),
        name="combine",
    )(slot_flat, slot_flat, gates, res, ln2_g.reshape(1, d), ln2_b.reshape(1, d), y_tiles)


def kernel(x, p, w_in, conv_w, a_log, dt_bias, gnorm_w, w_out, ln1_g, ln1_b, w_router, b_router,
           w_gate_up, b_gate_up, w_down, b_down, w_ple_gate, b_ple_gate, w_ple_proj, ln2_g, ln2_b):
    bsz, seq, d = x.shape
    t = bsz * seq
    x2d = x.reshape(t, d)
    qkv_a, qkv_b, z_b, scal, scal_t = in_proj(x2d, w_in[0])
    ya = dilated_attn(qkv_a, bsz, seq)
    yb = deltanet(qkv_b, z_b, scal, scal_t, conv_w[0], a_log[0], dt_bias[0], gnorm_w[0], bsz, seq)
    x1, res, route, gates, counts = mix_route(
        ya, yb, x2d, p[0].reshape(t, PLE_DIM), w_out[0], ln1_g[0], ln1_b[0], w_router[0],
        b_router[0], w_ple_gate[0], b_ple_gate[0], w_ple_proj[0])

    cnt = counts[0, :N_EXPERTS].astype(jnp.int32)
    padded = (cnt + MOE_TILE - 1) // MOE_TILE * MOE_TILE
    pad_end = jnp.cumsum(padded).astype(jnp.int32)
    pad_start = pad_end - padded
    n_blocks = (t * TOP_K) // MOE_TILE + N_EXPERTS
    n_used = (pad_end[-1:] // MOE_TILE).astype(jnp.int32)
    tile_start = jnp.arange(n_blocks, dtype=jnp.int32) * MOE_TILE
    block_e = jnp.minimum(jnp.sum(pad_end[None, :] <= tile_start[:, None], axis=1),
                          N_EXPERTS - 1).astype(jnp.int32)
    experts_iota = jnp.arange(N_EXPERTS, dtype=jnp.int32)
    group_off = jnp.sum(jnp.where(route[:, :TOP_K, None] == experts_iota, pad_start, 0), axis=-1)
    slot_flat = (group_off + route[:, TOP_K:2 * TOP_K]).reshape(-1).astype(jnp.int32)

    xb = dispatch(x1, slot_flat, pad_start, pad_end, n_blocks * MOE_TILE)
    yexp = experts(xb, block_e, n_used, w_gate_up[0], b_gate_up[0], w_down[0], b_down[0])
    out = combine(yexp, slot_flat, gates, res, ln2_g[0], ln2_b[0])
    return out.reshape(bsz, seq, d)
```

```python
import functools
import math

import jax
import jax.numpy as jnp
from jax import lax
from jax.experimental import pallas as pl
from jax.experimental.pallas import tpu as pltpu

LANES = 128
VMEM_LIMIT_BYTES = 56 * 1024 * 1024

D_MODEL = 1024
PLE_DIM = 256
A_HEADS = 8
A_HEAD_DIM = 64
A_WIDTH = A_HEADS * A_HEAD_DIM
DILATIONS = (16, 4, 1)
ATT_BLOCK = 128
B_HEADS = 4
B_HEAD_DIM = 128
B_WIDTH = B_HEADS * B_HEAD_DIM
CONV_WIDTH = 4
DELTA_CHUNK = 64
N_EXPERTS = 32
TOP_K = 4
D_EXPERT = D_MODEL
SWIGLU_LIMIT = 7.0
SWIGLU_ALPHA = 1.702
MOE_BLOCK = 128
LN_EPS = 1e-5
RMS_EPS = 1e-6
L2_EPS = 1e-6
DEPTH = 1
DEEPNORM_ALPHA = (2.0 * DEPTH) ** 0.25

F32 = jnp.float32
BF16 = jnp.bfloat16


def _params(n_parallel_axes=1):
    return pltpu.CompilerParams(
        dimension_semantics=("arbitrary",) * n_parallel_axes,
        vmem_limit_bytes=VMEM_LIMIT_BYTES)


def _in_proj_kernel(x_ref, wa_ref, wb_ref, wz_ref, ws_ref, wst_ref, a_ref, b_ref, z_ref, s_ref, st_ref):
    xb = x_ref[...].astype(BF16)
    a_ref[...] = jnp.dot(xb, wa_ref[...], preferred_element_type=F32)
    b_ref[...] = jnp.dot(xb, wb_ref[...], preferred_element_type=F32).astype(BF16)
    z_ref[...] = jnp.dot(xb, wz_ref[...], preferred_element_type=F32)
    s_ref[...] = jnp.dot(xb, ws_ref[...], preferred_element_type=F32)
    st_ref[...] = lax.dot_general(wst_ref[...], xb, (((1,), (1,)), ((), ())),
                                  preferred_element_type=F32)


def in_proj(x2d, w_in, tm=256):
    t, d = x2d.shape
    na, nb = 3 * A_WIDTH, 3 * B_WIDTH
    wa = w_in[:, :na].astype(BF16)
    wb = w_in[:, na:na + nb].astype(BF16)
    wz = w_in[:, na + nb:na + nb + B_WIDTH].astype(BF16)
    ws = jnp.pad(w_in[:, na + nb + B_WIDTH:], ((0, 0), (0, LANES - 2 * B_HEADS))).astype(BF16)
    wst = ws[:, :16].T
    full = lambda a: pl.BlockSpec(a.shape, lambda i: (0, 0))
    return pl.pallas_call(
        _in_proj_kernel,
        grid=(t // tm,),
        in_specs=[pl.BlockSpec((tm, d), lambda i: (i, 0)),
                  full(wa), full(wb), full(wz), full(ws), full(wst)],
        out_specs=[pl.BlockSpec((tm, na), lambda i: (i, 0)),
                   pl.BlockSpec((tm, nb), lambda i: (i, 0)),
                   pl.BlockSpec((tm, B_WIDTH), lambda i: (i, 0)),
                   pl.BlockSpec((tm, LANES), lambda i: (i, 0)),
                   pl.BlockSpec((16, tm), lambda i: (0, i))],
        out_shape=[jax.ShapeDtypeStruct((t, na), F32),
                   jax.ShapeDtypeStruct((t, nb), BF16),
                   jax.ShapeDtypeStruct((t, B_WIDTH), F32),
                   jax.ShapeDtypeStruct((t, LANES), F32),
                   jax.ShapeDtypeStruct((16, t), F32)],
        compiler_params=_params(),
        name="in_proj",
    )(x2d, wa, wb, wz, ws, wst)


ATTN_UNROLL_DENSE = 3
ATTN_UNROLL_SINGLE = 4


def _attn_kernel(q_ref, k_ref, v_ref, o_ref, m_sc, l_sc, acc_sc, *, seq):
    blk = ATT_BLOCK
    lane = lax.broadcasted_iota(jnp.int32, (blk, LANES), 1)
    head0 = lane < A_HEAD_DIM
    head0_kv = lax.broadcasted_iota(jnp.int32, (2 * blk, LANES), 1) < A_HEAD_DIM
    row = lax.broadcasted_iota(jnp.int32, (blk, blk), 0)
    col = lax.broadcasted_iota(jnp.int32, (blk, blk), 1)
    prev_ok = col >= row
    cur_ok = col <= row
    neg_inf = jnp.float32(-jnp.inf)
    scale = 1.0 / math.sqrt(A_HEAD_DIM)

    def blocks(starts, stride, has_prev, first_branch, last_branch):
        def rows(ref, s0):
            if stride == 1:
                return ref[pl.ds(s0, blk), :]
            return ref[pl.ds(s0, blk, stride=stride), :]

        def put(ref, s0, val):
            if stride == 1:
                ref[pl.ds(s0, blk), :] = val
            else:
                ref[pl.ds(s0, blk, stride=stride), :] = val

        hmask = (head0, jnp.logical_not(head0))
        items = []
        for start, hp in zip(starts, has_prev):
            q = rows(q_ref, start) * scale
            keys = rows(k_ref, start)
            vals = rows(v_ref, start)
            if hp:
                keys = jnp.concatenate([rows(k_ref, start - blk * stride), keys], axis=0)
                vals = jnp.concatenate([rows(v_ref, start - blk * stride), vals], axis=0)
            it = dict(start=start, hp=hp, keys=keys.astype(BF16), vals=vals,
                      qh=[jnp.where(hm, q, 0.0).astype(BF16) for hm in hmask])
            if not first_branch:
                it["m_old"] = rows(m_sc, start)
                it["l_old"] = rows(l_sc, start)
                it["acc_old"] = rows(acc_sc, start)
            items.append(it)

        for it in items:
            it["s"] = [lax.dot_general(qh, it["keys"], (((1,), (1,)), ((), ())),
                                       preferred_element_type=F32) for qh in it["qh"]]
        for it in items:
            ok = jnp.concatenate([prev_ok, cur_ok], axis=1) if it["hp"] else cur_ok
            it["m"], it["sum"], it["p"] = [], [], []
            for h in range(2):
                s = jnp.where(ok, it["s"][h], neg_inf)
                t = jnp.maximum(s[:, :blk], s[:, blk:]) if it["hp"] else s
                if not first_branch:
                    t = jnp.maximum(t, jnp.where(hmask[h], it["m_old"], neg_inf))
                m_h = jnp.max(t, axis=1, keepdims=True)
                p = jnp.exp(s - m_h)
                psum = p[:, :blk] + p[:, blk:] if it["hp"] else p
                it["m"].append(m_h)
                it["sum"].append(jnp.sum(psum, axis=1, keepdims=True))
                it["p"].append(p.astype(BF16))
        for it in items:
            kv0 = head0_kv if it["hp"] else head0
            kvmask = (kv0, jnp.logical_not(kv0))
            it["pv"] = [jnp.dot(it["p"][h], jnp.where(kvmask[h], it["vals"], 0.0).astype(BF16),
                                preferred_element_type=F32) for h in range(2)]
        for it in items:
            m_new = jnp.where(head0, it["m"][0], it["m"][1])
            l_new = jnp.where(head0, it["sum"][0], it["sum"][1])
            acc_new = it["pv"][0] + it["pv"][1]
            if not first_branch:
                alpha = jnp.exp(it["m_old"] - m_new)
                l_new = l_new + alpha * it["l_old"]
                acc_new = acc_new + alpha * it["acc_old"]
            if last_branch:
                put(o_ref, it["start"], acc_new / l_new)
            else:
                put(m_sc, it["start"], m_new)
                put(l_sc, it["start"], l_new)
                put(acc_sc, it["start"], acc_new)

    n_br = len(DILATIONS)
    for bi, dil in enumerate(DILATIONS):
        first, last = bi == 0, bi == n_br - 1
        nblk = seq // dil // blk
        if dil == 1:
            blocks([0], 1, [False], first, last)
            per = ATTN_UNROLL_DENSE
            assert (nblk - 1) % per == 0

            def body(g, c, first=first, last=last, per=per):
                starts = [pl.multiple_of((1 + g * per + j) * blk, blk) for j in range(per)]
                blocks(starts, 1, [True] * per, first, last)
                return c
            lax.fori_loop(0, (nblk - 1) // per, body, 0)
        elif nblk > 1:
            def body(r, c, dil=dil, nblk=nblk, first=first, last=last):
                blocks([r + n * blk * dil for n in range(nblk)], dil,
                       [n > 0 for n in range(nblk)], first, last)
                return c
            lax.fori_loop(0, dil, body, 0)
        else:
            per = ATTN_UNROLL_SINGLE
            assert dil % per == 0

            def body(g, c, dil=dil, first=first, last=last, per=per):
                blocks([g * per + j for j in range(per)], dil, [False] * per, first, last)
                return c
            lax.fori_loop(0, dil // per, body, 0)


def dilated_attn(qkv_a, bsz, seq):
    t = bsz * seq
    n_pairs = A_WIDTH // LANES
    blk = lambda off: pl.BlockSpec((seq, LANES), lambda b, p: (b, off + p))
    return pl.pallas_call(
        functools.partial(_attn_kernel, seq=seq),
        grid=(bsz, n_pairs),
        in_specs=[blk(0), blk(n_pairs), blk(2 * n_pairs)],
        out_specs=pl.BlockSpec((seq, LANES), lambda b, p: (b, p)),
        out_shape=jax.ShapeDtypeStruct((t, A_WIDTH), F32),
        scratch_shapes=[pltpu.VMEM((seq, LANES), F32)] * 3,
        compiler_params=_params(2),
        name="dilated_attn",
    )(qkv_a, qkv_a, qkv_a)


DELTA_HEADS_PER_STEP = 4
DELTA_P1_CHUNKS = 4
CONV_TILE = 256
CONV_HISTORY_ROWS = 16
_HI = lax.Precision.HIGHEST


def _dot_hi(a, b):
    return jnp.dot(a, b, precision=_HI, preferred_element_type=F32)


def _dot16(a, b):
    return jnp.dot(a, b, preferred_element_type=F32)


def _split2(x):
    hi = x.astype(BF16)
    return hi, (x - hi.astype(F32)).astype(BF16)


def _dot3(a, b):
    return _dot16(a[0], b[0]) + (_dot16(a[1], b[0]) + _dot16(a[0], b[1]))


def _split3(x):
    hi = x.astype(BF16)
    r = x - hi.astype(F32)
    mid = r.astype(BF16)
    return hi, mid, (r - mid.astype(F32)).astype(BF16)


def _dot_sel(m01, x):
    hi, mid, lo = _split3(x)
    return _dot16(m01, hi) + (_dot16(m01, mid) + _dot16(m01, lo))


def _dot_sel_r(x, m01):
    hi, mid, lo = _split3(x)
    return _dot16(hi, m01) + (_dot16(mid, m01) + _dot16(lo, m01))


def _stack_parts(x, lo_half, want_l=True, want_r=True):
    xh = x.astype(BF16)
    xh_f = xh.astype(F32)
    xl_f = x - xh_f
    left = right = None
    if want_l:
        mix = jnp.where(lo_half, xh_f, xl_f).astype(BF16)
        left = jnp.concatenate([mix, mix], axis=1)
    if want_r:
        xl = xl_f.astype(BF16)
        right = jnp.concatenate([xh, xh, xl, xl], axis=0)
    return left, right


def _softplus(x):
    return jnp.maximum(x, 0.0) + jnp.log(1.0 + jnp.exp(-jnp.abs(x)))


def _sigmoid(x):
    return 1.0 / (1.0 + jnp.exp(-x))


def _delta_kernel(q_ref, k_ref, v_ref, z_ref, sc_ref, st_ref, par_ref, part_ref,
                  wq_ref, wk_ref, wv_ref, nw_ref, o_ref,
                  conv_sc, qn_sc, kn_sc, vn_sc, w_sc, attn_sc, kdt_sc, gcr_sc, egl_sc,
                  state_sc, *, seq, hps):
    c = DELTA_CHUNK
    dk = B_HEAD_DIM
    n_chunks = seq // c
    tile = DELTA_P1_CHUNKS * c
    group = pl.program_id(1)
    lane = lax.broadcasted_iota(jnp.int32, (tile, LANES), 1)
    r2 = lax.broadcasted_iota(jnp.int32, (tile, tile), 0)
    c2 = lax.broadcasted_iota(jnp.int32, (tile, tile), 1)
    same_chunk = (r2 // c) == (c2 // c)
    cum_mat = jnp.where(same_chunk & (c2 <= r2), 1.0, 0.0).astype(BF16)
    tot_mat = jnp.where(same_chunk, 1.0, 0.0).astype(BF16)
    ri = lax.broadcasted_iota(jnp.int32, (c, 2 * c), 0)
    li = lax.broadcasted_iota(jnp.int32, (c, 2 * c), 1)
    lo_half = li < c
    ci = jnp.where(lo_half, li, li - c)
    causal = ci <= ri
    strict = ci < ri
    eye = jnp.where(ci == ri, 1.0, 0.0).astype(F32)
    upper = jnp.where(ri <= ci, 1.0, 0.0).astype(BF16)

    hist = CONV_HISTORY_ROWS
    for hh in range(hps):
        cols = slice(hh * dk, (hh + 1) * dk)
        for src, w_ref, dst, kind in ((q_ref, wq_ref, qn_sc, "q"), (k_ref, wk_ref, kn_sc, "k"),
                                      (v_ref, wv_ref, vn_sc, "v")):
            w = w_ref[:, cols]

            def conv_tile(xe, c0, w=w, dst=dst, kind=kind, hh=hh):
                conv_sc[...] = xe
                off = hist - (CONV_WIDTH - 1)
                y = w[0:1, :] * conv_sc[off:off + CONV_TILE, :]
                for j in range(1, CONV_WIDTH):
                    y = y + w[j:j + 1, :] * conv_sc[off + j:off + j + CONV_TILE, :]
                y = y * _sigmoid(y)
                if kind != "v":
                    y = y * lax.rsqrt(jnp.sum(y * y, axis=1, keepdims=True) + L2_EPS)
                if kind == "q":
                    y = y * (B_HEAD_DIM ** -0.5)
                dst[hh, pl.ds(c0, CONV_TILE), :] = y

            conv_tile(jnp.concatenate([jnp.zeros((hist, dk), F32),
                                       src[0:CONV_TILE, cols].astype(F32)], axis=0), 0)

            def conv_body(ti, carry, src=src, cols=cols, conv_tile=conv_tile):
                c0 = pl.multiple_of(ti * CONV_TILE, CONV_TILE)
                conv_tile(src[pl.ds(c0 - hist, hist + CONV_TILE), cols].astype(F32), c0)
                return carry

            lax.fori_loop(1, seq // CONV_TILE, conv_body, 0)

    for hh in range(hps):
        h = group * hps + hh
        a_row = st_ref[B_HEADS + h]
        g_row = -jnp.exp(part_ref[0, h]) * _softplus(a_row + part_ref[1, h])
        gcr_sc[hh] = _dot_sel_r(g_row, upper)

    alog_l = par_ref[0:1, :]
    dtb_l = par_ref[1:2, :]

    def phase1(i, carry):
        r0 = pl.multiple_of(i * tile, tile)
        rows = pl.ds(r0, tile)
        s = sc_ref[rows, :]
        beta_all = _sigmoid(s)
        g_all = -jnp.exp(alog_l) * _softplus(s + dtb_l)
        gc_all = _dot_sel(cum_mat, g_all)
        gl_all = _dot_sel(tot_mat, g_all)
        chains = []
        for hh in range(hps):
            h = group * hps + hh
            pick = lambda a, l: jnp.sum(jnp.where(lane == l, a, 0.0), axis=1, keepdims=True)
            beta = pick(beta_all, h)
            gc = pick(gc_all, B_HEADS + h)
            gl = pick(gl_all, B_HEADS + h)
            egc = jnp.exp(gc)
            q = qn_sc[hh, rows, :]
            k = kn_sc[hh, rows, :]
            v = vn_sc[hh, rows, :]
            kbeta = k * beta
            vbeta = v * beta
            qn_sc[hh, rows, :] = q * egc
            kdec = k * jnp.exp(gl - gc)
            kb16 = k.astype(BF16)
            for cc in range(DELTA_P1_CHUNKS):
                sl = slice(cc * c, (cc + 1) * c)
                n = DELTA_P1_CHUNKS * i + cc
                kdt_sc[hh, n] = kdec[sl].T.astype(BF16)
                egl_sc[hh, pl.ds(n, 1), :] = jnp.broadcast_to(jnp.exp(gl[cc * c:cc * c + 1]), (1, dk))
                diff = gc[sl] - gcr_sc[hh, pl.ds(n, 1), :]
                e = jnp.exp(jnp.where(causal, diff, 0.0))
                k2 = jnp.concatenate([kb16[sl], kb16[sl]], axis=0)
                lhs = jnp.concatenate([kbeta[sl], q[sl]], axis=0).astype(BF16)
                rhs = jnp.concatenate([vbeta[sl], kbeta[sl] * egc[sl]], axis=1)
                chains.append(dict(hh=hh, row=r0 + cc * c, e=e, k2=k2, lhs=lhs, rhs=rhs))

        for ch in chains:
            kq = lax.dot_general(ch["lhs"], ch["k2"], (((1,), (1,)), ((), ())),
                                 preferred_element_type=F32)
            ch["a"] = jnp.where(strict, kq[:c] * ch["e"], 0.0)
            attn_sc[ch["hh"], pl.ds(ch["row"], c), :] = (
                jnp.where(causal, kq[c:] * ch["e"], 0.0)[:, :c].astype(BF16))
        for ch in chains:
            ch["inv"] = eye - ch["a"]
            p_l, p_r = _stack_parts(ch["a"], lo_half)
            ch["p"] = _dot16(p_l, p_r)
        for _ in range(4):
            for ch in chains:
                p_l, p_r = _stack_parts(ch["p"], lo_half)
                inv_l, _ = _stack_parts(ch["inv"], lo_half, want_r=False)
                res = _dot16(jnp.concatenate([p_l, inv_l], axis=0), p_r)
                ch["p"] = res[:c]
                ch["inv"] = ch["inv"] + res[c:]
        for ch in chains:
            inv_l, _ = _stack_parts(ch["inv"], lo_half, want_r=False)
            _, p_r = _stack_parts(ch["p"], lo_half, want_l=False)
            ch["inv"] = ch["inv"] + _dot16(inv_l, p_r)
        for ch in chains:
            inv_l, _ = _stack_parts(ch["inv"], lo_half, want_r=False)
            _, rhs_r = _stack_parts(ch["rhs"], None, want_l=False)
            sol = _dot16(inv_l, rhs_r)
            vn_sc[ch["hh"], pl.ds(ch["row"], c), :] = sol[:, :dk]
            w_sc[ch["hh"], pl.ds(ch["row"], c), :] = sol[:, dk:].astype(BF16)
        return carry

    lax.fori_loop(0, n_chunks // DELTA_P1_CHUNKS, phase1, 0)

    state_sc[...] = jnp.zeros(state_sc.shape, F32)
    nw = nw_ref[...]

    def phase2(n, carry):
        r0 = pl.multiple_of(n * c, c)
        rows = pl.ds(r0, c)
        heads = range(hps)
        st = [state_sc[hh] for hh in heads]
        st16 = [s_.astype(BF16) for s_ in st]
        lhs = [jnp.concatenate([w_sc[hh, rows, :], qn_sc[hh, rows, :].astype(BF16)], axis=0)
               for hh in heads]
        ws = [_dot16(lhs[hh], st16[hh]) for hh in heads]
        vn16 = [(vn_sc[hh, rows, :] - ws[hh][:c]).astype(BF16) for hh in heads]
        av = [_dot16(attn_sc[hh, rows, :], vn16[hh]) for hh in heads]
        kv = [_dot16(kdt_sc[hh, n], vn16[hh]) for hh in heads]
        for hh in heads:
            state_sc[hh] = st[hh] * egl_sc[hh, pl.ds(n, 1), :] + kv[hh]
            out = ws[hh][c:] + av[hh]
            o = out * lax.rsqrt(jnp.mean(out * out, axis=1, keepdims=True) + RMS_EPS) * nw
            z = z_ref[rows, hh * dk:(hh + 1) * dk]
            o_ref[rows, hh * dk:(hh + 1) * dk] = (o * (z * _sigmoid(z))).astype(o_ref.dtype)
        return carry

    lax.fori_loop(0, n_chunks, phase2, 0)


def deltanet(qkv_b, z_b, scal, scal_t, conv_w, a_log, dt_bias, gnorm_w, bsz, seq):
    t = bsz * seq
    hps = DELTA_HEADS_PER_STEP
    n_groups = B_HEADS // hps
    wg = hps * B_HEAD_DIM
    c = DELTA_CHUNK
    n_chunks = seq // c
    st3 = scal_t.reshape(16, t // c, c)
    par = jnp.zeros((8, LANES), F32)
    par = par.at[0, B_HEADS:2 * B_HEADS].set(a_log).at[1, B_HEADS:2 * B_HEADS].set(dt_bias)
    part = jnp.stack([a_log, dt_bias]).astype(F32)
    blk = lambda off: pl.BlockSpec((seq, wg), lambda b, g: (b, off + g))
    wblk = lambda off: pl.BlockSpec((CONV_WIDTH, wg), lambda b, g: (0, off + g))
    sc = lambda shape, dt: pltpu.VMEM(shape, dt)
    return pl.pallas_call(
        functools.partial(_delta_kernel, seq=seq, hps=hps),
        grid=(bsz, n_groups),
        in_specs=[blk(0), blk(n_groups), blk(2 * n_groups), blk(0),
                  pl.BlockSpec((seq, LANES), lambda b, g: (b, 0)),
                  pl.BlockSpec((16, n_chunks, c), lambda b, g: (0, b, 0)),
                  pl.BlockSpec((8, LANES), lambda b, g: (0, 0)),
                  pl.BlockSpec(memory_space=pltpu.SMEM),
                  wblk(0), wblk(n_groups), wblk(2 * n_groups),
                  pl.BlockSpec((1, B_HEAD_DIM), lambda b, g: (0, 0))],
        out_specs=pl.BlockSpec((seq, wg), lambda b, g: (b, g)),
        out_shape=jax.ShapeDtypeStruct((t, B_WIDTH), BF16),
        scratch_shapes=[sc((CONV_HISTORY_ROWS + CONV_TILE, B_HEAD_DIM), F32),
                        sc((hps, seq, B_HEAD_DIM), F32), sc((hps, seq, B_HEAD_DIM), F32),
                        sc((hps, seq, B_HEAD_DIM), F32),
                        sc((hps, seq, B_HEAD_DIM), BF16),
                        sc((hps, seq, c), BF16), sc((hps, n_chunks, B_HEAD_DIM, c), BF16),
                        sc((hps, n_chunks, 2 * c), F32), sc((hps, n_chunks, B_HEAD_DIM), F32),
                        sc((hps, B_HEAD_DIM, B_HEAD_DIM), F32)],
        compiler_params=_params(2),
        name="deltanet",
    )(qkv_b, qkv_b, qkv_b, z_b, scal, st3, par, part,
      conv_w, conv_w, conv_w, gnorm_w.reshape(1, B_HEAD_DIM))


SUBLANES = 8
TOKEN_TILE_ROWS = D_MODEL // LANES
assert TOKEN_TILE_ROWS == SUBLANES


def _store_token_tiles(ref, x, n_tok):
    for j in range(TOKEN_TILE_ROWS):
        ref[pl.ds(j, n_tok, stride=TOKEN_TILE_ROWS), :] = x[:, j * LANES:(j + 1) * LANES]


def _load_token_tiles(ref, n_tok, lead=None):
    idx = lambda j: (pl.ds(j, n_tok, stride=TOKEN_TILE_ROWS), slice(None))
    if lead is None:
        return [ref[idx(j)] for j in range(TOKEN_TILE_ROWS)]
    return [ref[(lead,) + idx(j)] for j in range(TOKEN_TILE_ROWS)]


def _layer_norm(h, g, b):
    mu = jnp.mean(h, axis=1, keepdims=True)
    hc = h - mu
    var = jnp.mean(hc * hc, axis=1, keepdims=True)
    return hc * lax.rsqrt(var + LN_EPS) * g + b


def _mix_route_kernel(ya_ref, yb_ref, x_ref, p_ref, woa_ref, wob_ref, g1_ref, b1_ref,
                      wr_ref, br_ref, wpg_ref, bpg_ref, wpp_ref,
                      x1_ref, res_ref, route_ref, gate_ref, cnt_ref, run_sc, wr2_sc, *, tm):
    i = pl.program_id(0)

    @pl.when(i == 0)
    def _():
        run_sc[...] = jnp.zeros(run_sc.shape, F32)
        w_hi, w_lo = _split2(wr_ref[...])
        wr2_sc[:, :LANES] = w_hi
        wr2_sc[:, LANES:] = w_lo

    mix = (jnp.dot(ya_ref[...].astype(BF16), woa_ref[...], preferred_element_type=F32)
           + jnp.dot(yb_ref[...].astype(BF16), wob_ref[...], preferred_element_type=F32))
    x1 = _layer_norm(DEEPNORM_ALPHA * x_ref[...] + mix, g1_ref[...], b1_ref[...])
    _store_token_tiles(x1_ref, x1, tm)
    x1b = x1.astype(BF16)

    x1l = (x1 - x1b.astype(F32)).astype(BF16)
    r_hi = jnp.dot(x1b, wr2_sc[...], preferred_element_type=F32)
    r_lo = jnp.dot(x1l, wr2_sc[:, :LANES], preferred_element_type=F32)
    logits = r_hi[:, :LANES] + (r_hi[:, LANES:] + r_lo) + br_ref[...]

    pgate = _sigmoid(jnp.dot(x1b, wpg_ref[...], preferred_element_type=F32) + bpg_ref[...])
    proj = jnp.dot(p_ref[...].astype(BF16), wpp_ref[...], preferred_element_type=F32)
    res_ref[...] = DEEPNORM_ALPHA * x1 + pgate * proj

    lane = lax.broadcasted_iota(jnp.int32, (tm, LANES), 1)
    cur = jnp.where(lane < N_EXPERTS, logits, -jnp.inf)
    vals, hots = [], []
    for _ in range(TOP_K):
        m = jnp.max(cur, axis=1, keepdims=True)
        idx = jnp.min(jnp.where(cur == m, lane, LANES), axis=1, keepdims=True)
        hot = lane == idx
        cur = jnp.where(hot, -jnp.inf, cur)
        vals.append(m)
        hots.append((hot, idx))
    exps = [jnp.exp(v - vals[0]) for v in vals]
    den = exps[0] + exps[1] + exps[2] + exps[3]

    member = jnp.zeros((tm, LANES), F32)
    for hot, _ in hots:
        member = member + jnp.where(hot, 1.0, 0.0)
    rr = lax.broadcasted_iota(jnp.int32, (tm, tm), 0)
    cc = lax.broadcasted_iota(jnp.int32, (tm, tm), 1)
    before = jnp.where(cc < rr, 1.0, 0.0).astype(BF16)
    prior = jnp.dot(before, member.astype(BF16), preferred_element_type=F32) + run_sc[0:1, :]
    route = jnp.zeros((tm, LANES), jnp.int32)
    gate = jnp.zeros((tm, LANES), F32)
    for kk, (hot, idx) in enumerate(hots):
        rank = jnp.sum(jnp.where(hot, prior, 0.0), axis=1, keepdims=True).astype(jnp.int32)
        route = jnp.where(lane == kk, idx, route)
        route = jnp.where(lane == TOP_K + kk, rank, route)
        gate = jnp.where(lane == kk, exps[kk] / den, gate)
    route_ref[...] = route
    gate_ref[...] = gate
    run_sc[...] = run_sc[...] + jnp.sum(member, axis=0, keepdims=True)
    cnt_ref[...] = run_sc[...]


def mix_route(ya, yb, x2d, p2d, w_out, ln1_g, ln1_b, w_router, b_router, w_ple_gate, b_ple_gate,
              w_ple_proj, tm=256):
    t, d = x2d.shape
    woa = w_out[:A_WIDTH].astype(BF16)
    wob = w_out[A_WIDTH:].astype(BF16)
    wr = jnp.pad(w_router, ((0, 0), (0, LANES - N_EXPERTS)))
    br = jnp.pad(b_router, (0, LANES - N_EXPERTS)).reshape(1, LANES)
    row = lambda w: pl.BlockSpec((tm, w), lambda i: (i, 0))
    full = lambda a: pl.BlockSpec(a.shape, lambda i: (0,) * a.ndim)
    ops = [woa, wob, ln1_g.reshape(1, d), ln1_b.reshape(1, d), wr, br,
           w_ple_gate.astype(BF16), b_ple_gate.reshape(1, d), w_ple_proj.astype(BF16)]
    return pl.pallas_call(
        functools.partial(_mix_route_kernel, tm=tm),
        grid=(t // tm,),
        in_specs=[row(A_WIDTH), row(B_WIDTH), row(d), row(PLE_DIM)] + [full(a) for a in ops],
        out_specs=[pl.BlockSpec((tm * TOKEN_TILE_ROWS, LANES), lambda i: (i, 0)),
                   row(d), row(LANES), row(LANES),
                   pl.BlockSpec((8, LANES), lambda i: (0, 0))],
        out_shape=[jax.ShapeDtypeStruct((t * TOKEN_TILE_ROWS, LANES), F32),
                   jax.ShapeDtypeStruct((t, d), F32),
                   jax.ShapeDtypeStruct((t, LANES), jnp.int32),
                   jax.ShapeDtypeStruct((t, LANES), F32),
                   jax.ShapeDtypeStruct((8, LANES), F32)],
        scratch_shapes=[pltpu.VMEM((8, LANES), F32), pltpu.VMEM((d, 2 * LANES), BF16)],
        compiler_params=_params(),
        name="mix_route",
    )(ya, yb, x2d, p2d, *ops)


MOE_TILE = 512
DISPATCH_TM = 512
COMBINE_TM = 256
DMA_ROWS_PER_ITER = 4
DMA_PRIORITIES = 2


def _dispatch_kernel(pstart_ref, pend_ref, slot_ref, x_ref, xb_ref, zero_sc, sem, zsem, *, tm):
    i = pl.program_id(0)
    tr = TOKEN_TILE_ROWS
    tile_rows = MOE_TILE * tr

    def zero_copy(e):
        off = pl.multiple_of((pend_ref[e] - MOE_TILE) * tr, tile_rows)
        return pltpu.make_async_copy(zero_sc, xb_ref.at[pl.ds(off, tile_rows), :], zsem)

    @pl.when(i == 0)
    def _():
        zero_sc[...] = jnp.zeros(zero_sc.shape, zero_sc.dtype)
        for e in range(N_EXPERTS):
            @pl.when(pend_ref[e] > pstart_ref[e])
            def _():
                zero_copy(e).start()
        for e in range(N_EXPERTS):
            @pl.when(pend_ref[e] > pstart_ref[e])
            def _():
                zero_copy(e).wait()

        n_tiles = xb_ref.shape[0] // tile_rows
        first_unused = pend_ref[N_EXPERTS - 1] // MOE_TILE

        def tail_copy(j):
            off = pl.multiple_of(j * tile_rows, tile_rows)
            return pltpu.make_async_copy(zero_sc, xb_ref.at[pl.ds(off, tile_rows), :], zsem)

        def tail_start(j, c):
            tail_copy(j).start()
            return c

        def tail_wait(j, c):
            tail_copy(j).wait()
            return c

        lax.fori_loop(first_unused, n_tiles, tail_start, 0)
        lax.fori_loop(first_unused, n_tiles, tail_wait, 0)

    def token_copy(tok, slot):
        src = pl.multiple_of(tok * tr, tr)
        dst = pl.multiple_of(slot * tr, tr)
        return pltpu.make_async_copy(x_ref.at[pl.ds(src, tr), :], xb_ref.at[pl.ds(dst, tr), :], sem)

    def start(g, c):
        toks = [g * DMA_ROWS_PER_ITER + u for u in range(DMA_ROWS_PER_ITER)]
        slots = [[slot_ref[r * TOP_K + kk] for kk in range(TOP_K)] for r in toks]
        for r, row_slots in zip(toks, slots):
            for kk, slot in enumerate(row_slots):
                token_copy(r, slot).start(priority=kk % DMA_PRIORITIES)
        return c

    def wait(r, c):
        for kk in range(TOP_K):
            token_copy(0, 0).wait()
        return c

    lax.fori_loop(0, tm // DMA_ROWS_PER_ITER, start, 0)
    lax.fori_loop(0, tm, wait, 0)


def dispatch(x1_tiles, slot_flat, pad_start, pad_end, n_slots, tm=DISPATCH_TM):
    t = x1_tiles.shape[0] // TOKEN_TILE_ROWS
    return pl.pallas_call(
        functools.partial(_dispatch_kernel, tm=tm),
        grid_spec=pltpu.PrefetchScalarGridSpec(
            num_scalar_prefetch=2,
            grid=(t // tm,),
            in_specs=[pl.BlockSpec((tm * TOP_K,), lambda i, ps, pe: (i,), memory_space=pltpu.SMEM),
                      pl.BlockSpec((tm * TOKEN_TILE_ROWS, LANES), lambda i, ps, pe: (i, 0))],
            out_specs=pl.BlockSpec(memory_space=pl.ANY),
            scratch_shapes=[pltpu.VMEM((MOE_TILE * TOKEN_TILE_ROWS, LANES), F32),
                            pltpu.SemaphoreType.DMA(()), pltpu.SemaphoreType.DMA(())]),
        out_shape=jax.ShapeDtypeStruct((n_slots * TOKEN_TILE_ROWS, LANES), F32),
        compiler_params=_params(),
        name="dispatch",
    )(pad_start, pad_end, slot_flat, x1_tiles)


EXPERT_CAST_ROWS = 256
EXPERT_SUBTILES = 1


def _expert_kernel(be_ref, nu_ref, x_ref, wgu_ref, bgu_ref, wd_ref, bd_ref, y_ref, wgu_sc, wd_sc):
    i = pl.program_id(0)
    used = i < nu_ref[0]
    new_expert = jnp.logical_or(i == 0, be_ref[i] != be_ref[jnp.maximum(i - 1, 0)])

    @pl.when(jnp.logical_and(used, new_expert))
    def _():
        def cast(j, c):
            r = pl.ds(pl.multiple_of(j * EXPERT_CAST_ROWS, EXPERT_CAST_ROWS), EXPERT_CAST_ROWS)
            wgu_sc[r, :] = wgu_ref[0, r, :].astype(BF16)
            wd_sc[r, :] = wd_ref[0, r, :].astype(BF16)
            return c
        lax.fori_loop(0, D_MODEL // EXPERT_CAST_ROWS, cast, 0)

    @pl.when(used)
    def _():
        sub = MOE_TILE // EXPERT_SUBTILES
        tr = TOKEN_TILE_ROWS
        nchunk = 256
        for s in range(EXPERT_SUBTILES):
            base = s * sub * tr
            xc = [x_ref[pl.ds(base + j, sub, stride=tr), :].astype(BF16) for j in range(tr)]
            xb = jnp.concatenate(xc, axis=1)
            acts = []
            for c0 in range(0, D_EXPERT, nchunk):
                if c0 == 0:
                    half = D_MODEL // 2
                    xs = (jnp.concatenate(xc[:tr // 2], axis=1), jnp.concatenate(xc[tr // 2:], axis=1))
                    g = sum(jnp.dot(xs[h], wgu_sc[h * half:(h + 1) * half, c0:c0 + nchunk],
                                    preferred_element_type=F32) for h in range(2))
                    u = sum(jnp.dot(xs[h], wgu_sc[h * half:(h + 1) * half,
                                                  D_EXPERT + c0:D_EXPERT + c0 + nchunk],
                                    preferred_element_type=F32) for h in range(2))
                else:
                    g = jnp.dot(xb, wgu_sc[:, c0:c0 + nchunk], preferred_element_type=F32)
                    u = jnp.dot(xb, wgu_sc[:, D_EXPERT + c0:D_EXPERT + c0 + nchunk],
                                preferred_element_type=F32)
                g = g + bgu_ref[0, :, c0:c0 + nchunk]
                u = u + bgu_ref[0, :, D_EXPERT + c0:D_EXPERT + c0 + nchunk]
                g = jnp.minimum(g, SWIGLU_LIMIT)
                u = jnp.clip(u, -SWIGLU_LIMIT, SWIGLU_LIMIT)
                acts.append(((u + 1.0) * (g * _sigmoid(SWIGLU_ALPHA * g))).astype(BF16))
            act = jnp.concatenate(acts, axis=1)
            for n0 in range(0, D_MODEL, nchunk):
                yb = (jnp.dot(act, wd_sc[:, n0:n0 + nchunk], preferred_element_type=F32)
                      + bd_ref[0, :, n0:n0 + nchunk])
                for jj in range(nchunk // LANES):
                    j = n0 // LANES + jj
                    y_ref[pl.ds(base + j, sub, stride=tr), :] = yb[:, jj * LANES:(jj + 1) * LANES]

    @pl.when(jnp.logical_not(used))
    def _():
        y_ref[...] = jnp.zeros(y_ref.shape, F32)


def experts(xb_tiles, block_e, n_used, w_gate_up, b_gate_up, w_down, b_down):
    tile_rows = MOE_TILE * TOKEN_TILE_ROWS
    n_blocks = xb_tiles.shape[0] // tile_rows
    d = D_MODEL
    clamp = lambda i, be, nu: jnp.minimum(i, nu[0] - 1)
    return pl.pallas_call(
        _expert_kernel,
        grid_spec=pltpu.PrefetchScalarGridSpec(
            num_scalar_prefetch=2,
            grid=(n_blocks,),
            in_specs=[pl.BlockSpec((tile_rows, LANES), lambda i, be, nu: (clamp(i, be, nu), 0)),
                      pl.BlockSpec((1, d, 2 * D_EXPERT), lambda i, be, nu: (be[i], 0, 0)),
                      pl.BlockSpec((1, 1, 2 * D_EXPERT), lambda i, be, nu: (be[i], 0, 0)),
                      pl.BlockSpec((1, D_EXPERT, d), lambda i, be, nu: (be[i], 0, 0)),
                      pl.BlockSpec((1, 1, d), lambda i, be, nu: (be[i], 0, 0))],
            out_specs=pl.BlockSpec((tile_rows, LANES), lambda i, be, nu: (i, 0)),
            scratch_shapes=[pltpu.VMEM((d, 2 * D_EXPERT), BF16), pltpu.VMEM((D_EXPERT, d), BF16)]),
        out_shape=jax.ShapeDtypeStruct(xb_tiles.shape, F32),
        compiler_params=_params(),
        name="experts",
    )(block_e, n_used, xb_tiles, w_gate_up, b_gate_up.reshape(N_EXPERTS, 1, -1),
      w_down, b_down.reshape(N_EXPERTS, 1, -1))


def _combine_kernel(slot_ref, slot_next_ref, gate_ref, res_ref, g2_ref, b2_ref,
                    yb_ref, o_ref, buf_sc, sem, *, tm):
    i = pl.program_id(0)
    tr = TOKEN_TILE_ROWS
    cur = lax.rem(i, 2)

    def token_copy(buf, kk, r, slot):
        src = pl.multiple_of(slot * tr, tr)
        dst = pl.multiple_of(r * tr, tr)
        return pltpu.make_async_copy(yb_ref.at[pl.ds(src, tr), :],
                                     buf_sc.at[buf, kk, pl.ds(dst, tr), :], sem.at[buf])

    def issue(sref, buf):
        def start(g, c):
            toks = [g * DMA_ROWS_PER_ITER + u for u in range(DMA_ROWS_PER_ITER)]
            slots = [[sref[r * TOP_K + kk] for kk in range(TOP_K)] for r in toks]
            for r, row_slots in zip(toks, slots):
                for kk, slot in enumerate(row_slots):
                    token_copy(buf, kk, r, slot).start(priority=kk % DMA_PRIORITIES)
            return c
        lax.fori_loop(0, tm // DMA_ROWS_PER_ITER, start, 0)

    @pl.when(i == 0)
    def _():
        issue(slot_ref, 0)

    @pl.when(i + 1 < pl.num_programs(0))
    def _():
        issue(slot_next_ref, 1 - cur)

    def wait(r, c):
        for kk in range(TOP_K):
            token_copy(cur, 0, 0, 0).wait()
        return c

    lax.fori_loop(0, tm, wait, 0)

    gate = gate_ref[...]
    chunks = []
    for j in range(TOKEN_TILE_ROWS):
        hj = res_ref[:, j * LANES:(j + 1) * LANES]
        for kk in range(TOP_K):
            hj = hj + gate[:, kk:kk + 1] * buf_sc[cur, kk, pl.ds(j, tm, stride=tr), :]
        chunks.append(hj)
    mu = sum(jnp.sum(hj, axis=1, keepdims=True) for hj in chunks) * (1.0 / D_MODEL)
    cent = [hj - mu for hj in chunks]
    var = sum(jnp.sum(cj * cj, axis=1, keepdims=True) for cj in cent) * (1.0 / D_MODEL)
    inv = lax.rsqrt(var + LN_EPS)
    for j, cj in enumerate(cent):
        cols = slice(j * LANES, (j + 1) * LANES)
        o_ref[:, cols] = cj * inv * g2_ref[:, cols] + b2_ref[:, cols]


def combine(y_tiles, slot_flat, gates, res, ln2_g, ln2_b, tm=COMBINE_TM):
    t, d = res.shape
    n_steps = t // tm
    slot_spec = lambda f: pl.BlockSpec((tm * TOP_K,), f, memory_space=pltpu.SMEM)
    return pl.pallas_call(
        functools.partial(_combine_kernel, tm=tm),
        grid=(n_steps,),
        in_specs=[slot_spec(lambda i: (i,)),
                  slot_spec(lambda i: (jnp.minimum(i + 1, n_steps - 1),)),
                  pl.BlockSpec((tm, LANES), lambda i: (i, 0)),
                  pl.BlockSpec((tm, d), lambda i: (i, 0)),
                  pl.BlockSpec((1, d), lambda i: (0, 0)),
                  pl.BlockSpec((1, d), lambda i: (0, 0)),
                  pl.BlockSpec(memory_space=pl.ANY)],
        out_specs=pl.BlockSpec((tm, d), lambda i: (i, 0)),
        scratch_shapes=[pltpu.VMEM((2, TOP_K, tm * TOKEN_TILE_ROWS, LANES), F32),
                        pltpu.SemaphoreType.DMA((2,))],
        out_shape=jax.ShapeDtypeStruct((t, d), F32),
        compiler_params=_params(),
        name="combine",
    )(slot_flat, slot_flat, gates, res, ln2_g.reshape(1, d), ln2_b.reshape(1, d), y_tiles)


def kernel(x, p, w_in, conv_w, a_log, dt_bias, gnorm_w, w_out, ln1_g, ln1_b, w_router, b_router,
           w_gate_up, b_gate_up, w_down, b_down, w_ple_gate, b_ple_gate, w_ple_proj, ln2_g, ln2_b):
    bsz, seq, d = x.shape
    t = bsz * seq
    x2d = x.reshape(t, d)
    qkv_a, qkv_b, z_b, scal, scal_t = in_proj(x2d, w_in[0])
    ya = dilated_attn(qkv_a, bsz, seq)
    yb = deltanet(qkv_b, z_b, scal, scal_t, conv_w[0], a_log[0], dt_bias[0], gnorm_w[0], bsz, seq)
    x1, res, route, gates, counts = mix_route(
        ya, yb, x2d, p[0].reshape(t, PLE_DIM), w_out[0], ln1_g[0], ln1_b[0], w_router[0],
        b_router[0], w_ple_gate[0], b_ple_gate[0], w_ple_proj[0])

    cnt = counts[0, :N_EXPERTS].astype(jnp.int32)
    padded = (cnt + MOE_TILE - 1) // MOE_TILE * MOE_TILE
    pad_end = jnp.cumsum(padded).astype(jnp.int32)
    pad_start = pad_end - padded
    n_blocks = (t * TOP_K) // MOE_TILE + N_EXPERTS
    n_used = (pad_end[-1:] // MOE_TILE).astype(jnp.int32)
    tile_start = jnp.arange(n_blocks, dtype=jnp.int32) * MOE_TILE
    block_e = jnp.minimum(jnp.sum(pad_end[None, :] <= tile_start[:, None], axis=1),
                          N_EXPERTS - 1).astype(jnp.int32)
    experts_iota = jnp.arange(N_EXPERTS, dtype=jnp.int32)
    group_off = jnp.sum(jnp.where(route[:, :TOP_K, None] == experts_iota, pad_start, 0), axis=-1)
    slot_flat = (group_off + route[:, TOP_K:2 * TOP_K]).reshape(-1).astype(jnp.int32)

    xb = dispatch(x1, slot_flat, pad_start, pad_end, n_blocks * MOE_TILE)
    yexp = experts(xb, block_e, n_used, w_gate_up[0], b_gate_up[0], w_down[0], b_down[0])
    out = combine(yexp, slot_flat, gates, res, ln2_g[0], ln2_b[0])
    return out.reshape(bsz, seq, d)
```

```python
import functools
import math

import jax
import jax.numpy as jnp
from jax import lax
from jax.experimental import pallas as pl
from jax.experimental.pallas import tpu as pltpu

LANES = 128
VMEM_LIMIT_BYTES = 56 * 1024 * 1024

D_MODEL = 1024
PLE_DIM = 256
A_HEADS = 8
A_HEAD_DIM = 64
A_WIDTH = A_HEADS * A_HEAD_DIM
DILATIONS = (16, 4, 1)
ATT_BLOCK = 128
B_HEADS = 4
B_HEAD_DIM = 128
B_WIDTH = B_HEADS * B_HEAD_DIM
CONV_WIDTH = 4
DELTA_CHUNK = 64
N_EXPERTS = 32
TOP_K = 4
D_EXPERT = D_MODEL
SWIGLU_LIMIT = 7.0
SWIGLU_ALPHA = 1.702
MOE_BLOCK = 128
LN_EPS = 1e-5
RMS_EPS = 1e-6
L2_EPS = 1e-6
DEPTH = 1
DEEPNORM_ALPHA = (2.0 * DEPTH) ** 0.25

F32 = jnp.float32
BF16 = jnp.bfloat16


def _params(n_parallel_axes=1):
    return pltpu.CompilerParams(
        dimension_semantics=("arbitrary",) * n_parallel_axes,
        vmem_limit_bytes=VMEM_LIMIT_BYTES)


def _in_proj_kernel(x_ref, wa_ref, wb_ref, wz_ref, ws_ref, wst_ref, a_ref, b_ref, z_ref, s_ref, st_ref):
    xb = x_ref[...].astype(BF16)
    a_ref[...] = jnp.dot(xb, wa_ref[...], preferred_element_type=F32)
    b_ref[...] = jnp.dot(xb, wb_ref[...], preferred_element_type=F32).astype(BF16)
    z_ref[...] = jnp.dot(xb, wz_ref[...], preferred_element_type=F32)
    s_ref[...] = jnp.dot(xb, ws_ref[...], preferred_element_type=F32)
    st_ref[...] = lax.dot_general(wst_ref[...], xb, (((1,), (1,)), ((), ())),
                                  preferred_element_type=F32)


def in_proj(x2d, w_in, tm=512):
    t, d = x2d.shape
    na, nb = 3 * A_WIDTH, 3 * B_WIDTH
    wa = w_in[:, :na].astype(BF16)
    wb = w_in[:, na:na + nb].astype(BF16)
    wz = w_in[:, na + nb:na + nb + B_WIDTH].astype(BF16)
    ws = jnp.pad(w_in[:, na + nb + B_WIDTH:], ((0, 0), (0, LANES - 2 * B_HEADS))).astype(BF16)
    wst = ws[:, :16].T
    full = lambda a: pl.BlockSpec(a.shape, lambda i: (0, 0))
    return pl.pallas_call(
        _in_proj_kernel,
        grid=(t // tm,),
        in_specs=[pl.BlockSpec((tm, d), lambda i: (i, 0)),
                  full(wa), full(wb), full(wz), full(ws), full(wst)],
        out_specs=[pl.BlockSpec((tm, na), lambda i: (i, 0)),
                   pl.BlockSpec((tm, nb), lambda i: (i, 0)),
                   pl.BlockSpec((tm, B_WIDTH), lambda i: (i, 0)),
                   pl.BlockSpec((tm, LANES), lambda i: (i, 0)),
                   pl.BlockSpec((16, tm), lambda i: (0, i))],
        out_shape=[jax.ShapeDtypeStruct((t, na), F32),
                   jax.ShapeDtypeStruct((t, nb), BF16),
                   jax.ShapeDtypeStruct((t, B_WIDTH), F32),
                   jax.ShapeDtypeStruct((t, LANES), F32),
                   jax.ShapeDtypeStruct((16, t), F32)],
        compiler_params=_params(),
        name="in_proj",
    )(x2d, wa, wb, wz, ws, wst)


ATTN_UNROLL_DENSE = 3
ATTN_UNROLL_SINGLE = 4


def _attn_kernel(q_ref, k_ref, v_ref, o_ref, m_sc, l_sc, acc_sc, *, seq):
    blk = ATT_BLOCK
    lane = lax.broadcasted_iota(jnp.int32, (blk, LANES), 1)
    head0 = lane < A_HEAD_DIM
    head0_kv = lax.broadcasted_iota(jnp.int32, (2 * blk, LANES), 1) < A_HEAD_DIM
    row = lax.broadcasted_iota(jnp.int32, (blk, blk), 0)
    col = lax.broadcasted_iota(jnp.int32, (blk, blk), 1)
    prev_ok = col >= row
    cur_ok = col <= row
    neg_inf = jnp.float32(-jnp.inf)
    scale = 1.0 / math.sqrt(A_HEAD_DIM)

    def blocks(starts, stride, has_prev, first_branch, last_branch):
        def rows(ref, s0):
            if stride == 1:
                return ref[pl.ds(s0, blk), :]
            return ref[pl.ds(s0, blk, stride=stride), :]

        def put(ref, s0, val):
            if stride == 1:
                ref[pl.ds(s0, blk), :] = val
            else:
                ref[pl.ds(s0, blk, stride=stride), :] = val

        hmask = (head0, jnp.logical_not(head0))
        items = []
        for start, hp in zip(starts, has_prev):
            q = rows(q_ref, start) * scale
            keys = rows(k_ref, start)
            vals = rows(v_ref, start)
            if hp:
                keys = jnp.concatenate([rows(k_ref, start - blk * stride), keys], axis=0)
                vals = jnp.concatenate([rows(v_ref, start - blk * stride), vals], axis=0)
            it = dict(start=start, hp=hp, keys=keys.astype(BF16), vals=vals,
                      qh=[jnp.where(hm, q, 0.0).astype(BF16) for hm in hmask])
            if not first_branch:
                it["m_old"] = rows(m_sc, start)
                it["l_old"] = rows(l_sc, start)
                it["acc_old"] = rows(acc_sc, start)
            items.append(it)

        for it in items:
            it["s"] = [lax.dot_general(qh, it["keys"], (((1,), (1,)), ((), ())),
                                       preferred_element_type=F32) for qh in it["qh"]]
        for it in items:
            ok = jnp.concatenate([prev_ok, cur_ok], axis=1) if it["hp"] else cur_ok
            it["m"], it["sum"], it["p"] = [], [], []
            for h in range(2):
                s = jnp.where(ok, it["s"][h], neg_inf)
                t = jnp.maximum(s[:, :blk], s[:, blk:]) if it["hp"] else s
                if not first_branch:
                    t = jnp.maximum(t, jnp.where(hmask[h], it["m_old"], neg_inf))
                m_h = jnp.max(t, axis=1, keepdims=True)
                p = jnp.exp(s - m_h)
                psum = p[:, :blk] + p[:, blk:] if it["hp"] else p
                it["m"].append(m_h)
                it["sum"].append(jnp.sum(psum, axis=1, keepdims=True))
                it["p"].append(p.astype(BF16))
        for it in items:
            kv0 = head0_kv if it["hp"] else head0
            kvmask = (kv0, jnp.logical_not(kv0))
            it["pv"] = [jnp.dot(it["p"][h], jnp.where(kvmask[h], it["vals"], 0.0).astype(BF16),
                                preferred_element_type=F32) for h in range(2)]
        for it in items:
            m_new = jnp.where(head0, it["m"][0], it["m"][1])
            l_new = jnp.where(head0, it["sum"][0], it["sum"][1])
            acc_new = it["pv"][0] + it["pv"][1]
            if not first_branch:
                alpha = jnp.exp(it["m_old"] - m_new)
                l_new = l_new + alpha * it["l_old"]
                acc_new = acc_new + alpha * it["acc_old"]
            if last_branch:
                put(o_ref, it["start"], acc_new / l_new)
            else:
                put(m_sc, it["start"], m_new)
                put(l_sc, it["start"], l_new)
                put(acc_sc, it["start"], acc_new)

    n_br = len(DILATIONS)
    for bi, dil in enumerate(DILATIONS):
        first, last = bi == 0, bi == n_br - 1
        nblk = seq // dil // blk
        if dil == 1:
            blocks([0], 1, [False], first, last)
            per = ATTN_UNROLL_DENSE
            assert (nblk - 1) % per == 0

            def body(g, c, first=first, last=last, per=per):
                starts = [pl.multiple_of((1 + g * per + j) * blk, blk) for j in range(per)]
                blocks(starts, 1, [True] * per, first, last)
                return c
            lax.fori_loop(0, (nblk - 1) // per, body, 0)
        elif nblk > 1:
            def body(r, c, dil=dil, nblk=nblk, first=first, last=last):
                blocks([r + n * blk * dil for n in range(nblk)], dil,
                       [n > 0 for n in range(nblk)], first, last)
                return c
            lax.fori_loop(0, dil, body, 0)
        else:
            per = ATTN_UNROLL_SINGLE
            assert dil % per == 0

            def body(g, c, dil=dil, first=first, last=last, per=per):
                blocks([g * per + j for j in range(per)], dil, [False] * per, first, last)
                return c
            lax.fori_loop(0, dil // per, body, 0)


def dilated_attn(qkv_a, bsz, seq):
    t = bsz * seq
    n_pairs = A_WIDTH // LANES
    blk = lambda off: pl.BlockSpec((seq, LANES), lambda b, p: (b, off + p))
    return pl.pallas_call(
        functools.partial(_attn_kernel, seq=seq),
        grid=(bsz, n_pairs),
        in_specs=[blk(0), blk(n_pairs), blk(2 * n_pairs)],
        out_specs=pl.BlockSpec((seq, LANES), lambda b, p: (b, p)),
        out_shape=jax.ShapeDtypeStruct((t, A_WIDTH), F32),
        scratch_shapes=[pltpu.VMEM((seq, LANES), F32)] * 3,
        compiler_params=_params(2),
        name="dilated_attn",
    )(qkv_a, qkv_a, qkv_a)


DELTA_HEADS_PER_STEP = 4
DELTA_P1_CHUNKS = 4
CONV_TILE = 256
CONV_HISTORY_ROWS = 16
_HI = lax.Precision.HIGHEST


def _dot_hi(a, b):
    return jnp.dot(a, b, precision=_HI, preferred_element_type=F32)


def _dot16(a, b):
    return jnp.dot(a, b, preferred_element_type=F32)


def _split2(x):
    hi = x.astype(BF16)
    return hi, (x - hi.astype(F32)).astype(BF16)


def _dot3(a, b):
    return _dot16(a[0], b[0]) + (_dot16(a[1], b[0]) + _dot16(a[0], b[1]))


def _split3(x):
    hi = x.astype(BF16)
    r = x - hi.astype(F32)
    mid = r.astype(BF16)
    return hi, mid, (r - mid.astype(F32)).astype(BF16)


def _dot_sel(m01, x):
    hi, mid, lo = _split3(x)
    return _dot16(m01, hi) + (_dot16(m01, mid) + _dot16(m01, lo))


def _dot_sel_r(x, m01):
    hi, mid, lo = _split3(x)
    return _dot16(hi, m01) + (_dot16(mid, m01) + _dot16(lo, m01))


def _stack_parts(x, lo_half, want_l=True, want_r=True):
    xh = x.astype(BF16)
    xh_f = xh.astype(F32)
    xl_f = x - xh_f
    left = right = None
    if want_l:
        mix = jnp.where(lo_half, xh_f, xl_f).astype(BF16)
        left = jnp.concatenate([mix, mix], axis=1)
    if want_r:
        xl = xl_f.astype(BF16)
        right = jnp.concatenate([xh, xh, xl, xl], axis=0)
    return left, right


def _softplus(x):
    return jnp.maximum(x, 0.0) + jnp.log(1.0 + jnp.exp(-jnp.abs(x)))


def _sigmoid(x):
    return 1.0 / (1.0 + jnp.exp(-x))


def _delta_kernel(q_ref, k_ref, v_ref, z_ref, sc_ref, st_ref, par_ref, part_ref,
                  wq_ref, wk_ref, wv_ref, nw_ref, o_ref,
                  conv_sc, qn_sc, kn_sc, vn_sc, w_sc, attn_sc, kdt_sc, gcr_sc, egl_sc,
                  state_sc, *, seq, hps):
    c = DELTA_CHUNK
    dk = B_HEAD_DIM
    n_chunks = seq // c
    tile = DELTA_P1_CHUNKS * c
    group = pl.program_id(1)
    lane = lax.broadcasted_iota(jnp.int32, (tile, LANES), 1)
    r2 = lax.broadcasted_iota(jnp.int32, (tile, tile), 0)
    c2 = lax.broadcasted_iota(jnp.int32, (tile, tile), 1)
    same_chunk = (r2 // c) == (c2 // c)
    cum_mat = jnp.where(same_chunk & (c2 <= r2), 1.0, 0.0).astype(BF16)
    tot_mat = jnp.where(same_chunk, 1.0, 0.0).astype(BF16)
    ri = lax.broadcasted_iota(jnp.int32, (c, 2 * c), 0)
    li = lax.broadcasted_iota(jnp.int32, (c, 2 * c), 1)
    lo_half = li < c
    ci = jnp.where(lo_half, li, li - c)
    causal = ci <= ri
    strict = ci < ri
    eye = jnp.where(ci == ri, 1.0, 0.0).astype(F32)
    upper = jnp.where(ri <= ci, 1.0, 0.0).astype(BF16)

    hist = CONV_HISTORY_ROWS
    for hh in range(hps):
        cols = slice(hh * dk, (hh + 1) * dk)
        for src, w_ref, dst, kind in ((q_ref, wq_ref, qn_sc, "q"), (k_ref, wk_ref, kn_sc, "k"),
                                      (v_ref, wv_ref, vn_sc, "v")):
            w = w_ref[:, cols]

            def conv_tile(xe, c0, w=w, dst=dst, kind=kind, hh=hh):
                conv_sc[...] = xe
                off = hist - (CONV_WIDTH - 1)
                y = w[0:1, :] * conv_sc[off:off + CONV_TILE, :]
                for j in range(1, CONV_WIDTH):
                    y = y + w[j:j + 1, :] * conv_sc[off + j:off + j + CONV_TILE, :]
                y = y * _sigmoid(y)
                if kind != "v":
                    y = y * lax.rsqrt(jnp.sum(y * y, axis=1, keepdims=True) + L2_EPS)
                if kind == "q":
                    y = y * (B_HEAD_DIM ** -0.5)
                dst[hh, pl.ds(c0, CONV_TILE), :] = y

            conv_tile(jnp.concatenate([jnp.zeros((hist, dk), F32),
                                       src[0:CONV_TILE, cols].astype(F32)], axis=0), 0)

            def conv_body(ti, carry, src=src, cols=cols, conv_tile=conv_tile):
                c0 = pl.multiple_of(ti * CONV_TILE, CONV_TILE)
                conv_tile(src[pl.ds(c0 - hist, hist + CONV_TILE), cols].astype(F32), c0)
                return carry

            lax.fori_loop(1, seq // CONV_TILE, conv_body, 0)

    for hh in range(hps):
        h = group * hps + hh
        a_row = st_ref[B_HEADS + h]
        g_row = -jnp.exp(part_ref[0, h]) * _softplus(a_row + part_ref[1, h])
        gcr_sc[hh] = _dot_sel_r(g_row, upper)

    alog_l = par_ref[0:1, :]
    dtb_l = par_ref[1:2, :]

    def phase1(i, carry):
        r0 = pl.multiple_of(i * tile, tile)
        rows = pl.ds(r0, tile)
        s = sc_ref[rows, :]
        beta_all = _sigmoid(s)
        g_all = -jnp.exp(alog_l) * _softplus(s + dtb_l)
        gc_all = _dot_sel(cum_mat, g_all)
        gl_all = _dot_sel(tot_mat, g_all)
        chains = []
        for hh in range(hps):
            h = group * hps + hh
            pick = lambda a, l: jnp.sum(jnp.where(lane == l, a, 0.0), axis=1, keepdims=True)
            beta = pick(beta_all, h)
            gc = pick(gc_all, B_HEADS + h)
            gl = pick(gl_all, B_HEADS + h)
            egc = jnp.exp(gc)
            q = qn_sc[hh, rows, :]
            k = kn_sc[hh, rows, :]
            v = vn_sc[hh, rows, :]
            kbeta = k * beta
            vbeta = v * beta
            qn_sc[hh, rows, :] = q * egc
            kdec = k * jnp.exp(gl - gc)
            kb16 = k.astype(BF16)
            for cc in range(DELTA_P1_CHUNKS):
                sl = slice(cc * c, (cc + 1) * c)
                n = DELTA_P1_CHUNKS * i + cc
                kdt_sc[hh, n] = kdec[sl].T.astype(BF16)
                egl_sc[hh, pl.ds(n, 1), :] = jnp.broadcast_to(jnp.exp(gl[cc * c:cc * c + 1]), (1, dk))
                diff = gc[sl] - gcr_sc[hh, pl.ds(n, 1), :]
                e = jnp.exp(jnp.where(causal, diff, 0.0))
                k2 = jnp.concatenate([kb16[sl], kb16[sl]], axis=0)
                lhs = jnp.concatenate([kbeta[sl], q[sl]], axis=0).astype(BF16)
                rhs = jnp.concatenate([vbeta[sl], kbeta[sl] * egc[sl]], axis=1)
                chains.append(dict(hh=hh, row=r0 + cc * c, e=e, k2=k2, lhs=lhs, rhs=rhs))

        for ch in chains:
            kq = lax.dot_general(ch["lhs"], ch["k2"], (((1,), (1,)), ((), ())),
                                 preferred_element_type=F32)
            ch["a"] = jnp.where(strict, kq[:c] * ch["e"], 0.0)
            attn_sc[ch["hh"], pl.ds(ch["row"], c), :] = (
                jnp.where(causal, kq[c:] * ch["e"], 0.0)[:, :c].astype(BF16))
        for ch in chains:
            ch["inv"] = eye - ch["a"]
            p_l, p_r = _stack_parts(ch["a"], lo_half)
            ch["p"] = _dot16(p_l, p_r)
        for _ in range(4):
            for ch in chains:
                p_l, p_r = _stack_parts(ch["p"], lo_half)
                inv_l, _ = _stack_parts(ch["inv"], lo_half, want_r=False)
                res = _dot16(jnp.concatenate([p_l, inv_l], axis=0), p_r)
                ch["p"] = res[:c]
                ch["inv"] = ch["inv"] + res[c:]
        for ch in chains:
            inv_l, _ = _stack_parts(ch["inv"], lo_half, want_r=False)
            _, p_r = _stack_parts(ch["p"], lo_half, want_l=False)
            ch["inv"] = ch["inv"] + _dot16(inv_l, p_r)
        for ch in chains:
            inv_l, _ = _stack_parts(ch["inv"], lo_half, want_r=False)
            _, rhs_r = _stack_parts(ch["rhs"], None, want_l=False)
            sol = _dot16(inv_l, rhs_r)
            vn_sc[ch["hh"], pl.ds(ch["row"], c), :] = sol[:, :dk]
            w_sc[ch["hh"], pl.ds(ch["row"], c), :] = sol[:, dk:].astype(BF16)
        return carry

    lax.fori_loop(0, n_chunks // DELTA_P1_CHUNKS, phase1, 0)

    state_sc[...] = jnp.zeros(state_sc.shape, F32)
    nw = nw_ref[...]

    def phase2(n, carry):
        r0 = pl.multiple_of(n * c, c)
        rows = pl.ds(r0, c)
        heads = range(hps)
        st = [state_sc[hh] for hh in heads]
        st16 = [s_.astype(BF16) for s_ in st]
        lhs = [jnp.concatenate([w_sc[hh, rows, :], qn_sc[hh, rows, :].astype(BF16)], axis=0)
               for hh in heads]
        ws = [_dot16(lhs[hh], st16[hh]) for hh in heads]
        vn16 = [(vn_sc[hh, rows, :] - ws[hh][:c]).astype(BF16) for hh in heads]
        av = [_dot16(attn_sc[hh, rows, :], vn16[hh]) for hh in heads]
        kv = [_dot16(kdt_sc[hh, n], vn16[hh]) for hh in heads]
        for hh in heads:
            state_sc[hh] = st[hh] * egl_sc[hh, pl.ds(n, 1), :] + kv[hh]
            out = ws[hh][c:] + av[hh]
            o = out * lax.rsqrt(jnp.mean(out * out, axis=1, keepdims=True) + RMS_EPS) * nw
            z = z_ref[rows, hh * dk:(hh + 1) * dk]
            o_ref[rows, hh * dk:(hh + 1) * dk] = (o * (z * _sigmoid(z))).astype(o_ref.dtype)
        return carry

    lax.fori_loop(0, n_chunks, phase2, 0)


def deltanet(qkv_b, z_b, scal, scal_t, conv_w, a_log, dt_bias, gnorm_w, bsz, seq):
    t = bsz * seq
    hps = DELTA_HEADS_PER_STEP
    n_groups = B_HEADS // hps
    wg = hps * B_HEAD_DIM
    c = DELTA_CHUNK
    n_chunks = seq // c
    st3 = scal_t.reshape(16, t // c, c)
    par = jnp.zeros((8, LANES), F32)
    par = par.at[0, B_HEADS:2 * B_HEADS].set(a_log).at[1, B_HEADS:2 * B_HEADS].set(dt_bias)
    part = jnp.stack([a_log, dt_bias]).astype(F32)
    blk = lambda off: pl.BlockSpec((seq, wg), lambda b, g: (b, off + g))
    wblk = lambda off: pl.BlockSpec((CONV_WIDTH, wg), lambda b, g: (0, off + g))
    sc = lambda shape, dt: pltpu.VMEM(shape, dt)
    return pl.pallas_call(
        functools.partial(_delta_kernel, seq=seq, hps=hps),
        grid=(bsz, n_groups),
        in_specs=[blk(0), blk(n_groups), blk(2 * n_groups), blk(0),
                  pl.BlockSpec((seq, LANES), lambda b, g: (b, 0)),
                  pl.BlockSpec((16, n_chunks, c), lambda b, g: (0, b, 0)),
                  pl.BlockSpec((8, LANES), lambda b, g: (0, 0)),
                  pl.BlockSpec(memory_space=pltpu.SMEM),
                  wblk(0), wblk(n_groups), wblk(2 * n_groups),
                  pl.BlockSpec((1, B_HEAD_DIM), lambda b, g: (0, 0))],
        out_specs=pl.BlockSpec((seq, wg), lambda b, g: (b, g)),
        out_shape=jax.ShapeDtypeStruct((t, B_WIDTH), BF16),
        scratch_shapes=[sc((CONV_HISTORY_ROWS + CONV_TILE, B_HEAD_DIM), F32),
                        sc((hps, seq, B_HEAD_DIM), F32), sc((hps, seq, B_HEAD_DIM), F32),
                        sc((hps, seq, B_HEAD_DIM), F32),
                        sc((hps, seq, B_HEAD_DIM), BF16),
                        sc((hps, seq, c), BF16), sc((hps, n_chunks, B_HEAD_DIM, c), BF16),
                        sc((hps, n_chunks, 2 * c), F32), sc((hps, n_chunks, B_HEAD_DIM), F32),
                        sc((hps, B_HEAD_DIM, B_HEAD_DIM), F32)],
        compiler_params=_params(2),
        name="deltanet",
    )(qkv_b, qkv_b, qkv_b, z_b, scal, st3, par, part,
      conv_w, conv_w, conv_w, gnorm_w.reshape(1, B_HEAD_DIM))


SUBLANES = 8
TOKEN_TILE_ROWS = D_MODEL // LANES
assert TOKEN_TILE_ROWS == SUBLANES


def _store_token_tiles(ref, x, n_tok):
    for j in range(TOKEN_TILE_ROWS):
        ref[pl.ds(j, n_tok, stride=TOKEN_TILE_ROWS), :] = x[:, j * LANES:(j + 1) * LANES]


def _load_token_tiles(ref, n_tok, lead=None):
    idx = lambda j: (pl.ds(j, n_tok, stride=TOKEN_TILE_ROWS), slice(None))
    if lead is None:
        return [ref[idx(j)] for j in range(TOKEN_TILE_ROWS)]
    return [ref[(lead,) + idx(j)] for j in range(TOKEN_TILE_ROWS)]


def _layer_norm(h, g, b):
    mu = jnp.mean(h, axis=1, keepdims=True)
    hc = h - mu
    var = jnp.mean(hc * hc, axis=1, keepdims=True)
    return hc * lax.rsqrt(var + LN_EPS) * g + b


def _mix_route_kernel(ya_ref, yb_ref, x_ref, p_ref, woa_ref, wob_ref, g1_ref, b1_ref,
                      wr_ref, br_ref, wpg_ref, bpg_ref, wpp_ref,
                      x1_ref, res_ref, route_ref, gate_ref, cnt_ref, run_sc, wr2_sc, *, tm):
    i = pl.program_id(0)

    @pl.when(i == 0)
    def _():
        run_sc[...] = jnp.zeros(run_sc.shape, F32)
        w_hi, w_lo = _split2(wr_ref[...])
        wr2_sc[:, :LANES] = w_hi
        wr2_sc[:, LANES:] = w_lo

    mix = (jnp.dot(ya_ref[...].astype(BF16), woa_ref[...], preferred_element_type=F32)
           + jnp.dot(yb_ref[...].astype(BF16), wob_ref[...], preferred_element_type=F32))
    x1 = _layer_norm(DEEPNORM_ALPHA * x_ref[...] + mix, g1_ref[...], b1_ref[...])
    _store_token_tiles(x1_ref, x1, tm)
    x1b = x1.astype(BF16)

    x1l = (x1 - x1b.astype(F32)).astype(BF16)
    r_hi = jnp.dot(x1b, wr2_sc[...], preferred_element_type=F32)
    r_lo = jnp.dot(x1l, wr2_sc[:, :LANES], preferred_element_type=F32)
    logits = r_hi[:, :LANES] + (r_hi[:, LANES:] + r_lo) + br_ref[...]

    pgate = _sigmoid(jnp.dot(x1b, wpg_ref[...], preferred_element_type=F32) + bpg_ref[...])
    proj = jnp.dot(p_ref[...].astype(BF16), wpp_ref[...], preferred_element_type=F32)
    res_ref[...] = DEEPNORM_ALPHA * x1 + pgate * proj

    lane = lax.broadcasted_iota(jnp.int32, (tm, LANES), 1)
    cur = jnp.where(lane < N_EXPERTS, logits, -jnp.inf)
    vals, hots = [], []
    for _ in range(TOP_K):
        m = jnp.max(cur, axis=1, keepdims=True)
        idx = jnp.min(jnp.where(cur == m, lane, LANES), axis=1, keepdims=True)
        hot = lane == idx
        cur = jnp.where(hot, -jnp.inf, cur)
        vals.append(m)
        hots.append((hot, idx))
    exps = [jnp.exp(v - vals[0]) for v in vals]
    den = exps[0] + exps[1] + exps[2] + exps[3]

    member = jnp.zeros((tm, LANES), F32)
    for hot, _ in hots:
        member = member + jnp.where(hot, 1.0, 0.0)
    rr = lax.broadcasted_iota(jnp.int32, (tm, tm), 0)
    cc = lax.broadcasted_iota(jnp.int32, (tm, tm), 1)
    before = jnp.where(cc < rr, 1.0, 0.0).astype(BF16)
    prior = jnp.dot(before, member.astype(BF16), preferred_element_type=F32) + run_sc[0:1, :]
    route = jnp.zeros((tm, LANES), jnp.int32)
    gate = jnp.zeros((tm, LANES), F32)
    for kk, (hot, idx) in enumerate(hots):
        rank = jnp.sum(jnp.where(hot, prior, 0.0), axis=1, keepdims=True).astype(jnp.int32)
        route = jnp.where(lane == kk, idx, route)
        route = jnp.where(lane == TOP_K + kk, rank, route)
        gate = jnp.where(lane == kk, exps[kk] / den, gate)
    route_ref[...] = route
    gate_ref[...] = gate
    run_sc[...] = run_sc[...] + jnp.sum(member, axis=0, keepdims=True)
    cnt_ref[...] = run_sc[...]


def mix_route(ya, yb, x2d, p2d, w_out, ln1_g, ln1_b, w_router, b_router, w_ple_gate, b_ple_gate,
              w_ple_proj, tm=512):
    t, d = x2d.shape
    woa = w_out[:A_WIDTH].astype(BF16)
    wob = w_out[A_WIDTH:].astype(BF16)
    wr = jnp.pad(w_router, ((0, 0), (0, LANES - N_EXPERTS)))
    br = jnp.pad(b_router, (0, LANES - N_EXPERTS)).reshape(1, LANES)
    row = lambda w: pl.BlockSpec((tm, w), lambda i: (i, 0))
    full = lambda a: pl.BlockSpec(a.shape, lambda i: (0,) * a.ndim)
    ops = [woa, wob, ln1_g.reshape(1, d), ln1_b.reshape(1, d), wr, br,
           w_ple_gate.astype(BF16), b_ple_gate.reshape(1, d), w_ple_proj.astype(BF16)]
    return pl.pallas_call(
        functools.partial(_mix_route_kernel, tm=tm),
        grid=(t // tm,),
        in_specs=[row(A_WIDTH), row(B_WIDTH), row(d), row(PLE_DIM)] + [full(a) for a in ops],
        out_specs=[pl.BlockSpec((tm * TOKEN_TILE_ROWS, LANES), lambda i: (i, 0)),
                   row(d), row(LANES), row(LANES),
                   pl.BlockSpec((8, LANES), lambda i: (0, 0))],
        out_shape=[jax.ShapeDtypeStruct((t * TOKEN_TILE_ROWS, LANES), F32),
                   jax.ShapeDtypeStruct((t, d), F32),
                   jax.ShapeDtypeStruct((t, LANES), jnp.int32),
                   jax.ShapeDtypeStruct((t, LANES), F32),
                   jax.ShapeDtypeStruct((8, LANES), F32)],
        scratch_shapes=[pltpu.VMEM((8, LANES), F32), pltpu.VMEM((d, 2 * LANES), BF16)],
        compiler_params=_params(),
        name="mix_route",
    )(ya, yb, x2d, p2d, *ops)


MOE_TILE = 512
DISPATCH_TM = 512
COMBINE_TM = 256
DMA_ROWS_PER_ITER = 4
DMA_PRIORITIES = 2


def _dispatch_kernel(pstart_ref, pend_ref, slot_ref, x_ref, xb_ref, zero_sc, sem, zsem, *, tm):
    i = pl.program_id(0)
    tr = TOKEN_TILE_ROWS
    tile_rows = MOE_TILE * tr

    def zero_copy(e):
        off = pl.multiple_of((pend_ref[e] - MOE_TILE) * tr, tile_rows)
        return pltpu.make_async_copy(zero_sc, xb_ref.at[pl.ds(off, tile_rows), :], zsem)

    @pl.when(i == 0)
    def _():
        zero_sc[...] = jnp.zeros(zero_sc.shape, zero_sc.dtype)
        for e in range(N_EXPERTS):
            @pl.when(pend_ref[e] > pstart_ref[e])
            def _():
                zero_copy(e).start()
        for e in range(N_EXPERTS):
            @pl.when(pend_ref[e] > pstart_ref[e])
            def _():
                zero_copy(e).wait()

        n_tiles = xb_ref.shape[0] // tile_rows
        first_unused = pend_ref[N_EXPERTS - 1] // MOE_TILE

        def tail_copy(j):
            off = pl.multiple_of(j * tile_rows, tile_rows)
            return pltpu.make_async_copy(zero_sc, xb_ref.at[pl.ds(off, tile_rows), :], zsem)

        def tail_start(j, c):
            tail_copy(j).start()
            return c

        def tail_wait(j, c):
            tail_copy(j).wait()
            return c

        lax.fori_loop(first_unused, n_tiles, tail_start, 0)
        lax.fori_loop(first_unused, n_tiles, tail_wait, 0)

    def token_copy(tok, slot):
        src = pl.multiple_of(tok * tr, tr)
        dst = pl.multiple_of(slot * tr, tr)
        return pltpu.make_async_copy(x_ref.at[pl.ds(src, tr), :], xb_ref.at[pl.ds(dst, tr), :], sem)

    def start(g, c):
        toks = [g * DMA_ROWS_PER_ITER + u for u in range(DMA_ROWS_PER_ITER)]
        slots = [[slot_ref[r * TOP_K + kk] for kk in range(TOP_K)] for r in toks]
        for r, row_slots in zip(toks, slots):
            for kk, slot in enumerate(row_slots):
                token_copy(r, slot).start(priority=kk % DMA_PRIORITIES)
        return c

    def wait(r, c):
        for kk in range(TOP_K):
            token_copy(0, 0).wait()
        return c

    lax.fori_loop(0, tm // DMA_ROWS_PER_ITER, start, 0)
    lax.fori_loop(0, tm, wait, 0)


def dispatch(x1_tiles, slot_flat, pad_start, pad_end, n_slots, tm=DISPATCH_TM):
    t = x1_tiles.shape[0] // TOKEN_TILE_ROWS
    return pl.pallas_call(
        functools.partial(_dispatch_kernel, tm=tm),
        grid_spec=pltpu.PrefetchScalarGridSpec(
            num_scalar_prefetch=2,
            grid=(t // tm,),
            in_specs=[pl.BlockSpec((tm * TOP_K,), lambda i, ps, pe: (i,), memory_space=pltpu.SMEM),
                      pl.BlockSpec((tm * TOKEN_TILE_ROWS, LANES), lambda i, ps, pe: (i, 0))],
            out_specs=pl.BlockSpec(memory_space=pl.ANY),
            scratch_shapes=[pltpu.VMEM((MOE_TILE * TOKEN_TILE_ROWS, LANES), F32),
                            pltpu.SemaphoreType.DMA(()), pltpu.SemaphoreType.DMA(())]),
        out_shape=jax.ShapeDtypeStruct((n_slots * TOKEN_TILE_ROWS, LANES), F32),
        compiler_params=_params(),
        name="dispatch",
    )(pad_start, pad_end, slot_flat, x1_tiles)


EXPERT_CAST_ROWS = 256
EXPERT_SUBTILES = 1


def _expert_kernel(be_ref, nu_ref, x_ref, wgu_ref, bgu_ref, wd_ref, bd_ref, y_ref, wgu_sc, wd_sc):
    i = pl.program_id(0)
    used = i < nu_ref[0]
    new_expert = jnp.logical_or(i == 0, be_ref[i] != be_ref[jnp.maximum(i - 1, 0)])

    @pl.when(jnp.logical_and(used, new_expert))
    def _():
        def cast(j, c):
            r = pl.ds(pl.multiple_of(j * EXPERT_CAST_ROWS, EXPERT_CAST_ROWS), EXPERT_CAST_ROWS)
            wgu_sc[r, :] = wgu_ref[0, r, :].astype(BF16)
            wd_sc[r, :] = wd_ref[0, r, :].astype(BF16)
            return c
        lax.fori_loop(0, D_MODEL // EXPERT_CAST_ROWS, cast, 0)

    @pl.when(used)
    def _():
        sub = MOE_TILE // EXPERT_SUBTILES
        tr = TOKEN_TILE_ROWS
        nchunk = 256
        for s in range(EXPERT_SUBTILES):
            base = s * sub * tr
            xc = [x_ref[pl.ds(base + j, sub, stride=tr), :].astype(BF16) for j in range(tr)]
            xb = jnp.concatenate(xc, axis=1)
            acts = []
            for c0 in range(0, D_EXPERT, nchunk):
                if c0 == 0:
                    half = D_MODEL // 2
                    xs = (jnp.concatenate(xc[:tr // 2], axis=1), jnp.concatenate(xc[tr // 2:], axis=1))
                    g = sum(jnp.dot(xs[h], wgu_sc[h * half:(h + 1) * half, c0:c0 + nchunk],
                                    preferred_element_type=F32) for h in range(2))
                    u = sum(jnp.dot(xs[h], wgu_sc[h * half:(h + 1) * half,
                                                  D_EXPERT + c0:D_EXPERT + c0 + nchunk],
                                    preferred_element_type=F32) for h in range(2))
                else:
                    g = jnp.dot(xb, wgu_sc[:, c0:c0 + nchunk], preferred_element_type=F32)
                    u = jnp.dot(xb, wgu_sc[:, D_EXPERT + c0:D_EXPERT + c0 + nchunk],
                                preferred_element_type=F32)
                g = g + bgu_ref[0, :, c0:c0 + nchunk]
                u = u + bgu_ref[0, :, D_EXPERT + c0:D_EXPERT + c0 + nchunk]
                g = jnp.minimum(g, SWIGLU_LIMIT)
                u = jnp.clip(u, -SWIGLU_LIMIT, SWIGLU_LIMIT)
                acts.append(((u + 1.0) * (g * _sigmoid(SWIGLU_ALPHA * g))).astype(BF16))
            act = jnp.concatenate(acts, axis=1)
            for n0 in range(0, D_MODEL, nchunk):
                yb = (jnp.dot(act, wd_sc[:, n0:n0 + nchunk], preferred_element_type=F32)
                      + bd_ref[0, :, n0:n0 + nchunk])
                for jj in range(nchunk // LANES):
                    j = n0 // LANES + jj
                    y_ref[pl.ds(base + j, sub, stride=tr), :] = yb[:, jj * LANES:(jj + 1) * LANES]

    @pl.when(jnp.logical_not(used))
    def _():
        y_ref[...] = jnp.zeros(y_ref.shape, F32)


def experts(xb_tiles, block_e, n_used, w_gate_up, b_gate_up, w_down, b_down):
    tile_rows = MOE_TILE * TOKEN_TILE_ROWS
    n_blocks = xb_tiles.shape[0] // tile_rows
    d = D_MODEL
    clamp = lambda i, be, nu: jnp.minimum(i, nu[0] - 1)
    return pl.pallas_call(
        _expert_kernel,
        grid_spec=pltpu.PrefetchScalarGridSpec(
            num_scalar_prefetch=2,
            grid=(n_blocks,),
            in_specs=[pl.BlockSpec((tile_rows, LANES), lambda i, be, nu: (clamp(i, be, nu), 0)),
                      pl.BlockSpec((1, d, 2 * D_EXPERT), lambda i, be, nu: (be[i], 0, 0)),
                      pl.BlockSpec((1, 1, 2 * D_EXPERT), lambda i, be, nu: (be[i], 0, 0)),
                      pl.BlockSpec((1, D_EXPERT, d), lambda i, be, nu: (be[i], 0, 0)),
                      pl.BlockSpec((1, 1, d), lambda i, be, nu: (be[i], 0, 0))],
            out_specs=pl.BlockSpec((tile_rows, LANES), lambda i, be, nu: (i, 0)),
            scratch_shapes=[pltpu.VMEM((d, 2 * D_EXPERT), BF16), pltpu.VMEM((D_EXPERT, d), BF16)]),
        out_shape=jax.ShapeDtypeStruct(xb_tiles.shape, F32),
        compiler_params=_params(),
        name="experts",
    )(block_e, n_used, xb_tiles, w_gate_up, b_gate_up.reshape(N_EXPERTS, 1, -1),
      w_down, b_down.reshape(N_EXPERTS, 1, -1))


def _combine_kernel(slot_ref, slot_next_ref, gate_ref, res_ref, g2_ref, b2_ref,
                    yb_ref, o_ref, buf_sc, sem, *, tm):
    i = pl.program_id(0)
    tr = TOKEN_TILE_ROWS
    cur = lax.rem(i, 2)

    def token_copy(buf, kk, r, slot):
        src = pl.multiple_of(slot * tr, tr)
        dst = pl.multiple_of(r * tr, tr)
        return pltpu.make_async_copy(yb_ref.at[pl.ds(src, tr), :],
                                     buf_sc.at[buf, kk, pl.ds(dst, tr), :], sem.at[buf])

    def issue(sref, buf):
        def start(g, c):
            toks = [g * DMA_ROWS_PER_ITER + u for u in range(DMA_ROWS_PER_ITER)]
            slots = [[sref[r * TOP_K + kk] for kk in range(TOP_K)] for r in toks]
            for r, row_slots in zip(toks, slots):
                for kk, slot in enumerate(row_slots):
                    token_copy(buf, kk, r, slot).start(priority=kk % DMA_PRIORITIES)
            return c
        lax.fori_loop(0, tm // DMA_ROWS_PER_ITER, start, 0)

    @pl.when(i == 0)
    def _():
        issue(slot_ref, 0)

    @pl.when(i + 1 < pl.num_programs(0))
    def _():
        issue(slot_next_ref, 1 - cur)

    def wait(r, c):
        for kk in range(TOP_K):
            token_copy(cur, 0, 0, 0).wait()
        return c

    lax.fori_loop(0, tm, wait, 0)

    gate = gate_ref[...]
    chunks = []
    for j in range(TOKEN_TILE_ROWS):
        hj = res_ref[:, j * LANES:(j + 1) * LANES]
        for kk in range(TOP_K):
            hj = hj + gate[:, kk:kk + 1] * buf_sc[cur, kk, pl.ds(j, tm, stride=tr), :]
        chunks.append(hj)
    mu = sum(jnp.sum(hj, axis=1, keepdims=True) for hj in chunks) * (1.0 / D_MODEL)
    cent = [hj - mu for hj in chunks]
    var = sum(jnp.sum(cj * cj, axis=1, keepdims=True) for cj in cent) * (1.0 / D_MODEL)
    inv = lax.rsqrt(var + LN_EPS)
    for j, cj in enumerate(cent):
        cols = slice(j * LANES, (j + 1) * LANES)
        o_ref[:, cols] = cj * inv * g2_ref[:, cols] + b2_ref[:, cols]


def combine(y_tiles, slot_flat, gates, res, ln2_g, ln2_b, tm=COMBINE_TM):
    t, d = res.shape
    n_steps = t // tm
    slot_spec = lambda f: pl.BlockSpec((tm * TOP_K,), f, memory_space=pltpu.SMEM)
    return pl.pallas_call(
        functools.partial(_combine_kernel, tm=tm),
        grid=(n_steps,),
        in_specs=[slot_spec(lambda i: (i,)),
                  slot_spec(lambda i: (jnp.minimum(i + 1, n_steps - 1),)),
                  pl.BlockSpec((tm, LANES), lambda i: (i, 0)),
                  pl.BlockSpec((tm, d), lambda i: (i, 0)),
                  pl.BlockSpec((1, d), lambda i: (0, 0)),
                  pl.BlockSpec((1, d), lambda i: (0, 0)),
                  pl.BlockSpec(memory_space=pl.ANY)],
        out_specs=pl.BlockSpec((tm, d), lambda i: (i, 0)),
        scratch_shapes=[pltpu.VMEM((2, TOP_K, tm * TOKEN_TILE_ROWS, LANES), F32),
                        pltpu.SemaphoreType.DMA((2,))],
        out_shape=jax.ShapeDtypeStruct((t, d), F32),
        compiler_params=_params(),
        name="combine",
    )(slot_flat, slot_flat, gates, res, ln2_g.reshape(1, d), ln2_b.reshape(1, d), y_tiles)


def kernel(x, p, w_in, conv_w, a_log, dt_bias, gnorm_w, w_out, ln1_g, ln1_b, w_router, b_router,
           w_gate_up, b_gate_up, w_down, b_down, w_ple_gate, b_ple_gate, w_ple_proj, ln2_g, ln2_b):
    bsz, seq, d = x.shape
    t = bsz * seq
    x2d = x.reshape(t, d)
    qkv_a, qkv_b, z_b, scal, scal_t = in_proj(x2d, w_in[0])
    ya = dilated_attn(qkv_a, bsz, seq)
    yb = deltanet(qkv_b, z_b, scal, scal_t, conv_w[0], a_log[0], dt_bias[0], gnorm_w[0], bsz, seq)
    x1, res, route, gates, counts = mix_route(
        ya, yb, x2d, p[0].reshape(t, PLE_DIM), w_out[0], ln1_g[0], ln1_b[0], w_router[0],
        b_router[0], w_ple_gate[0], b_ple_gate[0], w_ple_proj[0])

    cnt = counts[0, :N_EXPERTS].astype(jnp.int32)
    padded = (cnt + MOE_TILE - 1) // MOE_TILE * MOE_TILE
    pad_end = jnp.cumsum(padded).astype(jnp.int32)
    pad_start = pad_end - padded
    n_blocks = (t * TOP_K) // MOE_TILE + N_EXPERTS
    n_used = (pad_end[-1:] // MOE_TILE).astype(jnp.int32)
    tile_start = jnp.arange(n_blocks, dtype=jnp.int32) * MOE_TILE
    block_e = jnp.minimum(jnp.sum(pad_end[None, :] <= tile_start[:, None], axis=1),
                          N_EXPERTS - 1).astype(jnp.int32)
    experts_iota = jnp.arange(N_EXPERTS, dtype=jnp.int32)
    group_off = jnp.sum(jnp.where(route[:, :TOP_K, None] == experts_iota, pad_start, 0), axis=-1)
    slot_flat = (group_off + route[:, TOP_K:2 * TOP_K]).reshape(-1).astype(jnp.int32)

    xb = dispatch(x1, slot_flat, pad_start, pad_end, n_blocks * MOE_TILE)
    yexp = experts(xb, block_e, n_used, w_gate_up[0], b_gate_up[0], w_down[0], b_down[0])
    out = combine(yexp, slot_flat, gates, res, ln2_g[0], ln2_b[0])
    return out.reshape(bsz, seq, d)
```

```python
import functools
import math

import jax
import jax.numpy as jnp
from jax import lax
from jax.experimental import pallas as pl
from jax.experimental.pallas import tpu as pltpu

LANES = 128
VMEM_LIMIT_BYTES = 56 * 1024 * 1024

D_MODEL = 1024
PLE_DIM = 256
A_HEADS = 8
A_HEAD_DIM = 64
A_WIDTH = A_HEADS * A_HEAD_DIM
DILATIONS = (16, 4, 1)
ATT_BLOCK = 128
B_HEADS = 4
B_HEAD_DIM = 128
B_WIDTH = B_HEADS * B_HEAD_DIM
CONV_WIDTH = 4
DELTA_CHUNK = 64
N_EXPERTS = 32
TOP_K = 4
D_EXPERT = D_MODEL
SWIGLU_LIMIT = 7.0
SWIGLU_ALPHA = 1.702
MOE_BLOCK = 128
LN_EPS = 1e-5
RMS_EPS = 1e-6
L2_EPS = 1e-6
DEPTH = 1
DEEPNORM_ALPHA = (2.0 * DEPTH) ** 0.25

F32 = jnp.float32
BF16 = jnp.bfloat16


def _params(n_parallel_axes=1):
    return pltpu.CompilerParams(
        dimension_semantics=("arbitrary",) * n_parallel_axes,
        vmem_limit_bytes=VMEM_LIMIT_BYTES)


def _in_proj_kernel(x_ref, wa_ref, wb_ref, wz_ref, ws_ref, wst_ref, a_ref, b_ref, z_ref, s_ref, st_ref):
    xb = x_ref[...].astype(BF16)
    a_ref[...] = jnp.dot(xb, wa_ref[...], preferred_element_type=F32)
    b_ref[...] = jnp.dot(xb, wb_ref[...], preferred_element_type=F32).astype(BF16)
    z_ref[...] = jnp.dot(xb, wz_ref[...], preferred_element_type=F32)
    s_ref[...] = jnp.dot(xb, ws_ref[...], preferred_element_type=F32)
    st_ref[...] = lax.dot_general(wst_ref[...], xb, (((1,), (1,)), ((), ())),
                                  preferred_element_type=F32)


def in_proj(x2d, w_in, tm=512):
    t, d = x2d.shape
    na, nb = 3 * A_WIDTH, 3 * B_WIDTH
    wa = w_in[:, :na].astype(BF16)
    wb = w_in[:, na:na + nb].astype(BF16)
    wz = w_in[:, na + nb:na + nb + B_WIDTH].astype(BF16)
    ws = jnp.pad(w_in[:, na + nb + B_WIDTH:], ((0, 0), (0, LANES - 2 * B_HEADS))).astype(BF16)
    wst = ws[:, :16].T
    full = lambda a: pl.BlockSpec(a.shape, lambda i: (0, 0))
    return pl.pallas_call(
        _in_proj_kernel,
        grid=(t // tm,),
        in_specs=[pl.BlockSpec((tm, d), lambda i: (i, 0)),
                  full(wa), full(wb), full(wz), full(ws), full(wst)],
        out_specs=[pl.BlockSpec((tm, na), lambda i: (i, 0)),
                   pl.BlockSpec((tm, nb), lambda i: (i, 0)),
                   pl.BlockSpec((tm, B_WIDTH), lambda i: (i, 0)),
                   pl.BlockSpec((tm, LANES), lambda i: (i, 0)),
                   pl.BlockSpec((16, tm), lambda i: (0, i))],
        out_shape=[jax.ShapeDtypeStruct((t, na), F32),
                   jax.ShapeDtypeStruct((t, nb), BF16),
                   jax.ShapeDtypeStruct((t, B_WIDTH), F32),
                   jax.ShapeDtypeStruct((t, LANES), F32),
                   jax.ShapeDtypeStruct((16, t), F32)],
        compiler_params=_params(),
        name="in_proj",
    )(x2d, wa, wb, wz, ws, wst)


ATTN_UNROLL_DENSE = 3
ATTN_UNROLL_SINGLE = 4


def _attn_kernel(q_ref, k_ref, v_ref, o_ref, m_sc, l_sc, acc_sc, *, seq):
    blk = ATT_BLOCK
    lane = lax.broadcasted_iota(jnp.int32, (blk, LANES), 1)
    head0 = lane < A_HEAD_DIM
    row = lax.broadcasted_iota(jnp.int32, (blk, blk), 0)
    col = lax.broadcasted_iota(jnp.int32, (blk, blk), 1)
    prev_ok = col >= row
    cur_ok = col <= row
    neg_inf = jnp.float32(-jnp.inf)
    scale = 1.0 / math.sqrt(A_HEAD_DIM)

    def blocks(starts, stride, has_prev, first_branch, last_branch):
        def rows(ref, s0):
            if stride == 1:
                return ref[pl.ds(s0, blk), :]
            return ref[pl.ds(s0, blk, stride=stride), :]

        def put(ref, s0, val):
            if stride == 1:
                ref[pl.ds(s0, blk), :] = val
            else:
                ref[pl.ds(s0, blk, stride=stride), :] = val

        hmask = (head0, jnp.logical_not(head0))
        items = []
        for start, hp in zip(starts, has_prev):
            q = rows(q_ref, start) * scale
            keys = rows(k_ref, start)
            vals = rows(v_ref, start)
            if hp:
                keys = jnp.concatenate([rows(k_ref, start - blk * stride), keys], axis=0)
                vals = jnp.concatenate([rows(v_ref, start - blk * stride), vals], axis=0)
            it = dict(start=start, hp=hp, keys=keys.astype(BF16), vals=vals,
                      qh=[jnp.where(hm, q, 0.0).astype(BF16) for hm in hmask])
            if not first_branch:
                it["m_old"] = rows(m_sc, start)
                it["l_old"] = rows(l_sc, start)
                it["acc_old"] = rows(acc_sc, start)
            items.append(it)

        for it in items:
            it["s"] = [lax.dot_general(qh, it["keys"], (((1,), (1,)), ((), ())),
                                       preferred_element_type=F32) for qh in it["qh"]]
        for it in items:
            ok = jnp.concatenate([prev_ok, cur_ok], axis=1) if it["hp"] else cur_ok
            it["m"], it["sum"], it["p"] = [], [], []
            for h in range(2):
                s = jnp.where(ok, it["s"][h], neg_inf)
                t = jnp.maximum(s[:, :blk], s[:, blk:]) if it["hp"] else s
                if not first_branch:
                    t = jnp.maximum(t, jnp.where(hmask[h], it["m_old"], neg_inf))
                m_h = jnp.max(t, axis=1, keepdims=True)
                p = jnp.exp(s - m_h)
                psum = p[:, :blk] + p[:, blk:] if it["hp"] else p
                it["m"].append(m_h)
                it["sum"].append(jnp.sum(psum, axis=1, keepdims=True))
                it["p"].append(p.astype(BF16))
        for it in items:
            v16 = it["vals"].astype(BF16)
            it["pv"] = [jnp.dot(it["p"][h], v16, preferred_element_type=F32) for h in range(2)]
        for it in items:
            m_new = jnp.where(head0, it["m"][0], it["m"][1])
            l_new = jnp.where(head0, it["sum"][0], it["sum"][1])
            acc_new = jnp.where(head0, it["pv"][0], it["pv"][1])
            if not first_branch:
                alpha = jnp.exp(it["m_old"] - m_new)
                l_new = l_new + alpha * it["l_old"]
                acc_new = acc_new + alpha * it["acc_old"]
            if last_branch:
                put(o_ref, it["start"], acc_new / l_new)
            else:
                put(m_sc, it["start"], m_new)
                put(l_sc, it["start"], l_new)
                put(acc_sc, it["start"], acc_new)

    n_br = len(DILATIONS)
    for bi, dil in enumerate(DILATIONS):
        first, last = bi == 0, bi == n_br - 1
        nblk = seq // dil // blk
        if dil == 1:
            blocks([0], 1, [False], first, last)
            per = ATTN_UNROLL_DENSE
            assert (nblk - 1) % per == 0

            def body(g, c, first=first, last=last, per=per):
                starts = [pl.multiple_of((1 + g * per + j) * blk, blk) for j in range(per)]
                blocks(starts, 1, [True] * per, first, last)
                return c
            lax.fori_loop(0, (nblk - 1) // per, body, 0)
        elif nblk > 1:
            def body(r, c, dil=dil, nblk=nblk, first=first, last=last):
                blocks([r + n * blk * dil for n in range(nblk)], dil,
                       [n > 0 for n in range(nblk)], first, last)
                return c
            lax.fori_loop(0, dil, body, 0)
        else:
            per = ATTN_UNROLL_SINGLE
            assert dil % per == 0

            def body(g, c, dil=dil, first=first, last=last, per=per):
                blocks([g * per + j for j in range(per)], dil, [False] * per, first, last)
                return c
            lax.fori_loop(0, dil // per, body, 0)


def dilated_attn(qkv_a, bsz, seq):
    t = bsz * seq
    n_pairs = A_WIDTH // LANES
    blk = lambda off: pl.BlockSpec((seq, LANES), lambda b, p: (b, off + p))
    return pl.pallas_call(
        functools.partial(_attn_kernel, seq=seq),
        grid=(bsz, n_pairs),
        in_specs=[blk(0), blk(n_pairs), blk(2 * n_pairs)],
        out_specs=pl.BlockSpec((seq, LANES), lambda b, p: (b, p)),
        out_shape=jax.ShapeDtypeStruct((t, A_WIDTH), F32),
        scratch_shapes=[pltpu.VMEM((seq, LANES), F32)] * 3,
        compiler_params=_params(2),
        name="dilated_attn",
    )(qkv_a, qkv_a, qkv_a)


DELTA_HEADS_PER_STEP = 4
DELTA_P1_CHUNKS = 2
CONV_TILE = 256
CONV_HISTORY_ROWS = 16
_HI = lax.Precision.HIGHEST


def _dot_hi(a, b):
    return jnp.dot(a, b, precision=_HI, preferred_element_type=F32)


def _dot16(a, b):
    return jnp.dot(a, b, preferred_element_type=F32)


def _split2(x):
    hi = x.astype(BF16)
    return hi, (x - hi.astype(F32)).astype(BF16)


def _dot3(a, b):
    return _dot16(a[0], b[0]) + (_dot16(a[1], b[0]) + _dot16(a[0], b[1]))


def _split3(x):
    hi = x.astype(BF16)
    r = x - hi.astype(F32)
    mid = r.astype(BF16)
    return hi, mid, (r - mid.astype(F32)).astype(BF16)


def _dot_sel(m01, x):
    hi, mid, lo = _split3(x)
    return _dot16(m01, hi) + (_dot16(m01, mid) + _dot16(m01, lo))


def _dot_sel_r(x, m01):
    hi, mid, lo = _split3(x)
    return _dot16(hi, m01) + (_dot16(mid, m01) + _dot16(lo, m01))


def _stack_parts(x, lo_half, want_l=True, want_r=True):
    xh = x.astype(BF16)
    xh_f = xh.astype(F32)
    xl_f = x - xh_f
    left = right = None
    if want_l:
        mix = jnp.where(lo_half, xh_f, xl_f).astype(BF16)
        left = jnp.concatenate([mix, mix], axis=1)
    if want_r:
        xl = xl_f.astype(BF16)
        right = jnp.concatenate([xh, xh, xl, xl], axis=0)
    return left, right


def _softplus(x):
    return jnp.maximum(x, 0.0) + jnp.log(1.0 + jnp.exp(-jnp.abs(x)))


def _sigmoid(x):
    return 1.0 / (1.0 + jnp.exp(-x))


def _delta_kernel(q_ref, k_ref, v_ref, z_ref, sc_ref, st_ref, par_ref, part_ref,
                  wq_ref, wk_ref, wv_ref, nw_ref, o_ref,
                  conv_sc, qn_sc, kn_sc, vn_sc, w_sc, attn_sc, kdt_sc, gcr_sc, egl_sc,
                  state_sc, *, seq, hps):
    c = DELTA_CHUNK
    dk = B_HEAD_DIM
    n_chunks = seq // c
    tile = DELTA_P1_CHUNKS * c
    group = pl.program_id(1)
    lane = lax.broadcasted_iota(jnp.int32, (tile, LANES), 1)
    r2 = lax.broadcasted_iota(jnp.int32, (tile, tile), 0)
    c2 = lax.broadcasted_iota(jnp.int32, (tile, tile), 1)
    same_chunk = (r2 // c) == (c2 // c)
    cum_mat = jnp.where(same_chunk & (c2 <= r2), 1.0, 0.0).astype(BF16)
    tot_mat = jnp.where(same_chunk, 1.0, 0.0).astype(BF16)
    ri = lax.broadcasted_iota(jnp.int32, (c, 2 * c), 0)
    li = lax.broadcasted_iota(jnp.int32, (c, 2 * c), 1)
    lo_half = li < c
    ci = jnp.where(lo_half, li, li - c)
    causal = ci <= ri
    strict = ci < ri
    eye = jnp.where(ci == ri, 1.0, 0.0).astype(F32)
    upper = jnp.where(ri <= ci, 1.0, 0.0).astype(BF16)

    hist = CONV_HISTORY_ROWS
    for hh in range(hps):
        cols = slice(hh * dk, (hh + 1) * dk)
        streams = ((q_ref, wq_ref, qn_sc, "q"), (k_ref, wk_ref, kn_sc, "k"), (v_ref, wv_ref, vn_sc, "v"))

        def conv_tile(si, xe, c0, hh=hh, cols=cols):
            _, w_ref, dst, kind = streams[si]
            w = w_ref[:, cols]
            conv_sc[si] = xe
            off = hist - (CONV_WIDTH - 1)
            y = w[0:1, :] * conv_sc[si, off:off + CONV_TILE, :]
            for j in range(1, CONV_WIDTH):
                y = y + w[j:j + 1, :] * conv_sc[si, off + j:off + j + CONV_TILE, :]
            y = y * _sigmoid(y)
            if kind != "v":
                y = y * lax.rsqrt(jnp.sum(y * y, axis=1, keepdims=True) + L2_EPS)
            if kind == "q":
                y = y * (B_HEAD_DIM ** -0.5)
            dst[hh, pl.ds(c0, CONV_TILE), :] = y

        for si, (src, _, _, _) in enumerate(streams):
            conv_tile(si, jnp.concatenate([jnp.zeros((hist, dk), F32),
                                           src[0:CONV_TILE, cols].astype(F32)], axis=0), 0)

        def conv_body(ti, carry, cols=cols, conv_tile=conv_tile):
            c0 = pl.multiple_of(ti * CONV_TILE, CONV_TILE)
            for si, (src, _, _, _) in enumerate(streams):
                conv_tile(si, src[pl.ds(c0 - hist, hist + CONV_TILE), cols].astype(F32), c0)
            return carry

        lax.fori_loop(1, seq // CONV_TILE, conv_body, 0)

    for hh in range(hps):
        h = group * hps + hh
        a_row = st_ref[B_HEADS + h]
        g_row = -jnp.exp(part_ref[0, h]) * _softplus(a_row + part_ref[1, h])
        gcr_sc[hh] = _dot_sel_r(g_row, upper)

    alog_l = par_ref[0:1, :]
    dtb_l = par_ref[1:2, :]

    def phase1(i, carry):
        r0 = pl.multiple_of(i * tile, tile)
        rows = pl.ds(r0, tile)
        s = sc_ref[rows, :]
        beta_all = _sigmoid(s)
        g_all = -jnp.exp(alog_l) * _softplus(s + dtb_l)
        gc_all = _dot_sel(cum_mat, g_all)
        gl_all = _dot_sel(tot_mat, g_all)
        chains = []
        for hh in range(hps):
            h = group * hps + hh
            pick = lambda a, l: jnp.sum(jnp.where(lane == l, a, 0.0), axis=1, keepdims=True)
            beta = pick(beta_all, h)
            gc = pick(gc_all, B_HEADS + h)
            gl = pick(gl_all, B_HEADS + h)
            egc = jnp.exp(gc)
            q = qn_sc[hh, rows, :]
            k = kn_sc[hh, rows, :]
            v = vn_sc[hh, rows, :]
            kbeta = k * beta
            vbeta = v * beta
            qn_sc[hh, rows, :] = q * egc
            kdec = k * jnp.exp(gl - gc)
            kb16 = k.astype(BF16)
            for cc in range(DELTA_P1_CHUNKS):
                sl = slice(cc * c, (cc + 1) * c)
                n = DELTA_P1_CHUNKS * i + cc
                kdt_sc[hh, n] = kdec[sl].T.astype(BF16)
                egl_sc[hh, pl.ds(n, 1), :] = jnp.broadcast_to(jnp.exp(gl[cc * c:cc * c + 1]), (1, dk))
                diff = gc[sl] - gcr_sc[hh, pl.ds(n, 1), :]
                e = jnp.exp(jnp.where(causal, diff, 0.0))
                k2 = jnp.concatenate([kb16[sl], kb16[sl]], axis=0)
                lhs = jnp.concatenate([kbeta[sl], q[sl]], axis=0).astype(BF16)
                rhs = jnp.concatenate([vbeta[sl], kbeta[sl] * egc[sl]], axis=1)
                chains.append(dict(hh=hh, row=r0 + cc * c, e=e, k2=k2, lhs=lhs, rhs=rhs))

        for ch in chains:
            kq = lax.dot_general(ch["lhs"], ch["k2"], (((1,), (1,)), ((), ())),
                                 preferred_element_type=F32)
            ch["a"] = jnp.where(strict, kq[:c] * ch["e"], 0.0)
            attn_sc[ch["hh"], pl.ds(ch["row"], c), :] = (
                jnp.where(causal, kq[c:] * ch["e"], 0.0)[:, :c].astype(BF16))
        for ch in chains:
            ch["inv"] = eye - ch["a"]
            p_l, p_r = _stack_parts(ch["a"], lo_half)
            ch["p"] = _dot16(p_l, p_r)
        for _ in range(4):
            for ch in chains:
                p_l, p_r = _stack_parts(ch["p"], lo_half)
                inv_l, _ = _stack_parts(ch["inv"], lo_half, want_r=False)
                res = _dot16(jnp.concatenate([p_l, inv_l], axis=0), p_r)
                ch["p"] = res[:c]
                ch["inv"] = ch["inv"] + res[c:]
        for ch in chains:
            inv_l, _ = _stack_parts(ch["inv"], lo_half, want_r=False)
            _, p_r = _stack_parts(ch["p"], lo_half, want_l=False)
            ch["inv"] = ch["inv"] + _dot16(inv_l, p_r)
        for ch in chains:
            inv_l, _ = _stack_parts(ch["inv"], lo_half, want_r=False)
            _, rhs_r = _stack_parts(ch["rhs"], None, want_l=False)
            sol = _dot16(inv_l, rhs_r)
            vn_sc[ch["hh"], pl.ds(ch["row"], c), :] = sol[:, :dk]
            w_sc[ch["hh"], pl.ds(ch["row"], c), :] = sol[:, dk:].astype(BF16)
        return carry

    lax.fori_loop(0, n_chunks // DELTA_P1_CHUNKS, phase1, 0)

    state_sc[...] = jnp.zeros(state_sc.shape, F32)
    nw = nw_ref[...]

    def phase2(n, carry):
        r0 = pl.multiple_of(n * c, c)
        rows = pl.ds(r0, c)
        heads = range(hps)
        st = [state_sc[hh] for hh in heads]
        st16 = [s_.astype(BF16) for s_ in st]
        lhs = [jnp.concatenate([w_sc[hh, rows, :], qn_sc[hh, rows, :].astype(BF16)], axis=0)
               for hh in heads]
        ws = [_dot16(lhs[hh], st16[hh]) for hh in heads]
        vn16 = [(vn_sc[hh, rows, :] - ws[hh][:c]).astype(BF16) for hh in heads]
        av = [_dot16(attn_sc[hh, rows, :], vn16[hh]) for hh in heads]
        kv = [_dot16(kdt_sc[hh, n], vn16[hh]) for hh in heads]
        for hh in heads:
            state_sc[hh] = st[hh] * egl_sc[hh, pl.ds(n, 1), :] + kv[hh]
            out = ws[hh][c:] + av[hh]
            o = out * lax.rsqrt(jnp.mean(out * out, axis=1, keepdims=True) + RMS_EPS) * nw
            z = z_ref[rows, hh * dk:(hh + 1) * dk]
            o_ref[rows, hh * dk:(hh + 1) * dk] = (o * (z * _sigmoid(z))).astype(o_ref.dtype)
        return carry

    lax.fori_loop(0, n_chunks, phase2, 0)


def deltanet(qkv_b, z_b, scal, scal_t, conv_w, a_log, dt_bias, gnorm_w, bsz, seq):
    t = bsz * seq
    hps = DELTA_HEADS_PER_STEP
    n_groups = B_HEADS // hps
    wg = hps * B_HEAD_DIM
    c = DELTA_CHUNK
    n_chunks = seq // c
    st3 = scal_t.reshape(16, t // c, c)
    par = jnp.zeros((8, LANES), F32)
    par = par.at[0, B_HEADS:2 * B_HEADS].set(a_log).at[1, B_HEADS:2 * B_HEADS].set(dt_bias)
    part = jnp.stack([a_log, dt_bias]).astype(F32)
    blk = lambda off: pl.BlockSpec((seq, wg), lambda b, g: (b, off + g))
    wblk = lambda off: pl.BlockSpec((CONV_WIDTH, wg), lambda b, g: (0, off + g))
    sc = lambda shape, dt: pltpu.VMEM(shape, dt)
    return pl.pallas_call(
        functools.partial(_delta_kernel, seq=seq, hps=hps),
        grid=(bsz, n_groups),
        in_specs=[blk(0), blk(n_groups), blk(2 * n_groups), blk(0),
                  pl.BlockSpec((seq, LANES), lambda b, g: (b, 0)),
                  pl.BlockSpec((16, n_chunks, c), lambda b, g: (0, b, 0)),
                  pl.BlockSpec((8, LANES), lambda b, g: (0, 0)),
                  pl.BlockSpec(memory_space=pltpu.SMEM),
                  wblk(0), wblk(n_groups), wblk(2 * n_groups),
                  pl.BlockSpec((1, B_HEAD_DIM), lambda b, g: (0, 0))],
        out_specs=pl.BlockSpec((seq, wg), lambda b, g: (b, g)),
        out_shape=jax.ShapeDtypeStruct((t, B_WIDTH), BF16),
        scratch_shapes=[sc((3, CONV_HISTORY_ROWS + CONV_TILE, B_HEAD_DIM), F32),
                        sc((hps, seq, B_HEAD_DIM), F32), sc((hps, seq, B_HEAD_DIM), F32),
                        sc((hps, seq, B_HEAD_DIM), F32),
                        sc((hps, seq, B_HEAD_DIM), BF16),
                        sc((hps, seq, c), BF16), sc((hps, n_chunks, B_HEAD_DIM, c), BF16),
                        sc((hps, n_chunks, 2 * c), F32), sc((hps, n_chunks, B_HEAD_DIM), F32),
                        sc((hps, B_HEAD_DIM, B_HEAD_DIM), F32)],
        compiler_params=_params(2),
        name="deltanet",
    )(qkv_b, qkv_b, qkv_b, z_b, scal, st3, par, part,
      conv_w, conv_w, conv_w, gnorm_w.reshape(1, B_HEAD_DIM))


SUBLANES = 8
TOKEN_TILE_ROWS = D_MODEL // LANES
assert TOKEN_TILE_ROWS == SUBLANES


def _store_token_tiles(ref, x, n_tok):
    for j in range(TOKEN_TILE_ROWS):
        ref[pl.ds(j, n_tok, stride=TOKEN_TILE_ROWS), :] = x[:, j * LANES:(j + 1) * LANES]


def _load_token_tiles(ref, n_tok, lead=None):
    idx = lambda j: (pl.ds(j, n_tok, stride=TOKEN_TILE_ROWS), slice(None))
    if lead is None:
        return [ref[idx(j)] for j in range(TOKEN_TILE_ROWS)]
    return [ref[(lead,) + idx(j)] for j in range(TOKEN_TILE_ROWS)]


def _layer_norm(h, g, b):
    mu = jnp.mean(h, axis=1, keepdims=True)
    hc = h - mu
    var = jnp.mean(hc * hc, axis=1, keepdims=True)
    return hc * lax.rsqrt(var + LN_EPS) * g + b


def _mix_route_kernel(ya_ref, yb_ref, x_ref, p_ref, woa_ref, wob_ref, g1_ref, b1_ref,
                      wr_ref, br_ref, wpg_ref, bpg_ref, wpp_ref,
                      x1_ref, res_ref, route_ref, gate_ref, cnt_ref, run_sc, wr2_sc, *, tm):
    i = pl.program_id(0)

    @pl.when(i == 0)
    def _():
        run_sc[...] = jnp.zeros(run_sc.shape, F32)
        w_hi, w_lo = _split2(wr_ref[...])
        wr2_sc[:, :LANES] = w_hi
        wr2_sc[:, LANES:] = w_lo

    mix = (jnp.dot(ya_ref[...].astype(BF16), woa_ref[...], preferred_element_type=F32)
           + jnp.dot(yb_ref[...].astype(BF16), wob_ref[...], preferred_element_type=F32))
    x1 = _layer_norm(DEEPNORM_ALPHA * x_ref[...] + mix, g1_ref[...], b1_ref[...])
    _store_token_tiles(x1_ref, x1, tm)
    x1b = x1.astype(BF16)

    x1l = (x1 - x1b.astype(F32)).astype(BF16)
    r_hi = jnp.dot(x1b, wr2_sc[...], preferred_element_type=F32)
    r_lo = jnp.dot(x1l, wr2_sc[:, :LANES], preferred_element_type=F32)
    logits = r_hi[:, :LANES] + (r_hi[:, LANES:] + r_lo) + br_ref[...]

    pgate = _sigmoid(jnp.dot(x1b, wpg_ref[...], preferred_element_type=F32) + bpg_ref[...])
    proj = jnp.dot(p_ref[...].astype(BF16), wpp_ref[...], preferred_element_type=F32)
    res_ref[...] = DEEPNORM_ALPHA * x1 + pgate * proj

    lane = lax.broadcasted_iota(jnp.int32, (tm, LANES), 1)
    cur = jnp.where(lane < N_EXPERTS, logits, -jnp.inf)
    vals, hots = [], []
    for _ in range(TOP_K):
        m = jnp.max(cur, axis=1, keepdims=True)
        idx = jnp.min(jnp.where(cur == m, lane, LANES), axis=1, keepdims=True)
        hot = lane == idx
        cur = jnp.where(hot, -jnp.inf, cur)
        vals.append(m)
        hots.append((hot, idx))
    exps = [jnp.exp(v - vals[0]) for v in vals]
    den = exps[0] + exps[1] + exps[2] + exps[3]

    member = jnp.zeros((tm, LANES), F32)
    for hot, _ in hots:
        member = member + jnp.where(hot, 1.0, 0.0)
    rr = lax.broadcasted_iota(jnp.int32, (tm, tm), 0)
    cc = lax.broadcasted_iota(jnp.int32, (tm, tm), 1)
    before = jnp.where(cc < rr, 1.0, 0.0).astype(BF16)
    prior = jnp.dot(before, member.astype(BF16), preferred_element_type=F32) + run_sc[0:1, :]
    route = jnp.zeros((tm, LANES), jnp.int32)
    gate = jnp.zeros((tm, LANES), F32)
    for kk, (hot, idx) in enumerate(hots):
        rank = jnp.sum(jnp.where(hot, prior, 0.0), axis=1, keepdims=True).astype(jnp.int32)
        route = jnp.where(lane == kk, idx, route)
        route = jnp.where(lane == TOP_K + kk, rank, route)
        gate = jnp.where(lane == kk, exps[kk] / den, gate)
    route_ref[...] = route
    gate_ref[...] = gate
    run_sc[...] = run_sc[...] + jnp.sum(member, axis=0, keepdims=True)
    cnt_ref[...] = run_sc[...]


def mix_route(ya, yb, x2d, p2d, w_out, ln1_g, ln1_b, w_router, b_router, w_ple_gate, b_ple_gate,
              w_ple_proj, tm=512):
    t, d = x2d.shape
    woa = w_out[:A_WIDTH].astype(BF16)
    wob = w_out[A_WIDTH:].astype(BF16)
    wr = jnp.pad(w_router, ((0, 0), (0, LANES - N_EXPERTS)))
    br = jnp.pad(b_router, (0, LANES - N_EXPERTS)).reshape(1, LANES)
    row = lambda w: pl.BlockSpec((tm, w), lambda i: (i, 0))
    full = lambda a: pl.BlockSpec(a.shape, lambda i: (0,) * a.ndim)
    ops = [woa, wob, ln1_g.reshape(1, d), ln1_b.reshape(1, d), wr, br,
           w_ple_gate.astype(BF16), b_ple_gate.reshape(1, d), w_ple_proj.astype(BF16)]
    return pl.pallas_call(
        functools.partial(_mix_route_kernel, tm=tm),
        grid=(t // tm,),
        in_specs=[row(A_WIDTH), row(B_WIDTH), row(d), row(PLE_DIM)] + [full(a) for a in ops],
        out_specs=[pl.BlockSpec((tm * TOKEN_TILE_ROWS, LANES), lambda i: (i, 0)),
                   row(d), row(LANES), row(LANES),
                   pl.BlockSpec((8, LANES), lambda i: (0, 0))],
        out_shape=[jax.ShapeDtypeStruct((t * TOKEN_TILE_ROWS, LANES), F32),
                   jax.ShapeDtypeStruct((t, d), F32),
                   jax.ShapeDtypeStruct((t, LANES), jnp.int32),
                   jax.ShapeDtypeStruct((t, LANES), F32),
                   jax.ShapeDtypeStruct((8, LANES), F32)],
        scratch_shapes=[pltpu.VMEM((8, LANES), F32), pltpu.VMEM((d, 2 * LANES), BF16)],
        compiler_params=_params(),
        name="mix_route",
    )(ya, yb, x2d, p2d, *ops)


MOE_TILE = 512
DISPATCH_TM = 512
COMBINE_TM = 256
DMA_ROWS_PER_ITER = 8
DMA_PRIORITIES = 2


def _dispatch_kernel(pstart_ref, pend_ref, slot_ref, x_ref, xb_ref, zero_sc, sem, zsem, *, tm):
    i = pl.program_id(0)
    tr = TOKEN_TILE_ROWS
    tile_rows = MOE_TILE * tr

    def zero_copy(e):
        off = pl.multiple_of((pend_ref[e] - MOE_TILE) * tr, tile_rows)
        return pltpu.make_async_copy(zero_sc, xb_ref.at[pl.ds(off, tile_rows), :], zsem)

    @pl.when(i == 0)
    def _():
        zero_sc[...] = jnp.zeros(zero_sc.shape, zero_sc.dtype)
        for e in range(N_EXPERTS):
            @pl.when(pend_ref[e] > pstart_ref[e])
            def _():
                zero_copy(e).start()
        for e in range(N_EXPERTS):
            @pl.when(pend_ref[e] > pstart_ref[e])
            def _():
                zero_copy(e).wait()

        n_tiles = xb_ref.shape[0] // tile_rows
        first_unused = pend_ref[N_EXPERTS - 1] // MOE_TILE

        def tail_copy(j):
            off = pl.multiple_of(j * tile_rows, tile_rows)
            return pltpu.make_async_copy(zero_sc, xb_ref.at[pl.ds(off, tile_rows), :], zsem)

        def tail_start(j, c):
            tail_copy(j).start()
            return c

        def tail_wait(j, c):
            tail_copy(j).wait()
            return c

        lax.fori_loop(first_unused, n_tiles, tail_start, 0)
        lax.fori_loop(first_unused, n_tiles, tail_wait, 0)

    def token_copy(tok, slot):
        src = pl.multiple_of(tok * tr, tr)
        dst = pl.multiple_of(slot * tr, tr)
        return pltpu.make_async_copy(x_ref.at[pl.ds(src, tr), :], xb_ref.at[pl.ds(dst, tr), :], sem)

    def start(g, c):
        toks = [g * DMA_ROWS_PER_ITER + u for u in range(DMA_ROWS_PER_ITER)]
        slots = [[slot_ref[r * TOP_K + kk] for kk in range(TOP_K)] for r in toks]
        for r, row_slots in zip(toks, slots):
            for kk, slot in enumerate(row_slots):
                token_copy(r, slot).start(priority=kk % DMA_PRIORITIES)
        return c

    def wait(r, c):
        for kk in range(TOP_K):
            token_copy(0, 0).wait()
        return c

    lax.fori_loop(0, tm // DMA_ROWS_PER_ITER, start, 0)
    lax.fori_loop(0, tm, wait, 0)


def dispatch(x1_tiles, slot_flat, pad_start, pad_end, n_slots, tm=DISPATCH_TM):
    t = x1_tiles.shape[0] // TOKEN_TILE_ROWS
    return pl.pallas_call(
        functools.partial(_dispatch_kernel, tm=tm),
        grid_spec=pltpu.PrefetchScalarGridSpec(
            num_scalar_prefetch=2,
            grid=(t // tm,),
            in_specs=[pl.BlockSpec((tm * TOP_K,), lambda i, ps, pe: (i,), memory_space=pltpu.SMEM),
                      pl.BlockSpec((tm * TOKEN_TILE_ROWS, LANES), lambda i, ps, pe: (i, 0))],
            out_specs=pl.BlockSpec(memory_space=pl.ANY),
            scratch_shapes=[pltpu.VMEM((MOE_TILE * TOKEN_TILE_ROWS, LANES), F32),
                            pltpu.SemaphoreType.DMA(()), pltpu.SemaphoreType.DMA(())]),
        out_shape=jax.ShapeDtypeStruct((n_slots * TOKEN_TILE_ROWS, LANES), F32),
        compiler_params=_params(),
        name="dispatch",
    )(pad_start, pad_end, slot_flat, x1_tiles)


EXPERT_CAST_ROWS = 256
EXPERT_SUBTILES = 1


def _expert_kernel(be_ref, nu_ref, x_ref, wgu_ref, bgu_ref, wd_ref, bd_ref, y_ref, wgu_sc, wd_sc):
    i = pl.program_id(0)
    used = i < nu_ref[0]
    new_expert = jnp.logical_or(i == 0, be_ref[i] != be_ref[jnp.maximum(i - 1, 0)])

    @pl.when(jnp.logical_and(used, new_expert))
    def _():
        def cast(j, c):
            r = pl.ds(pl.multiple_of(j * EXPERT_CAST_ROWS, EXPERT_CAST_ROWS), EXPERT_CAST_ROWS)
            wgu_sc[r, :] = wgu_ref[0, r, :].astype(BF16)
            wd_sc[r, :] = wd_ref[0, r, :].astype(BF16)
            return c
        lax.fori_loop(0, D_MODEL // EXPERT_CAST_ROWS, cast, 0)

    @pl.when(used)
    def _():
        sub = MOE_TILE // EXPERT_SUBTILES
        tr = TOKEN_TILE_ROWS
        nchunk = 256
        for s in range(EXPERT_SUBTILES):
            base = s * sub * tr
            xc = [x_ref[pl.ds(base + j, sub, stride=tr), :].astype(BF16) for j in range(tr)]
            xb = jnp.concatenate(xc, axis=1)
            acts = []
            for c0 in range(0, D_EXPERT, nchunk):
                if c0 == 0:
                    half = D_MODEL // 2
                    xs = (jnp.concatenate(xc[:tr // 2], axis=1), jnp.concatenate(xc[tr // 2:], axis=1))
                    g = sum(jnp.dot(xs[h], wgu_sc[h * half:(h + 1) * half, c0:c0 + nchunk],
                                    preferred_element_type=F32) for h in range(2))
                    u = sum(jnp.dot(xs[h], wgu_sc[h * half:(h + 1) * half,
                                                  D_EXPERT + c0:D_EXPERT + c0 + nchunk],
                                    preferred_element_type=F32) for h in range(2))
                else:
                    g = jnp.dot(xb, wgu_sc[:, c0:c0 + nchunk], preferred_element_type=F32)
                    u = jnp.dot(xb, wgu_sc[:, D_EXPERT + c0:D_EXPERT + c0 + nchunk],
                                preferred_element_type=F32)
                g = g + bgu_ref[0, :, c0:c0 + nchunk]
                u = u + bgu_ref[0, :, D_EXPERT + c0:D_EXPERT + c0 + nchunk]
                g = jnp.minimum(g, SWIGLU_LIMIT)
                u = jnp.clip(u, -SWIGLU_LIMIT, SWIGLU_LIMIT)
                acts.append(((u + 1.0) * (g * _sigmoid(SWIGLU_ALPHA * g))).astype(BF16))
            act = jnp.concatenate(acts, axis=1)
            for n0 in range(0, D_MODEL, nchunk):
                yb = (jnp.dot(act, wd_sc[:, n0:n0 + nchunk], preferred_element_type=F32)
                      + bd_ref[0, :, n0:n0 + nchunk])
                for jj in range(nchunk // LANES):
                    j = n0 // LANES + jj
                    y_ref[pl.ds(base + j, sub, stride=tr), :] = yb[:, jj * LANES:(jj + 1) * LANES]

    @pl.when(jnp.logical_not(used))
    def _():
        y_ref[...] = jnp.zeros(y_ref.shape, F32)


def experts(xb_tiles, block_e, n_used, w_gate_up, b_gate_up, w_down, b_down):
    tile_rows = MOE_TILE * TOKEN_TILE_ROWS
    n_blocks = xb_tiles.shape[0] // tile_rows
    d = D_MODEL
    clamp = lambda i, be, nu: jnp.minimum(i, nu[0] - 1)
    return pl.pallas_call(
        _expert_kernel,
        grid_spec=pltpu.PrefetchScalarGridSpec(
            num_scalar_prefetch=2,
            grid=(n_blocks,),
            in_specs=[pl.BlockSpec((tile_rows, LANES), lambda i, be, nu: (clamp(i, be, nu), 0)),
                      pl.BlockSpec((1, d, 2 * D_EXPERT), lambda i, be, nu: (be[i], 0, 0)),
                      pl.BlockSpec((1, 1, 2 * D_EXPERT), lambda i, be, nu: (be[i], 0, 0)),
                      pl.BlockSpec((1, D_EXPERT, d), lambda i, be, nu: (be[i], 0, 0)),
                      pl.BlockSpec((1, 1, d), lambda i, be, nu: (be[i], 0, 0))],
            out_specs=pl.BlockSpec((tile_rows, LANES), lambda i, be, nu: (i, 0)),
            scratch_shapes=[pltpu.VMEM((d, 2 * D_EXPERT), BF16), pltpu.VMEM((D_EXPERT, d), BF16)]),
        out_shape=jax.ShapeDtypeStruct(xb_tiles.shape, F32),
        compiler_params=_params(),
        name="experts",
    )(block_e, n_used, xb_tiles, w_gate_up, b_gate_up.reshape(N_EXPERTS, 1, -1),
      w_down, b_down.reshape(N_EXPERTS, 1, -1))


def _combine_kernel(slot_ref, slot_next_ref, gate_ref, res_ref, g2_ref, b2_ref,
                    yb_ref, o_ref, buf_sc, sem, *, tm):
    i = pl.program_id(0)
    tr = TOKEN_TILE_ROWS
    cur = lax.rem(i, 2)

    def token_copy(buf, kk, r, slot):
        src = pl.multiple_of(slot * tr, tr)
        dst = pl.multiple_of(r * tr, tr)
        return pltpu.make_async_copy(yb_ref.at[pl.ds(src, tr), :],
                                     buf_sc.at[buf, kk, pl.ds(dst, tr), :], sem.at[buf])

    def issue(sref, buf):
        def start(g, c):
            toks = [g * DMA_ROWS_PER_ITER + u for u in range(DMA_ROWS_PER_ITER)]
            slots = [[sref[r * TOP_K + kk] for kk in range(TOP_K)] for r in toks]
            for r, row_slots in zip(toks, slots):
                for kk, slot in enumerate(row_slots):
                    token_copy(buf, kk, r, slot).start(priority=kk % DMA_PRIORITIES)
            return c
        lax.fori_loop(0, tm // DMA_ROWS_PER_ITER, start, 0)

    @pl.when(i == 0)
    def _():
        issue(slot_ref, 0)

    @pl.when(i + 1 < pl.num_programs(0))
    def _():
        issue(slot_next_ref, 1 - cur)

    def wait(r, c):
        for kk in range(TOP_K):
            token_copy(cur, 0, 0, 0).wait()
        return c

    lax.fori_loop(0, tm, wait, 0)

    gate = gate_ref[...]
    chunks = []
    for j in range(TOKEN_TILE_ROWS):
        hj = res_ref[:, j * LANES:(j + 1) * LANES]
        for kk in range(TOP_K):
            hj = hj + gate[:, kk:kk + 1] * buf_sc[cur, kk, pl.ds(j, tm, stride=tr), :]
        chunks.append(hj)
    mu = sum(jnp.sum(hj, axis=1, keepdims=True) for hj in chunks) * (1.0 / D_MODEL)
    cent = [hj - mu for hj in chunks]
    var = sum(jnp.sum(cj * cj, axis=1, keepdims=True) for cj in cent) * (1.0 / D_MODEL)
    inv = lax.rsqrt(var + LN_EPS)
    for j, cj in enumerate(cent):
        cols = slice(j * LANES, (j + 1) * LANES)
        o_ref[:, cols] = cj * inv * g2_ref[:, cols] + b2_ref[:, cols]


def combine(y_tiles, slot_flat, gates, res, ln2_g, ln2_b, tm=COMBINE_TM):
    t, d = res.shape
    n_steps = t // tm
    slot_spec = lambda f: pl.BlockSpec((tm * TOP_K,), f, memory_space=pltpu.SMEM)
    return pl.pallas_call(
        functools.partial(_combine_kernel, tm=tm),
        grid=(n_steps,),
        in_specs=[slot_spec(lambda i: (i,)),
                  slot_spec(lambda i: (jnp.minimum(i + 1, n_steps - 1),)),
                  pl.BlockSpec((tm, LANES), lambda i: (i, 0)),
                  pl.BlockSpec((tm, d), lambda i: (i, 0)),
                  pl.BlockSpec((1, d), lambda i: (0, 0)),
                  pl.BlockSpec((1, d), lambda i: (0, 0)),
                  pl.BlockSpec(memory_space=pl.ANY)],
        out_specs=pl.BlockSpec((tm, d), lambda i: (i, 0)),
        scratch_shapes=[pltpu.VMEM((2, TOP_K, tm * TOKEN_TILE_ROWS, LANES), F32),
                        pltpu.SemaphoreType.DMA((2,))],
        out_shape=jax.ShapeDtypeStruct((t, d), F32),
        compiler_params=_params(),
        name="combine",
    )(slot_flat, slot_flat, gates, res, ln2_g.reshape(1, d), ln2_b.reshape(1, d), y_tiles)


def kernel(x, p, w_in, conv_w, a_log, dt_bias, gnorm_w, w_out, ln1_g, ln1_b, w_router, b_router,
           w_gate_up, b_gate_up, w_down, b_down, w_ple_gate, b_ple_gate, w_ple_proj, ln2_g, ln2_b):
    bsz, seq, d = x.shape
    t = bsz * seq
    x2d = x.reshape(t, d)
    qkv_a, qkv_b, z_b, scal, scal_t = in_proj(x2d, w_in[0])
    ya = dilated_attn(qkv_a, bsz, seq)
    yb = deltanet(qkv_b, z_b, scal, scal_t, conv_w[0], a_log[0], dt_bias[0], gnorm_w[0], bsz, seq)
    x1, res, route, gates, counts = mix_route(
        ya, yb, x2d, p[0].reshape(t, PLE_DIM), w_out[0], ln1_g[0], ln1_b[0], w_router[0],
        b_router[0], w_ple_gate[0], b_ple_gate[0], w_ple_proj[0])

    cnt = counts[0, :N_EXPERTS].astype(jnp.int32)
    padded = (cnt + MOE_TILE - 1) // MOE_TILE * MOE_TILE
    pad_end = jnp.cumsum(padded).astype(jnp.int32)
    pad_start = pad_end - padded
    n_blocks = (t * TOP_K) // MOE_TILE + N_EXPERTS
    n_used = (pad_end[-1:] // MOE_TILE).astype(jnp.int32)
    tile_start = jnp.arange(n_blocks, dtype=jnp.int32) * MOE_TILE
    block_e = jnp.minimum(jnp.sum(pad_end[None, :] <= tile_start[:, None], axis=1),
                          N_EXPERTS - 1).astype(jnp.int32)
    experts_iota = jnp.arange(N_EXPERTS, dtype=jnp.int32)
    group_off = jnp.sum(jnp.where(route[:, :TOP_K, None] == experts_iota, pad_start, 0), axis=-1)
    slot_flat = (group_off + route[:, TOP_K:2 * TOP_K]).reshape(-1).astype(jnp.int32)

    xb = dispatch(x1, slot_flat, pad_start, pad_end, n_blocks * MOE_TILE)
    yexp = experts(xb, block_e, n_used, w_gate_up[0], b_gate_up[0], w_down[0], b_down[0])
    out = combine(yexp, slot_flat, gates, res, ln2_g[0], ln2_b[0])
    return out.reshape(bsz, seq, d)
```

```python
import functools
import math

import jax
import jax.numpy as jnp
from jax import lax
from jax.experimental import pallas as pl
from jax.experimental.pallas import tpu as pltpu

LANES = 128
VMEM_LIMIT_BYTES = 56 * 1024 * 1024

D_MODEL = 1024
PLE_DIM = 256
A_HEADS = 8
A_HEAD_DIM = 64
A_WIDTH = A_HEADS * A_HEAD_DIM
DILATIONS = (16, 4, 1)
ATT_BLOCK = 128
B_HEADS = 4
B_HEAD_DIM = 128
B_WIDTH = B_HEADS * B_HEAD_DIM
CONV_WIDTH = 4
DELTA_CHUNK = 64
N_EXPERTS = 32
TOP_K = 4
D_EXPERT = D_MODEL
SWIGLU_LIMIT = 7.0
SWIGLU_ALPHA = 1.702
MOE_BLOCK = 128
LN_EPS = 1e-5
RMS_EPS = 1e-6
L2_EPS = 1e-6
DEPTH = 1
DEEPNORM_ALPHA = (2.0 * DEPTH) ** 0.25

F32 = jnp.float32
BF16 = jnp.bfloat16


def _params(n_parallel_axes=1):
    return pltpu.CompilerParams(
        dimension_semantics=("arbitrary",) * n_parallel_axes,
        vmem_limit_bytes=VMEM_LIMIT_BYTES)


def _in_proj_kernel(x_ref, wa_ref, wb_ref, wz_ref, ws_ref, wst_ref, a_ref, b_ref, z_ref, s_ref, st_ref):
    xb = x_ref[...].astype(BF16)
    a_ref[...] = jnp.dot(xb, wa_ref[...], preferred_element_type=F32)
    b_ref[...] = jnp.dot(xb, wb_ref[...], preferred_element_type=F32).astype(BF16)
    z_ref[...] = jnp.dot(xb, wz_ref[...], preferred_element_type=F32)
    s_ref[...] = jnp.dot(xb, ws_ref[...], preferred_element_type=F32)
    st_ref[...] = lax.dot_general(wst_ref[...], xb, (((1,), (1,)), ((), ())),
                                  preferred_element_type=F32)


def in_proj(x2d, w_in, tm=512):
    t, d = x2d.shape
    na, nb = 3 * A_WIDTH, 3 * B_WIDTH
    wa = w_in[:, :na].astype(BF16)
    wb = w_in[:, na:na + nb].astype(BF16)
    wz = w_in[:, na + nb:na + nb + B_WIDTH].astype(BF16)
    ws = jnp.pad(w_in[:, na + nb + B_WIDTH:], ((0, 0), (0, LANES - 2 * B_HEADS))).astype(BF16)
    wst = ws[:, :16].T
    full = lambda a: pl.BlockSpec(a.shape, lambda i: (0, 0))
    return pl.pallas_call(
        _in_proj_kernel,
        grid=(t // tm,),
        in_specs=[pl.BlockSpec((tm, d), lambda i: (i, 0)),
                  full(wa), full(wb), full(wz), full(ws), full(wst)],
        out_specs=[pl.BlockSpec((tm, na), lambda i: (i, 0)),
                   pl.BlockSpec((tm, nb), lambda i: (i, 0)),
                   pl.BlockSpec((tm, B_WIDTH), lambda i: (i, 0)),
                   pl.BlockSpec((tm, LANES), lambda i: (i, 0)),
                   pl.BlockSpec((16, tm), lambda i: (0, i))],
        out_shape=[jax.ShapeDtypeStruct((t, na), F32),
                   jax.ShapeDtypeStruct((t, nb), BF16),
                   jax.ShapeDtypeStruct((t, B_WIDTH), F32),
                   jax.ShapeDtypeStruct((t, LANES), F32),
                   jax.ShapeDtypeStruct((16, t), F32)],
        compiler_params=_params(),
        name="in_proj",
    )(x2d, wa, wb, wz, ws, wst)


ATTN_UNROLL_DENSE = 3
ATTN_UNROLL_SINGLE = 4


def _attn_kernel(q_ref, k_ref, v_ref, o_ref, m_sc, l_sc, acc_sc, *, seq):
    blk = ATT_BLOCK
    lane = lax.broadcasted_iota(jnp.int32, (blk, LANES), 1)
    head0 = lane < A_HEAD_DIM
    row = lax.broadcasted_iota(jnp.int32, (blk, blk), 0)
    col = lax.broadcasted_iota(jnp.int32, (blk, blk), 1)
    prev_ok = col >= row
    cur_ok = col <= row
    neg_inf = jnp.float32(-jnp.inf)
    scale = 1.0 / math.sqrt(A_HEAD_DIM)

    def blocks(starts, stride, has_prev, first_branch, last_branch):
        def rows(ref, s0):
            if stride == 1:
                return ref[pl.ds(s0, blk), :]
            return ref[pl.ds(s0, blk, stride=stride), :]

        def put(ref, s0, val):
            if stride == 1:
                ref[pl.ds(s0, blk), :] = val
            else:
                ref[pl.ds(s0, blk, stride=stride), :] = val

        hmask = (head0, jnp.logical_not(head0))
        items = []
        for start, hp in zip(starts, has_prev):
            q = rows(q_ref, start) * scale
            keys = rows(k_ref, start)
            vals = rows(v_ref, start)
            if hp:
                keys = jnp.concatenate([rows(k_ref, start - blk * stride), keys], axis=0)
                vals = jnp.concatenate([rows(v_ref, start - blk * stride), vals], axis=0)
            it = dict(start=start, hp=hp, keys=keys.astype(BF16), vals=vals,
                      qh=[jnp.where(hm, q, 0.0).astype(BF16) for hm in hmask])
            if not first_branch:
                it["m_old"] = rows(m_sc, start)
                it["l_old"] = rows(l_sc, start)
                it["acc_old"] = rows(acc_sc, start)
            items.append(it)

        for it in items:
            it["s"] = [lax.dot_general(qh, it["keys"], (((1,), (1,)), ((), ())),
                                       preferred_element_type=F32) for qh in it["qh"]]
        for it in items:
            ok = jnp.concatenate([prev_ok, cur_ok], axis=1) if it["hp"] else cur_ok
            it["m"], it["sum"], it["p"] = [], [], []
            for h in range(2):
                s = jnp.where(ok, it["s"][h], neg_inf)
                t = jnp.maximum(s[:, :blk], s[:, blk:]) if it["hp"] else s
                if not first_branch:
                    t = jnp.maximum(t, jnp.where(hmask[h], it["m_old"], neg_inf))
                m_h = jnp.max(t, axis=1, keepdims=True)
                p = jnp.exp(s - m_h)
                psum = p[:, :blk] + p[:, blk:] if it["hp"] else p
                it["m"].append(m_h)
                it["sum"].append(jnp.sum(psum, axis=1, keepdims=True))
                it["p"].append(p.astype(BF16))
        for it in items:
            v16 = it["vals"].astype(BF16)
            it["pv"] = [jnp.dot(it["p"][h], v16, preferred_element_type=F32) for h in range(2)]
        for it in items:
            m_new = jnp.where(head0, it["m"][0], it["m"][1])
            l_new = jnp.where(head0, it["sum"][0], it["sum"][1])
            acc_new = jnp.where(head0, it["pv"][0], it["pv"][1])
            if not first_branch:
                alpha = jnp.exp(it["m_old"] - m_new)
                l_new = l_new + alpha * it["l_old"]
                acc_new = acc_new + alpha * it["acc_old"]
            if last_branch:
                put(o_ref, it["start"], acc_new / l_new)
            else:
                put(m_sc, it["start"], m_new)
                put(l_sc, it["start"], l_new)
                put(acc_sc, it["start"], acc_new)

    n_br = len(DILATIONS)
    for bi, dil in enumerate(DILATIONS):
        first, last = bi == 0, bi == n_br - 1
        nblk = seq // dil // blk
        if dil == 1:
            blocks([0], 1, [False], first, last)
            per = ATTN_UNROLL_DENSE
            assert (nblk - 1) % per == 0

            def body(g, c, first=first, last=last, per=per):
                starts = [pl.multiple_of((1 + g * per + j) * blk, blk) for j in range(per)]
                blocks(starts, 1, [True] * per, first, last)
                return c
            lax.fori_loop(0, (nblk - 1) // per, body, 0)
        elif nblk > 1:
            def body(r, c, dil=dil, nblk=nblk, first=first, last=last):
                blocks([r + n * blk * dil for n in range(nblk)], dil,
                       [n > 0 for n in range(nblk)], first, last)
                return c
            lax.fori_loop(0, dil, body, 0)
        else:
            per = ATTN_UNROLL_SINGLE
            assert dil % per == 0

            def body(g, c, dil=dil, first=first, last=last, per=per):
                blocks([g * per + j for j in range(per)], dil, [False] * per, first, last)
                return c
            lax.fori_loop(0, dil // per, body, 0)


def dilated_attn(qkv_a, bsz, seq):
    t = bsz * seq
    n_pairs = A_WIDTH // LANES
    blk = lambda off: pl.BlockSpec((seq, LANES), lambda b, p: (b, off + p))
    return pl.pallas_call(
        functools.partial(_attn_kernel, seq=seq),
        grid=(bsz, n_pairs),
        in_specs=[blk(0), blk(n_pairs), blk(2 * n_pairs)],
        out_specs=pl.BlockSpec((seq, LANES), lambda b, p: (b, p)),
        out_shape=jax.ShapeDtypeStruct((t, A_WIDTH), F32),
        scratch_shapes=[pltpu.VMEM((seq, LANES), F32)] * 3,
        compiler_params=_params(2),
        name="dilated_attn",
    )(qkv_a, qkv_a, qkv_a)


DELTA_HEADS_PER_STEP = 4
DELTA_P1_CHUNKS = 4
CONV_TILE = 256
CONV_HISTORY_ROWS = 16
_HI = lax.Precision.HIGHEST


def _dot_hi(a, b):
    return jnp.dot(a, b, precision=_HI, preferred_element_type=F32)


def _dot16(a, b):
    return jnp.dot(a, b, preferred_element_type=F32)


def _split2(x):
    hi = x.astype(BF16)
    return hi, (x - hi.astype(F32)).astype(BF16)


def _dot3(a, b):
    return _dot16(a[0], b[0]) + (_dot16(a[1], b[0]) + _dot16(a[0], b[1]))


def _split3(x):
    hi = x.astype(BF16)
    r = x - hi.astype(F32)
    mid = r.astype(BF16)
    return hi, mid, (r - mid.astype(F32)).astype(BF16)


def _dot_sel(m01, x):
    hi, mid, lo = _split3(x)
    return _dot16(m01, hi) + (_dot16(m01, mid) + _dot16(m01, lo))


def _dot_sel_r(x, m01):
    hi, mid, lo = _split3(x)
    return _dot16(hi, m01) + (_dot16(mid, m01) + _dot16(lo, m01))


def _stack_parts(x, lo_half, want_l=True, want_r=True):
    xh = x.astype(BF16)
    xh_f = xh.astype(F32)
    xl_f = x - xh_f
    left = right = None
    if want_l:
        mix = jnp.where(lo_half, xh_f, xl_f).astype(BF16)
        left = jnp.concatenate([mix, mix], axis=1)
    if want_r:
        xl = xl_f.astype(BF16)
        right = jnp.concatenate([xh, xh, xl, xl], axis=0)
    return left, right


def _softplus(x):
    return jnp.maximum(x, 0.0) + jnp.log(1.0 + jnp.exp(-jnp.abs(x)))


def _sigmoid(x):
    return 1.0 / (1.0 + jnp.exp(-x))


def _delta_kernel(q_ref, k_ref, v_ref, z_ref, sc_ref, st_ref, par_ref, part_ref,
                  wq_ref, wk_ref, wv_ref, nw_ref, o_ref,
                  conv_sc, qn_sc, kn_sc, vn_sc, w_sc, attn_sc, kdt_sc, gcr_sc, egl_sc,
                  state_sc, *, seq, hps):
    c = DELTA_CHUNK
    dk = B_HEAD_DIM
    n_chunks = seq // c
    tile = DELTA_P1_CHUNKS * c
    group = pl.program_id(1)
    lane = lax.broadcasted_iota(jnp.int32, (tile, LANES), 1)
    r2 = lax.broadcasted_iota(jnp.int32, (tile, tile), 0)
    c2 = lax.broadcasted_iota(jnp.int32, (tile, tile), 1)
    same_chunk = (r2 // c) == (c2 // c)
    cum_mat = jnp.where(same_chunk & (c2 <= r2), 1.0, 0.0).astype(BF16)
    tot_mat = jnp.where(same_chunk, 1.0, 0.0).astype(BF16)
    ri = lax.broadcasted_iota(jnp.int32, (c, 2 * c), 0)
    li = lax.broadcasted_iota(jnp.int32, (c, 2 * c), 1)
    lo_half = li < c
    ci = jnp.where(lo_half, li, li - c)
    causal = ci <= ri
    strict = ci < ri
    eye = jnp.where(ci == ri, 1.0, 0.0).astype(F32)
    upper = jnp.where(ri <= ci, 1.0, 0.0).astype(BF16)

    hist = CONV_HISTORY_ROWS
    for hh in range(hps):
        cols = slice(hh * dk, (hh + 1) * dk)
        streams = ((q_ref, wq_ref, qn_sc, "q"), (k_ref, wk_ref, kn_sc, "k"), (v_ref, wv_ref, vn_sc, "v"))

        def conv_tile(si, xe, c0, hh=hh, cols=cols):
            _, w_ref, dst, kind = streams[si]
            w = w_ref[:, cols]
            conv_sc[si] = xe
            off = hist - (CONV_WIDTH - 1)
            y = w[0:1, :] * conv_sc[si, off:off + CONV_TILE, :]
            for j in range(1, CONV_WIDTH):
                y = y + w[j:j + 1, :] * conv_sc[si, off + j:off + j + CONV_TILE, :]
            y = y * _sigmoid(y)
            if kind != "v":
                y = y * lax.rsqrt(jnp.sum(y * y, axis=1, keepdims=True) + L2_EPS)
            if kind == "q":
                y = y * (B_HEAD_DIM ** -0.5)
            dst[hh, pl.ds(c0, CONV_TILE), :] = y

        for si, (src, _, _, _) in enumerate(streams):
            conv_tile(si, jnp.concatenate([jnp.zeros((hist, dk), F32),
                                           src[0:CONV_TILE, cols].astype(F32)], axis=0), 0)

        def conv_body(ti, carry, cols=cols, conv_tile=conv_tile):
            c0 = pl.multiple_of(ti * CONV_TILE, CONV_TILE)
            for si, (src, _, _, _) in enumerate(streams):
                conv_tile(si, src[pl.ds(c0 - hist, hist + CONV_TILE), cols].astype(F32), c0)
            return carry

        lax.fori_loop(1, seq // CONV_TILE, conv_body, 0)

    for hh in range(hps):
        h = group * hps + hh
        a_row = st_ref[B_HEADS + h]
        g_row = -jnp.exp(part_ref[0, h]) * _softplus(a_row + part_ref[1, h])
        gcr_sc[hh] = _dot_sel_r(g_row, upper)

    alog_l = par_ref[0:1, :]
    dtb_l = par_ref[1:2, :]

    def phase1(i, carry):
        r0 = pl.multiple_of(i * tile, tile)
        rows = pl.ds(r0, tile)
        s = sc_ref[rows, :]
        beta_all = _sigmoid(s)
        g_all = -jnp.exp(alog_l) * _softplus(s + dtb_l)
        gc_all = _dot_sel(cum_mat, g_all)
        gl_all = _dot_sel(tot_mat, g_all)
        chains = []
        for hh in range(hps):
            h = group * hps + hh
            pick = lambda a, l: jnp.sum(jnp.where(lane == l, a, 0.0), axis=1, keepdims=True)
            beta = pick(beta_all, h)
            gc = pick(gc_all, B_HEADS + h)
            gl = pick(gl_all, B_HEADS + h)
            egc = jnp.exp(gc)
            q = qn_sc[hh, rows, :]
            k = kn_sc[hh, rows, :]
            v = vn_sc[hh, rows, :]
            kbeta = k * beta
            vbeta = v * beta
            qn_sc[hh, rows, :] = q * egc
            kdec = k * jnp.exp(gl - gc)
            kb16 = k.astype(BF16)
            for cc in range(DELTA_P1_CHUNKS):
                sl = slice(cc * c, (cc + 1) * c)
                n = DELTA_P1_CHUNKS * i + cc
                kdt_sc[hh, n] = kdec[sl].T.astype(BF16)
                egl_sc[hh, pl.ds(n, 1), :] = jnp.broadcast_to(jnp.exp(gl[cc * c:cc * c + 1]), (1, dk))
                diff = gc[sl] - gcr_sc[hh, pl.ds(n, 1), :]
                e = jnp.exp(jnp.where(causal, diff, 0.0))
                k2 = jnp.concatenate([kb16[sl], kb16[sl]], axis=0)
                lhs = jnp.concatenate([kbeta[sl], q[sl]], axis=0).astype(BF16)
                rhs = jnp.concatenate([vbeta[sl], kbeta[sl] * egc[sl]], axis=1)
                chains.append(dict(hh=hh, row=r0 + cc * c, e=e, k2=k2, lhs=lhs, rhs=rhs))

        for ch in chains:
            kq = lax.dot_general(ch["lhs"], ch["k2"], (((1,), (1,)), ((), ())),
                                 preferred_element_type=F32)
            ch["a"] = jnp.where(strict, kq[:c] * ch["e"], 0.0)
            attn_sc[ch["hh"], pl.ds(ch["row"], c), :] = (
                jnp.where(causal, kq[c:] * ch["e"], 0.0)[:, :c].astype(BF16))
        for ch in chains:
            ch["inv"] = eye - ch["a"]
            p_l, p_r = _stack_parts(ch["a"], lo_half)
            ch["p"] = _dot16(p_l, p_r)
        for _ in range(4):
            for ch in chains:
                p_l, p_r = _stack_parts(ch["p"], lo_half)
                inv_l, _ = _stack_parts(ch["inv"], lo_half, want_r=False)
                res = _dot16(jnp.concatenate([p_l, inv_l], axis=0), p_r)
                ch["p"] = res[:c]
                ch["inv"] = ch["inv"] + res[c:]
        for ch in chains:
            inv_l, _ = _stack_parts(ch["inv"], lo_half, want_r=False)
            _, p_r = _stack_parts(ch["p"], lo_half, want_l=False)
            ch["inv"] = ch["inv"] + _dot16(inv_l, p_r)
        for ch in chains:
            inv_l, _ = _stack_parts(ch["inv"], lo_half, want_r=False)
            _, rhs_r = _stack_parts(ch["rhs"], None, want_l=False)
            sol = _dot16(inv_l, rhs_r)
            vn_sc[ch["hh"], pl.ds(ch["row"], c), :] = sol[:, :dk]
            w_sc[ch["hh"], pl.ds(ch["row"], c), :] = sol[:, dk:].astype(BF16)
        return carry

    lax.fori_loop(0, n_chunks // DELTA_P1_CHUNKS, phase1, 0)

    state_sc[...] = jnp.zeros(state_sc.shape, F32)
    nw = nw_ref[...]

    def phase2(n, carry):
        r0 = pl.multiple_of(n * c, c)
        rows = pl.ds(r0, c)
        heads = range(hps)
        st = [state_sc[hh] for hh in heads]
        st16 = [s_.astype(BF16) for s_ in st]
        lhs = [jnp.concatenate([w_sc[hh, rows, :], qn_sc[hh, rows, :].astype(BF16)], axis=0)
               for hh in heads]
        ws = [_dot16(lhs[hh], st16[hh]) for hh in heads]
        vn16 = [(vn_sc[hh, rows, :] - ws[hh][:c]).astype(BF16) for hh in heads]
        av = [_dot16(attn_sc[hh, rows, :], vn16[hh]) for hh in heads]
        kv = [_dot16(kdt_sc[hh, n], vn16[hh]) for hh in heads]
        for hh in heads:
            state_sc[hh] = st[hh] * egl_sc[hh, pl.ds(n, 1), :] + kv[hh]
            out = ws[hh][c:] + av[hh]
            o = out * lax.rsqrt(jnp.mean(out * out, axis=1, keepdims=True) + RMS_EPS) * nw
            z = z_ref[rows, hh * dk:(hh + 1) * dk]
            o_ref[rows, hh * dk:(hh + 1) * dk] = (o * (z * _sigmoid(z))).astype(o_ref.dtype)
        return carry

    lax.fori_loop(0, n_chunks, phase2, 0)


def deltanet(qkv_b, z_b, scal, scal_t, conv_w, a_log, dt_bias, gnorm_w, bsz, seq):
    t = bsz * seq
    hps = DELTA_HEADS_PER_STEP
    n_groups = B_HEADS // hps
    wg = hps * B_HEAD_DIM
    c = DELTA_CHUNK
    n_chunks = seq // c
    st3 = scal_t.reshape(16, t // c, c)
    par = jnp.zeros((8, LANES), F32)
    par = par.at[0, B_HEADS:2 * B_HEADS].set(a_log).at[1, B_HEADS:2 * B_HEADS].set(dt_bias)
    part = jnp.stack([a_log, dt_bias]).astype(F32)
    blk = lambda off: pl.BlockSpec((seq, wg), lambda b, g: (b, off + g))
    wblk = lambda off: pl.BlockSpec((CONV_WIDTH, wg), lambda b, g: (0, off + g))
    sc = lambda shape, dt: pltpu.VMEM(shape, dt)
    return pl.pallas_call(
        functools.partial(_delta_kernel, seq=seq, hps=hps),
        grid=(bsz, n_groups),
        in_specs=[blk(0), blk(n_groups), blk(2 * n_groups), blk(0),
                  pl.BlockSpec((seq, LANES), lambda b, g: (b, 0)),
                  pl.BlockSpec((16, n_chunks, c), lambda b, g: (0, b, 0)),
                  pl.BlockSpec((8, LANES), lambda b, g: (0, 0)),
                  pl.BlockSpec(memory_space=pltpu.SMEM),
                  wblk(0), wblk(n_groups), wblk(2 * n_groups),
                  pl.BlockSpec((1, B_HEAD_DIM), lambda b, g: (0, 0))],
        out_specs=pl.BlockSpec((seq, wg), lambda b, g: (b, g)),
        out_shape=jax.ShapeDtypeStruct((t, B_WIDTH), BF16),
        scratch_shapes=[sc((3, CONV_HISTORY_ROWS + CONV_TILE, B_HEAD_DIM), F32),
                        sc((hps, seq, B_HEAD_DIM), F32), sc((hps, seq, B_HEAD_DIM), F32),
                        sc((hps, seq, B_HEAD_DIM), F32),
                        sc((hps, seq, B_HEAD_DIM), BF16),
                        sc((hps, seq, c), BF16), sc((hps, n_chunks, B_HEAD_DIM, c), BF16),
                        sc((hps, n_chunks, 2 * c), F32), sc((hps, n_chunks, B_HEAD_DIM), F32),
                        sc((hps, B_HEAD_DIM, B_HEAD_DIM), F32)],
        compiler_params=_params(2),
        name="deltanet",
    )(qkv_b, qkv_b, qkv_b, z_b, scal, st3, par, part,
      conv_w, conv_w, conv_w, gnorm_w.reshape(1, B_HEAD_DIM))


SUBLANES = 8
TOKEN_TILE_ROWS = D_MODEL // LANES
assert TOKEN_TILE_ROWS == SUBLANES


def _store_token_tiles(ref, x, n_tok):
    for j in range(TOKEN_TILE_ROWS):
        ref[pl.ds(j, n_tok, stride=TOKEN_TILE_ROWS), :] = x[:, j * LANES:(j + 1) * LANES]


def _load_token_tiles(ref, n_tok, lead=None):
    idx = lambda j: (pl.ds(j, n_tok, stride=TOKEN_TILE_ROWS), slice(None))
    if lead is None:
        return [ref[idx(j)] for j in range(TOKEN_TILE_ROWS)]
    return [ref[(lead,) + idx(j)] for j in range(TOKEN_TILE_ROWS)]


def _layer_norm(h, g, b):
    mu = jnp.mean(h, axis=1, keepdims=True)
    hc = h - mu
    var = jnp.mean(hc * hc, axis=1, keepdims=True)
    return hc * lax.rsqrt(var + LN_EPS) * g + b


def _mix_route_kernel(ya_ref, yb_ref, x_ref, p_ref, woa_ref, wob_ref, g1_ref, b1_ref,
                      wr_ref, br_ref, wpg_ref, bpg_ref, wpp_ref,
                      x1_ref, res_ref, route_ref, gate_ref, cnt_ref, run_sc, wr2_sc, *, tm):
    i = pl.program_id(0)

    @pl.when(i == 0)
    def _():
        run_sc[...] = jnp.zeros(run_sc.shape, F32)
        w_hi, w_lo = _split2(wr_ref[...])
        wr2_sc[:, :LANES] = w_hi
        wr2_sc[:, LANES:] = w_lo

    mix = (jnp.dot(ya_ref[...].astype(BF16), woa_ref[...], preferred_element_type=F32)
           + jnp.dot(yb_ref[...].astype(BF16), wob_ref[...], preferred_element_type=F32))
    x1 = _layer_norm(DEEPNORM_ALPHA * x_ref[...] + mix, g1_ref[...], b1_ref[...])
    _store_token_tiles(x1_ref, x1, tm)
    x1b = x1.astype(BF16)

    x1l = (x1 - x1b.astype(F32)).astype(BF16)
    r_hi = jnp.dot(x1b, wr2_sc[...], preferred_element_type=F32)
    r_lo = jnp.dot(x1l, wr2_sc[:, :LANES], preferred_element_type=F32)
    logits = r_hi[:, :LANES] + (r_hi[:, LANES:] + r_lo) + br_ref[...]

    pgate = _sigmoid(jnp.dot(x1b, wpg_ref[...], preferred_element_type=F32) + bpg_ref[...])
    proj = jnp.dot(p_ref[...].astype(BF16), wpp_ref[...], preferred_element_type=F32)
    res_ref[...] = DEEPNORM_ALPHA * x1 + pgate * proj

    lane = lax.broadcasted_iota(jnp.int32, (tm, LANES), 1)
    cur = jnp.where(lane < N_EXPERTS, logits, -jnp.inf)
    vals, hots = [], []
    for _ in range(TOP_K):
        m = jnp.max(cur, axis=1, keepdims=True)
        idx = jnp.min(jnp.where(cur == m, lane, LANES), axis=1, keepdims=True)
        hot = lane == idx
        cur = jnp.where(hot, -jnp.inf, cur)
        vals.append(m)
        hots.append((hot, idx))
    exps = [jnp.exp(v - vals[0]) for v in vals]
    den = exps[0] + exps[1] + exps[2] + exps[3]

    member = jnp.zeros((tm, LANES), F32)
    for hot, _ in hots:
        member = member + jnp.where(hot, 1.0, 0.0)
    rr = lax.broadcasted_iota(jnp.int32, (tm, tm), 0)
    cc = lax.broadcasted_iota(jnp.int32, (tm, tm), 1)
    before = jnp.where(cc < rr, 1.0, 0.0).astype(BF16)
    prior = jnp.dot(before, member.astype(BF16), preferred_element_type=F32) + run_sc[0:1, :]
    route = jnp.zeros((tm, LANES), jnp.int32)
    gate = jnp.zeros((tm, LANES), F32)
    for kk, (hot, idx) in enumerate(hots):
        rank = jnp.sum(jnp.where(hot, prior, 0.0), axis=1, keepdims=True).astype(jnp.int32)
        route = jnp.where(lane == kk, idx, route)
        route = jnp.where(lane == TOP_K + kk, rank, route)
        gate = jnp.where(lane == kk, exps[kk] / den, gate)
    route_ref[...] = route
    gate_ref[...] = gate
    run_sc[...] = run_sc[...] + jnp.sum(member, axis=0, keepdims=True)
    cnt_ref[...] = run_sc[...]


def mix_route(ya, yb, x2d, p2d, w_out, ln1_g, ln1_b, w_router, b_router, w_ple_gate, b_ple_gate,
              w_ple_proj, tm=512):
    t, d = x2d.shape
    woa = w_out[:A_WIDTH].astype(BF16)
    wob = w_out[A_WIDTH:].astype(BF16)
    wr = jnp.pad(w_router, ((0, 0), (0, LANES - N_EXPERTS)))
    br = jnp.pad(b_router, (0, LANES - N_EXPERTS)).reshape(1, LANES)
    row = lambda w: pl.BlockSpec((tm, w), lambda i: (i, 0))
    full = lambda a: pl.BlockSpec(a.shape, lambda i: (0,) * a.ndim)
    ops = [woa, wob, ln1_g.reshape(1, d), ln1_b.reshape(1, d), wr, br,
           w_ple_gate.astype(BF16), b_ple_gate.reshape(1, d), w_ple_proj.astype(BF16)]
    return pl.pallas_call(
        functools.partial(_mix_route_kernel, tm=tm),
        grid=(t // tm,),
        in_specs=[row(A_WIDTH), row(B_WIDTH), row(d), row(PLE_DIM)] + [full(a) for a in ops],
        out_specs=[pl.BlockSpec((tm * TOKEN_TILE_ROWS, LANES), lambda i: (i, 0)),
                   row(d), row(LANES), row(LANES),
                   pl.BlockSpec((8, LANES), lambda i: (0, 0))],
        out_shape=[jax.ShapeDtypeStruct((t * TOKEN_TILE_ROWS, LANES), F32),
                   jax.ShapeDtypeStruct((t, d), F32),
                   jax.ShapeDtypeStruct((t, LANES), jnp.int32),
                   jax.ShapeDtypeStruct((t, LANES), F32),
                   jax.ShapeDtypeStruct((8, LANES), F32)],
        scratch_shapes=[pltpu.VMEM((8, LANES), F32), pltpu.VMEM((d, 2 * LANES), BF16)],
        compiler_params=_params(),
        name="mix_route",
    )(ya, yb, x2d, p2d, *ops)


MOE_TILE = 512
DISPATCH_TM = 512
COMBINE_TM = 256
DMA_ROWS_PER_ITER = 8
DMA_PRIORITIES = 2


def _dispatch_kernel(pstart_ref, pend_ref, slot_ref, x_ref, xb_ref, zero_sc, sem, zsem, *, tm):
    i = pl.program_id(0)
    tr = TOKEN_TILE_ROWS
    tile_rows = MOE_TILE * tr

    def zero_copy(e):
        off = pl.multiple_of((pend_ref[e] - MOE_TILE) * tr, tile_rows)
        return pltpu.make_async_copy(zero_sc, xb_ref.at[pl.ds(off, tile_rows), :], zsem)

    @pl.when(i == 0)
    def _():
        zero_sc[...] = jnp.zeros(zero_sc.shape, zero_sc.dtype)
        for e in range(N_EXPERTS):
            @pl.when(pend_ref[e] > pstart_ref[e])
            def _():
                zero_copy(e).start()
        for e in range(N_EXPERTS):
            @pl.when(pend_ref[e] > pstart_ref[e])
            def _():
                zero_copy(e).wait()

        n_tiles = xb_ref.shape[0] // tile_rows
        first_unused = pend_ref[N_EXPERTS - 1] // MOE_TILE

        def tail_copy(j):
            off = pl.multiple_of(j * tile_rows, tile_rows)
            return pltpu.make_async_copy(zero_sc, xb_ref.at[pl.ds(off, tile_rows), :], zsem)

        def tail_start(j, c):
            tail_copy(j).start()
            return c

        def tail_wait(j, c):
            tail_copy(j).wait()
            return c

        lax.fori_loop(first_unused, n_tiles, tail_start, 0)
        lax.fori_loop(first_unused, n_tiles, tail_wait, 0)

    def token_copy(tok, slot):
        src = pl.multiple_of(tok * tr, tr)
        dst = pl.multiple_of(slot * tr, tr)
        return pltpu.make_async_copy(x_ref.at[pl.ds(src, tr), :], xb_ref.at[pl.ds(dst, tr), :], sem)

    def start(g, c):
        toks = [g * DMA_ROWS_PER_ITER + u for u in range(DMA_ROWS_PER_ITER)]
        slots = [[slot_ref[r * TOP_K + kk] for kk in range(TOP_K)] for r in toks]
        for r, row_slots in zip(toks, slots):
            for kk, slot in enumerate(row_slots):
                token_copy(r, slot).start(priority=kk % DMA_PRIORITIES)
        return c

    lax.fori_loop(0, tm // DMA_ROWS_PER_ITER, start, 0)
    for _ in range(TOP_K):
        pltpu.make_async_copy(x_ref, xb_ref.at[pl.ds(0, tm * tr), :], sem).wait()


def dispatch(x1_tiles, slot_flat, pad_start, pad_end, n_slots, tm=DISPATCH_TM):
    t = x1_tiles.shape[0] // TOKEN_TILE_ROWS
    return pl.pallas_call(
        functools.partial(_dispatch_kernel, tm=tm),
        grid_spec=pltpu.PrefetchScalarGridSpec(
            num_scalar_prefetch=2,
            grid=(t // tm,),
            in_specs=[pl.BlockSpec((tm * TOP_K,), lambda i, ps, pe: (i,), memory_space=pltpu.SMEM),
                      pl.BlockSpec((tm * TOKEN_TILE_ROWS, LANES), lambda i, ps, pe: (i, 0))],
            out_specs=pl.BlockSpec(memory_space=pl.ANY),
            scratch_shapes=[pltpu.VMEM((MOE_TILE * TOKEN_TILE_ROWS, LANES), F32),
                            pltpu.SemaphoreType.DMA(()), pltpu.SemaphoreType.DMA(())]),
        out_shape=jax.ShapeDtypeStruct((n_slots * TOKEN_TILE_ROWS, LANES), F32),
        compiler_params=_params(),
        name="dispatch",
    )(pad_start, pad_end, slot_flat, x1_tiles)


EXPERT_CAST_ROWS = 256
EXPERT_SUBTILES = 1


def _expert_kernel(be_ref, nu_ref, x_ref, wgu_ref, bgu_ref, wd_ref, bd_ref, y_ref, wgu_sc, wd_sc):
    i = pl.program_id(0)
    used = i < nu_ref[0]
    new_expert = jnp.logical_or(i == 0, be_ref[i] != be_ref[jnp.maximum(i - 1, 0)])

    @pl.when(jnp.logical_and(used, new_expert))
    def _():
        def cast(j, c):
            r = pl.ds(pl.multiple_of(j * EXPERT_CAST_ROWS, EXPERT_CAST_ROWS), EXPERT_CAST_ROWS)
            wgu_sc[r, :] = wgu_ref[0, r, :].astype(BF16)
            wd_sc[r, :] = wd_ref[0, r, :].astype(BF16)
            return c
        lax.fori_loop(0, D_MODEL // EXPERT_CAST_ROWS, cast, 0)

    @pl.when(used)
    def _():
        sub = MOE_TILE // EXPERT_SUBTILES
        tr = TOKEN_TILE_ROWS
        nchunk = 256
        for s in range(EXPERT_SUBTILES):
            base = s * sub * tr
            xc = [x_ref[pl.ds(base + j, sub, stride=tr), :].astype(BF16) for j in range(tr)]
            xb = jnp.concatenate(xc, axis=1)
            acts = []
            for c0 in range(0, D_EXPERT, nchunk):
                if c0 == 0:
                    half = D_MODEL // 2
                    xs = (jnp.concatenate(xc[:tr // 2], axis=1), jnp.concatenate(xc[tr // 2:], axis=1))
                    g = sum(jnp.dot(xs[h], wgu_sc[h * half:(h + 1) * half, c0:c0 + nchunk],
                                    preferred_element_type=F32) for h in range(2))
                    u = sum(jnp.dot(xs[h], wgu_sc[h * half:(h + 1) * half,
                                                  D_EXPERT + c0:D_EXPERT + c0 + nchunk],
                                    preferred_element_type=F32) for h in range(2))
                else:
                    g = jnp.dot(xb, wgu_sc[:, c0:c0 + nchunk], preferred_element_type=F32)
                    u = jnp.dot(xb, wgu_sc[:, D_EXPERT + c0:D_EXPERT + c0 + nchunk],
                                preferred_element_type=F32)
                g = g + bgu_ref[0, :, c0:c0 + nchunk]
                u = u + bgu_ref[0, :, D_EXPERT + c0:D_EXPERT + c0 + nchunk]
                g = jnp.minimum(g, SWIGLU_LIMIT)
                u = jnp.clip(u, -SWIGLU_LIMIT, SWIGLU_LIMIT)
                acts.append(((u + 1.0) * (g * _sigmoid(SWIGLU_ALPHA * g))).astype(BF16))
            act = jnp.concatenate(acts, axis=1)
            for n0 in range(0, D_MODEL, nchunk):
                yb = (jnp.dot(act, wd_sc[:, n0:n0 + nchunk], preferred_element_type=F32)
                      + bd_ref[0, :, n0:n0 + nchunk])
                for jj in range(nchunk // LANES):
                    j = n0 // LANES + jj
                    y_ref[pl.ds(base + j, sub, stride=tr), :] = yb[:, jj * LANES:(jj + 1) * LANES]

    @pl.when(jnp.logical_not(used))
    def _():
        y_ref[...] = jnp.zeros(y_ref.shape, F32)


def experts(xb_tiles, block_e, n_used, w_gate_up, b_gate_up, w_down, b_down):
    tile_rows = MOE_TILE * TOKEN_TILE_ROWS
    n_blocks = xb_tiles.shape[0] // tile_rows
    d = D_MODEL
    clamp = lambda i, be, nu: jnp.minimum(i, nu[0] - 1)
    return pl.pallas_call(
        _expert_kernel,
        grid_spec=pltpu.PrefetchScalarGridSpec(
            num_scalar_prefetch=2,
            grid=(n_blocks,),
            in_specs=[pl.BlockSpec((tile_rows, LANES), lambda i, be, nu: (clamp(i, be, nu), 0)),
                      pl.BlockSpec((1, d, 2 * D_EXPERT), lambda i, be, nu: (be[i], 0, 0)),
                      pl.BlockSpec((1, 1, 2 * D_EXPERT), lambda i, be, nu: (be[i], 0, 0)),
                      pl.BlockSpec((1, D_EXPERT, d), lambda i, be, nu: (be[i], 0, 0)),
                      pl.BlockSpec((1, 1, d), lambda i, be, nu: (be[i], 0, 0))],
            out_specs=pl.BlockSpec((tile_rows, LANES), lambda i, be, nu: (i, 0)),
            scratch_shapes=[pltpu.VMEM((d, 2 * D_EXPERT), BF16), pltpu.VMEM((D_EXPERT, d), BF16)]),
        out_shape=jax.ShapeDtypeStruct(xb_tiles.shape, F32),
        compiler_params=_params(),
        name="experts",
    )(block_e, n_used, xb_tiles, w_gate_up, b_gate_up.reshape(N_EXPERTS, 1, -1),
      w_down, b_down.reshape(N_EXPERTS, 1, -1))


def _combine_kernel(slot_ref, slot_next_ref, gate_ref, res_ref, g2_ref, b2_ref,
                    yb_ref, o_ref, buf_sc, sem, *, tm):
    i = pl.program_id(0)
    tr = TOKEN_TILE_ROWS
    cur = lax.rem(i, 2)

    def token_copy(buf, kk, r, slot):
        src = pl.multiple_of(slot * tr, tr)
        dst = pl.multiple_of(r * tr, tr)
        return pltpu.make_async_copy(yb_ref.at[pl.ds(src, tr), :],
                                     buf_sc.at[buf, kk, pl.ds(dst, tr), :], sem.at[buf])

    def issue(sref, buf):
        def start(g, c):
            toks = [g * DMA_ROWS_PER_ITER + u for u in range(DMA_ROWS_PER_ITER)]
            slots = [[sref[r * TOP_K + kk] for kk in range(TOP_K)] for r in toks]
            for r, row_slots in zip(toks, slots):
                for kk, slot in enumerate(row_slots):
                    token_copy(buf, kk, r, slot).start(priority=kk % DMA_PRIORITIES)
            return c
        lax.fori_loop(0, tm // DMA_ROWS_PER_ITER, start, 0)

    @pl.when(i == 0)
    def _():
        issue(slot_ref, 0)

    @pl.when(i + 1 < pl.num_programs(0))
    def _():
        issue(slot_next_ref, 1 - cur)

    for kk in range(TOP_K):
        pltpu.make_async_copy(yb_ref.at[pl.ds(0, tm * tr), :], buf_sc.at[cur, kk], sem.at[cur]).wait()

    gate = gate_ref[...]
    chunks = []
    for j in range(TOKEN_TILE_ROWS):
        hj = res_ref[:, j * LANES:(j + 1) * LANES]
        for kk in range(TOP_K):
            hj = hj + gate[:, kk:kk + 1] * buf_sc[cur, kk, pl.ds(j, tm, stride=tr), :]
        chunks.append(hj)
    mu = sum(jnp.sum(hj, axis=1, keepdims=True) for hj in chunks) * (1.0 / D_MODEL)
    cent = [hj - mu for hj in chunks]
    var = sum(jnp.sum(cj * cj, axis=1, keepdims=True) for cj in cent) * (1.0 / D_MODEL)
    inv = lax.rsqrt(var + LN_EPS)
    for j, cj in enumerate(cent):
        cols = slice(j * LANES, (j + 1) * LANES)
        o_ref[:, cols] = cj * inv * g2_ref[:, cols] + b2_ref[:, cols]


def combine(y_tiles, slot_flat, gates, res, ln2_g, ln2_b, tm=COMBINE_TM):
    t, d = res.shape
    n_steps = t // tm
    slot_spec = lambda f: pl.BlockSpec((tm * TOP_K,), f, memory_space=pltpu.SMEM)
    return pl.pallas_call(
        functools.partial(_combine_kernel, tm=tm),
        grid=(n_steps,),
        in_specs=[slot_spec(lambda i: (i,)),
                  slot_spec(lambda i: (jnp.minimum(i + 1, n_steps - 1),)),
                  pl.BlockSpec((tm, LANES), lambda i: (i, 0)),
                  pl.BlockSpec((tm, d), lambda i: (i, 0)),
                  pl.BlockSpec((1, d), lambda i: (0, 0)),
                  pl.BlockSpec((1, d), lambda i: (0, 0)),
                  pl.BlockSpec(memory_space=pl.ANY)],
        out_specs=pl.BlockSpec((tm, d), lambda i: (i, 0)),
        scratch_shapes=[pltpu.VMEM((2, TOP_K, tm * TOKEN_TILE_ROWS, LANES), F32),
                        pltpu.SemaphoreType.DMA((2,))],
        out_shape=jax.ShapeDtypeStruct((t, d), F32),
        compiler_params=_params(),
        name="combine",
    )(slot_flat, slot_flat, gates, res, ln2_g.reshape(1, d), ln2_b.reshape(1, d), y_tiles)


def kernel(x, p, w_in, conv_w, a_log, dt_bias, gnorm_w, w_out, ln1_g, ln1_b, w_router, b_router,
           w_gate_up, b_gate_up, w_down, b_down, w_ple_gate, b_ple_gate, w_ple_proj, ln2_g, ln2_b):
    bsz, seq, d = x.shape
    t = bsz * seq
    x2d = x.reshape(t, d)
    qkv_a, qkv_b, z_b, scal, scal_t = in_proj(x2d, w_in[0])
    ya = dilated_attn(qkv_a, bsz, seq)
    yb = deltanet(qkv_b, z_b, scal, scal_t, conv_w[0], a_log[0], dt_bias[0], gnorm_w[0], bsz, seq)
    x1, res, route, gates, counts = mix_route(
        ya, yb, x2d, p[0].reshape(t, PLE_DIM), w_out[0], ln1_g[0], ln1_b[0], w_router[0],
        b_router[0], w_ple_gate[0], b_ple_gate[0], w_ple_proj[0])

    cnt = counts[0, :N_EXPERTS].astype(jnp.int32)
    padded = (cnt + MOE_TILE - 1) // MOE_TILE * MOE_TILE
    pad_end = jnp.cumsum(padded).astype(jnp.int32)
    pad_start = pad_end - padded
    n_blocks = (t * TOP_K) // MOE_TILE + N_EXPERTS
    n_used = (pad_end[-1:] // MOE_TILE).astype(jnp.int32)
    tile_start = jnp.arange(n_blocks, dtype=jnp.int32) * MOE_TILE
    block_e = jnp.minimum(jnp.sum(pad_end[None, :] <= tile_start[:, None], axis=1),
                          N_EXPERTS - 1).astype(jnp.int32)
    experts_iota = jnp.arange(N_EXPERTS, dtype=jnp.int32)
    group_off = jnp.sum(jnp.where(route[:, :TOP_K, None] == experts_iota, pad_start, 0), axis=-1)
    slot_flat = (group_off + route[:, TOP_K:2 * TOP_K]).reshape(-1).astype(jnp.int32)

    xb = dispatch(x1, slot_flat, pad_start, pad_end, n_blocks * MOE_TILE)
    yexp = experts(xb, block_e, n_used, w_gate_up[0], b_gate_up[0], w_down[0], b_down[0])
    out = combine(yexp, slot_flat, gates, res, ln2_g[0], ln2_b[0])
    return out.reshape(bsz, seq, d)
```

```python
import functools
import math

import jax
import jax.numpy as jnp
from jax import lax
from jax.experimental import pallas as pl
from jax.experimental.pallas import tpu as pltpu

LANES = 128
VMEM_LIMIT_BYTES = 56 * 1024 * 1024

D_MODEL = 1024
PLE_DIM = 256
A_HEADS = 8
A_HEAD_DIM = 64
A_WIDTH = A_HEADS * A_HEAD_DIM
DILATIONS = (16, 4, 1)
ATT_BLOCK = 128
B_HEADS = 4
B_HEAD_DIM = 128
B_WIDTH = B_HEADS * B_HEAD_DIM
CONV_WIDTH = 4
DELTA_CHUNK = 64
N_EXPERTS = 32
TOP_K = 4
D_EXPERT = D_MODEL
SWIGLU_LIMIT = 7.0
SWIGLU_ALPHA = 1.702
MOE_BLOCK = 128
LN_EPS = 1e-5
RMS_EPS = 1e-6
L2_EPS = 1e-6
DEPTH = 1
DEEPNORM_ALPHA = (2.0 * DEPTH) ** 0.25

F32 = jnp.float32
BF16 = jnp.bfloat16


def _params(n_parallel_axes=1):
    return pltpu.CompilerParams(
        dimension_semantics=("arbitrary",) * n_parallel_axes,
        vmem_limit_bytes=VMEM_LIMIT_BYTES)


def _in_proj_kernel(x_ref, wa_ref, wb_ref, wz_ref, ws_ref, wst_ref, a_ref, b_ref, z_ref, s_ref, st_ref):
    xb = x_ref[...].astype(BF16)
    a_ref[...] = jnp.dot(xb, wa_ref[...], preferred_element_type=F32)
    b_ref[...] = jnp.dot(xb, wb_ref[...], preferred_element_type=F32).astype(BF16)
    z_ref[...] = jnp.dot(xb, wz_ref[...], preferred_element_type=F32)
    s_ref[...] = jnp.dot(xb, ws_ref[...], preferred_element_type=F32)
    st_ref[...] = lax.dot_general(wst_ref[...], xb, (((1,), (1,)), ((), ())),
                                  preferred_element_type=F32)


def in_proj(x2d, w_in, tm=512):
    t, d = x2d.shape
    na, nb = 3 * A_WIDTH, 3 * B_WIDTH
    wa = w_in[:, :na].astype(BF16)
    wb = w_in[:, na:na + nb].astype(BF16)
    wz = w_in[:, na + nb:na + nb + B_WIDTH].astype(BF16)
    ws = jnp.pad(w_in[:, na + nb + B_WIDTH:], ((0, 0), (0, LANES - 2 * B_HEADS))).astype(BF16)
    wst = ws[:, :16].T
    full = lambda a: pl.BlockSpec(a.shape, lambda i: (0, 0))
    return pl.pallas_call(
        _in_proj_kernel,
        grid=(t // tm,),
        in_specs=[pl.BlockSpec((tm, d), lambda i: (i, 0)),
                  full(wa), full(wb), full(wz), full(ws), full(wst)],
        out_specs=[pl.BlockSpec((tm, na), lambda i: (i, 0)),
                   pl.BlockSpec((tm, nb), lambda i: (i, 0)),
                   pl.BlockSpec((tm, B_WIDTH), lambda i: (i, 0)),
                   pl.BlockSpec((tm, LANES), lambda i: (i, 0)),
                   pl.BlockSpec((16, tm), lambda i: (0, i))],
        out_shape=[jax.ShapeDtypeStruct((t, na), F32),
                   jax.ShapeDtypeStruct((t, nb), BF16),
                   jax.ShapeDtypeStruct((t, B_WIDTH), F32),
                   jax.ShapeDtypeStruct((t, LANES), F32),
                   jax.ShapeDtypeStruct((16, t), F32)],
        compiler_params=_params(),
        name="in_proj",
    )(x2d, wa, wb, wz, ws, wst)


ATTN_UNROLL_DENSE = 5
ATTN_UNROLL_SINGLE = 8


def _attn_kernel(q_ref, k_ref, v_ref, o_ref, m_sc, l_sc, acc_sc, *, seq):
    blk = ATT_BLOCK
    lane = lax.broadcasted_iota(jnp.int32, (blk, LANES), 1)
    head0 = lane < A_HEAD_DIM
    row = lax.broadcasted_iota(jnp.int32, (blk, blk), 0)
    col = lax.broadcasted_iota(jnp.int32, (blk, blk), 1)
    prev_ok = col >= row
    cur_ok = col <= row
    neg_inf = jnp.float32(-jnp.inf)
    scale = 1.0 / math.sqrt(A_HEAD_DIM)

    def blocks(starts, stride, has_prev, first_branch, last_branch):
        def rows(ref, s0):
            if stride == 1:
                return ref[pl.ds(s0, blk), :]
            return ref[pl.ds(s0, blk, stride=stride), :]

        def put(ref, s0, val):
            if stride == 1:
                ref[pl.ds(s0, blk), :] = val
            else:
                ref[pl.ds(s0, blk, stride=stride), :] = val

        hmask = (head0, jnp.logical_not(head0))
        items = []
        for start, hp in zip(starts, has_prev):
            q = rows(q_ref, start) * scale
            keys = rows(k_ref, start)
            vals = rows(v_ref, start)
            if hp:
                keys = jnp.concatenate([rows(k_ref, start - blk * stride), keys], axis=0)
                vals = jnp.concatenate([rows(v_ref, start - blk * stride), vals], axis=0)
            it = dict(start=start, hp=hp, keys=keys.astype(BF16), vals=vals,
                      qh=[jnp.where(hm, q, 0.0).astype(BF16) for hm in hmask])
            if not first_branch:
                it["m_old"] = rows(m_sc, start)
                it["l_old"] = rows(l_sc, start)
                it["acc_old"] = rows(acc_sc, start)
            items.append(it)

        for it in items:
            it["s"] = [lax.dot_general(qh, it["keys"], (((1,), (1,)), ((), ())),
                                       preferred_element_type=F32) for qh in it["qh"]]
        for it in items:
            ok = jnp.concatenate([prev_ok, cur_ok], axis=1) if it["hp"] else cur_ok
            it["m"], it["sum"], it["p"] = [], [], []
            for h in range(2):
                s = jnp.where(ok, it["s"][h], neg_inf)
                t = jnp.maximum(s[:, :blk], s[:, blk:]) if it["hp"] else s
                if not first_branch:
                    t = jnp.maximum(t, jnp.where(hmask[h], it["m_old"], neg_inf))
                m_h = jnp.max(t, axis=1, keepdims=True)
                p = jnp.exp(s - m_h)
                psum = p[:, :blk] + p[:, blk:] if it["hp"] else p
                it["m"].append(m_h)
                it["sum"].append(jnp.sum(psum, axis=1, keepdims=True))
                it["p"].append(p.astype(BF16))
        for it in items:
            v16 = it["vals"].astype(BF16)
            it["pv"] = [jnp.dot(it["p"][h], v16, preferred_element_type=F32) for h in range(2)]
        for it in items:
            m_new = jnp.where(head0, it["m"][0], it["m"][1])
            l_new = jnp.where(head0, it["sum"][0], it["sum"][1])
            acc_new = jnp.where(head0, it["pv"][0], it["pv"][1])
            if not first_branch:
                alpha = jnp.exp(it["m_old"] - m_new)
                l_new = l_new + alpha * it["l_old"]
                acc_new = acc_new + alpha * it["acc_old"]
            if last_branch:
                put(o_ref, it["start"], acc_new / l_new)
            else:
                put(m_sc, it["start"], m_new)
                put(l_sc, it["start"], l_new)
                put(acc_sc, it["start"], acc_new)

    n_br = len(DILATIONS)
    for bi, dil in enumerate(DILATIONS):
        first, last = bi == 0, bi == n_br - 1
        nblk = seq // dil // blk
        if dil == 1:
            blocks([0], 1, [False], first, last)
            per = ATTN_UNROLL_DENSE
            assert (nblk - 1) % per == 0

            def body(g, c, first=first, last=last, per=per):
                starts = [pl.multiple_of((1 + g * per + j) * blk, blk) for j in range(per)]
                blocks(starts, 1, [True] * per, first, last)
                return c
            lax.fori_loop(0, (nblk - 1) // per, body, 0)
        elif nblk > 1:
            def body(r, c, dil=dil, nblk=nblk, first=first, last=last):
                blocks([r + n * blk * dil for n in range(nblk)], dil,
                       [n > 0 for n in range(nblk)], first, last)
                return c
            lax.fori_loop(0, dil, body, 0)
        else:
            per = ATTN_UNROLL_SINGLE
            assert dil % per == 0

            def body(g, c, dil=dil, first=first, last=last, per=per):
                blocks([g * per + j for j in range(per)], dil, [False] * per, first, last)
                return c
            lax.fori_loop(0, dil // per, body, 0)


def dilated_attn(qkv_a, bsz, seq):
    t = bsz * seq
    n_pairs = A_WIDTH // LANES
    blk = lambda off: pl.BlockSpec((seq, LANES), lambda b, p: (b, off + p))
    return pl.pallas_call(
        functools.partial(_attn_kernel, seq=seq),
        grid=(bsz, n_pairs),
        in_specs=[blk(0), blk(n_pairs), blk(2 * n_pairs)],
        out_specs=pl.BlockSpec((seq, LANES), lambda b, p: (b, p)),
        out_shape=jax.ShapeDtypeStruct((t, A_WIDTH), F32),
        scratch_shapes=[pltpu.VMEM((seq, LANES), F32)] * 3,
        compiler_params=_params(2),
        name="dilated_attn",
    )(qkv_a, qkv_a, qkv_a)


DELTA_HEADS_PER_STEP = 4
DELTA_P1_CHUNKS = 4
CONV_TILE = 256
CONV_HISTORY_ROWS = 16
_HI = lax.Precision.HIGHEST


def _dot_hi(a, b):
    return jnp.dot(a, b, precision=_HI, preferred_element_type=F32)


def _dot16(a, b):
    return jnp.dot(a, b, preferred_element_type=F32)


def _split2(x):
    hi = x.astype(BF16)
    return hi, (x - hi.astype(F32)).astype(BF16)


def _dot3(a, b):
    return _dot16(a[0], b[0]) + (_dot16(a[1], b[0]) + _dot16(a[0], b[1]))


def _split3(x):
    hi = x.astype(BF16)
    r = x - hi.astype(F32)
    mid = r.astype(BF16)
    return hi, mid, (r - mid.astype(F32)).astype(BF16)


def _dot_sel(m01, x):
    hi, mid, lo = _split3(x)
    return _dot16(m01, hi) + (_dot16(m01, mid) + _dot16(m01, lo))


def _dot_sel_r(x, m01):
    hi, mid, lo = _split3(x)
    return _dot16(hi, m01) + (_dot16(mid, m01) + _dot16(lo, m01))


def _stack_parts(x, lo_half, want_l=True, want_r=True):
    xh = x.astype(BF16)
    xh_f = xh.astype(F32)
    xl_f = x - xh_f
    left = right = None
    if want_l:
        mix = jnp.where(lo_half, xh_f, xl_f).astype(BF16)
        left = jnp.concatenate([mix, mix], axis=1)
    if want_r:
        xl = xl_f.astype(BF16)
        right = jnp.concatenate([xh, xh, xl, xl], axis=0)
    return left, right


def _softplus(x):
    return jnp.maximum(x, 0.0) + jnp.log(1.0 + jnp.exp(-jnp.abs(x)))


def _sigmoid(x):
    return 1.0 / (1.0 + jnp.exp(-x))


def _delta_kernel(q_ref, k_ref, v_ref, z_ref, sc_ref, st_ref, par_ref, part_ref,
                  wq_ref, wk_ref, wv_ref, nw_ref, o_ref,
                  conv_sc, qn_sc, kn_sc, vn_sc, w_sc, attn_sc, kdt_sc, gcr_sc, egl_sc,
                  state_sc, *, seq, hps):
    c = DELTA_CHUNK
    dk = B_HEAD_DIM
    n_chunks = seq // c
    tile = DELTA_P1_CHUNKS * c
    group = pl.program_id(1)
    lane = lax.broadcasted_iota(jnp.int32, (tile, LANES), 1)
    r2 = lax.broadcasted_iota(jnp.int32, (tile, tile), 0)
    c2 = lax.broadcasted_iota(jnp.int32, (tile, tile), 1)
    same_chunk = (r2 // c) == (c2 // c)
    cum_mat = jnp.where(same_chunk & (c2 <= r2), 1.0, 0.0).astype(BF16)
    tot_mat = jnp.where(same_chunk, 1.0, 0.0).astype(BF16)
    ri = lax.broadcasted_iota(jnp.int32, (c, 2 * c), 0)
    li = lax.broadcasted_iota(jnp.int32, (c, 2 * c), 1)
    lo_half = li < c
    ci = jnp.where(lo_half, li, li - c)
    causal = ci <= ri
    strict = ci < ri
    eye = jnp.where(ci == ri, 1.0, 0.0).astype(F32)
    upper = jnp.where(ri <= ci, 1.0, 0.0).astype(BF16)

    hist = CONV_HISTORY_ROWS
    for hh in range(hps):
        cols = slice(hh * dk, (hh + 1) * dk)
        streams = ((q_ref, wq_ref, qn_sc, "q"), (k_ref, wk_ref, kn_sc, "k"), (v_ref, wv_ref, vn_sc, "v"))

        def conv_tile(si, xe, c0, hh=hh, cols=cols):
            _, w_ref, dst, kind = streams[si]
            w = w_ref[:, cols]
            conv_sc[si] = xe
            off = hist - (CONV_WIDTH - 1)
            y = w[0:1, :] * conv_sc[si, off:off + CONV_TILE, :]
            for j in range(1, CONV_WIDTH):
                y = y + w[j:j + 1, :] * conv_sc[si, off + j:off + j + CONV_TILE, :]
            y = y * _sigmoid(y)
            if kind != "v":
                y = y * lax.rsqrt(jnp.sum(y * y, axis=1, keepdims=True) + L2_EPS)
            if kind == "q":
                y = y * (B_HEAD_DIM ** -0.5)
            dst[hh, pl.ds(c0, CONV_TILE), :] = y

        for si, (src, _, _, _) in enumerate(streams):
            conv_tile(si, jnp.concatenate([jnp.zeros((hist, dk), F32),
                                           src[0:CONV_TILE, cols].astype(F32)], axis=0), 0)

        def conv_body(ti, carry, cols=cols, conv_tile=conv_tile):
            c0 = pl.multiple_of(ti * CONV_TILE, CONV_TILE)
            for si, (src, _, _, _) in enumerate(streams):
                conv_tile(si, src[pl.ds(c0 - hist, hist + CONV_TILE), cols].astype(F32), c0)
            return carry

        lax.fori_loop(1, seq // CONV_TILE, conv_body, 0)

    for hh in range(hps):
        h = group * hps + hh
        a_row = st_ref[B_HEADS + h]
        g_row = -jnp.exp(part_ref[0, h]) * _softplus(a_row + part_ref[1, h])
        gcr_sc[hh] = _dot_sel_r(g_row, upper)

    alog_l = par_ref[0:1, :]
    dtb_l = par_ref[1:2, :]

    def aligned(x, m):
        return x if isinstance(x, int) else pl.multiple_of(x, m)

    def phase1(i):
        r0 = aligned(i * tile, tile)
        rows = pl.ds(r0, tile)
        s = sc_ref[rows, :]
        beta_all = _sigmoid(s)
        g_all = -jnp.exp(alog_l) * _softplus(s + dtb_l)
        gc_all = _dot_sel(cum_mat, g_all)
        gl_all = _dot_sel(tot_mat, g_all)
        chains = []
        for hh in range(hps):
            h = group * hps + hh
            pick = lambda a, l: jnp.sum(jnp.where(lane == l, a, 0.0), axis=1, keepdims=True)
            beta = pick(beta_all, h)
            gc = pick(gc_all, B_HEADS + h)
            gl = pick(gl_all, B_HEADS + h)
            egc = jnp.exp(gc)
            q = qn_sc[hh, rows, :]
            k = kn_sc[hh, rows, :]
            v = vn_sc[hh, rows, :]
            kbeta = k * beta
            vbeta = v * beta
            qn_sc[hh, rows, :] = q * egc
            kdec = k * jnp.exp(gl - gc)
            kb16 = k.astype(BF16)
            for cc in range(DELTA_P1_CHUNKS):
                sl = slice(cc * c, (cc + 1) * c)
                n = DELTA_P1_CHUNKS * i + cc
                kdt_sc[hh, n] = kdec[sl].T.astype(BF16)
                egl_sc[hh, pl.ds(n, 1), :] = jnp.broadcast_to(jnp.exp(gl[cc * c:cc * c + 1]), (1, dk))
                diff = gc[sl] - gcr_sc[hh, pl.ds(n, 1), :]
                e = jnp.exp(jnp.where(causal, diff, 0.0))
                k2 = jnp.concatenate([kb16[sl], kb16[sl]], axis=0)
                lhs = jnp.concatenate([kbeta[sl], q[sl]], axis=0).astype(BF16)
                rhs = jnp.concatenate([vbeta[sl], kbeta[sl] * egc[sl]], axis=1)
                chains.append(dict(hh=hh, row=r0 + cc * c, e=e, k2=k2, lhs=lhs, rhs=rhs))
        yield

        for ch in chains:
            kq = lax.dot_general(ch["lhs"], ch["k2"], (((1,), (1,)), ((), ())),
                                 preferred_element_type=F32)
            ch["a"] = jnp.where(strict, kq[:c] * ch["e"], 0.0)
            attn_sc[ch["hh"], pl.ds(ch["row"], c), :] = (
                jnp.where(causal, kq[c:] * ch["e"], 0.0)[:, :c].astype(BF16))
        yield
        for ch in chains:
            ch["inv"] = eye - ch["a"]
            p_l, p_r = _stack_parts(ch["a"], lo_half)
            ch["p"] = _dot16(p_l, p_r)
        yield
        for _ in range(4):
            for ch in chains:
                p_l, p_r = _stack_parts(ch["p"], lo_half)
                inv_l, _ = _stack_parts(ch["inv"], lo_half, want_r=False)
                res = _dot16(jnp.concatenate([p_l, inv_l], axis=0), p_r)
                ch["p"] = res[:c]
                ch["inv"] = ch["inv"] + res[c:]
            yield
        for ch in chains:
            inv_l, _ = _stack_parts(ch["inv"], lo_half, want_r=False)
            _, p_r = _stack_parts(ch["p"], lo_half, want_l=False)
            ch["inv"] = ch["inv"] + _dot16(inv_l, p_r)
        yield
        for ch in chains:
            inv_l, _ = _stack_parts(ch["inv"], lo_half, want_r=False)
            _, rhs_r = _stack_parts(ch["rhs"], None, want_l=False)
            sol = _dot16(inv_l, rhs_r)
            vn_sc[ch["hh"], pl.ds(ch["row"], c), :] = sol[:, :dk]
            w_sc[ch["hh"], pl.ds(ch["row"], c), :] = sol[:, dk:].astype(BF16)

    nw = nw_ref[...]

    def phase2(i):
        for cc in range(DELTA_P1_CHUNKS):
            n = DELTA_P1_CHUNKS * i + cc
            r0 = aligned(n * c, c)
            rows = pl.ds(r0, c)
            heads = range(hps)
            st = [state_sc[hh] for hh in heads]
            st16 = [s_.astype(BF16) for s_ in st]
            lhs = [jnp.concatenate([w_sc[hh, rows, :], qn_sc[hh, rows, :].astype(BF16)], axis=0)
                   for hh in heads]
            ws = [_dot16(lhs[hh], st16[hh]) for hh in heads]
            yield
            vn16 = [(vn_sc[hh, rows, :] - ws[hh][:c]).astype(BF16) for hh in heads]
            av = [_dot16(attn_sc[hh, rows, :], vn16[hh]) for hh in heads]
            kv = [_dot16(kdt_sc[hh, n], vn16[hh]) for hh in heads]
            for hh in heads:
                state_sc[hh] = st[hh] * egl_sc[hh, pl.ds(n, 1), :] + kv[hh]
                out = ws[hh][c:] + av[hh]
                o = out * lax.rsqrt(jnp.mean(out * out, axis=1, keepdims=True) + RMS_EPS) * nw
                z = z_ref[rows, hh * dk:(hh + 1) * dk]
                o_ref[rows, hh * dk:(hh + 1) * dk] = (o * (z * _sigmoid(z))).astype(o_ref.dtype)
            yield

    def interleave(*gens):
        live = list(gens)
        while live:
            for g in list(live):
                try:
                    next(g)
                except StopIteration:
                    live.remove(g)

    n_tiles = n_chunks // DELTA_P1_CHUNKS
    state_sc[...] = jnp.zeros(state_sc.shape, F32)
    interleave(phase1(0))

    def pipelined(i, carry):
        interleave(phase1(i), phase2(i - 1))
        return carry

    lax.fori_loop(1, n_tiles, pipelined, 0)
    interleave(phase2(n_tiles - 1))


def deltanet(qkv_b, z_b, scal, scal_t, conv_w, a_log, dt_bias, gnorm_w, bsz, seq):
    t = bsz * seq
    hps = DELTA_HEADS_PER_STEP
    n_groups = B_HEADS // hps
    wg = hps * B_HEAD_DIM
    c = DELTA_CHUNK
    n_chunks = seq // c
    st3 = scal_t.reshape(16, t // c, c)
    par = jnp.zeros((8, LANES), F32)
    par = par.at[0, B_HEADS:2 * B_HEADS].set(a_log).at[1, B_HEADS:2 * B_HEADS].set(dt_bias)
    part = jnp.stack([a_log, dt_bias]).astype(F32)
    blk = lambda off: pl.BlockSpec((seq, wg), lambda b, g: (b, off + g))
    wblk = lambda off: pl.BlockSpec((CONV_WIDTH, wg), lambda b, g: (0, off + g))
    sc = lambda shape, dt: pltpu.VMEM(shape, dt)
    return pl.pallas_call(
        functools.partial(_delta_kernel, seq=seq, hps=hps),
        grid=(bsz, n_groups),
        in_specs=[blk(0), blk(n_groups), blk(2 * n_groups), blk(0),
                  pl.BlockSpec((seq, LANES), lambda b, g: (b, 0)),
                  pl.BlockSpec((16, n_chunks, c), lambda b, g: (0, b, 0)),
                  pl.BlockSpec((8, LANES), lambda b, g: (0, 0)),
                  pl.BlockSpec(memory_space=pltpu.SMEM),
                  wblk(0), wblk(n_groups), wblk(2 * n_groups),
                  pl.BlockSpec((1, B_HEAD_DIM), lambda b, g: (0, 0))],
        out_specs=pl.BlockSpec((seq, wg), lambda b, g: (b, g)),
        out_shape=jax.ShapeDtypeStruct((t, B_WIDTH), BF16),
        scratch_shapes=[sc((3, CONV_HISTORY_ROWS + CONV_TILE, B_HEAD_DIM), F32),
                        sc((hps, seq, B_HEAD_DIM), F32), sc((hps, seq, B_HEAD_DIM), F32),
                        sc((hps, seq, B_HEAD_DIM), F32),
                        sc((hps, seq, B_HEAD_DIM), BF16),
                        sc((hps, seq, c), BF16), sc((hps, n_chunks, B_HEAD_DIM, c), BF16),
                        sc((hps, n_chunks, 2 * c), F32), sc((hps, n_chunks, B_HEAD_DIM), F32),
                        sc((hps, B_HEAD_DIM, B_HEAD_DIM), F32)],
        compiler_params=_params(2),
        name="deltanet",
    )(qkv_b, qkv_b, qkv_b, z_b, scal, st3, par, part,
      conv_w, conv_w, conv_w, gnorm_w.reshape(1, B_HEAD_DIM))


SUBLANES = 8
TOKEN_TILE_ROWS = D_MODEL // LANES
assert TOKEN_TILE_ROWS == SUBLANES


def _store_token_tiles(ref, x, n_tok):
    for j in range(TOKEN_TILE_ROWS):
        ref[pl.ds(j, n_tok, stride=TOKEN_TILE_ROWS), :] = x[:, j * LANES:(j + 1) * LANES]


def _load_token_tiles(ref, n_tok, lead=None):
    idx = lambda j: (pl.ds(j, n_tok, stride=TOKEN_TILE_ROWS), slice(None))
    if lead is None:
        return [ref[idx(j)] for j in range(TOKEN_TILE_ROWS)]
    return [ref[(lead,) + idx(j)] for j in range(TOKEN_TILE_ROWS)]


def _layer_norm(h, g, b):
    mu = jnp.mean(h, axis=1, keepdims=True)
    hc = h - mu
    var = jnp.mean(hc * hc, axis=1, keepdims=True)
    return hc * lax.rsqrt(var + LN_EPS) * g + b


def _mix_route_kernel(ya_ref, yb_ref, x_ref, p_ref, woa_ref, wob_ref, g1_ref, b1_ref,
                      wr_ref, br_ref, wpg_ref, bpg_ref, wpp_ref,
                      x1_ref, res_ref, route_ref, gate_ref, cnt_ref, run_sc, wr2_sc, *, tm):
    i = pl.program_id(0)

    @pl.when(i == 0)
    def _():
        run_sc[...] = jnp.zeros(run_sc.shape, F32)
        w_hi, w_lo = _split2(wr_ref[...])
        wr2_sc[:, :LANES] = w_hi
        wr2_sc[:, LANES:] = w_lo

    mix = (jnp.dot(ya_ref[...].astype(BF16), woa_ref[...], preferred_element_type=F32)
           + jnp.dot(yb_ref[...].astype(BF16), wob_ref[...], preferred_element_type=F32))
    x1 = _layer_norm(DEEPNORM_ALPHA * x_ref[...] + mix, g1_ref[...], b1_ref[...])
    _store_token_tiles(x1_ref, x1, tm)
    x1b = x1.astype(BF16)

    x1l = (x1 - x1b.astype(F32)).astype(BF16)
    r_hi = jnp.dot(x1b, wr2_sc[...], preferred_element_type=F32)
    r_lo = jnp.dot(x1l, wr2_sc[:, :LANES], preferred_element_type=F32)
    logits = r_hi[:, :LANES] + (r_hi[:, LANES:] + r_lo) + br_ref[...]

    pgate = _sigmoid(jnp.dot(x1b, wpg_ref[...], preferred_element_type=F32) + bpg_ref[...])
    proj = jnp.dot(p_ref[...].astype(BF16), wpp_ref[...], preferred_element_type=F32)
    res_ref[...] = DEEPNORM_ALPHA * x1 + pgate * proj

    lane = lax.broadcasted_iota(jnp.int32, (tm, LANES), 1)
    cur = jnp.where(lane < N_EXPERTS, logits, -jnp.inf)
    vals, hots = [], []
    for _ in range(TOP_K):
        m = jnp.max(cur, axis=1, keepdims=True)
        idx = jnp.min(jnp.where(cur == m, lane, LANES), axis=1, keepdims=True)
        hot = lane == idx
        cur = jnp.where(hot, -jnp.inf, cur)
        vals.append(m)
        hots.append((hot, idx))
    exps = [jnp.exp(v - vals[0]) for v in vals]
    den = exps[0] + exps[1] + exps[2] + exps[3]

    member = jnp.zeros((tm, LANES), F32)
    for hot, _ in hots:
        member = member + jnp.where(hot, 1.0, 0.0)
    rr = lax.broadcasted_iota(jnp.int32, (tm, tm), 0)
    cc = lax.broadcasted_iota(jnp.int32, (tm, tm), 1)
    before = jnp.where(cc < rr, 1.0, 0.0).astype(BF16)
    prior = jnp.dot(before, member.astype(BF16), preferred_element_type=F32) + run_sc[0:1, :]
    route = jnp.zeros((tm, LANES), jnp.int32)
    gate = jnp.zeros((tm, LANES), F32)
    for kk, (hot, idx) in enumerate(hots):
        rank = jnp.sum(jnp.where(hot, prior, 0.0), axis=1, keepdims=True).astype(jnp.int32)
        route = jnp.where(lane == kk, idx, route)
        route = jnp.where(lane == TOP_K + kk, rank, route)
        gate = jnp.where(lane == kk, exps[kk] / den, gate)
    route_ref[...] = route
    gate_ref[...] = gate
    run_sc[...] = run_sc[...] + jnp.sum(member, axis=0, keepdims=True)
    cnt_ref[...] = run_sc[...]


def mix_route(ya, yb, x2d, p2d, w_out, ln1_g, ln1_b, w_router, b_router, w_ple_gate, b_ple_gate,
              w_ple_proj, tm=512):
    t, d = x2d.shape
    woa = w_out[:A_WIDTH].astype(BF16)
    wob = w_out[A_WIDTH:].astype(BF16)
    wr = jnp.pad(w_router, ((0, 0), (0, LANES - N_EXPERTS)))
    br = jnp.pad(b_router, (0, LANES - N_EXPERTS)).reshape(1, LANES)
    row = lambda w: pl.BlockSpec((tm, w), lambda i: (i, 0))
    full = lambda a: pl.BlockSpec(a.shape, lambda i: (0,) * a.ndim)
    tiles = pl.BlockSpec((tm * TOKEN_TILE_ROWS, LANES), lambda i: (i, 0))
    ops = [woa, wob, ln1_g.reshape(1, d), ln1_b.reshape(1, d), wr, br,
           w_ple_gate.astype(BF16), b_ple_gate.reshape(1, d), w_ple_proj.astype(BF16)]
    return pl.pallas_call(
        functools.partial(_mix_route_kernel, tm=tm),
        grid=(t // tm,),
        in_specs=[row(A_WIDTH), row(B_WIDTH), row(d), row(PLE_DIM)] + [full(a) for a in ops],
        out_specs=[tiles, row(d), row(LANES), row(LANES),
                   pl.BlockSpec((8, LANES), lambda i: (0, 0))],
        out_shape=[jax.ShapeDtypeStruct((t * TOKEN_TILE_ROWS, LANES), F32),
                   jax.ShapeDtypeStruct((t, d), F32),
                   jax.ShapeDtypeStruct((t, LANES), jnp.int32),
                   jax.ShapeDtypeStruct((t, LANES), F32),
                   jax.ShapeDtypeStruct((8, LANES), F32)],
        scratch_shapes=[pltpu.VMEM((8, LANES), F32), pltpu.VMEM((d, 2 * LANES), BF16)],
        compiler_params=_params(),
        name="mix_route",
    )(ya, yb, x2d, p2d, *ops)


MOE_TILE = 512
DISPATCH_TM = 512
COMBINE_TM = 256
DMA_ROWS_PER_ITER = 8
DMA_PRIORITIES = 2


def _dispatch_kernel(pstart_ref, pend_ref, slot_ref, x_ref, xb_ref, zero_sc, sem, zsem, *, tm):
    i = pl.program_id(0)
    tr = TOKEN_TILE_ROWS
    tile_rows = MOE_TILE * tr

    def zero_copy(e):
        off = pl.multiple_of((pend_ref[e] - MOE_TILE) * tr, tile_rows)
        return pltpu.make_async_copy(zero_sc, xb_ref.at[pl.ds(off, tile_rows), :], zsem)

    @pl.when(i == 0)
    def _():
        zero_sc[...] = jnp.zeros(zero_sc.shape, zero_sc.dtype)
        for e in range(N_EXPERTS):
            @pl.when(pend_ref[e] > pstart_ref[e])
            def _():
                zero_copy(e).start()
        for e in range(N_EXPERTS):
            @pl.when(pend_ref[e] > pstart_ref[e])
            def _():
                zero_copy(e).wait()

        n_tiles = xb_ref.shape[0] // tile_rows
        first_unused = pend_ref[N_EXPERTS - 1] // MOE_TILE

        def tail_copy(j):
            off = pl.multiple_of(j * tile_rows, tile_rows)
            return pltpu.make_async_copy(zero_sc, xb_ref.at[pl.ds(off, tile_rows), :], zsem)

        def tail_start(j, c):
            tail_copy(j).start()
            return c

        def tail_wait(j, c):
            tail_copy(j).wait()
            return c

        lax.fori_loop(first_unused, n_tiles, tail_start, 0)
        lax.fori_loop(first_unused, n_tiles, tail_wait, 0)

    def token_copy(tok, slot):
        src = pl.multiple_of(tok * tr, tr)
        dst = pl.multiple_of(slot * tr, tr)
        return pltpu.make_async_copy(x_ref.at[pl.ds(src, tr), :], xb_ref.at[pl.ds(dst, tr), :], sem)

    def start(g, c):
        toks = [g * DMA_ROWS_PER_ITER + u for u in range(DMA_ROWS_PER_ITER)]
        slots = [[slot_ref[r * TOP_K + kk] for kk in range(TOP_K)] for r in toks]
        for r, row_slots in zip(toks, slots):
            for kk, slot in enumerate(row_slots):
                token_copy(r, slot).start(priority=kk % DMA_PRIORITIES)
        return c

    lax.fori_loop(0, tm // DMA_ROWS_PER_ITER, start, 0)
    for _ in range(TOP_K):
        pltpu.make_async_copy(x_ref, xb_ref.at[pl.ds(0, tm * tr), :], sem).wait()


def dispatch(x1_tiles, slot_flat, pad_start, pad_end, n_slots, tm=DISPATCH_TM):
    t = x1_tiles.shape[0] // TOKEN_TILE_ROWS
    return pl.pallas_call(
        functools.partial(_dispatch_kernel, tm=tm),
        grid_spec=pltpu.PrefetchScalarGridSpec(
            num_scalar_prefetch=2,
            grid=(t // tm,),
            in_specs=[pl.BlockSpec((tm * TOP_K,), lambda i, ps, pe: (i,), memory_space=pltpu.SMEM),
                      pl.BlockSpec((tm * TOKEN_TILE_ROWS, LANES), lambda i, ps, pe: (i, 0))],
            out_specs=pl.BlockSpec(memory_space=pl.ANY),
            scratch_shapes=[pltpu.VMEM((MOE_TILE * TOKEN_TILE_ROWS, LANES), F32),
                            pltpu.SemaphoreType.DMA(()), pltpu.SemaphoreType.DMA(())]),
        out_shape=jax.ShapeDtypeStruct((n_slots * TOKEN_TILE_ROWS, LANES), F32),
        compiler_params=_params(),
        name="dispatch",
    )(pad_start, pad_end, slot_flat, x1_tiles)


EXPERT_CAST_ROWS = 256
EXPERT_SUBTILES = 1


def _expert_kernel(be_ref, nu_ref, x_ref, wgu_ref, bgu_ref, wd_ref, bd_ref, y_ref, wgu_sc, wd_sc):
    i = pl.program_id(0)
    used = i < nu_ref[0]
    new_expert = jnp.logical_or(i == 0, be_ref[i] != be_ref[jnp.maximum(i - 1, 0)])

    @pl.when(jnp.logical_and(used, new_expert))
    def _():
        def cast(j, c):
            r = pl.ds(pl.multiple_of(j * EXPERT_CAST_ROWS, EXPERT_CAST_ROWS), EXPERT_CAST_ROWS)
            wgu_sc[r, :] = wgu_ref[0, r, :].astype(BF16)
            wd_sc[r, :] = wd_ref[0, r, :].astype(BF16)
            return c
        lax.fori_loop(0, D_MODEL // EXPERT_CAST_ROWS, cast, 0)

    @pl.when(used)
    def _():
        sub = MOE_TILE // EXPERT_SUBTILES
        tr = TOKEN_TILE_ROWS
        nchunk = 256
        for s in range(EXPERT_SUBTILES):
            base = s * sub * tr
            xc = [x_ref[pl.ds(base + j, sub, stride=tr), :].astype(BF16) for j in range(tr)]
            xb = jnp.concatenate(xc, axis=1)
            acts = []
            for c0 in range(0, D_EXPERT, nchunk):
                if c0 == 0:
                    half = D_MODEL // 2
                    xs = (jnp.concatenate(xc[:tr // 2], axis=1), jnp.concatenate(xc[tr // 2:], axis=1))
                    g = sum(jnp.dot(xs[h], wgu_sc[h * half:(h + 1) * half, c0:c0 + nchunk],
                                    preferred_element_type=F32) for h in range(2))
                    u = sum(jnp.dot(xs[h], wgu_sc[h * half:(h + 1) * half,
                                                  D_EXPERT + c0:D_EXPERT + c0 + nchunk],
                                    preferred_element_type=F32) for h in range(2))
                else:
                    g = jnp.dot(xb, wgu_sc[:, c0:c0 + nchunk], preferred_element_type=F32)
                    u = jnp.dot(xb, wgu_sc[:, D_EXPERT + c0:D_EXPERT + c0 + nchunk],
                                preferred_element_type=F32)
                g = g + bgu_ref[0, :, c0:c0 + nchunk]
                u = u + bgu_ref[0, :, D_EXPERT + c0:D_EXPERT + c0 + nchunk]
                g = jnp.minimum(g, SWIGLU_LIMIT)
                u = jnp.clip(u, -SWIGLU_LIMIT, SWIGLU_LIMIT)
                acts.append(((u + 1.0) * (g * _sigmoid(SWIGLU_ALPHA * g))).astype(BF16))
            act = jnp.concatenate(acts, axis=1)
            for n0 in range(0, D_MODEL, nchunk):
                yb = (jnp.dot(act, wd_sc[:, n0:n0 + nchunk], preferred_element_type=F32)
                      + bd_ref[0, :, n0:n0 + nchunk])
                for jj in range(nchunk // LANES):
                    j = n0 // LANES + jj
                    y_ref[pl.ds(base + j, sub, stride=tr), :] = yb[:, jj * LANES:(jj + 1) * LANES]

    @pl.when(jnp.logical_not(used))
    def _():
        y_ref[...] = jnp.zeros(y_ref.shape, F32)


def experts(xb_tiles, block_e, n_used, w_gate_up, b_gate_up, w_down, b_down):
    tile_rows = MOE_TILE * TOKEN_TILE_ROWS
    n_blocks = xb_tiles.shape[0] // tile_rows
    d = D_MODEL
    clamp = lambda i, be, nu: jnp.minimum(i, nu[0] - 1)
    return pl.pallas_call(
        _expert_kernel,
        grid_spec=pltpu.PrefetchScalarGridSpec(
            num_scalar_prefetch=2,
            grid=(n_blocks,),
            in_specs=[pl.BlockSpec((tile_rows, LANES), lambda i, be, nu: (clamp(i, be, nu), 0)),
                      pl.BlockSpec((1, d, 2 * D_EXPERT), lambda i, be, nu: (be[i], 0, 0)),
                      pl.BlockSpec((1, 1, 2 * D_EXPERT), lambda i, be, nu: (be[i], 0, 0)),
                      pl.BlockSpec((1, D_EXPERT, d), lambda i, be, nu: (be[i], 0, 0)),
                      pl.BlockSpec((1, 1, d), lambda i, be, nu: (be[i], 0, 0))],
            out_specs=pl.BlockSpec((tile_rows, LANES), lambda i, be, nu: (i, 0)),
            scratch_shapes=[pltpu.VMEM((d, 2 * D_EXPERT), BF16), pltpu.VMEM((D_EXPERT, d), BF16)]),
        out_shape=jax.ShapeDtypeStruct(xb_tiles.shape, F32),
        compiler_params=_params(),
        name="experts",
    )(block_e, n_used, xb_tiles, w_gate_up, b_gate_up.reshape(N_EXPERTS, 1, -1),
      w_down, b_down.reshape(N_EXPERTS, 1, -1))


def _combine_kernel(slot_ref, slot_next_ref, gate_ref, res_ref, g2_ref, b2_ref,
                    yb_ref, o_ref, buf_sc, sem, *, tm):
    i = pl.program_id(0)
    tr = TOKEN_TILE_ROWS
    cur = lax.rem(i, 2)

    def token_copy(buf, kk, r, slot):
        src = pl.multiple_of(slot * tr, tr)
        dst = pl.multiple_of(r * tr, tr)
        return pltpu.make_async_copy(yb_ref.at[pl.ds(src, tr), :],
                                     buf_sc.at[buf, kk, pl.ds(dst, tr), :], sem.at[buf])

    def issue(sref, buf):
        def start(g, c):
            toks = [g * DMA_ROWS_PER_ITER + u for u in range(DMA_ROWS_PER_ITER)]
            slots = [[sref[r * TOP_K + kk] for kk in range(TOP_K)] for r in toks]
            for r, row_slots in zip(toks, slots):
                for kk, slot in enumerate(row_slots):
                    token_copy(buf, kk, r, slot).start(priority=kk % DMA_PRIORITIES)
            return c
        lax.fori_loop(0, tm // DMA_ROWS_PER_ITER, start, 0)

    @pl.when(i == 0)
    def _():
        issue(slot_ref, 0)

    @pl.when(i + 1 < pl.num_programs(0))
    def _():
        issue(slot_next_ref, 1 - cur)

    for kk in range(TOP_K):
        pltpu.make_async_copy(yb_ref.at[pl.ds(0, tm * tr), :], buf_sc.at[cur, kk], sem.at[cur]).wait()

    gate = gate_ref[...]
    chunks = []
    for j in range(TOKEN_TILE_ROWS):
        hj = res_ref[:, j * LANES:(j + 1) * LANES]
        for kk in range(TOP_K):
            hj = hj + gate[:, kk:kk + 1] * buf_sc[cur, kk, pl.ds(j, tm, stride=tr), :]
        chunks.append(hj)
    mu = sum(jnp.sum(hj, axis=1, keepdims=True) for hj in chunks) * (1.0 / D_MODEL)
    cent = [hj - mu for hj in chunks]
    var = sum(jnp.sum(cj * cj, axis=1, keepdims=True) for cj in cent) * (1.0 / D_MODEL)
    inv = lax.rsqrt(var + LN_EPS)
    for j, cj in enumerate(cent):
        cols = slice(j * LANES, (j + 1) * LANES)
        o_ref[:, cols] = cj * inv * g2_ref[:, cols] + b2_ref[:, cols]


def combine(y_tiles, slot_flat, gates, res, ln2_g, ln2_b, tm=COMBINE_TM):
    t, d = res.shape
    n_steps = t // tm
    slot_spec = lambda f: pl.BlockSpec((tm * TOP_K,), f, memory_space=pltpu.SMEM)
    return pl.pallas_call(
        functools.partial(_combine_kernel, tm=tm),
        grid=(n_steps,),
        in_specs=[slot_spec(lambda i: (i,)),
                  slot_spec(lambda i: (jnp.minimum(i + 1, n_steps - 1),)),
                  pl.BlockSpec((tm, LANES), lambda i: (i, 0)),
                  pl.BlockSpec((tm, d), lambda i: (i, 0)),
                  pl.BlockSpec((1, d), lambda i: (0, 0)),
                  pl.BlockSpec((1, d), lambda i: (0, 0)),
                  pl.BlockSpec(memory_space=pl.ANY)],
        out_specs=pl.BlockSpec((tm, d), lambda i: (i, 0)),
        scratch_shapes=[pltpu.VMEM((2, TOP_K, tm * TOKEN_TILE_ROWS, LANES), F32),
                        pltpu.SemaphoreType.DMA((2,))],
        out_shape=jax.ShapeDtypeStruct((t, d), F32),
        compiler_params=_params(),
        name="combine",
    )(slot_flat, slot_flat, gates, res, ln2_g.reshape(1, d), ln2_b.reshape(1, d), y_tiles)


def kernel(x, p, w_in, conv_w, a_log, dt_bias, gnorm_w, w_out, ln1_g, ln1_b, w_router, b_router,
           w_gate_up, b_gate_up, w_down, b_down, w_ple_gate, b_ple_gate, w_ple_proj, ln2_g, ln2_b):
    bsz, seq, d = x.shape
    t = bsz * seq
    x2d = x.reshape(t, d)
    qkv_a, qkv_b, z_b, scal, scal_t = in_proj(x2d, w_in[0])
    ya = dilated_attn(qkv_a, bsz, seq)
    yb = deltanet(qkv_b, z_b, scal, scal_t, conv_w[0], a_log[0], dt_bias[0], gnorm_w[0], bsz, seq)
    x1, res, route, gates, counts = mix_route(
        ya, yb, x2d, p[0].reshape(t, PLE_DIM), w_out[0], ln1_g[0], ln1_b[0], w_router[0],
        b_router[0], w_ple_gate[0], b_ple_gate[0], w_ple_proj[0])

    cnt = counts[0, :N_EXPERTS].astype(jnp.int32)
    padded = (cnt + MOE_TILE - 1) // MOE_TILE * MOE_TILE
    pad_end = jnp.cumsum(padded).astype(jnp.int32)
    pad_start = pad_end - padded
    n_blocks = (t * TOP_K) // MOE_TILE + N_EXPERTS
    n_used = (pad_end[-1:] // MOE_TILE).astype(jnp.int32)
    tile_start = jnp.arange(n_blocks, dtype=jnp.int32) * MOE_TILE
    block_e = jnp.minimum(jnp.sum(pad_end[None, :] <= tile_start[:, None], axis=1),
                          N_EXPERTS - 1).astype(jnp.int32)
    experts_iota = jnp.arange(N_EXPERTS, dtype=jnp.int32)
    group_off = jnp.sum(jnp.where(route[:, :TOP_K, None] == experts_iota, pad_start, 0), axis=-1)
    slot_flat = (group_off + route[:, TOP_K:2 * TOP_K]).reshape(-1).astype(jnp.int32)

    xb = dispatch(x1, slot_flat, pad_start, pad_end, n_blocks * MOE_TILE)
    yexp = experts(xb, block_e, n_used, w_gate_up[0], b_gate_up[0], w_down[0], b_down[0])
    out = combine(yexp, slot_flat, gates, res, ln2_g[0], ln2_b[0])
    return out.reshape(bsz, seq, d)
```

```python
import functools
import math

import jax
import jax.numpy as jnp
from jax import lax
from jax.experimental import pallas as pl
from jax.experimental.pallas import tpu as pltpu

LANES = 128
VMEM_LIMIT_BYTES = 56 * 1024 * 1024

D_MODEL = 1024
PLE_DIM = 256
A_HEADS = 8
A_HEAD_DIM = 64
A_WIDTH = A_HEADS * A_HEAD_DIM
DILATIONS = (16, 4, 1)
ATT_BLOCK = 128
B_HEADS = 4
B_HEAD_DIM = 128
B_WIDTH = B_HEADS * B_HEAD_DIM
CONV_WIDTH = 4
DELTA_CHUNK = 64
N_EXPERTS = 32
TOP_K = 4
D_EXPERT = D_MODEL
SWIGLU_LIMIT = 7.0
SWIGLU_ALPHA = 1.702
MOE_BLOCK = 128
LN_EPS = 1e-5
RMS_EPS = 1e-6
L2_EPS = 1e-6
DEPTH = 1
DEEPNORM_ALPHA = (2.0 * DEPTH) ** 0.25

F32 = jnp.float32
BF16 = jnp.bfloat16


def _params(n_parallel_axes=1):
    return pltpu.CompilerParams(
        dimension_semantics=("arbitrary",) * n_parallel_axes,
        vmem_limit_bytes=VMEM_LIMIT_BYTES)


def _in_proj_kernel(x_ref, wa_ref, wb_ref, wz_ref, ws_ref, wst_ref, a_ref, b_ref, z_ref, s_ref, st_ref):
    xb = x_ref[...].astype(BF16)
    a_ref[...] = jnp.dot(xb, wa_ref[...], preferred_element_type=F32)
    b_ref[...] = jnp.dot(xb, wb_ref[...], preferred_element_type=F32).astype(BF16)
    z_ref[...] = jnp.dot(xb, wz_ref[...], preferred_element_type=F32)
    s_ref[...] = jnp.dot(xb, ws_ref[...], preferred_element_type=F32)
    st_ref[...] = lax.dot_general(wst_ref[...], xb, (((1,), (1,)), ((), ())),
                                  preferred_element_type=F32)


def in_proj(x2d, w_in, tm=512):
    t, d = x2d.shape
    na, nb = 3 * A_WIDTH, 3 * B_WIDTH
    wa = w_in[:, :na].astype(BF16)
    wb = w_in[:, na:na + nb].astype(BF16)
    wz = w_in[:, na + nb:na + nb + B_WIDTH].astype(BF16)
    ws = jnp.pad(w_in[:, na + nb + B_WIDTH:], ((0, 0), (0, LANES - 2 * B_HEADS))).astype(BF16)
    wst = ws[:, :16].T
    full = lambda a: pl.BlockSpec(a.shape, lambda i: (0, 0))
    return pl.pallas_call(
        _in_proj_kernel,
        grid=(t // tm,),
        in_specs=[pl.BlockSpec((tm, d), lambda i: (i, 0)),
                  full(wa), full(wb), full(wz), full(ws), full(wst)],
        out_specs=[pl.BlockSpec((tm, na), lambda i: (i, 0)),
                   pl.BlockSpec((tm, nb), lambda i: (i, 0)),
                   pl.BlockSpec((tm, B_WIDTH), lambda i: (i, 0)),
                   pl.BlockSpec((tm, LANES), lambda i: (i, 0)),
                   pl.BlockSpec((16, tm), lambda i: (0, i))],
        out_shape=[jax.ShapeDtypeStruct((t, na), F32),
                   jax.ShapeDtypeStruct((t, nb), BF16),
                   jax.ShapeDtypeStruct((t, B_WIDTH), F32),
                   jax.ShapeDtypeStruct((t, LANES), F32),
                   jax.ShapeDtypeStruct((16, t), F32)],
        compiler_params=_params(),
        name="in_proj",
    )(x2d, wa, wb, wz, ws, wst)


ATTN_UNROLL_DENSE = 5
ATTN_UNROLL_SINGLE = 8


def _attn_kernel(q_ref, k_ref, v_ref, o_ref, m_sc, l_sc, acc_sc, *, seq):
    blk = ATT_BLOCK
    lane = lax.broadcasted_iota(jnp.int32, (blk, LANES), 1)
    head0 = lane < A_HEAD_DIM
    row = lax.broadcasted_iota(jnp.int32, (blk, blk), 0)
    col = lax.broadcasted_iota(jnp.int32, (blk, blk), 1)
    prev_ok = col >= row
    cur_ok = col <= row
    neg_inf = jnp.float32(-jnp.inf)
    scale = 1.0 / math.sqrt(A_HEAD_DIM)

    def blocks(starts, stride, has_prev, first_branch, last_branch):
        def rows(ref, s0):
            if stride == 1:
                return ref[pl.ds(s0, blk), :]
            return ref[pl.ds(s0, blk, stride=stride), :]

        def put(ref, s0, val):
            if stride == 1:
                ref[pl.ds(s0, blk), :] = val
            else:
                ref[pl.ds(s0, blk, stride=stride), :] = val

        hmask = (head0, jnp.logical_not(head0))
        items = []
        for start, hp in zip(starts, has_prev):
            q = rows(q_ref, start) * scale
            keys = rows(k_ref, start)
            vals = rows(v_ref, start)
            if hp:
                keys = jnp.concatenate([rows(k_ref, start - blk * stride), keys], axis=0)
                vals = jnp.concatenate([rows(v_ref, start - blk * stride), vals], axis=0)
            it = dict(start=start, hp=hp, keys=keys.astype(BF16), vals=vals,
                      qh=[jnp.where(hm, q, 0.0).astype(BF16) for hm in hmask])
            if not first_branch:
                it["m_old"] = rows(m_sc, start)
                it["l_old"] = rows(l_sc, start)
                it["acc_old"] = rows(acc_sc, start)
            items.append(it)

        for it in items:
            it["s"] = [lax.dot_general(qh, it["keys"], (((1,), (1,)), ((), ())),
                                       preferred_element_type=F32) for qh in it["qh"]]
        for it in items:
            ok = jnp.concatenate([prev_ok, cur_ok], axis=1) if it["hp"] else cur_ok
            it["m"], it["sum"], it["p"] = [], [], []
            for h in range(2):
                s = jnp.where(ok, it["s"][h], neg_inf)
                t = jnp.maximum(s[:, :blk], s[:, blk:]) if it["hp"] else s
                if not first_branch:
                    t = jnp.maximum(t, jnp.where(hmask[h], it["m_old"], neg_inf))
                m_h = jnp.max(t, axis=1, keepdims=True)
                p = jnp.exp(s - m_h)
                psum = p[:, :blk] + p[:, blk:] if it["hp"] else p
                it["m"].append(m_h)
                it["sum"].append(jnp.sum(psum, axis=1, keepdims=True))
                it["p"].append(p.astype(BF16))
        for it in items:
            v16 = it["vals"].astype(BF16)
            it["pv"] = [jnp.dot(it["p"][h], v16, preferred_element_type=F32) for h in range(2)]
        for it in items:
            m_new = jnp.where(head0, it["m"][0], it["m"][1])
            l_new = jnp.where(head0, it["sum"][0], it["sum"][1])
            acc_new = jnp.where(head0, it["pv"][0], it["pv"][1])
            if not first_branch:
                alpha = jnp.exp(it["m_old"] - m_new)
                l_new = l_new + alpha * it["l_old"]
                acc_new = acc_new + alpha * it["acc_old"]
            if last_branch:
                put(o_ref, it["start"], acc_new / l_new)
            else:
                put(m_sc, it["start"], m_new)
                put(l_sc, it["start"], l_new)
                put(acc_sc, it["start"], acc_new)

    n_br = len(DILATIONS)
    for bi, dil in enumerate(DILATIONS):
        first, last = bi == 0, bi == n_br - 1
        nblk = seq // dil // blk
        if dil == 1:
            blocks([0], 1, [False], first, last)
            per = ATTN_UNROLL_DENSE
            assert (nblk - 1) % per == 0

            def body(g, c, first=first, last=last, per=per):
                starts = [pl.multiple_of((1 + g * per + j) * blk, blk) for j in range(per)]
                blocks(starts, 1, [True] * per, first, last)
                return c
            lax.fori_loop(0, (nblk - 1) // per, body, 0)
        elif nblk > 1:
            def body(r, c, dil=dil, nblk=nblk, first=first, last=last):
                blocks([r + n * blk * dil for n in range(nblk)], dil,
                       [n > 0 for n in range(nblk)], first, last)
                return c
            lax.fori_loop(0, dil, body, 0)
        else:
            per = ATTN_UNROLL_SINGLE
            assert dil % per == 0

            def body(g, c, dil=dil, first=first, last=last, per=per):
                blocks([g * per + j for j in range(per)], dil, [False] * per, first, last)
                return c
            lax.fori_loop(0, dil // per, body, 0)


def dilated_attn(qkv_a, bsz, seq):
    t = bsz * seq
    n_pairs = A_WIDTH // LANES
    blk = lambda off: pl.BlockSpec((seq, LANES), lambda b, p: (b, off + p))
    return pl.pallas_call(
        functools.partial(_attn_kernel, seq=seq),
        grid=(bsz, n_pairs),
        in_specs=[blk(0), blk(n_pairs), blk(2 * n_pairs)],
        out_specs=pl.BlockSpec((seq, LANES), lambda b, p: (b, p)),
        out_shape=jax.ShapeDtypeStruct((t, A_WIDTH), F32),
        scratch_shapes=[pltpu.VMEM((seq, LANES), F32)] * 3,
        compiler_params=_params(2),
        name="dilated_attn",
    )(qkv_a, qkv_a, qkv_a)


DELTA_HEADS_PER_STEP = 4
DELTA_P1_CHUNKS = 4
CONV_TILE = 256
CONV_HISTORY_ROWS = 16
_HI = lax.Precision.HIGHEST


def _dot_hi(a, b):
    return jnp.dot(a, b, precision=_HI, preferred_element_type=F32)


def _dot16(a, b):
    return jnp.dot(a, b, preferred_element_type=F32)


def _split2(x):
    hi = x.astype(BF16)
    return hi, (x - hi.astype(F32)).astype(BF16)


def _dot3(a, b):
    return _dot16(a[0], b[0]) + (_dot16(a[1], b[0]) + _dot16(a[0], b[1]))


def _split3(x):
    hi = x.astype(BF16)
    r = x - hi.astype(F32)
    mid = r.astype(BF16)
    return hi, mid, (r - mid.astype(F32)).astype(BF16)


def _dot_sel(m01, x):
    hi, mid, lo = _split3(x)
    return _dot16(m01, hi) + (_dot16(m01, mid) + _dot16(m01, lo))


def _dot_sel_r(x, m01):
    hi, mid, lo = _split3(x)
    return _dot16(hi, m01) + (_dot16(mid, m01) + _dot16(lo, m01))


def _stack_parts(x, lo_half, want_l=True, want_r=True):
    xh = x.astype(BF16)
    xh_f = xh.astype(F32)
    xl_f = x - xh_f
    left = right = None
    if want_l:
        mix = jnp.where(lo_half, xh_f, xl_f).astype(BF16)
        left = jnp.concatenate([mix, mix], axis=1)
    if want_r:
        xl = xl_f.astype(BF16)
        right = jnp.concatenate([xh, xh, xl, xl], axis=0)
    return left, right


def _softplus(x):
    return jnp.maximum(x, 0.0) + jnp.log(1.0 + jnp.exp(-jnp.abs(x)))


def _sigmoid(x):
    return 1.0 / (1.0 + jnp.exp(-x))


def _delta_kernel(q_ref, k_ref, v_ref, z_ref, sc_ref, st_ref, par_ref, part_ref,
                  wq_ref, wk_ref, wv_ref, nw_ref, o_ref,
                  conv_sc, qn_sc, kn_sc, vn_sc, w_sc, attn_sc, kdt_sc, gcr_sc, egl_sc,
                  state_sc, *, seq, hps):
    c = DELTA_CHUNK
    dk = B_HEAD_DIM
    n_chunks = seq // c
    tile = DELTA_P1_CHUNKS * c
    group = pl.program_id(1)
    lane = lax.broadcasted_iota(jnp.int32, (tile, LANES), 1)
    r2 = lax.broadcasted_iota(jnp.int32, (tile, tile), 0)
    c2 = lax.broadcasted_iota(jnp.int32, (tile, tile), 1)
    same_chunk = (r2 // c) == (c2 // c)
    cum_mat = jnp.where(same_chunk & (c2 <= r2), 1.0, 0.0).astype(BF16)
    tot_mat = jnp.where(same_chunk, 1.0, 0.0).astype(BF16)
    ri = lax.broadcasted_iota(jnp.int32, (c, 2 * c), 0)
    li = lax.broadcasted_iota(jnp.int32, (c, 2 * c), 1)
    lo_half = li < c
    ci = jnp.where(lo_half, li, li - c)
    causal = ci <= ri
    strict = ci < ri
    eye = jnp.where(ci == ri, 1.0, 0.0).astype(F32)
    upper = jnp.where(ri <= ci, 1.0, 0.0).astype(BF16)

    hist = CONV_HISTORY_ROWS
    for hh in range(hps):
        cols = slice(hh * dk, (hh + 1) * dk)
        streams = ((q_ref, wq_ref, qn_sc, "q"), (k_ref, wk_ref, kn_sc, "k"), (v_ref, wv_ref, vn_sc, "v"))

        def conv_tile(si, xe, c0, hh=hh, cols=cols):
            _, w_ref, dst, kind = streams[si]
            w = w_ref[:, cols]
            conv_sc[si] = xe
            off = hist - (CONV_WIDTH - 1)
            y = w[0:1, :] * conv_sc[si, off:off + CONV_TILE, :]
            for j in range(1, CONV_WIDTH):
                y = y + w[j:j + 1, :] * conv_sc[si, off + j:off + j + CONV_TILE, :]
            y = y * _sigmoid(y)
            if kind != "v":
                y = y * lax.rsqrt(jnp.sum(y * y, axis=1, keepdims=True) + L2_EPS)
            if kind == "q":
                y = y * (B_HEAD_DIM ** -0.5)
            dst[hh, pl.ds(c0, CONV_TILE), :] = y

        for si, (src, _, _, _) in enumerate(streams):
            conv_tile(si, jnp.concatenate([jnp.zeros((hist, dk), F32),
                                           src[0:CONV_TILE, cols].astype(F32)], axis=0), 0)

        def conv_body(ti, carry, cols=cols, conv_tile=conv_tile):
            c0 = pl.multiple_of(ti * CONV_TILE, CONV_TILE)
            for si, (src, _, _, _) in enumerate(streams):
                conv_tile(si, src[pl.ds(c0 - hist, hist + CONV_TILE), cols].astype(F32), c0)
            return carry

        lax.fori_loop(1, seq // CONV_TILE, conv_body, 0)

    for hh in range(hps):
        h = group * hps + hh
        a_row = st_ref[B_HEADS + h]
        g_row = -jnp.exp(part_ref[0, h]) * _softplus(a_row + part_ref[1, h])
        gcr_sc[hh] = _dot_sel_r(g_row, upper)

    alog_l = par_ref[0:1, :]
    dtb_l = par_ref[1:2, :]

    def aligned(x, m):
        return x if isinstance(x, int) else pl.multiple_of(x, m)

    def phase1(i):
        r0 = aligned(i * tile, tile)
        rows = pl.ds(r0, tile)
        s = sc_ref[rows, :]
        beta_all = _sigmoid(s)
        g_all = -jnp.exp(alog_l) * _softplus(s + dtb_l)
        gc_all = _dot_sel(cum_mat, g_all)
        gl_all = _dot_sel(tot_mat, g_all)
        chains = []
        for hh in range(hps):
            h = group * hps + hh
            pick = lambda a, l: jnp.sum(jnp.where(lane == l, a, 0.0), axis=1, keepdims=True)
            beta = pick(beta_all, h)
            gc = pick(gc_all, B_HEADS + h)
            gl = pick(gl_all, B_HEADS + h)
            egc = jnp.exp(gc)
            q = qn_sc[hh, rows, :]
            k = kn_sc[hh, rows, :]
            v = vn_sc[hh, rows, :]
            kbeta = k * beta
            vbeta = v * beta
            qn_sc[hh, rows, :] = q * egc
            kdec = k * jnp.exp(gl - gc)
            kb16 = k.astype(BF16)
            for cc in range(DELTA_P1_CHUNKS):
                sl = slice(cc * c, (cc + 1) * c)
                n = DELTA_P1_CHUNKS * i + cc
                kdt_sc[hh, n] = kdec[sl].T.astype(BF16)
                egl_sc[hh, pl.ds(n, 1), :] = jnp.broadcast_to(jnp.exp(gl[cc * c:cc * c + 1]), (1, dk))
                diff = gc[sl] - gcr_sc[hh, pl.ds(n, 1), :]
                e = jnp.exp(jnp.where(causal, diff, 0.0))
                k2 = jnp.concatenate([kb16[sl], kb16[sl]], axis=0)
                lhs = jnp.concatenate([kbeta[sl], q[sl]], axis=0).astype(BF16)
                rhs = jnp.concatenate([vbeta[sl], kbeta[sl] * egc[sl]], axis=1)
                chains.append(dict(hh=hh, row=r0 + cc * c, e=e, k2=k2, lhs=lhs, rhs=rhs))
        yield

        for ch in chains:
            kq = lax.dot_general(ch["lhs"], ch["k2"], (((1,), (1,)), ((), ())),
                                 preferred_element_type=F32)
            ch["a"] = jnp.where(strict, kq[:c] * ch["e"], 0.0)
            attn_sc[ch["hh"], pl.ds(ch["row"], c), :] = (
                jnp.where(causal, kq[c:] * ch["e"], 0.0)[:, :c].astype(BF16))
        yield
        for ch in chains:
            ch["inv"] = eye - ch["a"]
            p_l, p_r = _stack_parts(ch["a"], lo_half)
            ch["p"] = _dot16(p_l, p_r)
        yield
        for _ in range(4):
            for ch in chains:
                p_l, p_r = _stack_parts(ch["p"], lo_half)
                inv_l, _ = _stack_parts(ch["inv"], lo_half, want_r=False)
                res = _dot16(jnp.concatenate([p_l, inv_l], axis=0), p_r)
                ch["p"] = res[:c]
                ch["inv"] = ch["inv"] + res[c:]
            yield
        for ch in chains:
            inv_l, _ = _stack_parts(ch["inv"], lo_half, want_r=False)
            _, p_r = _stack_parts(ch["p"], lo_half, want_l=False)
            ch["inv"] = ch["inv"] + _dot16(inv_l, p_r)
        yield
        for ch in chains:
            inv_l, _ = _stack_parts(ch["inv"], lo_half, want_r=False)
            _, rhs_r = _stack_parts(ch["rhs"], None, want_l=False)
            sol = _dot16(inv_l, rhs_r)
            vn_sc[ch["hh"], pl.ds(ch["row"], c), :] = sol[:, :dk]
            w_sc[ch["hh"], pl.ds(ch["row"], c), :] = sol[:, dk:].astype(BF16)

    nw = nw_ref[...]

    def phase2(i):
        for cc in range(DELTA_P1_CHUNKS):
            n = DELTA_P1_CHUNKS * i + cc
            r0 = aligned(n * c, c)
            rows = pl.ds(r0, c)
            heads = range(hps)
            st = [state_sc[hh] for hh in heads]
            st16 = [s_.astype(BF16) for s_ in st]
            lhs = [jnp.concatenate([w_sc[hh, rows, :], qn_sc[hh, rows, :].astype(BF16)], axis=0)
                   for hh in heads]
            ws = [_dot16(lhs[hh], st16[hh]) for hh in heads]
            yield
            vn16 = [(vn_sc[hh, rows, :] - ws[hh][:c]).astype(BF16) for hh in heads]
            av = [_dot16(attn_sc[hh, rows, :], vn16[hh]) for hh in heads]
            kv = [_dot16(kdt_sc[hh, n], vn16[hh]) for hh in heads]
            for hh in heads:
                state_sc[hh] = st[hh] * egl_sc[hh, pl.ds(n, 1), :] + kv[hh]
                out = ws[hh][c:] + av[hh]
                o = out * lax.rsqrt(jnp.mean(out * out, axis=1, keepdims=True) + RMS_EPS) * nw
                z = z_ref[rows, hh * dk:(hh + 1) * dk]
                o_ref[rows, hh * dk:(hh + 1) * dk] = (o * (z * _sigmoid(z))).astype(o_ref.dtype)
            yield

    def interleave(*gens):
        live = list(gens)
        while live:
            for g in list(live):
                try:
                    next(g)
                except StopIteration:
                    live.remove(g)

    n_tiles = n_chunks // DELTA_P1_CHUNKS
    state_sc[...] = jnp.zeros(state_sc.shape, F32)
    interleave(phase1(0))

    def pipelined(i, carry):
        interleave(phase1(i), phase2(i - 1))
        return carry

    lax.fori_loop(1, n_tiles, pipelined, 0)
    interleave(phase2(n_tiles - 1))


def deltanet(qkv_b, z_b, scal, scal_t, conv_w, a_log, dt_bias, gnorm_w, bsz, seq):
    t = bsz * seq
    hps = DELTA_HEADS_PER_STEP
    n_groups = B_HEADS // hps
    wg = hps * B_HEAD_DIM
    c = DELTA_CHUNK
    n_chunks = seq // c
    st3 = scal_t.reshape(16, t // c, c)
    par = jnp.zeros((8, LANES), F32)
    par = par.at[0, B_HEADS:2 * B_HEADS].set(a_log).at[1, B_HEADS:2 * B_HEADS].set(dt_bias)
    part = jnp.stack([a_log, dt_bias]).astype(F32)
    blk = lambda off: pl.BlockSpec((seq, wg), lambda b, g: (b, off + g))
    wblk = lambda off: pl.BlockSpec((CONV_WIDTH, wg), lambda b, g: (0, off + g))
    sc = lambda shape, dt: pltpu.VMEM(shape, dt)
    return pl.pallas_call(
        functools.partial(_delta_kernel, seq=seq, hps=hps),
        grid=(bsz, n_groups),
        in_specs=[blk(0), blk(n_groups), blk(2 * n_groups), blk(0),
                  pl.BlockSpec((seq, LANES), lambda b, g: (b, 0)),
                  pl.BlockSpec((16, n_chunks, c), lambda b, g: (0, b, 0)),
                  pl.BlockSpec((8, LANES), lambda b, g: (0, 0)),
                  pl.BlockSpec(memory_space=pltpu.SMEM),
                  wblk(0), wblk(n_groups), wblk(2 * n_groups),
                  pl.BlockSpec((1, B_HEAD_DIM), lambda b, g: (0, 0))],
        out_specs=pl.BlockSpec((seq, wg), lambda b, g: (b, g)),
        out_shape=jax.ShapeDtypeStruct((t, B_WIDTH), BF16),
        scratch_shapes=[sc((3, CONV_HISTORY_ROWS + CONV_TILE, B_HEAD_DIM), F32),
                        sc((hps, seq, B_HEAD_DIM), F32), sc((hps, seq, B_HEAD_DIM), F32),
                        sc((hps, seq, B_HEAD_DIM), F32),
                        sc((hps, seq, B_HEAD_DIM), BF16),
                        sc((hps, seq, c), BF16), sc((hps, n_chunks, B_HEAD_DIM, c), BF16),
                        sc((hps, n_chunks, 2 * c), F32), sc((hps, n_chunks, B_HEAD_DIM), F32),
                        sc((hps, B_HEAD_DIM, B_HEAD_DIM), F32)],
        compiler_params=_params(2),
        name="deltanet",
    )(qkv_b, qkv_b, qkv_b, z_b, scal, st3, par, part,
      conv_w, conv_w, conv_w, gnorm_w.reshape(1, B_HEAD_DIM))


SUBLANES = 8
TOKEN_TILE_ROWS = D_MODEL // LANES
assert TOKEN_TILE_ROWS == SUBLANES


def _store_token_tiles(ref, x, n_tok):
    for j in range(TOKEN_TILE_ROWS):
        ref[pl.ds(j, n_tok, stride=TOKEN_TILE_ROWS), :] = x[:, j * LANES:(j + 1) * LANES]


def _load_token_tiles(ref, n_tok, lead=None):
    idx = lambda j: (pl.ds(j, n_tok, stride=TOKEN_TILE_ROWS), slice(None))
    if lead is None:
        return [ref[idx(j)] for j in range(TOKEN_TILE_ROWS)]
    return [ref[(lead,) + idx(j)] for j in range(TOKEN_TILE_ROWS)]


def _layer_norm(h, g, b):
    mu = jnp.mean(h, axis=1, keepdims=True)
    hc = h - mu
    var = jnp.mean(hc * hc, axis=1, keepdims=True)
    return hc * lax.rsqrt(var + LN_EPS) * g + b


def _mix_route_kernel(ya_ref, yb_ref, x_ref, p_ref, woa_ref, wob_ref, g1_ref, b1_ref,
                      wr_ref, br_ref, wpg_ref, bpg_ref, wpp_ref,
                      x1_ref, res_ref, route_ref, gate_ref, cnt_ref, run_sc, wr2_sc, *, tm):
    i = pl.program_id(0)

    @pl.when(i == 0)
    def _():
        run_sc[...] = jnp.zeros(run_sc.shape, F32)
        w_hi, w_lo = _split2(wr_ref[...])
        wr2_sc[:, :LANES] = w_hi
        wr2_sc[:, LANES:] = w_lo

    mix = (jnp.dot(ya_ref[...].astype(BF16), woa_ref[...], preferred_element_type=F32)
           + jnp.dot(yb_ref[...].astype(BF16), wob_ref[...], preferred_element_type=F32))
    x1 = _layer_norm(DEEPNORM_ALPHA * x_ref[...] + mix, g1_ref[...], b1_ref[...])
    _store_token_tiles(x1_ref, x1, tm)
    x1b = x1.astype(BF16)

    x1l = (x1 - x1b.astype(F32)).astype(BF16)
    r_hi = jnp.dot(x1b, wr2_sc[...], preferred_element_type=F32)
    r_lo = jnp.dot(x1l, wr2_sc[:, :LANES], preferred_element_type=F32)
    logits = r_hi[:, :LANES] + (r_hi[:, LANES:] + r_lo) + br_ref[...]

    pgate = _sigmoid(jnp.dot(x1b, wpg_ref[...], preferred_element_type=F32) + bpg_ref[...])
    proj = jnp.dot(p_ref[...].astype(BF16), wpp_ref[...], preferred_element_type=F32)
    res_ref[...] = DEEPNORM_ALPHA * x1 + pgate * proj

    lane = lax.broadcasted_iota(jnp.int32, (tm, LANES), 1)
    cur = jnp.where(lane < N_EXPERTS, logits, -jnp.inf)
    vals, hots = [], []
    for _ in range(TOP_K):
        m = jnp.max(cur, axis=1, keepdims=True)
        idx = jnp.min(jnp.where(cur == m, lane, LANES), axis=1, keepdims=True)
        hot = lane == idx
        cur = jnp.where(hot, -jnp.inf, cur)
        vals.append(m)
        hots.append((hot, idx))
    exps = [jnp.exp(v - vals[0]) for v in vals]
    den = exps[0] + exps[1] + exps[2] + exps[3]

    member = jnp.zeros((tm, LANES), F32)
    for hot, _ in hots:
        member = member + jnp.where(hot, 1.0, 0.0)
    rr = lax.broadcasted_iota(jnp.int32, (tm, tm), 0)
    cc = lax.broadcasted_iota(jnp.int32, (tm, tm), 1)
    before = jnp.where(cc < rr, 1.0, 0.0).astype(BF16)
    prior = jnp.dot(before, member.astype(BF16), preferred_element_type=F32) + run_sc[0:1, :]
    route = jnp.zeros((tm, LANES), jnp.int32)
    gate = jnp.zeros((tm, LANES), F32)
    for kk, (hot, idx) in enumerate(hots):
        rank = jnp.sum(jnp.where(hot, prior, 0.0), axis=1, keepdims=True).astype(jnp.int32)
        route = jnp.where(lane == kk, idx, route)
        route = jnp.where(lane == TOP_K + kk, rank, route)
        gate = jnp.where(lane == kk, exps[kk] / den, gate)
    route_ref[...] = route
    gate_ref[...] = gate
    run_sc[...] = run_sc[...] + jnp.sum(member, axis=0, keepdims=True)
    cnt_ref[...] = run_sc[...]


def mix_route(ya, yb, x2d, p2d, w_out, ln1_g, ln1_b, w_router, b_router, w_ple_gate, b_ple_gate,
              w_ple_proj, tm=512):
    t, d = x2d.shape
    woa = w_out[:A_WIDTH].astype(BF16)
    wob = w_out[A_WIDTH:].astype(BF16)
    wr = jnp.pad(w_router, ((0, 0), (0, LANES - N_EXPERTS)))
    br = jnp.pad(b_router, (0, LANES - N_EXPERTS)).reshape(1, LANES)
    row = lambda w: pl.BlockSpec((tm, w), lambda i: (i, 0))
    full = lambda a: pl.BlockSpec(a.shape, lambda i: (0,) * a.ndim)
    tiles = pl.BlockSpec((tm * TOKEN_TILE_ROWS, LANES), lambda i: (i, 0))
    ops = [woa, wob, ln1_g.reshape(1, d), ln1_b.reshape(1, d), wr, br,
           w_ple_gate.astype(BF16), b_ple_gate.reshape(1, d), w_ple_proj.astype(BF16)]
    return pl.pallas_call(
        functools.partial(_mix_route_kernel, tm=tm),
        grid=(t // tm,),
        in_specs=[row(A_WIDTH), row(B_WIDTH), row(d), row(PLE_DIM)] + [full(a) for a in ops],
        out_specs=[tiles, row(d), row(LANES), row(LANES),
                   pl.BlockSpec((8, LANES), lambda i: (0, 0))],
        out_shape=[jax.ShapeDtypeStruct((t * TOKEN_TILE_ROWS, LANES), F32),
                   jax.ShapeDtypeStruct((t, d), F32),
                   jax.ShapeDtypeStruct((t, LANES), jnp.int32),
                   jax.ShapeDtypeStruct((t, LANES), F32),
                   jax.ShapeDtypeStruct((8, LANES), F32)],
        scratch_shapes=[pltpu.VMEM((8, LANES), F32), pltpu.VMEM((d, 2 * LANES), BF16)],
        compiler_params=_params(),
        name="mix_route",
    )(ya, yb, x2d, p2d, *ops)


MOE_TILE = 512
DISPATCH_TM = 512
COMBINE_TM = 256
DMA_ROWS_PER_ITER = 8
DMA_PRIORITIES = 2


def _dispatch_kernel(pstart_ref, pend_ref, slot_ref, x_ref, xb_ref, zero_sc, sem, zsem, *, tm):
    i = pl.program_id(0)
    tr = TOKEN_TILE_ROWS
    tile_rows = MOE_TILE * tr

    def zero_copy(e):
        off = pl.multiple_of((pend_ref[e] - MOE_TILE) * tr, tile_rows)
        return pltpu.make_async_copy(zero_sc, xb_ref.at[pl.ds(off, tile_rows), :], zsem)

    @pl.when(i == 0)
    def _():
        zero_sc[...] = jnp.zeros(zero_sc.shape, zero_sc.dtype)
        for e in range(N_EXPERTS):
            @pl.when(pend_ref[e] > pstart_ref[e])
            def _():
                zero_copy(e).start()
        for e in range(N_EXPERTS):
            @pl.when(pend_ref[e] > pstart_ref[e])
            def _():
                zero_copy(e).wait()

        n_tiles = xb_ref.shape[0] // tile_rows
        first_unused = pend_ref[N_EXPERTS - 1] // MOE_TILE

        def tail_copy(j):
            off = pl.multiple_of(j * tile_rows, tile_rows)
            return pltpu.make_async_copy(zero_sc, xb_ref.at[pl.ds(off, tile_rows), :], zsem)

        def tail_start(j, c):
            tail_copy(j).start()
            return c

        def tail_wait(j, c):
            tail_copy(j).wait()
            return c

        lax.fori_loop(first_unused, n_tiles, tail_start, 0)
        lax.fori_loop(first_unused, n_tiles, tail_wait, 0)

    def token_copy(tok, slot):
        src = pl.multiple_of(tok * tr, tr)
        dst = pl.multiple_of(slot * tr, tr)
        return pltpu.make_async_copy(x_ref.at[pl.ds(src, tr), :], xb_ref.at[pl.ds(dst, tr), :], sem)

    def start(g, c):
        toks = [g * DMA_ROWS_PER_ITER + u for u in range(DMA_ROWS_PER_ITER)]
        slots = [[slot_ref[r * TOP_K + kk] for kk in range(TOP_K)] for r in toks]
        for r, row_slots in zip(toks, slots):
            for kk, slot in enumerate(row_slots):
                token_copy(r, slot).start(priority=kk % DMA_PRIORITIES)
        return c

    lax.fori_loop(0, tm // DMA_ROWS_PER_ITER, start, 0)
    for _ in range(TOP_K):
        pltpu.make_async_copy(x_ref, xb_ref.at[pl.ds(0, tm * tr), :], sem).wait()


def dispatch(x1_tiles, slot_flat, pad_start, pad_end, n_slots, tm=DISPATCH_TM):
    t = x1_tiles.shape[0] // TOKEN_TILE_ROWS
    return pl.pallas_call(
        functools.partial(_dispatch_kernel, tm=tm),
        grid_spec=pltpu.PrefetchScalarGridSpec(
            num_scalar_prefetch=2,
            grid=(t // tm,),
            in_specs=[pl.BlockSpec((tm * TOP_K,), lambda i, ps, pe: (i,), memory_space=pltpu.SMEM),
                      pl.BlockSpec((tm * TOKEN_TILE_ROWS, LANES), lambda i, ps, pe: (i, 0))],
            out_specs=pl.BlockSpec(memory_space=pl.ANY),
            scratch_shapes=[pltpu.VMEM((MOE_TILE * TOKEN_TILE_ROWS, LANES), F32),
                            pltpu.SemaphoreType.DMA(()), pltpu.SemaphoreType.DMA(())]),
        out_shape=jax.ShapeDtypeStruct((n_slots * TOKEN_TILE_ROWS, LANES), F32),
        compiler_params=_params(),
        name="dispatch",
    )(pad_start, pad_end, slot_flat, x1_tiles)


EXPERT_CAST_ROWS = 256
EXPERT_SUBTILES = 1


def _expert_kernel(be_ref, nu_ref, x_ref, wgu_ref, bgu_ref, wd_ref, bd_ref, y_ref, wgu_sc, wd_sc):
    i = pl.program_id(0)
    used = i < nu_ref[0]
    new_expert = jnp.logical_or(i == 0, be_ref[i] != be_ref[jnp.maximum(i - 1, 0)])

    @pl.when(jnp.logical_and(used, new_expert))
    def _():
        def cast(j, c):
            r = pl.ds(pl.multiple_of(j * EXPERT_CAST_ROWS, EXPERT_CAST_ROWS), EXPERT_CAST_ROWS)
            wgu_sc[r, :] = wgu_ref[0, r, :].astype(BF16)
            wd_sc[r, :] = wd_ref[0, r, :].astype(BF16)
            return c
        lax.fori_loop(0, D_MODEL // EXPERT_CAST_ROWS, cast, 0)

    @pl.when(used)
    def _():
        sub = MOE_TILE // EXPERT_SUBTILES
        tr = TOKEN_TILE_ROWS
        nchunk = 256
        for s in range(EXPERT_SUBTILES):
            base = s * sub * tr
            xc = [x_ref[pl.ds(base + j, sub, stride=tr), :].astype(BF16) for j in range(tr)]
            xb = jnp.concatenate(xc, axis=1)
            acts = []
            for c0 in range(0, D_EXPERT, nchunk):
                if c0 == 0:
                    half = D_MODEL // 2
                    xs = (jnp.concatenate(xc[:tr // 2], axis=1), jnp.concatenate(xc[tr // 2:], axis=1))
                    g = sum(jnp.dot(xs[h], wgu_sc[h * half:(h + 1) * half, c0:c0 + nchunk],
                                    preferred_element_type=F32) for h in range(2))
                    u = sum(jnp.dot(xs[h], wgu_sc[h * half:(h + 1) * half,
                                                  D_EXPERT + c0:D_EXPERT + c0 + nchunk],
                                    preferred_element_type=F32) for h in range(2))
                else:
                    g = jnp.dot(xb, wgu_sc[:, c0:c0 + nchunk], preferred_element_type=F32)
                    u = jnp.dot(xb, wgu_sc[:, D_EXPERT + c0:D_EXPERT + c0 + nchunk],
                                preferred_element_type=F32)
                g = g + bgu_ref[0, :, c0:c0 + nchunk]
                u = u + bgu_ref[0, :, D_EXPERT + c0:D_EXPERT + c0 + nchunk]
                g = jnp.minimum(g, SWIGLU_LIMIT)
                u = jnp.clip(u, -SWIGLU_LIMIT, SWIGLU_LIMIT)
                acts.append(((u + 1.0) * (g * _sigmoid(SWIGLU_ALPHA * g))).astype(BF16))
            act = jnp.concatenate(acts, axis=1)
            for n0 in range(0, D_MODEL, nchunk):
                yb = (jnp.dot(act, wd_sc[:, n0:n0 + nchunk], preferred_element_type=F32)
                      + bd_ref[0, :, n0:n0 + nchunk])
                for jj in range(nchunk // LANES):
                    j = n0 // LANES + jj
                    y_ref[pl.ds(base + j, sub, stride=tr), :] = yb[:, jj * LANES:(jj + 1) * LANES]

    @pl.when(jnp.logical_not(used))
    def _():
        y_ref[...] = jnp.zeros(y_ref.shape, F32)


def experts(xb_tiles, block_e, n_used, w_gate_up, b_gate_up, w_down, b_down):
    tile_rows = MOE_TILE * TOKEN_TILE_ROWS
    n_blocks = xb_tiles.shape[0] // tile_rows
    d = D_MODEL
    clamp = lambda i, be, nu: jnp.minimum(i, nu[0] - 1)
    return pl.pallas_call(
        _expert_kernel,
        grid_spec=pltpu.PrefetchScalarGridSpec(
            num_scalar_prefetch=2,
            grid=(n_blocks,),
            in_specs=[pl.BlockSpec((tile_rows, LANES), lambda i, be, nu: (clamp(i, be, nu), 0)),
                      pl.BlockSpec((1, d, 2 * D_EXPERT), lambda i, be, nu: (be[i], 0, 0)),
                      pl.BlockSpec((1, 1, 2 * D_EXPERT), lambda i, be, nu: (be[i], 0, 0)),
                      pl.BlockSpec((1, D_EXPERT, d), lambda i, be, nu: (be[i], 0, 0)),
                      pl.BlockSpec((1, 1, d), lambda i, be, nu: (be[i], 0, 0))],
            out_specs=pl.BlockSpec((tile_rows, LANES), lambda i, be, nu: (i, 0)),
            scratch_shapes=[pltpu.VMEM((d, 2 * D_EXPERT), BF16), pltpu.VMEM((D_EXPERT, d), BF16)]),
        out_shape=jax.ShapeDtypeStruct(xb_tiles.shape, F32),
        compiler_params=_params(),
        name="experts",
    )(block_e, n_used, xb_tiles, w_gate_up, b_gate_up.reshape(N_EXPERTS, 1, -1),
      w_down, b_down.reshape(N_EXPERTS, 1, -1))


COMBINE_BUFFERS = 3


def _combine_kernel(slot_ref, slot1_ref, slot2_ref, gate_ref, res_ref, g2_ref, b2_ref,
                    yb_ref, o_ref, buf_sc, sem, *, tm):
    i = pl.program_id(0)
    n_steps = pl.num_programs(0)
    tr = TOKEN_TILE_ROWS
    group = DMA_ROWS_PER_ITER
    cur = lax.rem(i, COMBINE_BUFFERS)
    ahead = lax.rem(i + 2, COMBINE_BUFFERS)

    def token_copy(buf, kk, r, slot):
        src = pl.multiple_of(slot * tr, tr)
        dst = pl.multiple_of(r * tr, tr)
        return pltpu.make_async_copy(yb_ref.at[pl.ds(src, tr), :],
                                     buf_sc.at[buf, kk, pl.ds(dst, tr), :], sem.at[buf])

    def issue_group(sref, buf, g):
        toks = [g * group + u for u in range(group)]
        slots = [[sref[r * TOP_K + kk] for kk in range(TOP_K)] for r in toks]
        for r, row_slots in zip(toks, slots):
            for kk, slot in enumerate(row_slots):
                token_copy(buf, kk, r, slot).start(priority=kk % DMA_PRIORITIES)

    big = group * tr

    def compute_group(gb, issue=None):
        rows = pl.ds(pl.multiple_of(gb * big, big), big)
        gate = gate_ref[rows, :]
        chunks = []
        for j in range(tr):
            res = res_ref[rows, j * LANES:(j + 1) * LANES]
            ys = [buf_sc[cur, kk, pl.ds(gb * (big * tr) + j, big, stride=tr), :]
                  for kk in range(TOP_K)]
            if issue is not None:
                issue(gb * tr + j)
            hj = res
            for kk in range(TOP_K):
                hj = hj + gate[:, kk:kk + 1] * ys[kk]
            chunks.append(hj)
        mu = sum(jnp.sum(hj, axis=1, keepdims=True) for hj in chunks) * (1.0 / D_MODEL)
        cent = [hj - mu for hj in chunks]
        var = sum(jnp.sum(cj * cj, axis=1, keepdims=True) for cj in cent) * (1.0 / D_MODEL)
        inv = lax.rsqrt(var + LN_EPS)
        for j, cj in enumerate(cent):
            cols = slice(j * LANES, (j + 1) * LANES)
            o_ref[rows, cols] = cj * inv * g2_ref[:, cols] + b2_ref[:, cols]

    def issue_only(sref, buf):
        def body(g, c):
            issue_group(sref, buf, g)
            return c
        lax.fori_loop(0, tm // group, body, 0)

    @pl.when(i == 0)
    def _():
        issue_only(slot_ref, 0)

    @pl.when(jnp.logical_and(i == 0, n_steps > 1))
    def _():
        issue_only(slot1_ref, 1)

    for kk in range(TOP_K):
        pltpu.make_async_copy(yb_ref.at[pl.ds(0, tm * tr), :], buf_sc.at[cur, kk], sem.at[cur]).wait()

    @pl.when(i + 2 < n_steps)
    def _():
        def body(gb, c):
            compute_group(gb, issue=lambda g: issue_group(slot2_ref, ahead, g))
            return c
        lax.fori_loop(0, tm // big, body, 0)

    @pl.when(i + 2 >= n_steps)
    def _():
        def body(gb, c):
            compute_group(gb)
            return c
        lax.fori_loop(0, tm // big, body, 0)


def combine(y_tiles, slot_flat, gates, res, ln2_g, ln2_b, tm=COMBINE_TM):
    t, d = res.shape
    n_steps = t // tm
    slot_spec = lambda k: pl.BlockSpec((tm * TOP_K,), lambda i: (jnp.minimum(i + k, n_steps - 1),),
                                       memory_space=pltpu.SMEM)
    return pl.pallas_call(
        functools.partial(_combine_kernel, tm=tm),
        grid=(n_steps,),
        in_specs=[slot_spec(0), slot_spec(1), slot_spec(2),
                  pl.BlockSpec((tm, LANES), lambda i: (i, 0)),
                  pl.BlockSpec((tm, d), lambda i: (i, 0)),
                  pl.BlockSpec((1, d), lambda i: (0, 0)),
                  pl.BlockSpec((1, d), lambda i: (0, 0)),
                  pl.BlockSpec(memory_space=pl.ANY)],
        out_specs=pl.BlockSpec((tm, d), lambda i: (i, 0)),
        scratch_shapes=[pltpu.VMEM((COMBINE_BUFFERS, TOP_K, tm * TOKEN_TILE_ROWS, LANES), F32),
                        pltpu.SemaphoreType.DMA((COMBINE_BUFFERS,))],
        out_shape=jax.ShapeDtypeStruct((t, d), F32),
        compiler_params=_params(),
        name="combine",
    )(slot_flat, slot_flat, slot_flat, gates, res, ln2_g.reshape(1, d), ln2_b.reshape(1, d),
      y_tiles)


def kernel(x, p, w_in, conv_w, a_log, dt_bias, gnorm_w, w_out, ln1_g, ln1_b, w_router, b_router,
           w_gate_up, b_gate_up, w_down, b_down, w_ple_gate, b_ple_gate, w_ple_proj, ln2_g, ln2_b):
    bsz, seq, d = x.shape
    t = bsz * seq
    x2d = x.reshape(t, d)
    qkv_a, qkv_b, z_b, scal, scal_t = in_proj(x2d, w_in[0])
    ya = dilated_attn(qkv_a, bsz, seq)
    yb = deltanet(qkv_b, z_b, scal, scal_t, conv_w[0], a_log[0], dt_bias[0], gnorm_w[0], bsz, seq)
    x1, res, route, gates, counts = mix_route(
        ya, yb, x2d, p[0].reshape(t, PLE_DIM), w_out[0], ln1_g[0], ln1_b[0], w_router[0],
        b_router[0], w_ple_gate[0], b_ple_gate[0], w_ple_proj[0])

    cnt = counts[0, :N_EXPERTS].astype(jnp.int32)
    padded = (cnt + MOE_TILE - 1) // MOE_TILE * MOE_TILE
    pad_end = jnp.cumsum(padded).astype(jnp.int32)
    pad_start = pad_end - padded
    n_blocks = (t * TOP_K) // MOE_TILE + N_EXPERTS
    n_used = (pad_end[-1:] // MOE_TILE).astype(jnp.int32)
    tile_start = jnp.arange(n_blocks, dtype=jnp.int32) * MOE_TILE
    block_e = jnp.minimum(jnp.sum(pad_end[None, :] <= tile_start[:, None], axis=1),
                          N_EXPERTS - 1).astype(jnp.int32)
    experts_iota = jnp.arange(N_EXPERTS, dtype=jnp.int32)
    group_off = jnp.sum(jnp.where(route[:, :TOP_K, None] == experts_iota, pad_start, 0), axis=-1)
    slot_flat = (group_off + route[:, TOP_K:2 * TOP_K]).reshape(-1).astype(jnp.int32)

    xb = dispatch(x1, slot_flat, pad_start, pad_end, n_blocks * MOE_TILE)
    yexp = experts(xb, block_e, n_used, w_gate_up[0], b_gate_up[0], w_down[0], b_down[0])
    out = combine(yexp, slot_flat, gates, res, ln2_g[0], ln2_b[0])
    return out.reshape(bsz, seq, d)
```

```python
import functools
import math

import jax
import jax.numpy as jnp
from jax import lax
from jax.experimental import pallas as pl
from jax.experimental.pallas import tpu as pltpu

LANES = 128
VMEM_LIMIT_BYTES = 56 * 1024 * 1024

D_MODEL = 1024
PLE_DIM = 256
A_HEADS = 8
A_HEAD_DIM = 64
A_WIDTH = A_HEADS * A_HEAD_DIM
DILATIONS = (16, 4, 1)
ATT_BLOCK = 128
B_HEADS = 4
B_HEAD_DIM = 128
B_WIDTH = B_HEADS * B_HEAD_DIM
CONV_WIDTH = 4
DELTA_CHUNK = 64
N_EXPERTS = 32
TOP_K = 4
D_EXPERT = D_MODEL
SWIGLU_LIMIT = 7.0
SWIGLU_ALPHA = 1.702
MOE_BLOCK = 128
LN_EPS = 1e-5
RMS_EPS = 1e-6
L2_EPS = 1e-6
DEPTH = 1
DEEPNORM_ALPHA = (2.0 * DEPTH) ** 0.25

F32 = jnp.float32
BF16 = jnp.bfloat16


def _params(n_parallel_axes=1):
    return pltpu.CompilerParams(
        dimension_semantics=("arbitrary",) * n_parallel_axes,
        vmem_limit_bytes=VMEM_LIMIT_BYTES)


def _in_proj_kernel(x_ref, wa_ref, wb_ref, wz_ref, ws_ref, wst_ref, a_ref, b_ref, z_ref, s_ref, st_ref):
    xb = x_ref[...].astype(BF16)
    a_ref[...] = jnp.dot(xb, wa_ref[...], preferred_element_type=F32)
    b_ref[...] = jnp.dot(xb, wb_ref[...], preferred_element_type=F32).astype(BF16)
    z_ref[...] = jnp.dot(xb, wz_ref[...], preferred_element_type=F32)
    s_ref[...] = jnp.dot(xb, ws_ref[...], preferred_element_type=F32)
    st_ref[...] = lax.dot_general(wst_ref[...], xb, (((1,), (1,)), ((), ())),
                                  preferred_element_type=F32)


def in_proj(x2d, w_in, tm=512):
    t, d = x2d.shape
    na, nb = 3 * A_WIDTH, 3 * B_WIDTH
    wa = w_in[:, :na].astype(BF16)
    wb = w_in[:, na:na + nb].astype(BF16)
    wz = w_in[:, na + nb:na + nb + B_WIDTH].astype(BF16)
    ws = jnp.pad(w_in[:, na + nb + B_WIDTH:], ((0, 0), (0, LANES - 2 * B_HEADS))).astype(BF16)
    wst = ws[:, :16].T
    full = lambda a: pl.BlockSpec(a.shape, lambda i: (0, 0))
    return pl.pallas_call(
        _in_proj_kernel,
        grid=(t // tm,),
        in_specs=[pl.BlockSpec((tm, d), lambda i: (i, 0)),
                  full(wa), full(wb), full(wz), full(ws), full(wst)],
        out_specs=[pl.BlockSpec((tm, na), lambda i: (i, 0)),
                   pl.BlockSpec((tm, nb), lambda i: (i, 0)),
                   pl.BlockSpec((tm, B_WIDTH), lambda i: (i, 0)),
                   pl.BlockSpec((tm, LANES), lambda i: (i, 0)),
                   pl.BlockSpec((16, tm), lambda i: (0, i))],
        out_shape=[jax.ShapeDtypeStruct((t, na), F32),
                   jax.ShapeDtypeStruct((t, nb), BF16),
                   jax.ShapeDtypeStruct((t, B_WIDTH), F32),
                   jax.ShapeDtypeStruct((t, LANES), F32),
                   jax.ShapeDtypeStruct((16, t), F32)],
        compiler_params=_params(),
        name="in_proj",
    )(x2d, wa, wb, wz, ws, wst)


ATTN_UNROLL_DENSE = 5
ATTN_UNROLL_SINGLE = 8


def _attn_kernel(q_ref, k_ref, v_ref, o_ref, m_sc, l_sc, acc_sc, *, seq):
    blk = ATT_BLOCK
    lane = lax.broadcasted_iota(jnp.int32, (blk, LANES), 1)
    head0 = lane < A_HEAD_DIM
    row = lax.broadcasted_iota(jnp.int32, (blk, blk), 0)
    col = lax.broadcasted_iota(jnp.int32, (blk, blk), 1)
    prev_ok = col >= row
    cur_ok = col <= row
    neg_inf = jnp.float32(-jnp.inf)
    scale = 1.0 / math.sqrt(A_HEAD_DIM)

    def blocks(starts, stride, has_prev, first_branch, last_branch):
        def rows(ref, s0):
            if stride == 1:
                return ref[pl.ds(s0, blk), :]
            return ref[pl.ds(s0, blk, stride=stride), :]

        def put(ref, s0, val):
            if stride == 1:
                ref[pl.ds(s0, blk), :] = val
            else:
                ref[pl.ds(s0, blk, stride=stride), :] = val

        hmask = (head0, jnp.logical_not(head0))
        items = []
        for start, hp in zip(starts, has_prev):
            q = rows(q_ref, start) * scale
            keys = rows(k_ref, start)
            vals = rows(v_ref, start)
            if hp:
                keys = jnp.concatenate([rows(k_ref, start - blk * stride), keys], axis=0)
                vals = jnp.concatenate([rows(v_ref, start - blk * stride), vals], axis=0)
            it = dict(start=start, hp=hp, keys=keys.astype(BF16), vals=vals,
                      qh=[jnp.where(hm, q, 0.0).astype(BF16) for hm in hmask])
            if not first_branch:
                it["m_old"] = rows(m_sc, start)
                it["l_old"] = rows(l_sc, start)
                it["acc_old"] = rows(acc_sc, start)
            items.append(it)

        for it in items:
            it["s"] = [lax.dot_general(qh, it["keys"], (((1,), (1,)), ((), ())),
                                       preferred_element_type=F32) for qh in it["qh"]]
        for it in items:
            ok = jnp.concatenate([prev_ok, cur_ok], axis=1) if it["hp"] else cur_ok
            it["m"], it["sum"], it["p"] = [], [], []
            for h in range(2):
                s = jnp.where(ok, it["s"][h], neg_inf)
                t = jnp.maximum(s[:, :blk], s[:, blk:]) if it["hp"] else s
                if not first_branch:
                    t = jnp.maximum(t, jnp.where(hmask[h], it["m_old"], neg_inf))
                m_h = jnp.max(t, axis=1, keepdims=True)
                p = jnp.exp(s - m_h)
                psum = p[:, :blk] + p[:, blk:] if it["hp"] else p
                it["m"].append(m_h)
                it["sum"].append(jnp.sum(psum, axis=1, keepdims=True))
                it["p"].append(p.astype(BF16))
        for it in items:
            v16 = it["vals"].astype(BF16)
            it["pv"] = [jnp.dot(it["p"][h], v16, preferred_element_type=F32) for h in range(2)]
        for it in items:
            m_new = jnp.where(head0, it["m"][0], it["m"][1])
            l_new = jnp.where(head0, it["sum"][0], it["sum"][1])
            acc_new = jnp.where(head0, it["pv"][0], it["pv"][1])
            if not first_branch:
                alpha = jnp.exp(it["m_old"] - m_new)
                l_new = l_new + alpha * it["l_old"]
                acc_new = acc_new + alpha * it["acc_old"]
            if last_branch:
                put(o_ref, it["start"], acc_new / l_new)
            else:
                put(m_sc, it["start"], m_new)
                put(l_sc, it["start"], l_new)
                put(acc_sc, it["start"], acc_new)

    n_br = len(DILATIONS)
    for bi, dil in enumerate(DILATIONS):
        first, last = bi == 0, bi == n_br - 1
        nblk = seq // dil // blk
        if dil == 1:
            blocks([0], 1, [False], first, last)
            per = ATTN_UNROLL_DENSE
            assert (nblk - 1) % per == 0

            def body(g, c, first=first, last=last, per=per):
                starts = [pl.multiple_of((1 + g * per + j) * blk, blk) for j in range(per)]
                blocks(starts, 1, [True] * per, first, last)
                return c
            lax.fori_loop(0, (nblk - 1) // per, body, 0)
        elif nblk > 1:
            def body(r, c, dil=dil, nblk=nblk, first=first, last=last):
                blocks([r + n * blk * dil for n in range(nblk)], dil,
                       [n > 0 for n in range(nblk)], first, last)
                return c
            lax.fori_loop(0, dil, body, 0)
        else:
            per = ATTN_UNROLL_SINGLE
            assert dil % per == 0

            def body(g, c, dil=dil, first=first, last=last, per=per):
                blocks([g * per + j for j in range(per)], dil, [False] * per, first, last)
                return c
            lax.fori_loop(0, dil // per, body, 0)


def dilated_attn(qkv_a, bsz, seq):
    t = bsz * seq
    n_pairs = A_WIDTH // LANES
    blk = lambda off: pl.BlockSpec((seq, LANES), lambda b, p: (b, off + p))
    return pl.pallas_call(
        functools.partial(_attn_kernel, seq=seq),
        grid=(bsz, n_pairs),
        in_specs=[blk(0), blk(n_pairs), blk(2 * n_pairs)],
        out_specs=pl.BlockSpec((seq, LANES), lambda b, p: (b, p)),
        out_shape=jax.ShapeDtypeStruct((t, A_WIDTH), F32),
        scratch_shapes=[pltpu.VMEM((seq, LANES), F32)] * 3,
        compiler_params=_params(2),
        name="dilated_attn",
    )(qkv_a, qkv_a, qkv_a)


DELTA_HEADS_PER_STEP = 4
DELTA_P1_CHUNKS = 4
CONV_TILE = 256
CONV_HISTORY_ROWS = 16


def _dot16(a, b):
    return jnp.dot(a, b, preferred_element_type=F32)


def _split2(x):
    hi = x.astype(BF16)
    return hi, (x - hi.astype(F32)).astype(BF16)


def _split3(x):
    hi = x.astype(BF16)
    r = x - hi.astype(F32)
    mid = r.astype(BF16)
    return hi, mid, (r - mid.astype(F32)).astype(BF16)


def _dot_sel(m01, x):
    hi, mid, lo = _split3(x)
    return _dot16(m01, hi) + (_dot16(m01, mid) + _dot16(m01, lo))


def _dot_sel_r(x, m01):
    hi, mid, lo = _split3(x)
    return _dot16(hi, m01) + (_dot16(mid, m01) + _dot16(lo, m01))


def _stack_parts(x, lo_half, want_l=True, want_r=True):
    xh = x.astype(BF16)
    xh_f = xh.astype(F32)
    xl_f = x - xh_f
    left = right = None
    if want_l:
        mix = jnp.where(lo_half, xh_f, xl_f).astype(BF16)
        left = jnp.concatenate([mix, mix], axis=1)
    if want_r:
        xl = xl_f.astype(BF16)
        right = jnp.concatenate([xh, xh, xl, xl], axis=0)
    return left, right


def _softplus(x):
    return jnp.maximum(x, 0.0) + jnp.log(1.0 + jnp.exp(-jnp.abs(x)))


def _sigmoid(x):
    return 1.0 / (1.0 + jnp.exp(-x))


def _delta_kernel(q_ref, k_ref, v_ref, z_ref, sc_ref, st_ref, par_ref, part_ref,
                  wq_ref, wk_ref, wv_ref, nw_ref, o_ref,
                  conv_sc, qn_sc, kn_sc, vn_sc, w_sc, attn_sc, kdt_sc, gcr_sc, egl_sc,
                  state_sc, *, seq, hps):
    c = DELTA_CHUNK
    dk = B_HEAD_DIM
    n_chunks = seq // c
    tile = DELTA_P1_CHUNKS * c
    group = pl.program_id(1)
    lane = lax.broadcasted_iota(jnp.int32, (tile, LANES), 1)
    r2 = lax.broadcasted_iota(jnp.int32, (tile, tile), 0)
    c2 = lax.broadcasted_iota(jnp.int32, (tile, tile), 1)
    same_chunk = (r2 // c) == (c2 // c)
    cum_mat = jnp.where(same_chunk & (c2 <= r2), 1.0, 0.0).astype(BF16)
    tot_mat = jnp.where(same_chunk, 1.0, 0.0).astype(BF16)
    ri = lax.broadcasted_iota(jnp.int32, (c, 2 * c), 0)
    li = lax.broadcasted_iota(jnp.int32, (c, 2 * c), 1)
    lo_half = li < c
    ci = jnp.where(lo_half, li, li - c)
    causal = ci <= ri
    strict = ci < ri
    eye = jnp.where(ci == ri, 1.0, 0.0).astype(F32)
    upper = jnp.where(ri <= ci, 1.0, 0.0).astype(BF16)

    hist = CONV_HISTORY_ROWS
    for hh in range(hps):
        cols = slice(hh * dk, (hh + 1) * dk)
        streams = ((q_ref, wq_ref, qn_sc, "q"), (k_ref, wk_ref, kn_sc, "k"), (v_ref, wv_ref, vn_sc, "v"))

        def conv_tile(si, xe, c0, hh=hh, cols=cols):
            _, w_ref, dst, kind = streams[si]
            w = w_ref[:, cols]
            conv_sc[si] = xe
            off = hist - (CONV_WIDTH - 1)
            y = w[0:1, :] * conv_sc[si, off:off + CONV_TILE, :]
            for j in range(1, CONV_WIDTH):
                y = y + w[j:j + 1, :] * conv_sc[si, off + j:off + j + CONV_TILE, :]
            y = y * _sigmoid(y)
            if kind != "v":
                y = y * lax.rsqrt(jnp.sum(y * y, axis=1, keepdims=True) + L2_EPS)
            if kind == "q":
                y = y * (B_HEAD_DIM ** -0.5)
            dst[hh, pl.ds(c0, CONV_TILE), :] = y

        for si, (src, _, _, _) in enumerate(streams):
            conv_tile(si, jnp.concatenate([jnp.zeros((hist, dk), F32),
                                           src[0:CONV_TILE, cols].astype(F32)], axis=0), 0)

        def conv_body(ti, carry, cols=cols, conv_tile=conv_tile):
            c0 = pl.multiple_of(ti * CONV_TILE, CONV_TILE)
            for si, (src, _, _, _) in enumerate(streams):
                conv_tile(si, src[pl.ds(c0 - hist, hist + CONV_TILE), cols].astype(F32), c0)
            return carry

        lax.fori_loop(1, seq // CONV_TILE, conv_body, 0)

    for hh in range(hps):
        h = group * hps + hh
        a_row = st_ref[B_HEADS + h]
        g_row = -jnp.exp(part_ref[0, h]) * _softplus(a_row + part_ref[1, h])
        gcr_sc[hh] = _dot_sel_r(g_row, upper)

    alog_l = par_ref[0:1, :]
    dtb_l = par_ref[1:2, :]

    def aligned(x, m):
        return x if isinstance(x, int) else pl.multiple_of(x, m)

    def phase1(i):
        r0 = aligned(i * tile, tile)
        rows = pl.ds(r0, tile)
        s = sc_ref[rows, :]
        beta_all = _sigmoid(s)
        g_all = -jnp.exp(alog_l) * _softplus(s + dtb_l)
        gc_all = _dot_sel(cum_mat, g_all)
        gl_all = _dot_sel(tot_mat, g_all)
        chains = []
        for hh in range(hps):
            h = group * hps + hh
            pick = lambda a, l: jnp.sum(jnp.where(lane == l, a, 0.0), axis=1, keepdims=True)
            beta = pick(beta_all, h)
            gc = pick(gc_all, B_HEADS + h)
            gl = pick(gl_all, B_HEADS + h)
            egc = jnp.exp(gc)
            q = qn_sc[hh, rows, :]
            k = kn_sc[hh, rows, :]
            v = vn_sc[hh, rows, :]
            kbeta = k * beta
            vbeta = v * beta
            qn_sc[hh, rows, :] = q * egc
            kdec = k * jnp.exp(gl - gc)
            kb16 = k.astype(BF16)
            for cc in range(DELTA_P1_CHUNKS):
                sl = slice(cc * c, (cc + 1) * c)
                n = DELTA_P1_CHUNKS * i + cc
                kdt_sc[hh, n] = kdec[sl].T.astype(BF16)
                egl_sc[hh, pl.ds(n, 1), :] = jnp.broadcast_to(jnp.exp(gl[cc * c:cc * c + 1]), (1, dk))
                diff = gc[sl] - gcr_sc[hh, pl.ds(n, 1), :]
                e = jnp.exp(jnp.where(causal, diff, 0.0))
                k2 = jnp.concatenate([kb16[sl], kb16[sl]], axis=0)
                lhs = jnp.concatenate([kbeta[sl], q[sl]], axis=0).astype(BF16)
                rhs = jnp.concatenate([vbeta[sl], kbeta[sl] * egc[sl]], axis=1)
                chains.append(dict(hh=hh, row=r0 + cc * c, e=e, k2=k2, lhs=lhs, rhs=rhs))
        yield

        for ch in chains:
            kq = lax.dot_general(ch["lhs"], ch["k2"], (((1,), (1,)), ((), ())),
                                 preferred_element_type=F32)
            ch["a"] = jnp.where(strict, kq[:c] * ch["e"], 0.0)
            attn_sc[ch["hh"], pl.ds(ch["row"], c), :] = (
                jnp.where(causal, kq[c:] * ch["e"], 0.0)[:, :c].astype(BF16))
        yield
        for ch in chains:
            ch["inv"] = eye - ch["a"]
            p_l, p_r = _stack_parts(ch["a"], lo_half)
            ch["p"] = _dot16(p_l, p_r)
        yield
        for _ in range(4):
            for ch in chains:
                p_l, p_r = _stack_parts(ch["p"], lo_half)
                inv_l, _ = _stack_parts(ch["inv"], lo_half, want_r=False)
                res = _dot16(jnp.concatenate([p_l, inv_l], axis=0), p_r)
                ch["p"] = res[:c]
                ch["inv"] = ch["inv"] + res[c:]
            yield
        for ch in chains:
            inv_l, _ = _stack_parts(ch["inv"], lo_half, want_r=False)
            _, p_r = _stack_parts(ch["p"], lo_half, want_l=False)
            ch["inv"] = ch["inv"] + _dot16(inv_l, p_r)
        yield
        for ch in chains:
            inv_l, _ = _stack_parts(ch["inv"], lo_half, want_r=False)
            _, rhs_r = _stack_parts(ch["rhs"], None, want_l=False)
            sol = _dot16(inv_l, rhs_r)
            vn_sc[ch["hh"], pl.ds(ch["row"], c), :] = sol[:, :dk]
            w_sc[ch["hh"], pl.ds(ch["row"], c), :] = sol[:, dk:].astype(BF16)

    nw = nw_ref[...]

    def phase2(i):
        for cc in range(DELTA_P1_CHUNKS):
            n = DELTA_P1_CHUNKS * i + cc
            r0 = aligned(n * c, c)
            rows = pl.ds(r0, c)
            heads = range(hps)
            st = [state_sc[hh] for hh in heads]
            st16 = [s_.astype(BF16) for s_ in st]
            lhs = [jnp.concatenate([w_sc[hh, rows, :], qn_sc[hh, rows, :].astype(BF16)], axis=0)
                   for hh in heads]
            ws = [_dot16(lhs[hh], st16[hh]) for hh in heads]
            yield
            vn16 = [(vn_sc[hh, rows, :] - ws[hh][:c]).astype(BF16) for hh in heads]
            av = [_dot16(attn_sc[hh, rows, :], vn16[hh]) for hh in heads]
            kv = [_dot16(kdt_sc[hh, n], vn16[hh]) for hh in heads]
            for hh in heads:
                state_sc[hh] = st[hh] * egl_sc[hh, pl.ds(n, 1), :] + kv[hh]
                out = ws[hh][c:] + av[hh]
                o = out * lax.rsqrt(jnp.mean(out * out, axis=1, keepdims=True) + RMS_EPS) * nw
                z = z_ref[rows, hh * dk:(hh + 1) * dk]
                o_ref[rows, hh * dk:(hh + 1) * dk] = (o * (z * _sigmoid(z))).astype(o_ref.dtype)
            yield

    def interleave(*gens):
        live = list(gens)
        while live:
            for g in list(live):
                try:
                    next(g)
                except StopIteration:
                    live.remove(g)

    n_tiles = n_chunks // DELTA_P1_CHUNKS
    state_sc[...] = jnp.zeros(state_sc.shape, F32)
    interleave(phase1(0))

    def pipelined(i, carry):
        interleave(phase1(i), phase2(i - 1))
        return carry

    lax.fori_loop(1, n_tiles, pipelined, 0)
    interleave(phase2(n_tiles - 1))


def deltanet(qkv_b, z_b, scal, scal_t, conv_w, a_log, dt_bias, gnorm_w, bsz, seq):
    t = bsz * seq
    hps = DELTA_HEADS_PER_STEP
    n_groups = B_HEADS // hps
    wg = hps * B_HEAD_DIM
    c = DELTA_CHUNK
    n_chunks = seq // c
    st3 = scal_t.reshape(16, t // c, c)
    par = jnp.zeros((8, LANES), F32)
    par = par.at[0, B_HEADS:2 * B_HEADS].set(a_log).at[1, B_HEADS:2 * B_HEADS].set(dt_bias)
    part = jnp.stack([a_log, dt_bias]).astype(F32)
    blk = lambda off: pl.BlockSpec((seq, wg), lambda b, g: (b, off + g))
    wblk = lambda off: pl.BlockSpec((CONV_WIDTH, wg), lambda b, g: (0, off + g))
    sc = lambda shape, dt: pltpu.VMEM(shape, dt)
    return pl.pallas_call(
        functools.partial(_delta_kernel, seq=seq, hps=hps),
        grid=(bsz, n_groups),
        in_specs=[blk(0), blk(n_groups), blk(2 * n_groups), blk(0),
                  pl.BlockSpec((seq, LANES), lambda b, g: (b, 0)),
                  pl.BlockSpec((16, n_chunks, c), lambda b, g: (0, b, 0)),
                  pl.BlockSpec((8, LANES), lambda b, g: (0, 0)),
                  pl.BlockSpec(memory_space=pltpu.SMEM),
                  wblk(0), wblk(n_groups), wblk(2 * n_groups),
                  pl.BlockSpec((1, B_HEAD_DIM), lambda b, g: (0, 0))],
        out_specs=pl.BlockSpec((seq, wg), lambda b, g: (b, g)),
        out_shape=jax.ShapeDtypeStruct((t, B_WIDTH), BF16),
        scratch_shapes=[sc((3, CONV_HISTORY_ROWS + CONV_TILE, B_HEAD_DIM), F32),
                        sc((hps, seq, B_HEAD_DIM), F32), sc((hps, seq, B_HEAD_DIM), F32),
                        sc((hps, seq, B_HEAD_DIM), F32),
                        sc((hps, seq, B_HEAD_DIM), BF16),
                        sc((hps, seq, c), BF16), sc((hps, n_chunks, B_HEAD_DIM, c), BF16),
                        sc((hps, n_chunks, 2 * c), F32), sc((hps, n_chunks, B_HEAD_DIM), F32),
                        sc((hps, B_HEAD_DIM, B_HEAD_DIM), F32)],
        compiler_params=_params(2),
        name="deltanet",
    )(qkv_b, qkv_b, qkv_b, z_b, scal, st3, par, part,
      conv_w, conv_w, conv_w, gnorm_w.reshape(1, B_HEAD_DIM))


SUBLANES = 8
TOKEN_TILE_ROWS = D_MODEL // LANES
assert TOKEN_TILE_ROWS == SUBLANES


def _store_token_tiles(ref, x, n_tok):
    for j in range(TOKEN_TILE_ROWS):
        ref[pl.ds(j, n_tok, stride=TOKEN_TILE_ROWS), :] = x[:, j * LANES:(j + 1) * LANES]


def _layer_norm(h, g, b):
    mu = jnp.mean(h, axis=1, keepdims=True)
    hc = h - mu
    var = jnp.mean(hc * hc, axis=1, keepdims=True)
    return hc * lax.rsqrt(var + LN_EPS) * g + b


def _mix_route_kernel(ya_ref, yb_ref, x_ref, p_ref, woa_ref, wob_ref, g1_ref, b1_ref,
                      wr_ref, br_ref, wpg_ref, bpg_ref, wpp_ref,
                      x1_ref, res_ref, route_ref, gate_ref, cnt_ref, run_sc, wr2_sc, *, tm):
    i = pl.program_id(0)

    @pl.when(i == 0)
    def _():
        run_sc[...] = jnp.zeros(run_sc.shape, F32)
        w_hi, w_lo = _split2(wr_ref[...])
        wr2_sc[:, :LANES] = w_hi
        wr2_sc[:, LANES:] = w_lo

    mix = (jnp.dot(ya_ref[...].astype(BF16), woa_ref[...], preferred_element_type=F32)
           + jnp.dot(yb_ref[...].astype(BF16), wob_ref[...], preferred_element_type=F32))
    x1 = _layer_norm(DEEPNORM_ALPHA * x_ref[...] + mix, g1_ref[...], b1_ref[...])
    _store_token_tiles(x1_ref, x1, tm)
    x1b = x1.astype(BF16)

    x1l = (x1 - x1b.astype(F32)).astype(BF16)
    r_hi = jnp.dot(x1b, wr2_sc[...], preferred_element_type=F32)
    r_lo = jnp.dot(x1l, wr2_sc[:, :LANES], preferred_element_type=F32)
    logits = r_hi[:, :LANES] + (r_hi[:, LANES:] + r_lo) + br_ref[...]

    pgate = _sigmoid(jnp.dot(x1b, wpg_ref[...], preferred_element_type=F32) + bpg_ref[...])
    proj = jnp.dot(p_ref[...].astype(BF16), wpp_ref[...], preferred_element_type=F32)
    res_ref[...] = DEEPNORM_ALPHA * x1 + pgate * proj

    lane = lax.broadcasted_iota(jnp.int32, (tm, LANES), 1)
    cur = jnp.where(lane < N_EXPERTS, logits, -jnp.inf)
    vals, hots = [], []
    for _ in range(TOP_K):
        m = jnp.max(cur, axis=1, keepdims=True)
        idx = jnp.min(jnp.where(cur == m, lane, LANES), axis=1, keepdims=True)
        hot = lane == idx
        cur = jnp.where(hot, -jnp.inf, cur)
        vals.append(m)
        hots.append((hot, idx))
    exps = [jnp.exp(v - vals[0]) for v in vals]
    den = exps[0] + exps[1] + exps[2] + exps[3]

    member = jnp.zeros((tm, LANES), F32)
    for hot, _ in hots:
        member = member + jnp.where(hot, 1.0, 0.0)
    rr = lax.broadcasted_iota(jnp.int32, (tm, tm), 0)
    cc = lax.broadcasted_iota(jnp.int32, (tm, tm), 1)
    before = jnp.where(cc < rr, 1.0, 0.0).astype(BF16)
    prior = jnp.dot(before, member.astype(BF16), preferred_element_type=F32) + run_sc[0:1, :]
    route = jnp.zeros((tm, LANES), jnp.int32)
    gate = jnp.zeros((tm, LANES), F32)
    for kk, (hot, idx) in enumerate(hots):
        rank = jnp.sum(jnp.where(hot, prior, 0.0), axis=1, keepdims=True).astype(jnp.int32)
        route = jnp.where(lane == kk, idx, route)
        route = jnp.where(lane == TOP_K + kk, rank, route)
        gate = jnp.where(lane == kk, exps[kk] / den, gate)
    route_ref[...] = route
    gate_ref[...] = gate
    run_sc[...] = run_sc[...] + jnp.sum(member, axis=0, keepdims=True)
    cnt_ref[...] = run_sc[...]


def mix_route(ya, yb, x2d, p2d, w_out, ln1_g, ln1_b, w_router, b_router, w_ple_gate, b_ple_gate,
              w_ple_proj, tm=512):
    t, d = x2d.shape
    woa = w_out[:A_WIDTH].astype(BF16)
    wob = w_out[A_WIDTH:].astype(BF16)
    wr = jnp.pad(w_router, ((0, 0), (0, LANES - N_EXPERTS)))
    br = jnp.pad(b_router, (0, LANES - N_EXPERTS)).reshape(1, LANES)
    row = lambda w: pl.BlockSpec((tm, w), lambda i: (i, 0))
    full = lambda a: pl.BlockSpec(a.shape, lambda i: (0,) * a.ndim)
    tiles = pl.BlockSpec((tm * TOKEN_TILE_ROWS, LANES), lambda i: (i, 0))
    ops = [woa, wob, ln1_g.reshape(1, d), ln1_b.reshape(1, d), wr, br,
           w_ple_gate.astype(BF16), b_ple_gate.reshape(1, d), w_ple_proj.astype(BF16)]
    return pl.pallas_call(
        functools.partial(_mix_route_kernel, tm=tm),
        grid=(t // tm,),
        in_specs=[row(A_WIDTH), row(B_WIDTH), row(d), row(PLE_DIM)] + [full(a) for a in ops],
        out_specs=[tiles, row(d), row(LANES), row(LANES),
                   pl.BlockSpec((8, LANES), lambda i: (0, 0))],
        out_shape=[jax.ShapeDtypeStruct((t * TOKEN_TILE_ROWS, LANES), F32),
                   jax.ShapeDtypeStruct((t, d), F32),
                   jax.ShapeDtypeStruct((t, LANES), jnp.int32),
                   jax.ShapeDtypeStruct((t, LANES), F32),
                   jax.ShapeDtypeStruct((8, LANES), F32)],
        scratch_shapes=[pltpu.VMEM((8, LANES), F32), pltpu.VMEM((d, 2 * LANES), BF16)],
        compiler_params=_params(),
        name="mix_route",
    )(ya, yb, x2d, p2d, *ops)


MOE_TILE = 512
DISPATCH_TM = 512
COMBINE_TM = 256
DMA_ROWS_PER_ITER = 8
DMA_PRIORITIES = 2


def _dispatch_kernel(pstart_ref, pend_ref, slot_ref, x_ref, xb_ref, zero_sc, sem, zsem, *, tm):
    i = pl.program_id(0)
    tr = TOKEN_TILE_ROWS
    tile_rows = MOE_TILE * tr

    def zero_copy(e):
        off = pl.multiple_of((pend_ref[e] - MOE_TILE) * tr, tile_rows)
        return pltpu.make_async_copy(zero_sc, xb_ref.at[pl.ds(off, tile_rows), :], zsem)

    @pl.when(i == 0)
    def _():
        zero_sc[...] = jnp.zeros(zero_sc.shape, zero_sc.dtype)
        for e in range(N_EXPERTS):
            @pl.when(pend_ref[e] > pstart_ref[e])
            def _():
                zero_copy(e).start()
        for e in range(N_EXPERTS):
            @pl.when(pend_ref[e] > pstart_ref[e])
            def _():
                zero_copy(e).wait()

        n_tiles = xb_ref.shape[0] // tile_rows
        first_unused = pend_ref[N_EXPERTS - 1] // MOE_TILE

        def tail_copy(j):
            off = pl.multiple_of(j * tile_rows, tile_rows)
            return pltpu.make_async_copy(zero_sc, xb_ref.at[pl.ds(off, tile_rows), :], zsem)

        def tail_start(j, c):
            tail_copy(j).start()
            return c

        def tail_wait(j, c):
            tail_copy(j).wait()
            return c

        lax.fori_loop(first_unused, n_tiles, tail_start, 0)
        lax.fori_loop(first_unused, n_tiles, tail_wait, 0)

    def token_copy(tok, slot):
        src = pl.multiple_of(tok * tr, tr)
        dst = pl.multiple_of(slot * tr, tr)
        return pltpu.make_async_copy(x_ref.at[pl.ds(src, tr), :], xb_ref.at[pl.ds(dst, tr), :], sem)

    def start(g, c):
        toks = [g * DMA_ROWS_PER_ITER + u for u in range(DMA_ROWS_PER_ITER)]
        slots = [[slot_ref[r * TOP_K + kk] for kk in range(TOP_K)] for r in toks]
        for r, row_slots in zip(toks, slots):
            for kk, slot in enumerate(row_slots):
                token_copy(r, slot).start(priority=kk % DMA_PRIORITIES)
        return c

    lax.fori_loop(0, tm // DMA_ROWS_PER_ITER, start, 0)
    for _ in range(TOP_K):
        pltpu.make_async_copy(x_ref, xb_ref.at[pl.ds(0, tm * tr), :], sem).wait()


def dispatch(x1_tiles, slot_flat, pad_start, pad_end, n_slots, tm=DISPATCH_TM):
    t = x1_tiles.shape[0] // TOKEN_TILE_ROWS
    return pl.pallas_call(
        functools.partial(_dispatch_kernel, tm=tm),
        grid_spec=pltpu.PrefetchScalarGridSpec(
            num_scalar_prefetch=2,
            grid=(t // tm,),
            in_specs=[pl.BlockSpec((tm * TOP_K,), lambda i, ps, pe: (i,), memory_space=pltpu.SMEM),
                      pl.BlockSpec((tm * TOKEN_TILE_ROWS, LANES), lambda i, ps, pe: (i, 0))],
            out_specs=pl.BlockSpec(memory_space=pl.ANY),
            scratch_shapes=[pltpu.VMEM((MOE_TILE * TOKEN_TILE_ROWS, LANES), F32),
                            pltpu.SemaphoreType.DMA(()), pltpu.SemaphoreType.DMA(())]),
        out_shape=jax.ShapeDtypeStruct((n_slots * TOKEN_TILE_ROWS, LANES), F32),
        compiler_params=_params(),
        name="dispatch",
    )(pad_start, pad_end, slot_flat, x1_tiles)


EXPERT_CAST_ROWS = 256
EXPERT_COL_CHUNK = 256


def _expert_kernel(be_ref, nu_ref, x_ref, wgu_ref, bgu_ref, wd_ref, bd_ref, y_ref, wgu_sc, wd_sc):
    i = pl.program_id(0)
    used = i < nu_ref[0]
    new_expert = jnp.logical_or(i == 0, be_ref[i] != be_ref[jnp.maximum(i - 1, 0)])

    @pl.when(jnp.logical_and(used, new_expert))
    def _():
        def cast(j, c):
            r = pl.ds(pl.multiple_of(j * EXPERT_CAST_ROWS, EXPERT_CAST_ROWS), EXPERT_CAST_ROWS)
            wgu_sc[r, :] = wgu_ref[0, r, :].astype(BF16)
            wd_sc[r, :] = wd_ref[0, r, :].astype(BF16)
            return c
        lax.fori_loop(0, D_MODEL // EXPERT_CAST_ROWS, cast, 0)

    @pl.when(used)
    def _():
        tr = TOKEN_TILE_ROWS
        nchunk = EXPERT_COL_CHUNK
        xc = [x_ref[pl.ds(j, MOE_TILE, stride=tr), :].astype(BF16) for j in range(tr)]
        xb = jnp.concatenate(xc, axis=1)
        acts = []
        for c0 in range(0, D_EXPERT, nchunk):
            if c0 == 0:
                half = D_MODEL // 2
                xs = (jnp.concatenate(xc[:tr // 2], axis=1), jnp.concatenate(xc[tr // 2:], axis=1))
                g = sum(jnp.dot(xs[h], wgu_sc[h * half:(h + 1) * half, c0:c0 + nchunk],
                                preferred_element_type=F32) for h in range(2))
                u = sum(jnp.dot(xs[h], wgu_sc[h * half:(h + 1) * half,
                                              D_EXPERT + c0:D_EXPERT + c0 + nchunk],
                                preferred_element_type=F32) for h in range(2))
            else:
                g = jnp.dot(xb, wgu_sc[:, c0:c0 + nchunk], preferred_element_type=F32)
                u = jnp.dot(xb, wgu_sc[:, D_EXPERT + c0:D_EXPERT + c0 + nchunk],
                            preferred_element_type=F32)
            g = g + bgu_ref[0, :, c0:c0 + nchunk]
            u = u + bgu_ref[0, :, D_EXPERT + c0:D_EXPERT + c0 + nchunk]
            g = jnp.minimum(g, SWIGLU_LIMIT)
            u = jnp.clip(u, -SWIGLU_LIMIT, SWIGLU_LIMIT)
            acts.append(((u + 1.0) * (g * _sigmoid(SWIGLU_ALPHA * g))).astype(BF16))
        act = jnp.concatenate(acts, axis=1)
        for n0 in range(0, D_MODEL, nchunk):
            yb = (jnp.dot(act, wd_sc[:, n0:n0 + nchunk], preferred_element_type=F32)
                  + bd_ref[0, :, n0:n0 + nchunk])
            for jj in range(nchunk // LANES):
                j = n0 // LANES + jj
                y_ref[pl.ds(j, MOE_TILE, stride=tr), :] = yb[:, jj * LANES:(jj + 1) * LANES]

    @pl.when(jnp.logical_not(used))
    def _():
        y_ref[...] = jnp.zeros(y_ref.shape, F32)


def experts(xb_tiles, block_e, n_used, w_gate_up, b_gate_up, w_down, b_down):
    tile_rows = MOE_TILE * TOKEN_TILE_ROWS
    n_blocks = xb_tiles.shape[0] // tile_rows
    d = D_MODEL
    clamp = lambda i, be, nu: jnp.minimum(i, nu[0] - 1)
    return pl.pallas_call(
        _expert_kernel,
        grid_spec=pltpu.PrefetchScalarGridSpec(
            num_scalar_prefetch=2,
            grid=(n_blocks,),
            in_specs=[pl.BlockSpec((tile_rows, LANES), lambda i, be, nu: (clamp(i, be, nu), 0)),
                      pl.BlockSpec((1, d, 2 * D_EXPERT), lambda i, be, nu: (be[i], 0, 0)),
                      pl.BlockSpec((1, 1, 2 * D_EXPERT), lambda i, be, nu: (be[i], 0, 0)),
                      pl.BlockSpec((1, D_EXPERT, d), lambda i, be, nu: (be[i], 0, 0)),
                      pl.BlockSpec((1, 1, d), lambda i, be, nu: (be[i], 0, 0))],
            out_specs=pl.BlockSpec((tile_rows, LANES), lambda i, be, nu: (i, 0)),
            scratch_shapes=[pltpu.VMEM((d, 2 * D_EXPERT), BF16), pltpu.VMEM((D_EXPERT, d), BF16)]),
        out_shape=jax.ShapeDtypeStruct(xb_tiles.shape, F32),
        compiler_params=_params(),
        name="experts",
    )(block_e, n_used, xb_tiles, w_gate_up, b_gate_up.reshape(N_EXPERTS, 1, -1),
      w_down, b_down.reshape(N_EXPERTS, 1, -1))


COMBINE_BUFFERS = 3


def _combine_kernel(slot_ref, slot1_ref, slot2_ref, gate_ref, res_ref, g2_ref, b2_ref,
                    yb_ref, o_ref, buf_sc, sem, *, tm):
    i = pl.program_id(0)
    n_steps = pl.num_programs(0)
    tr = TOKEN_TILE_ROWS
    group = DMA_ROWS_PER_ITER
    cur = lax.rem(i, COMBINE_BUFFERS)
    ahead = lax.rem(i + 2, COMBINE_BUFFERS)

    def token_copy(buf, kk, r, slot):
        src = pl.multiple_of(slot * tr, tr)
        dst = pl.multiple_of(r * tr, tr)
        return pltpu.make_async_copy(yb_ref.at[pl.ds(src, tr), :],
                                     buf_sc.at[buf, kk, pl.ds(dst, tr), :], sem.at[buf])

    def issue_group(sref, buf, g):
        toks = [g * group + u for u in range(group)]
        slots = [[sref[r * TOP_K + kk] for kk in range(TOP_K)] for r in toks]
        for r, row_slots in zip(toks, slots):
            for kk, slot in enumerate(row_slots):
                token_copy(buf, kk, r, slot).start(priority=kk % DMA_PRIORITIES)

    big = group * tr

    def compute_group(gb, issue=None):
        rows = pl.ds(pl.multiple_of(gb * big, big), big)
        gate = gate_ref[rows, :]
        chunks = []
        for j in range(tr):
            res = res_ref[rows, j * LANES:(j + 1) * LANES]
            ys = [buf_sc[cur, kk, pl.ds(gb * (big * tr) + j, big, stride=tr), :]
                  for kk in range(TOP_K)]
            if issue is not None:
                issue(gb * tr + j)
            hj = res
            for kk in range(TOP_K):
                hj = hj + gate[:, kk:kk + 1] * ys[kk]
            chunks.append(hj)
        mu = sum(jnp.sum(hj, axis=1, keepdims=True) for hj in chunks) * (1.0 / D_MODEL)
        cent = [hj - mu for hj in chunks]
        var = sum(jnp.sum(cj * cj, axis=1, keepdims=True) for cj in cent) * (1.0 / D_MODEL)
        inv = lax.rsqrt(var + LN_EPS)
        for j, cj in enumerate(cent):
            cols = slice(j * LANES, (j + 1) * LANES)
            o_ref[rows, cols] = cj * inv * g2_ref[:, cols] + b2_ref[:, cols]

    def issue_only(sref, buf):
        def body(g, c):
            issue_group(sref, buf, g)
            return c
        lax.fori_loop(0, tm // group, body, 0)

    @pl.when(i == 0)
    def _():
        issue_only(slot_ref, 0)

    @pl.when(jnp.logical_and(i == 0, n_steps > 1))
    def _():
        issue_only(slot1_ref, 1)

    for kk in range(TOP_K):
        pltpu.make_async_copy(yb_ref.at[pl.ds(0, tm * tr), :], buf_sc.at[cur, kk], sem.at[cur]).wait()

    @pl.when(i + 2 < n_steps)
    def _():
        def body(gb, c):
            compute_group(gb, issue=lambda g: issue_group(slot2_ref, ahead, g))
            return c
        lax.fori_loop(0, tm // big, body, 0)

    @pl.when(i + 2 >= n_steps)
    def _():
        def body(gb, c):
            compute_group(gb)
            return c
        lax.fori_loop(0, tm // big, body, 0)


def combine(y_tiles, slot_flat, gates, res, ln2_g, ln2_b, tm=COMBINE_TM):
    t, d = res.shape
    n_steps = t // tm
    slot_spec = lambda k: pl.BlockSpec((tm * TOP_K,), lambda i: (jnp.minimum(i + k, n_steps - 1),),
                                       memory_space=pltpu.SMEM)
    return pl.pallas_call(
        functools.partial(_combine_kernel, tm=tm),
        grid=(n_steps,),
        in_specs=[slot_spec(0), slot_spec(1), slot_spec(2),
                  pl.BlockSpec((tm, LANES), lambda i: (i, 0)),
                  pl.BlockSpec((tm, d), lambda i: (i, 0)),
                  pl.BlockSpec((1, d), lambda i: (0, 0)),
                  pl.BlockSpec((1, d), lambda i: (0, 0)),
                  pl.BlockSpec(memory_space=pl.ANY)],
        out_specs=pl.BlockSpec((tm, d), lambda i: (i, 0)),
        scratch_shapes=[pltpu.VMEM((COMBINE_BUFFERS, TOP_K, tm * TOKEN_TILE_ROWS, LANES), F32),
                        pltpu.SemaphoreType.DMA((COMBINE_BUFFERS,))],
        out_shape=jax.ShapeDtypeStruct((t, d), F32),
        compiler_params=_params(),
        name="combine",
    )(slot_flat, slot_flat, slot_flat, gates, res, ln2_g.reshape(1, d), ln2_b.reshape(1, d),
      y_tiles)


def kernel(x, p, w_in, conv_w, a_log, dt_bias, gnorm_w, w_out, ln1_g, ln1_b, w_router, b_router,
           w_gate_up, b_gate_up, w_down, b_down, w_ple_gate, b_ple_gate, w_ple_proj, ln2_g, ln2_b):
    bsz, seq, d = x.shape
    t = bsz * seq
    x2d = x.reshape(t, d)
    qkv_a, qkv_b, z_b, scal, scal_t = in_proj(x2d, w_in[0])
    ya = dilated_attn(qkv_a, bsz, seq)
    yb = deltanet(qkv_b, z_b, scal, scal_t, conv_w[0], a_log[0], dt_bias[0], gnorm_w[0], bsz, seq)
    x1, res, route, gates, counts = mix_route(
        ya, yb, x2d, p[0].reshape(t, PLE_DIM), w_out[0], ln1_g[0], ln1_b[0], w_router[0],
        b_router[0], w_ple_gate[0], b_ple_gate[0], w_ple_proj[0])

    cnt = counts[0, :N_EXPERTS].astype(jnp.int32)
    padded = (cnt + MOE_TILE - 1) // MOE_TILE * MOE_TILE
    pad_end = jnp.cumsum(padded).astype(jnp.int32)
    pad_start = pad_end - padded
    n_blocks = (t * TOP_K) // MOE_TILE + N_EXPERTS
    n_used = (pad_end[-1:] // MOE_TILE).astype(jnp.int32)
    tile_start = jnp.arange(n_blocks, dtype=jnp.int32) * MOE_TILE
    block_e = jnp.minimum(jnp.sum(pad_end[None, :] <= tile_start[:, None], axis=1),
                          N_EXPERTS - 1).astype(jnp.int32)
    experts_iota = jnp.arange(N_EXPERTS, dtype=jnp.int32)
    group_off = jnp.sum(jnp.where(route[:, :TOP_K, None] == experts_iota, pad_start, 0), axis=-1)
    slot_flat = (group_off + route[:, TOP_K:2 * TOP_K]).reshape(-1).astype(jnp.int32)

    xb = dispatch(x1, slot_flat, pad_start, pad_end, n_blocks * MOE_TILE)
    yexp = experts(xb, block_e, n_used, w_gate_up[0], b_gate_up[0], w_down[0], b_down[0])
    out = combine(yexp, slot_flat, gates, res, ln2_g[0], ln2_b[0])
    return out.reshape(bsz, seq, d)
```

```python
import functools
import math

import jax
import jax.numpy as jnp
from jax import lax
from jax.experimental import pallas as pl
from jax.experimental.pallas import tpu as pltpu

LANES = 128
VMEM_LIMIT_BYTES = 56 * 1024 * 1024

D_MODEL = 1024
PLE_DIM = 256
A_HEADS = 8
A_HEAD_DIM = 64
A_WIDTH = A_HEADS * A_HEAD_DIM
DILATIONS = (16, 4, 1)
ATT_BLOCK = 128
B_HEADS = 4
B_HEAD_DIM = 128
B_WIDTH = B_HEADS * B_HEAD_DIM
CONV_WIDTH = 4
DELTA_CHUNK = 64
N_EXPERTS = 32
TOP_K = 4
D_EXPERT = D_MODEL
SWIGLU_LIMIT = 7.0
SWIGLU_ALPHA = 1.702
MOE_BLOCK = 128
LN_EPS = 1e-5
RMS_EPS = 1e-6
L2_EPS = 1e-6
DEPTH = 1
DEEPNORM_ALPHA = (2.0 * DEPTH) ** 0.25

F32 = jnp.float32
BF16 = jnp.bfloat16


def _params(n_parallel_axes=1):
    return pltpu.CompilerParams(
        dimension_semantics=("arbitrary",) * n_parallel_axes,
        vmem_limit_bytes=VMEM_LIMIT_BYTES)


def _in_proj_kernel(x_ref, wa_ref, wb_ref, wz_ref, ws_ref, wst_ref, a_ref, b_ref, z_ref, s_ref, st_ref):
    xb = x_ref[...].astype(BF16)
    a_ref[...] = jnp.dot(xb, wa_ref[...], preferred_element_type=F32)
    b_ref[...] = jnp.dot(xb, wb_ref[...], preferred_element_type=F32).astype(BF16)
    z_ref[...] = jnp.dot(xb, wz_ref[...], preferred_element_type=F32)
    s_ref[...] = jnp.dot(xb, ws_ref[...], preferred_element_type=F32)
    st_ref[...] = lax.dot_general(wst_ref[...], xb, (((1,), (1,)), ((), ())),
                                  preferred_element_type=F32)


def in_proj(x2d, w_in, tm=512):
    t, d = x2d.shape
    na, nb = 3 * A_WIDTH, 3 * B_WIDTH
    wa = w_in[:, :na].astype(BF16)
    wb = w_in[:, na:na + nb].astype(BF16)
    wz = w_in[:, na + nb:na + nb + B_WIDTH].astype(BF16)
    ws = jnp.pad(w_in[:, na + nb + B_WIDTH:], ((0, 0), (0, LANES - 2 * B_HEADS))).astype(BF16)
    wst = ws[:, :16].T
    full = lambda a: pl.BlockSpec(a.shape, lambda i: (0, 0))
    return pl.pallas_call(
        _in_proj_kernel,
        grid=(t // tm,),
        in_specs=[pl.BlockSpec((tm, d), lambda i: (i, 0)),
                  full(wa), full(wb), full(wz), full(ws), full(wst)],
        out_specs=[pl.BlockSpec((tm, na), lambda i: (i, 0)),
                   pl.BlockSpec((tm, nb), lambda i: (i, 0)),
                   pl.BlockSpec((tm, B_WIDTH), lambda i: (i, 0)),
                   pl.BlockSpec((tm, LANES), lambda i: (i, 0)),
                   pl.BlockSpec((16, tm), lambda i: (0, i))],
        out_shape=[jax.ShapeDtypeStruct((t, na), F32),
                   jax.ShapeDtypeStruct((t, nb), BF16),
                   jax.ShapeDtypeStruct((t, B_WIDTH), F32),
                   jax.ShapeDtypeStruct((t, LANES), F32),
                   jax.ShapeDtypeStruct((16, t), F32)],
        compiler_params=_params(),
        name="in_proj",
    )(x2d, wa, wb, wz, ws, wst)


ATTN_UNROLL_DENSE = 5
ATTN_UNROLL_SINGLE = 8


def _attn_kernel(q_ref, k_ref, v_ref, o_ref, m_sc, l_sc, acc_sc, *, seq):
    blk = ATT_BLOCK
    lane = lax.broadcasted_iota(jnp.int32, (blk, LANES), 1)
    head0 = lane < A_HEAD_DIM
    row = lax.broadcasted_iota(jnp.int32, (blk, blk), 0)
    col = lax.broadcasted_iota(jnp.int32, (blk, blk), 1)
    prev_ok = col >= row
    cur_ok = col <= row
    neg_inf = jnp.float32(-jnp.inf)
    scale = 1.0 / math.sqrt(A_HEAD_DIM)

    def blocks(starts, stride, has_prev, first_branch, last_branch):
        def rows(ref, s0):
            if stride == 1:
                return ref[pl.ds(s0, blk), :]
            return ref[pl.ds(s0, blk, stride=stride), :]

        def put(ref, s0, val):
            if stride == 1:
                ref[pl.ds(s0, blk), :] = val
            else:
                ref[pl.ds(s0, blk, stride=stride), :] = val

        hmask = (head0, jnp.logical_not(head0))
        items = []
        for start, hp in zip(starts, has_prev):
            q = rows(q_ref, start) * scale
            keys = rows(k_ref, start)
            vals = rows(v_ref, start)
            if hp:
                keys = jnp.concatenate([rows(k_ref, start - blk * stride), keys], axis=0)
                vals = jnp.concatenate([rows(v_ref, start - blk * stride), vals], axis=0)
            it = dict(start=start, hp=hp, keys=keys.astype(BF16), vals=vals,
                      qh=[jnp.where(hm, q, 0.0).astype(BF16) for hm in hmask])
            if not first_branch:
                it["m_old"] = rows(m_sc, start)
                it["l_old"] = rows(l_sc, start)
                it["acc_old"] = rows(acc_sc, start)
            items.append(it)

        for it in items:
            it["s"] = [lax.dot_general(qh, it["keys"], (((1,), (1,)), ((), ())),
                                       preferred_element_type=F32) for qh in it["qh"]]
        for it in items:
            ok = jnp.concatenate([prev_ok, cur_ok], axis=1) if it["hp"] else cur_ok
            it["m"], it["sum"], it["p"] = [], [], []
            for h in range(2):
                s = jnp.where(ok, it["s"][h], neg_inf)
                t = jnp.maximum(s[:, :blk], s[:, blk:]) if it["hp"] else s
                if not first_branch:
                    t = jnp.maximum(t, jnp.where(hmask[h], it["m_old"], neg_inf))
                m_h = jnp.max(t, axis=1, keepdims=True)
                p = jnp.exp(s - m_h)
                psum = p[:, :blk] + p[:, blk:] if it["hp"] else p
                it["m"].append(m_h)
                it["sum"].append(jnp.sum(psum, axis=1, keepdims=True))
                it["p"].append(p.astype(BF16))
        for it in items:
            v16 = it["vals"].astype(BF16)
            it["pv"] = [jnp.dot(it["p"][h], v16, preferred_element_type=F32) for h in range(2)]
        for it in items:
            m_new = jnp.where(head0, it["m"][0], it["m"][1])
            l_new = jnp.where(head0, it["sum"][0], it["sum"][1])
            acc_new = jnp.where(head0, it["pv"][0], it["pv"][1])
            if not first_branch:
                alpha = jnp.exp(it["m_old"] - m_new)
                l_new = l_new + alpha * it["l_old"]
                acc_new = acc_new + alpha * it["acc_old"]
            if last_branch:
                put(o_ref, it["start"], acc_new / l_new)
            else:
                put(m_sc, it["start"], m_new)
                put(l_sc, it["start"], l_new)
                put(acc_sc, it["start"], acc_new)

    n_br = len(DILATIONS)
    for bi, dil in enumerate(DILATIONS):
        first, last = bi == 0, bi == n_br - 1
        nblk = seq // dil // blk
        if dil == 1:
            blocks([0], 1, [False], first, last)
            per = ATTN_UNROLL_DENSE
            assert (nblk - 1) % per == 0

            def body(g, c, first=first, last=last, per=per):
                starts = [pl.multiple_of((1 + g * per + j) * blk, blk) for j in range(per)]
                blocks(starts, 1, [True] * per, first, last)
                return c
            lax.fori_loop(0, (nblk - 1) // per, body, 0)
        elif nblk > 1:
            def body(r, c, dil=dil, nblk=nblk, first=first, last=last):
                blocks([r + n * blk * dil for n in range(nblk)], dil,
                       [n > 0 for n in range(nblk)], first, last)
                return c
            lax.fori_loop(0, dil, body, 0)
        else:
            per = ATTN_UNROLL_SINGLE
            assert dil % per == 0

            def body(g, c, dil=dil, first=first, last=last, per=per):
                blocks([g * per + j for j in range(per)], dil, [False] * per, first, last)
                return c
            lax.fori_loop(0, dil // per, body, 0)


def dilated_attn(qkv_a, bsz, seq):
    t = bsz * seq
    n_pairs = A_WIDTH // LANES
    blk = lambda off: pl.BlockSpec((seq, LANES), lambda b, p: (b, off + p))
    return pl.pallas_call(
        functools.partial(_attn_kernel, seq=seq),
        grid=(bsz, n_pairs),
        in_specs=[blk(0), blk(n_pairs), blk(2 * n_pairs)],
        out_specs=pl.BlockSpec((seq, LANES), lambda b, p: (b, p)),
        out_shape=jax.ShapeDtypeStruct((t, A_WIDTH), F32),
        scratch_shapes=[pltpu.VMEM((seq, LANES), F32)] * 3,
        compiler_params=_params(2),
        name="dilated_attn",
    )(qkv_a, qkv_a, qkv_a)


DELTA_HEADS_PER_STEP = 4
DELTA_P1_CHUNKS = 4
CONV_TILE = 256
CONV_HISTORY_ROWS = 16


def _dot16(a, b):
    return jnp.dot(a, b, preferred_element_type=F32)


def _split2(x):
    hi = x.astype(BF16)
    return hi, (x - hi.astype(F32)).astype(BF16)


def _split3(x):
    hi = x.astype(BF16)
    r = x - hi.astype(F32)
    mid = r.astype(BF16)
    return hi, mid, (r - mid.astype(F32)).astype(BF16)


def _dot_sel(m01, x):
    hi, mid, lo = _split3(x)
    return _dot16(m01, hi) + (_dot16(m01, mid) + _dot16(m01, lo))


def _dot_sel_r(x, m01):
    hi, mid, lo = _split3(x)
    return _dot16(hi, m01) + (_dot16(mid, m01) + _dot16(lo, m01))


def _stack_parts(x, lo_half, want_l=True, want_r=True):
    xh = x.astype(BF16)
    xh_f = xh.astype(F32)
    xl_f = x - xh_f
    left = right = None
    if want_l:
        mix = jnp.where(lo_half, xh_f, xl_f).astype(BF16)
        left = jnp.concatenate([mix, mix], axis=1)
    if want_r:
        xl = xl_f.astype(BF16)
        right = jnp.concatenate([xh, xh, xl, xl], axis=0)
    return left, right


def _softplus(x):
    return jnp.maximum(x, 0.0) + jnp.log(1.0 + jnp.exp(-jnp.abs(x)))


def _sigmoid(x):
    return 1.0 / (1.0 + jnp.exp(-x))


def _delta_kernel(q_ref, k_ref, v_ref, z_ref, sc_ref, st_ref, par_ref, part_ref,
                  wq_ref, wk_ref, wv_ref, nw_ref, o_ref,
                  conv_sc, qn_sc, kn_sc, vn_sc, w_sc, attn_sc, kdt_sc, gcr_sc, egl_sc,
                  state_sc, *, seq, hps):
    c = DELTA_CHUNK
    dk = B_HEAD_DIM
    n_chunks = seq // c
    tile = DELTA_P1_CHUNKS * c
    group = pl.program_id(1)
    lane = lax.broadcasted_iota(jnp.int32, (tile, LANES), 1)
    r2 = lax.broadcasted_iota(jnp.int32, (tile, tile), 0)
    c2 = lax.broadcasted_iota(jnp.int32, (tile, tile), 1)
    same_chunk = (r2 // c) == (c2 // c)
    cum_mat = jnp.where(same_chunk & (c2 <= r2), 1.0, 0.0).astype(BF16)
    tot_mat = jnp.where(same_chunk, 1.0, 0.0).astype(BF16)
    ri = lax.broadcasted_iota(jnp.int32, (c, 2 * c), 0)
    li = lax.broadcasted_iota(jnp.int32, (c, 2 * c), 1)
    lo_half = li < c
    ci = jnp.where(lo_half, li, li - c)
    causal = ci <= ri
    strict = ci < ri
    eye = jnp.where(ci == ri, 1.0, 0.0).astype(F32)
    upper = jnp.where(ri <= ci, 1.0, 0.0).astype(BF16)

    hist = CONV_HISTORY_ROWS
    for hh in range(hps):
        cols = slice(hh * dk, (hh + 1) * dk)
        streams = ((q_ref, wq_ref, qn_sc, "q"), (k_ref, wk_ref, kn_sc, "k"), (v_ref, wv_ref, vn_sc, "v"))

        def conv_tile(si, xe, c0, hh=hh, cols=cols):
            _, w_ref, dst, kind = streams[si]
            w = w_ref[:, cols]
            conv_sc[si] = xe
            off = hist - (CONV_WIDTH - 1)
            y = w[0:1, :] * conv_sc[si, off:off + CONV_TILE, :]
            for j in range(1, CONV_WIDTH):
                y = y + w[j:j + 1, :] * conv_sc[si, off + j:off + j + CONV_TILE, :]
            y = y * _sigmoid(y)
            if kind != "v":
                y = y * lax.rsqrt(jnp.sum(y * y, axis=1, keepdims=True) + L2_EPS)
            if kind == "q":
                y = y * (B_HEAD_DIM ** -0.5)
            dst[hh, pl.ds(c0, CONV_TILE), :] = y

        for si, (src, _, _, _) in enumerate(streams):
            conv_tile(si, jnp.concatenate([jnp.zeros((hist, dk), F32),
                                           src[0:CONV_TILE, cols].astype(F32)], axis=0), 0)

        def conv_body(ti, carry, cols=cols, conv_tile=conv_tile):
            c0 = pl.multiple_of(ti * CONV_TILE, CONV_TILE)
            for si, (src, _, _, _) in enumerate(streams):
                conv_tile(si, src[pl.ds(c0 - hist, hist + CONV_TILE), cols].astype(F32), c0)
            return carry

        lax.fori_loop(1, seq // CONV_TILE, conv_body, 0)

    for hh in range(hps):
        h = group * hps + hh
        a_row = st_ref[B_HEADS + h]
        g_row = -jnp.exp(part_ref[0, h]) * _softplus(a_row + part_ref[1, h])
        gcr_sc[hh] = _dot_sel_r(g_row, upper)

    alog_l = par_ref[0:1, :]
    dtb_l = par_ref[1:2, :]

    def aligned(x, m):
        return x if isinstance(x, int) else pl.multiple_of(x, m)

    def phase1(i):
        r0 = aligned(i * tile, tile)
        rows = pl.ds(r0, tile)
        s = sc_ref[rows, :]
        beta_all = _sigmoid(s)
        g_all = -jnp.exp(alog_l) * _softplus(s + dtb_l)
        gc_all = _dot_sel(cum_mat, g_all)
        gl_all = _dot_sel(tot_mat, g_all)
        chains = []
        for hh in range(hps):
            h = group * hps + hh
            pick = lambda a, l: jnp.sum(jnp.where(lane == l, a, 0.0), axis=1, keepdims=True)
            beta = pick(beta_all, h)
            gc = pick(gc_all, B_HEADS + h)
            gl = pick(gl_all, B_HEADS + h)
            egc = jnp.exp(gc)
            q = qn_sc[hh, rows, :]
            k = kn_sc[hh, rows, :]
            v = vn_sc[hh, rows, :]
            kbeta = k * beta
            vbeta = v * beta
            qn_sc[hh, rows, :] = q * egc
            kdec = k * jnp.exp(gl - gc)
            kb16 = k.astype(BF16)
            for cc in range(DELTA_P1_CHUNKS):
                sl = slice(cc * c, (cc + 1) * c)
                n = DELTA_P1_CHUNKS * i + cc
                kdt_sc[hh, n] = kdec[sl].T.astype(BF16)
                egl_sc[hh, pl.ds(n, 1), :] = jnp.broadcast_to(jnp.exp(gl[cc * c:cc * c + 1]), (1, dk))
                diff = gc[sl] - gcr_sc[hh, pl.ds(n, 1), :]
                e = jnp.exp(jnp.where(causal, diff, 0.0))
                k2 = jnp.concatenate([kb16[sl], kb16[sl]], axis=0)
                lhs = jnp.concatenate([kbeta[sl], q[sl]], axis=0).astype(BF16)
                rhs = jnp.concatenate([vbeta[sl], kbeta[sl] * egc[sl]], axis=1)
                chains.append(dict(hh=hh, row=r0 + cc * c, e=e, k2=k2, lhs=lhs, rhs=rhs))
        yield

        for ch in chains:
            kq = lax.dot_general(ch["lhs"], ch["k2"], (((1,), (1,)), ((), ())),
                                 preferred_element_type=F32)
            ch["a"] = jnp.where(strict, kq[:c] * ch["e"], 0.0)
            attn_sc[ch["hh"], pl.ds(ch["row"], c), :] = (
                jnp.where(causal, kq[c:] * ch["e"], 0.0)[:, :c].astype(BF16))
        yield
        for ch in chains:
            ch["inv"] = eye - ch["a"]
            p_l, p_r = _stack_parts(ch["a"], lo_half)
            ch["p"] = _dot16(p_l, p_r)
        yield
        for _ in range(4):
            for ch in chains:
                p_l, p_r = _stack_parts(ch["p"], lo_half)
                inv_l, _ = _stack_parts(ch["inv"], lo_half, want_r=False)
                res = _dot16(jnp.concatenate([p_l, inv_l], axis=0), p_r)
                ch["p"] = res[:c]
                ch["inv"] = ch["inv"] + res[c:]
            yield
        for ch in chains:
            inv_l, _ = _stack_parts(ch["inv"], lo_half, want_r=False)
            _, p_r = _stack_parts(ch["p"], lo_half, want_l=False)
            ch["inv"] = ch["inv"] + _dot16(inv_l, p_r)
        yield
        for ch in chains:
            inv_l, _ = _stack_parts(ch["inv"], lo_half, want_r=False)
            _, rhs_r = _stack_parts(ch["rhs"], None, want_l=False)
            sol = _dot16(inv_l, rhs_r)
            vn_sc[ch["hh"], pl.ds(ch["row"], c), :] = sol[:, :dk]
            w_sc[ch["hh"], pl.ds(ch["row"], c), :] = sol[:, dk:].astype(BF16)

    nw = nw_ref[...]

    def phase2(i):
        for cc in range(DELTA_P1_CHUNKS):
            n = DELTA_P1_CHUNKS * i + cc
            r0 = aligned(n * c, c)
            rows = pl.ds(r0, c)
            heads = range(hps)
            st = [state_sc[hh] for hh in heads]
            st16 = [s_.astype(BF16) for s_ in st]
            lhs = [jnp.concatenate([w_sc[hh, rows, :], qn_sc[hh, rows, :].astype(BF16)], axis=0)
                   for hh in heads]
            ws = [_dot16(lhs[hh], st16[hh]) for hh in heads]
            yield
            vn16 = [(vn_sc[hh, rows, :] - ws[hh][:c]).astype(BF16) for hh in heads]
            av = [_dot16(attn_sc[hh, rows, :], vn16[hh]) for hh in heads]
            kv = [_dot16(kdt_sc[hh, n], vn16[hh]) for hh in heads]
            for hh in heads:
                state_sc[hh] = st[hh] * egl_sc[hh, pl.ds(n, 1), :] + kv[hh]
                out = ws[hh][c:] + av[hh]
                o = out * lax.rsqrt(jnp.mean(out * out, axis=1, keepdims=True) + RMS_EPS) * nw
                z = z_ref[rows, hh * dk:(hh + 1) * dk]
                o_ref[rows, hh * dk:(hh + 1) * dk] = (o * (z * _sigmoid(z))).astype(o_ref.dtype)
            yield

    def interleave(*gens):
        live = list(gens)
        while live:
            for g in list(live):
                try:
                    next(g)
                except StopIteration:
                    live.remove(g)

    n_tiles = n_chunks // DELTA_P1_CHUNKS
    state_sc[...] = jnp.zeros(state_sc.shape, F32)
    interleave(phase1(0))

    def pipelined(i, carry):
        interleave(phase1(i), phase2(i - 1))
        return carry

    lax.fori_loop(1, n_tiles, pipelined, 0)
    interleave(phase2(n_tiles - 1))


def deltanet(qkv_b, z_b, scal, scal_t, conv_w, a_log, dt_bias, gnorm_w, bsz, seq):
    t = bsz * seq
    hps = DELTA_HEADS_PER_STEP
    n_groups = B_HEADS // hps
    wg = hps * B_HEAD_DIM
    c = DELTA_CHUNK
    n_chunks = seq // c
    st3 = scal_t.reshape(16, t // c, c)
    par = jnp.zeros((8, LANES), F32)
    par = par.at[0, B_HEADS:2 * B_HEADS].set(a_log).at[1, B_HEADS:2 * B_HEADS].set(dt_bias)
    part = jnp.stack([a_log, dt_bias]).astype(F32)
    blk = lambda off: pl.BlockSpec((seq, wg), lambda b, g: (b, off + g))
    wblk = lambda off: pl.BlockSpec((CONV_WIDTH, wg), lambda b, g: (0, off + g))
    sc = lambda shape, dt: pltpu.VMEM(shape, dt)
    return pl.pallas_call(
        functools.partial(_delta_kernel, seq=seq, hps=hps),
        grid=(bsz, n_groups),
        in_specs=[blk(0), blk(n_groups), blk(2 * n_groups), blk(0),
                  pl.BlockSpec((seq, LANES), lambda b, g: (b, 0)),
                  pl.BlockSpec((16, n_chunks, c), lambda b, g: (0, b, 0)),
                  pl.BlockSpec((8, LANES), lambda b, g: (0, 0)),
                  pl.BlockSpec(memory_space=pltpu.SMEM),
                  wblk(0), wblk(n_groups), wblk(2 * n_groups),
                  pl.BlockSpec((1, B_HEAD_DIM), lambda b, g: (0, 0))],
        out_specs=pl.BlockSpec((seq, wg), lambda b, g: (b, g)),
        out_shape=jax.ShapeDtypeStruct((t, B_WIDTH), BF16),
        scratch_shapes=[sc((3, CONV_HISTORY_ROWS + CONV_TILE, B_HEAD_DIM), F32),
                        sc((hps, seq, B_HEAD_DIM), F32), sc((hps, seq, B_HEAD_DIM), F32),
                        sc((hps, seq, B_HEAD_DIM), F32),
                        sc((hps, seq, B_HEAD_DIM), BF16),
                        sc((hps, seq, c), BF16), sc((hps, n_chunks, B_HEAD_DIM, c), BF16),
                        sc((hps, n_chunks, 2 * c), F32), sc((hps, n_chunks, B_HEAD_DIM), F32),
                        sc((hps, B_HEAD_DIM, B_HEAD_DIM), F32)],
        compiler_params=_params(2),
        name="deltanet",
    )(qkv_b, qkv_b, qkv_b, z_b, scal, st3, par, part,
      conv_w, conv_w, conv_w, gnorm_w.reshape(1, B_HEAD_DIM))


SUBLANES = 8
TOKEN_TILE_ROWS = D_MODEL // LANES
assert TOKEN_TILE_ROWS == SUBLANES


def _store_token_tiles(ref, x, n_tok):
    for j in range(TOKEN_TILE_ROWS):
        ref[pl.ds(j, n_tok, stride=TOKEN_TILE_ROWS), :] = x[:, j * LANES:(j + 1) * LANES]


def _layer_norm(h, g, b):
    mu = jnp.mean(h, axis=1, keepdims=True)
    hc = h - mu
    var = jnp.mean(hc * hc, axis=1, keepdims=True)
    return hc * lax.rsqrt(var + LN_EPS) * g + b


def _mix_route_kernel(ya_ref, yb_ref, x_ref, p_ref, woa_ref, wob_ref, g1_ref, b1_ref,
                      wr_ref, br_ref, wpg_ref, bpg_ref, wpp_ref,
                      x1_ref, res_ref, route_ref, gate_ref, cnt_ref, run_sc, wr2_sc, *, tm):
    i = pl.program_id(0)

    @pl.when(i == 0)
    def _():
        run_sc[...] = jnp.zeros(run_sc.shape, F32)
        w_hi, w_lo = _split2(wr_ref[...])
        wr2_sc[:, :LANES] = w_hi
        wr2_sc[:, LANES:] = w_lo

    mix = (jnp.dot(ya_ref[...].astype(BF16), woa_ref[...], preferred_element_type=F32)
           + jnp.dot(yb_ref[...].astype(BF16), wob_ref[...], preferred_element_type=F32))
    x1 = _layer_norm(DEEPNORM_ALPHA * x_ref[...] + mix, g1_ref[...], b1_ref[...])
    _store_token_tiles(x1_ref, x1, tm)
    x1b = x1.astype(BF16)

    x1l = (x1 - x1b.astype(F32)).astype(BF16)
    r_hi = jnp.dot(x1b, wr2_sc[...], preferred_element_type=F32)
    r_lo = jnp.dot(x1l, wr2_sc[:, :LANES], preferred_element_type=F32)
    logits = r_hi[:, :LANES] + (r_hi[:, LANES:] + r_lo) + br_ref[...]

    pgate = _sigmoid(jnp.dot(x1b, wpg_ref[...], preferred_element_type=F32) + bpg_ref[...])
    proj = jnp.dot(p_ref[...].astype(BF16), wpp_ref[...], preferred_element_type=F32)
    res_ref[...] = DEEPNORM_ALPHA * x1 + pgate * proj

    lane = lax.broadcasted_iota(jnp.int32, (tm, LANES), 1)
    cur = jnp.where(lane < N_EXPERTS, logits, -jnp.inf)
    vals, hots = [], []
    for _ in range(TOP_K):
        m = jnp.max(cur, axis=1, keepdims=True)
        idx = jnp.min(jnp.where(cur == m, lane, LANES), axis=1, keepdims=True)
        hot = lane == idx
        cur = jnp.where(hot, -jnp.inf, cur)
        vals.append(m)
        hots.append((hot, idx))
    exps = [jnp.exp(v - vals[0]) for v in vals]
    den = exps[0] + exps[1] + exps[2] + exps[3]

    member = jnp.zeros((tm, LANES), F32)
    for hot, _ in hots:
        member = member + jnp.where(hot, 1.0, 0.0)
    rr = lax.broadcasted_iota(jnp.int32, (tm, tm), 0)
    cc = lax.broadcasted_iota(jnp.int32, (tm, tm), 1)
    before = jnp.where(cc < rr, 1.0, 0.0).astype(BF16)
    prior = jnp.dot(before, member.astype(BF16), preferred_element_type=F32) + run_sc[0:1, :]
    route = jnp.zeros((tm, LANES), jnp.int32)
    gate = jnp.zeros((tm, LANES), F32)
    for kk, (hot, idx) in enumerate(hots):
        rank = jnp.sum(jnp.where(hot, prior, 0.0), axis=1, keepdims=True).astype(jnp.int32)
        route = jnp.where(lane == kk, idx, route)
        route = jnp.where(lane == TOP_K + kk, rank, route)
        gate = jnp.where(lane == kk, exps[kk] / den, gate)
    route_ref[...] = route
    gate_ref[...] = gate
    run_sc[...] = run_sc[...] + jnp.sum(member, axis=0, keepdims=True)
    cnt_ref[...] = run_sc[...]


def mix_route(ya, yb, x2d, p2d, w_out, ln1_g, ln1_b, w_router, b_router, w_ple_gate, b_ple_gate,
              w_ple_proj, tm=512):
    t, d = x2d.shape
    woa = w_out[:A_WIDTH].astype(BF16)
    wob = w_out[A_WIDTH:].astype(BF16)
    wr = jnp.pad(w_router, ((0, 0), (0, LANES - N_EXPERTS)))
    br = jnp.pad(b_router, (0, LANES - N_EXPERTS)).reshape(1, LANES)
    row = lambda w: pl.BlockSpec((tm, w), lambda i: (i, 0))
    full = lambda a: pl.BlockSpec(a.shape, lambda i: (0,) * a.ndim)
    tiles = pl.BlockSpec((tm * TOKEN_TILE_ROWS, LANES), lambda i: (i, 0))
    ops = [woa, wob, ln1_g.reshape(1, d), ln1_b.reshape(1, d), wr, br,
           w_ple_gate.astype(BF16), b_ple_gate.reshape(1, d), w_ple_proj.astype(BF16)]
    return pl.pallas_call(
        functools.partial(_mix_route_kernel, tm=tm),
        grid=(t // tm,),
        in_specs=[row(A_WIDTH), row(B_WIDTH), row(d), row(PLE_DIM)] + [full(a) for a in ops],
        out_specs=[tiles, row(d), row(LANES), row(LANES),
                   pl.BlockSpec((8, LANES), lambda i: (0, 0))],
        out_shape=[jax.ShapeDtypeStruct((t * TOKEN_TILE_ROWS, LANES), F32),
                   jax.ShapeDtypeStruct((t, d), F32),
                   jax.ShapeDtypeStruct((t, LANES), jnp.int32),
                   jax.ShapeDtypeStruct((t, LANES), F32),
                   jax.ShapeDtypeStruct((8, LANES), F32)],
        scratch_shapes=[pltpu.VMEM((8, LANES), F32), pltpu.VMEM((d, 2 * LANES), BF16)],
        compiler_params=_params(),
        name="mix_route",
    )(ya, yb, x2d, p2d, *ops)


MOE_TILE = 512
DISPATCH_TM = 1024
COMBINE_TM = 256
DMA_ROWS_PER_ITER = 8
DMA_PRIORITIES = 2


def _dispatch_kernel(pstart_ref, pend_ref, slot_ref, x_ref, xb_ref, zero_sc, sem, zsem, *, tm):
    i = pl.program_id(0)
    tr = TOKEN_TILE_ROWS
    tile_rows = MOE_TILE * tr

    def zero_copy(e):
        off = pl.multiple_of((pend_ref[e] - MOE_TILE) * tr, tile_rows)
        return pltpu.make_async_copy(zero_sc, xb_ref.at[pl.ds(off, tile_rows), :], zsem)

    @pl.when(i == 0)
    def _():
        zero_sc[...] = jnp.zeros(zero_sc.shape, zero_sc.dtype)
        for e in range(N_EXPERTS):
            @pl.when(pend_ref[e] > pstart_ref[e])
            def _():
                zero_copy(e).start()
        for e in range(N_EXPERTS):
            @pl.when(pend_ref[e] > pstart_ref[e])
            def _():
                zero_copy(e).wait()

        n_tiles = xb_ref.shape[0] // tile_rows
        first_unused = pend_ref[N_EXPERTS - 1] // MOE_TILE

        def tail_copy(j):
            off = pl.multiple_of(j * tile_rows, tile_rows)
            return pltpu.make_async_copy(zero_sc, xb_ref.at[pl.ds(off, tile_rows), :], zsem)

        def tail_start(j, c):
            tail_copy(j).start()
            return c

        def tail_wait(j, c):
            tail_copy(j).wait()
            return c

        lax.fori_loop(first_unused, n_tiles, tail_start, 0)
        lax.fori_loop(first_unused, n_tiles, tail_wait, 0)

    def token_copy(tok, slot):
        src = pl.multiple_of(tok * tr, tr)
        dst = pl.multiple_of(slot * tr, tr)
        return pltpu.make_async_copy(x_ref.at[pl.ds(src, tr), :], xb_ref.at[pl.ds(dst, tr), :], sem)

    def start(g, c):
        toks = [g * DMA_ROWS_PER_ITER + u for u in range(DMA_ROWS_PER_ITER)]
        slots = [[slot_ref[r * TOP_K + kk] for kk in range(TOP_K)] for r in toks]
        for r, row_slots in zip(toks, slots):
            for kk, slot in enumerate(row_slots):
                token_copy(r, slot).start(priority=kk % DMA_PRIORITIES)
        return c

    lax.fori_loop(0, tm // DMA_ROWS_PER_ITER, start, 0)
    for _ in range(TOP_K):
        pltpu.make_async_copy(x_ref, xb_ref.at[pl.ds(0, tm * tr), :], sem).wait()


def dispatch(x1_tiles, slot_flat, pad_start, pad_end, n_slots, tm=DISPATCH_TM):
    t = x1_tiles.shape[0] // TOKEN_TILE_ROWS
    return pl.pallas_call(
        functools.partial(_dispatch_kernel, tm=tm),
        grid_spec=pltpu.PrefetchScalarGridSpec(
            num_scalar_prefetch=2,
            grid=(t // tm,),
            in_specs=[pl.BlockSpec((tm * TOP_K,), lambda i, ps, pe: (i,), memory_space=pltpu.SMEM),
                      pl.BlockSpec((tm * TOKEN_TILE_ROWS, LANES), lambda i, ps, pe: (i, 0))],
            out_specs=pl.BlockSpec(memory_space=pl.ANY),
            scratch_shapes=[pltpu.VMEM((MOE_TILE * TOKEN_TILE_ROWS, LANES), F32),
                            pltpu.SemaphoreType.DMA(()), pltpu.SemaphoreType.DMA(())]),
        out_shape=jax.ShapeDtypeStruct((n_slots * TOKEN_TILE_ROWS, LANES), F32),
        compiler_params=_params(),
        name="dispatch",
    )(pad_start, pad_end, slot_flat, x1_tiles)


EXPERT_CAST_ROWS = 256
EXPERT_COL_CHUNK = 256


def _expert_kernel(be_ref, nu_ref, x_ref, wgu_ref, bgu_ref, wd_ref, bd_ref, y_ref, wgu_sc, wd_sc):
    i = pl.program_id(0)
    used = i < nu_ref[0]
    new_expert = jnp.logical_or(i == 0, be_ref[i] != be_ref[jnp.maximum(i - 1, 0)])

    @pl.when(jnp.logical_and(used, new_expert))
    def _():
        def cast(j, c):
            r = pl.ds(pl.multiple_of(j * EXPERT_CAST_ROWS, EXPERT_CAST_ROWS), EXPERT_CAST_ROWS)
            wgu_sc[r, :] = wgu_ref[0, r, :].astype(BF16)
            wd_sc[r, :] = wd_ref[0, r, :].astype(BF16)
            return c
        lax.fori_loop(0, D_MODEL // EXPERT_CAST_ROWS, cast, 0)

    @pl.when(used)
    def _():
        tr = TOKEN_TILE_ROWS
        nchunk = EXPERT_COL_CHUNK
        xc = [x_ref[pl.ds(j, MOE_TILE, stride=tr), :].astype(BF16) for j in range(tr)]
        xb = jnp.concatenate(xc, axis=1)
        acts = []
        for c0 in range(0, D_EXPERT, nchunk):
            if c0 == 0:
                half = D_MODEL // 2
                xs = (jnp.concatenate(xc[:tr // 2], axis=1), jnp.concatenate(xc[tr // 2:], axis=1))
                g = sum(jnp.dot(xs[h], wgu_sc[h * half:(h + 1) * half, c0:c0 + nchunk],
                                preferred_element_type=F32) for h in range(2))
                u = sum(jnp.dot(xs[h], wgu_sc[h * half:(h + 1) * half,
                                              D_EXPERT + c0:D_EXPERT + c0 + nchunk],
                                preferred_element_type=F32) for h in range(2))
            else:
                g = jnp.dot(xb, wgu_sc[:, c0:c0 + nchunk], preferred_element_type=F32)
                u = jnp.dot(xb, wgu_sc[:, D_EXPERT + c0:D_EXPERT + c0 + nchunk],
                            preferred_element_type=F32)
            g = g + bgu_ref[0, :, c0:c0 + nchunk]
            u = u + bgu_ref[0, :, D_EXPERT + c0:D_EXPERT + c0 + nchunk]
            g = jnp.minimum(g, SWIGLU_LIMIT)
            u = jnp.clip(u, -SWIGLU_LIMIT, SWIGLU_LIMIT)
            acts.append(((u + 1.0) * (g * _sigmoid(SWIGLU_ALPHA * g))).astype(BF16))
        act = jnp.concatenate(acts, axis=1)
        for n0 in range(0, D_MODEL, nchunk):
            yb = (jnp.dot(act, wd_sc[:, n0:n0 + nchunk], preferred_element_type=F32)
                  + bd_ref[0, :, n0:n0 + nchunk])
            for jj in range(nchunk // LANES):
                j = n0 // LANES + jj
                y_ref[pl.ds(j, MOE_TILE, stride=tr), :] = yb[:, jj * LANES:(jj + 1) * LANES]

    @pl.when(jnp.logical_not(used))
    def _():
        y_ref[...] = jnp.zeros(y_ref.shape, F32)


def experts(xb_tiles, block_e, n_used, w_gate_up, b_gate_up, w_down, b_down):
    tile_rows = MOE_TILE * TOKEN_TILE_ROWS
    n_blocks = xb_tiles.shape[0] // tile_rows
    d = D_MODEL
    clamp = lambda i, be, nu: jnp.minimum(i, nu[0] - 1)
    return pl.pallas_call(
        _expert_kernel,
        grid_spec=pltpu.PrefetchScalarGridSpec(
            num_scalar_prefetch=2,
            grid=(n_blocks,),
            in_specs=[pl.BlockSpec((tile_rows, LANES), lambda i, be, nu: (clamp(i, be, nu), 0)),
                      pl.BlockSpec((1, d, 2 * D_EXPERT), lambda i, be, nu: (be[i], 0, 0)),
                      pl.BlockSpec((1, 1, 2 * D_EXPERT), lambda i, be, nu: (be[i], 0, 0)),
                      pl.BlockSpec((1, D_EXPERT, d), lambda i, be, nu: (be[i], 0, 0)),
                      pl.BlockSpec((1, 1, d), lambda i, be, nu: (be[i], 0, 0))],
            out_specs=pl.BlockSpec((tile_rows, LANES), lambda i, be, nu: (i, 0)),
            scratch_shapes=[pltpu.VMEM((d, 2 * D_EXPERT), BF16), pltpu.VMEM((D_EXPERT, d), BF16)]),
        out_shape=jax.ShapeDtypeStruct(xb_tiles.shape, F32),
        compiler_params=_params(),
        name="experts",
    )(block_e, n_used, xb_tiles, w_gate_up, b_gate_up.reshape(N_EXPERTS, 1, -1),
      w_down, b_down.reshape(N_EXPERTS, 1, -1))


COMBINE_BUFFERS = 3


def _combine_kernel(slot_ref, slot1_ref, slot2_ref, gate_ref, res_ref, g2_ref, b2_ref,
                    yb_ref, o_ref, buf_sc, sem, *, tm):
    i = pl.program_id(0)
    n_steps = pl.num_programs(0)
    tr = TOKEN_TILE_ROWS
    group = DMA_ROWS_PER_ITER
    cur = lax.rem(i, COMBINE_BUFFERS)
    ahead = lax.rem(i + 2, COMBINE_BUFFERS)

    def token_copy(buf, kk, r, slot):
        src = pl.multiple_of(slot * tr, tr)
        dst = pl.multiple_of(r * tr, tr)
        return pltpu.make_async_copy(yb_ref.at[pl.ds(src, tr), :],
                                     buf_sc.at[buf, kk, pl.ds(dst, tr), :], sem.at[buf])

    def issue_group(sref, buf, g):
        toks = [g * group + u for u in range(group)]
        slots = [[sref[r * TOP_K + kk] for kk in range(TOP_K)] for r in toks]
        for r, row_slots in zip(toks, slots):
            for kk, slot in enumerate(row_slots):
                token_copy(buf, kk, r, slot).start(priority=kk % DMA_PRIORITIES)

    big = group * tr

    def compute_group(gb, issue=None):
        rows = pl.ds(pl.multiple_of(gb * big, big), big)
        gate = gate_ref[rows, :]
        chunks = []
        for j in range(tr):
            res = res_ref[rows, j * LANES:(j + 1) * LANES]
            ys = [buf_sc[cur, kk, pl.ds(gb * (big * tr) + j, big, stride=tr), :]
                  for kk in range(TOP_K)]
            if issue is not None:
                issue(gb * tr + j)
            hj = res
            for kk in range(TOP_K):
                hj = hj + gate[:, kk:kk + 1] * ys[kk]
            chunks.append(hj)
        mu = sum(jnp.sum(hj, axis=1, keepdims=True) for hj in chunks) * (1.0 / D_MODEL)
        cent = [hj - mu for hj in chunks]
        var = sum(jnp.sum(cj * cj, axis=1, keepdims=True) for cj in cent) * (1.0 / D_MODEL)
        inv = lax.rsqrt(var + LN_EPS)
        for j, cj in enumerate(cent):
            cols = slice(j * LANES, (j + 1) * LANES)
            o_ref[rows, cols] = cj * inv * g2_ref[:, cols] + b2_ref[:, cols]

    def issue_only(sref, buf):
        def body(g, c):
            issue_group(sref, buf, g)
            return c
        lax.fori_loop(0, tm // group, body, 0)

    @pl.when(i == 0)
    def _():
        issue_only(slot_ref, 0)

    @pl.when(jnp.logical_and(i == 0, n_steps > 1))
    def _():
        issue_only(slot1_ref, 1)

    for kk in range(TOP_K):
        pltpu.make_async_copy(yb_ref.at[pl.ds(0, tm * tr), :], buf_sc.at[cur, kk], sem.at[cur]).wait()

    @pl.when(i + 2 < n_steps)
    def _():
        def body(gb, c):
            compute_group(gb, issue=lambda g: issue_group(slot2_ref, ahead, g))
            return c
        lax.fori_loop(0, tm // big, body, 0)

    @pl.when(i + 2 >= n_steps)
    def _():
        def body(gb, c):
            compute_group(gb)
            return c
        lax.fori_loop(0, tm // big, body, 0)


def combine(y_tiles, slot_flat, gates, res, ln2_g, ln2_b, tm=COMBINE_TM):
    t, d = res.shape
    n_steps = t // tm
    slot_spec = lambda k: pl.BlockSpec((tm * TOP_K,), lambda i: (jnp.minimum(i + k, n_steps - 1),),
                                       memory_space=pltpu.SMEM)
    return pl.pallas_call(
        functools.partial(_combine_kernel, tm=tm),
        grid=(n_steps,),
        in_specs=[slot_spec(0), slot_spec(1), slot_spec(2),
                  pl.BlockSpec((tm, LANES), lambda i: (i, 0)),
                  pl.BlockSpec((tm, d), lambda i: (i, 0)),
                  pl.BlockSpec((1, d), lambda i: (0, 0)),
                  pl.BlockSpec((1, d), lambda i: (0, 0)),
                  pl.BlockSpec(memory_space=pl.ANY)],
        out_specs=pl.BlockSpec((tm, d), lambda i: (i, 0)),
        scratch_shapes=[pltpu.VMEM((COMBINE_BUFFERS, TOP_K, tm * TOKEN_TILE_ROWS, LANES), F32),
                        pltpu.SemaphoreType.DMA((COMBINE_BUFFERS,))],
        out_shape=jax.ShapeDtypeStruct((t, d), F32),
        compiler_params=_params(),
        name="combine",
    )(slot_flat, slot_flat, slot_flat, gates, res, ln2_g.reshape(1, d), ln2_b.reshape(1, d),
      y_tiles)


def kernel(x, p, w_in, conv_w, a_log, dt_bias, gnorm_w, w_out, ln1_g, ln1_b, w_router, b_router,
           w_gate_up, b_gate_up, w_down, b_down, w_ple_gate, b_ple_gate, w_ple_proj, ln2_g, ln2_b):
    bsz, seq, d = x.shape
    t = bsz * seq
    x2d = x.reshape(t, d)
    qkv_a, qkv_b, z_b, scal, scal_t = in_proj(x2d, w_in[0])
    ya = dilated_attn(qkv_a, bsz, seq)
    yb = deltanet(qkv_b, z_b, scal, scal_t, conv_w[0], a_log[0], dt_bias[0], gnorm_w[0], bsz, seq)
    x1, res, route, gates, counts = mix_route(
        ya, yb, x2d, p[0].reshape(t, PLE_DIM), w_out[0], ln1_g[0], ln1_b[0], w_router[0],
        b_router[0], w_ple_gate[0], b_ple_gate[0], w_ple_proj[0])

    cnt = counts[0, :N_EXPERTS].astype(jnp.int32)
    padded = (cnt + MOE_TILE - 1) // MOE_TILE * MOE_TILE
    pad_end = jnp.cumsum(padded).astype(jnp.int32)
    pad_start = pad_end - padded
    n_blocks = (t * TOP_K) // MOE_TILE + N_EXPERTS
    n_used = (pad_end[-1:] // MOE_TILE).astype(jnp.int32)
    tile_start = jnp.arange(n_blocks, dtype=jnp.int32) * MOE_TILE
    block_e = jnp.minimum(jnp.sum(pad_end[None, :] <= tile_start[:, None], axis=1),
                          N_EXPERTS - 1).astype(jnp.int32)
    experts_iota = jnp.arange(N_EXPERTS, dtype=jnp.int32)
    group_off = jnp.sum(jnp.where(route[:, :TOP_K, None] == experts_iota, pad_start, 0), axis=-1)
    slot_flat = (group_off + route[:, TOP_K:2 * TOP_K]).reshape(-1).astype(jnp.int32)

    xb = dispatch(x1, slot_flat, pad_start, pad_end, n_blocks * MOE_TILE)
    yexp = experts(xb, block_e, n_used, w_gate_up[0], b_gate_up[0], w_down[0], b_down[0])
    out = combine(yexp, slot_flat, gates, res, ln2_g[0], ln2_b[0])
    return out.reshape(bsz, seq, d)
```

```python
import functools
import math

import jax
import jax.numpy as jnp
from jax import lax
from jax.experimental import pallas as pl
from jax.experimental.pallas import tpu as pltpu

LANES = 128
VMEM_LIMIT_BYTES = 56 * 1024 * 1024

D_MODEL = 1024
PLE_DIM = 256
A_HEADS = 8
A_HEAD_DIM = 64
A_WIDTH = A_HEADS * A_HEAD_DIM
DILATIONS = (16, 4, 1)
ATT_BLOCK = 128
B_HEADS = 4
B_HEAD_DIM = 128
B_WIDTH = B_HEADS * B_HEAD_DIM
CONV_WIDTH = 4
DELTA_CHUNK = 64
N_EXPERTS = 32
TOP_K = 4
D_EXPERT = D_MODEL
SWIGLU_LIMIT = 7.0
SWIGLU_ALPHA = 1.702
MOE_BLOCK = 128
LN_EPS = 1e-5
RMS_EPS = 1e-6
L2_EPS = 1e-6
DEPTH = 1
DEEPNORM_ALPHA = (2.0 * DEPTH) ** 0.25

F32 = jnp.float32
BF16 = jnp.bfloat16


def _params(n_parallel_axes=1):
    return pltpu.CompilerParams(
        dimension_semantics=("arbitrary",) * n_parallel_axes,
        vmem_limit_bytes=VMEM_LIMIT_BYTES)


def _in_proj_kernel(x_ref, wa_ref, wb_ref, wz_ref, ws_ref, wst_ref, a_ref, b_ref, z_ref, s_ref, st_ref):
    xb = x_ref[...].astype(BF16)
    a_ref[...] = jnp.dot(xb, wa_ref[...], preferred_element_type=F32)
    b_ref[...] = jnp.dot(xb, wb_ref[...], preferred_element_type=F32).astype(BF16)
    z_ref[...] = jnp.dot(xb, wz_ref[...], preferred_element_type=F32)
    s_ref[...] = jnp.dot(xb, ws_ref[...], preferred_element_type=F32)
    st_ref[...] = lax.dot_general(wst_ref[...], xb, (((1,), (1,)), ((), ())),
                                  preferred_element_type=F32)


def in_proj(x2d, w_in, tm=512):
    t, d = x2d.shape
    na, nb = 3 * A_WIDTH, 3 * B_WIDTH
    wa = w_in[:, :na].astype(BF16)
    wb = w_in[:, na:na + nb].astype(BF16)
    wz = w_in[:, na + nb:na + nb + B_WIDTH].astype(BF16)
    ws = jnp.pad(w_in[:, na + nb + B_WIDTH:], ((0, 0), (0, LANES - 2 * B_HEADS))).astype(BF16)
    wst = ws[:, :16].T
    full = lambda a: pl.BlockSpec(a.shape, lambda i: (0, 0))
    return pl.pallas_call(
        _in_proj_kernel,
        grid=(t // tm,),
        in_specs=[pl.BlockSpec((tm, d), lambda i: (i, 0)),
                  full(wa), full(wb), full(wz), full(ws), full(wst)],
        out_specs=[pl.BlockSpec((tm, na), lambda i: (i, 0)),
                   pl.BlockSpec((tm, nb), lambda i: (i, 0)),
                   pl.BlockSpec((tm, B_WIDTH), lambda i: (i, 0)),
                   pl.BlockSpec((tm, LANES), lambda i: (i, 0)),
                   pl.BlockSpec((16, tm), lambda i: (0, i))],
        out_shape=[jax.ShapeDtypeStruct((t, na), F32),
                   jax.ShapeDtypeStruct((t, nb), BF16),
                   jax.ShapeDtypeStruct((t, B_WIDTH), F32),
                   jax.ShapeDtypeStruct((t, LANES), F32),
                   jax.ShapeDtypeStruct((16, t), F32)],
        compiler_params=_params(),
        name="in_proj",
    )(x2d, wa, wb, wz, ws, wst)


ATTN_UNROLL_DENSE = 5
ATTN_UNROLL_SINGLE = 8


def _attn_kernel(q_ref, k_ref, v_ref, o_ref, m_sc, l_sc, acc_sc, *, seq):
    blk = ATT_BLOCK
    lane = lax.broadcasted_iota(jnp.int32, (blk, LANES), 1)
    head0 = lane < A_HEAD_DIM
    row = lax.broadcasted_iota(jnp.int32, (blk, blk), 0)
    col = lax.broadcasted_iota(jnp.int32, (blk, blk), 1)
    prev_ok = col >= row
    cur_ok = col <= row
    neg_inf = jnp.float32(-jnp.inf)
    scale = 1.0 / math.sqrt(A_HEAD_DIM)

    def blocks(starts, stride, has_prev, first_branch, last_branch):
        def rows(ref, s0):
            if stride == 1:
                return ref[pl.ds(s0, blk), :]
            return ref[pl.ds(s0, blk, stride=stride), :]

        def put(ref, s0, val):
            if stride == 1:
                ref[pl.ds(s0, blk), :] = val
            else:
                ref[pl.ds(s0, blk, stride=stride), :] = val

        hmask = (head0, jnp.logical_not(head0))
        items = []
        for start, hp in zip(starts, has_prev):
            q = rows(q_ref, start) * scale
            keys = rows(k_ref, start)
            vals = rows(v_ref, start)
            if hp:
                keys = jnp.concatenate([rows(k_ref, start - blk * stride), keys], axis=0)
                vals = jnp.concatenate([rows(v_ref, start - blk * stride), vals], axis=0)
            it = dict(start=start, hp=hp, keys=keys.astype(BF16), vals=vals,
                      qh=[jnp.where(hm, q, 0.0).astype(BF16) for hm in hmask])
            if not first_branch:
                it["m_old"] = rows(m_sc, start)
                it["l_old"] = rows(l_sc, start)
                it["acc_old"] = rows(acc_sc, start)
            items.append(it)

        for it in items:
            it["s"] = [lax.dot_general(qh, it["keys"], (((1,), (1,)), ((), ())),
                                       preferred_element_type=F32) for qh in it["qh"]]
        for it in items:
            ok = jnp.concatenate([prev_ok, cur_ok], axis=1) if it["hp"] else cur_ok
            it["m"], it["sum"], it["p"] = [], [], []
            for h in range(2):
                s = jnp.where(ok, it["s"][h], neg_inf)
                t = jnp.maximum(s[:, :blk], s[:, blk:]) if it["hp"] else s
                if not first_branch:
                    t = jnp.maximum(t, jnp.where(hmask[h], it["m_old"], neg_inf))
                m_h = jnp.max(t, axis=1, keepdims=True)
                p = jnp.exp(s - m_h)
                psum = p[:, :blk] + p[:, blk:] if it["hp"] else p
                it["m"].append(m_h)
                it["sum"].append(jnp.sum(psum, axis=1, keepdims=True))
                it["p"].append(p.astype(BF16))
        for it in items:
            v16 = it["vals"].astype(BF16)
            it["pv"] = [jnp.dot(it["p"][h], v16, preferred_element_type=F32) for h in range(2)]
        for it in items:
            m_new = jnp.where(head0, it["m"][0], it["m"][1])
            l_new = jnp.where(head0, it["sum"][0], it["sum"][1])
            acc_new = jnp.where(head0, it["pv"][0], it["pv"][1])
            if not first_branch:
                alpha = jnp.exp(it["m_old"] - m_new)
                l_new = l_new + alpha * it["l_old"]
                acc_new = acc_new + alpha * it["acc_old"]
            if last_branch:
                put(o_ref, it["start"], acc_new / l_new)
            else:
                put(m_sc, it["start"], m_new)
                put(l_sc, it["start"], l_new)
                put(acc_sc, it["start"], acc_new)

    n_br = len(DILATIONS)
    for bi, dil in enumerate(DILATIONS):
        first, last = bi == 0, bi == n_br - 1
        nblk = seq // dil // blk
        if dil == 1:
            blocks([0], 1, [False], first, last)
            per = ATTN_UNROLL_DENSE
            assert (nblk - 1) % per == 0

            def body(g, c, first=first, last=last, per=per):
                starts = [pl.multiple_of((1 + g * per + j) * blk, blk) for j in range(per)]
                blocks(starts, 1, [True] * per, first, last)
                return c
            lax.fori_loop(0, (nblk - 1) // per, body, 0)
        elif nblk > 1:
            def body(r, c, dil=dil, nblk=nblk, first=first, last=last):
                blocks([r + n * blk * dil for n in range(nblk)], dil,
                       [n > 0 for n in range(nblk)], first, last)
                return c
            lax.fori_loop(0, dil, body, 0)
        else:
            per = ATTN_UNROLL_SINGLE
            assert dil % per == 0

            def body(g, c, dil=dil, first=first, last=last, per=per):
                blocks([g * per + j for j in range(per)], dil, [False] * per, first, last)
                return c
            lax.fori_loop(0, dil // per, body, 0)


def dilated_attn(qkv_a, bsz, seq):
    t = bsz * seq
    n_pairs = A_WIDTH // LANES
    blk = lambda off: pl.BlockSpec((seq, LANES), lambda b, p: (b, off + p))
    return pl.pallas_call(
        functools.partial(_attn_kernel, seq=seq),
        grid=(bsz, n_pairs),
        in_specs=[blk(0), blk(n_pairs), blk(2 * n_pairs)],
        out_specs=pl.BlockSpec((seq, LANES), lambda b, p: (b, p)),
        out_shape=jax.ShapeDtypeStruct((t, A_WIDTH), F32),
        scratch_shapes=[pltpu.VMEM((seq, LANES), F32)] * 3,
        compiler_params=_params(2),
        name="dilated_attn",
    )(qkv_a, qkv_a, qkv_a)


DELTA_HEADS_PER_STEP = 4
DELTA_P1_CHUNKS = 4
CONV_TILE = 256
CONV_HISTORY_ROWS = 16


def _dot16(a, b):
    return jnp.dot(a, b, preferred_element_type=F32)


def _split2(x):
    hi = x.astype(BF16)
    return hi, (x - hi.astype(F32)).astype(BF16)


def _split3(x):
    hi = x.astype(BF16)
    r = x - hi.astype(F32)
    mid = r.astype(BF16)
    return hi, mid, (r - mid.astype(F32)).astype(BF16)


def _dot_sel(m01, x):
    hi, mid, lo = _split3(x)
    return _dot16(m01, hi) + (_dot16(m01, mid) + _dot16(m01, lo))


def _dot_sel_r(x, m01):
    hi, mid, lo = _split3(x)
    return _dot16(hi, m01) + (_dot16(mid, m01) + _dot16(lo, m01))


def _stack_parts(x, lo_half, want_l=True, want_r=True):
    xh = x.astype(BF16)
    xh_f = xh.astype(F32)
    xl_f = x - xh_f
    left = right = None
    if want_l:
        mix = jnp.where(lo_half, xh_f, xl_f).astype(BF16)
        left = jnp.concatenate([mix, mix], axis=1)
    if want_r:
        xl = xl_f.astype(BF16)
        right = jnp.concatenate([xh, xh, xl, xl], axis=0)
    return left, right


def _softplus(x):
    return jnp.maximum(x, 0.0) + jnp.log(1.0 + jnp.exp(-jnp.abs(x)))


def _sigmoid(x):
    return 1.0 / (1.0 + jnp.exp(-x))


def _delta_kernel(q_ref, k_ref, v_ref, z_ref, sc_ref, st_ref, par_ref, part_ref,
                  wq_ref, wk_ref, wv_ref, nw_ref, o_ref,
                  conv_sc, qn_sc, kn_sc, vn_sc, w_sc, attn_sc, kdt_sc, gcr_sc, egl_sc,
                  state_sc, *, seq, hps):
    c = DELTA_CHUNK
    dk = B_HEAD_DIM
    n_chunks = seq // c
    tile = DELTA_P1_CHUNKS * c
    group = pl.program_id(1)
    lane = lax.broadcasted_iota(jnp.int32, (tile, LANES), 1)
    r2 = lax.broadcasted_iota(jnp.int32, (tile, tile), 0)
    c2 = lax.broadcasted_iota(jnp.int32, (tile, tile), 1)
    same_chunk = (r2 // c) == (c2 // c)
    cum_mat = jnp.where(same_chunk & (c2 <= r2), 1.0, 0.0).astype(BF16)
    tot_mat = jnp.where(same_chunk, 1.0, 0.0).astype(BF16)
    ri = lax.broadcasted_iota(jnp.int32, (c, 2 * c), 0)
    li = lax.broadcasted_iota(jnp.int32, (c, 2 * c), 1)
    lo_half = li < c
    ci = jnp.where(lo_half, li, li - c)
    causal = ci <= ri
    strict = ci < ri
    eye = jnp.where(ci == ri, 1.0, 0.0).astype(F32)
    upper = jnp.where(ri <= ci, 1.0, 0.0).astype(BF16)

    hist = CONV_HISTORY_ROWS
    for hh in range(hps):
        cols = slice(hh * dk, (hh + 1) * dk)
        streams = ((q_ref, wq_ref, qn_sc, "q"), (k_ref, wk_ref, kn_sc, "k"), (v_ref, wv_ref, vn_sc, "v"))

        def conv_tile(si, xe, c0, hh=hh, cols=cols):
            _, w_ref, dst, kind = streams[si]
            w = w_ref[:, cols]
            conv_sc[si] = xe
            off = hist - (CONV_WIDTH - 1)
            y = w[0:1, :] * conv_sc[si, off:off + CONV_TILE, :]
            for j in range(1, CONV_WIDTH):
                y = y + w[j:j + 1, :] * conv_sc[si, off + j:off + j + CONV_TILE, :]
            y = y * _sigmoid(y)
            if kind != "v":
                y = y * lax.rsqrt(jnp.sum(y * y, axis=1, keepdims=True) + L2_EPS)
            if kind == "q":
                y = y * (B_HEAD_DIM ** -0.5)
            dst[hh, pl.ds(c0, CONV_TILE), :] = y

        for si, (src, _, _, _) in enumerate(streams):
            conv_tile(si, jnp.concatenate([jnp.zeros((hist, dk), F32),
                                           src[0:CONV_TILE, cols].astype(F32)], axis=0), 0)

        def conv_body(ti, carry, cols=cols, conv_tile=conv_tile):
            c0 = pl.multiple_of(ti * CONV_TILE, CONV_TILE)
            for si, (src, _, _, _) in enumerate(streams):
                conv_tile(si, src[pl.ds(c0 - hist, hist + CONV_TILE), cols].astype(F32), c0)
            return carry

        lax.fori_loop(1, seq // CONV_TILE, conv_body, 0)

    for hh in range(hps):
        h = group * hps + hh
        a_row = st_ref[B_HEADS + h]
        g_row = -jnp.exp(part_ref[0, h]) * _softplus(a_row + part_ref[1, h])
        gcr_sc[hh] = _dot_sel_r(g_row, upper)

    alog_l = par_ref[0:1, :]
    dtb_l = par_ref[1:2, :]

    def aligned(x, m):
        return x if isinstance(x, int) else pl.multiple_of(x, m)

    def phase1(i):
        r0 = aligned(i * tile, tile)
        rows = pl.ds(r0, tile)
        s = sc_ref[rows, :]
        beta_all = _sigmoid(s)
        g_all = -jnp.exp(alog_l) * _softplus(s + dtb_l)
        gc_all = _dot_sel(cum_mat, g_all)
        gl_all = _dot_sel(tot_mat, g_all)
        chains = []
        for hh in range(hps):
            h = group * hps + hh
            pick = lambda a, l: jnp.sum(jnp.where(lane == l, a, 0.0), axis=1, keepdims=True)
            beta = pick(beta_all, h)
            gc = pick(gc_all, B_HEADS + h)
            gl = pick(gl_all, B_HEADS + h)
            egc = jnp.exp(gc)
            q = qn_sc[hh, rows, :]
            k = kn_sc[hh, rows, :]
            v = vn_sc[hh, rows, :]
            kbeta = k * beta
            vbeta = v * beta
            qn_sc[hh, rows, :] = q * egc
            kdec = k * jnp.exp(gl - gc)
            kb16 = k.astype(BF16)
            for cc in range(DELTA_P1_CHUNKS):
                sl = slice(cc * c, (cc + 1) * c)
                n = DELTA_P1_CHUNKS * i + cc
                kdt_sc[hh, n] = kdec[sl].T.astype(BF16)
                egl_sc[hh, pl.ds(n, 1), :] = jnp.broadcast_to(jnp.exp(gl[cc * c:cc * c + 1]), (1, dk))
                diff = gc[sl] - gcr_sc[hh, pl.ds(n, 1), :]
                e = jnp.exp(jnp.where(causal, diff, 0.0))
                k2 = jnp.concatenate([kb16[sl], kb16[sl]], axis=0)
                lhs = jnp.concatenate([kbeta[sl], q[sl]], axis=0).astype(BF16)
                rhs = jnp.concatenate([vbeta[sl], kbeta[sl] * egc[sl]], axis=1)
                chains.append(dict(hh=hh, row=r0 + cc * c, e=e, k2=k2, lhs=lhs, rhs=rhs))
        yield

        for ch in chains:
            kq = lax.dot_general(ch["lhs"], ch["k2"], (((1,), (1,)), ((), ())),
                                 preferred_element_type=F32)
            ch["a"] = jnp.where(strict, kq[:c] * ch["e"], 0.0)
            attn_sc[ch["hh"], pl.ds(ch["row"], c), :] = (
                jnp.where(causal, kq[c:] * ch["e"], 0.0)[:, :c].astype(BF16))
        yield
        for ch in chains:
            ch["inv"] = eye - ch["a"]
            p_l, p_r = _stack_parts(ch["a"], lo_half)
            ch["p"] = _dot16(p_l, p_r)
        yield
        for _ in range(4):
            for ch in chains:
                p_l, p_r = _stack_parts(ch["p"], lo_half)
                inv_l, _ = _stack_parts(ch["inv"], lo_half, want_r=False)
                res = _dot16(jnp.concatenate([p_l, inv_l], axis=0), p_r)
                ch["p"] = res[:c]
                ch["inv"] = ch["inv"] + res[c:]
            yield
        for ch in chains:
            inv_l, _ = _stack_parts(ch["inv"], lo_half, want_r=False)
            _, p_r = _stack_parts(ch["p"], lo_half, want_l=False)
            ch["inv"] = ch["inv"] + _dot16(inv_l, p_r)
        yield
        for ch in chains:
            inv_l, _ = _stack_parts(ch["inv"], lo_half, want_r=False)
            _, rhs_r = _stack_parts(ch["rhs"], None, want_l=False)
            sol = _dot16(inv_l, rhs_r)
            vn_sc[ch["hh"], pl.ds(ch["row"], c), :] = sol[:, :dk]
            w_sc[ch["hh"], pl.ds(ch["row"], c), :] = sol[:, dk:].astype(BF16)

    nw = nw_ref[...]

    def phase2(i):
        for cc in range(DELTA_P1_CHUNKS):
            n = DELTA_P1_CHUNKS * i + cc
            r0 = aligned(n * c, c)
            rows = pl.ds(r0, c)
            heads = range(hps)
            st = [state_sc[hh] for hh in heads]
            st16 = [s_.astype(BF16) for s_ in st]
            lhs = [jnp.concatenate([w_sc[hh, rows, :], qn_sc[hh, rows, :].astype(BF16)], axis=0)
                   for hh in heads]
            ws = [_dot16(lhs[hh], st16[hh]) for hh in heads]
            yield
            vn16 = [(vn_sc[hh, rows, :] - ws[hh][:c]).astype(BF16) for hh in heads]
            av = [_dot16(attn_sc[hh, rows, :], vn16[hh]) for hh in heads]
            kv = [_dot16(kdt_sc[hh, n], vn16[hh]) for hh in heads]
            for hh in heads:
                state_sc[hh] = st[hh] * egl_sc[hh, pl.ds(n, 1), :] + kv[hh]
                out = ws[hh][c:] + av[hh]
                o = out * lax.rsqrt(jnp.mean(out * out, axis=1, keepdims=True) + RMS_EPS) * nw
                z = z_ref[rows, hh * dk:(hh + 1) * dk]
                o_ref[rows, hh * dk:(hh + 1) * dk] = (o * (z * _sigmoid(z))).astype(o_ref.dtype)
            yield

    def interleave(*gens):
        live = list(gens)
        while live:
            for g in list(live):
                try:
                    next(g)
                except StopIteration:
                    live.remove(g)

    n_tiles = n_chunks // DELTA_P1_CHUNKS
    state_sc[...] = jnp.zeros(state_sc.shape, F32)
    interleave(phase1(0))

    def pipelined(i, carry):
        interleave(phase1(i), phase2(i - 1))
        return carry

    lax.fori_loop(1, n_tiles, pipelined, 0)
    interleave(phase2(n_tiles - 1))


def deltanet(qkv_b, z_b, scal, scal_t, conv_w, a_log, dt_bias, gnorm_w, bsz, seq):
    t = bsz * seq
    hps = DELTA_HEADS_PER_STEP
    n_groups = B_HEADS // hps
    wg = hps * B_HEAD_DIM
    c = DELTA_CHUNK
    n_chunks = seq // c
    st3 = scal_t.reshape(16, t // c, c)
    par = jnp.zeros((8, LANES), F32)
    par = par.at[0, B_HEADS:2 * B_HEADS].set(a_log).at[1, B_HEADS:2 * B_HEADS].set(dt_bias)
    part = jnp.stack([a_log, dt_bias]).astype(F32)
    blk = lambda off: pl.BlockSpec((seq, wg), lambda b, g: (b, off + g))
    wblk = lambda off: pl.BlockSpec((CONV_WIDTH, wg), lambda b, g: (0, off + g))
    sc = lambda shape, dt: pltpu.VMEM(shape, dt)
    return pl.pallas_call(
        functools.partial(_delta_kernel, seq=seq, hps=hps),
        grid=(bsz, n_groups),
        in_specs=[blk(0), blk(n_groups), blk(2 * n_groups), blk(0),
                  pl.BlockSpec((seq, LANES), lambda b, g: (b, 0)),
                  pl.BlockSpec((16, n_chunks, c), lambda b, g: (0, b, 0)),
                  pl.BlockSpec((8, LANES), lambda b, g: (0, 0)),
                  pl.BlockSpec(memory_space=pltpu.SMEM),
                  wblk(0), wblk(n_groups), wblk(2 * n_groups),
                  pl.BlockSpec((1, B_HEAD_DIM), lambda b, g: (0, 0))],
        out_specs=pl.BlockSpec((seq, wg), lambda b, g: (b, g)),
        out_shape=jax.ShapeDtypeStruct((t, B_WIDTH), BF16),
        scratch_shapes=[sc((3, CONV_HISTORY_ROWS + CONV_TILE, B_HEAD_DIM), F32),
                        sc((hps, seq, B_HEAD_DIM), F32), sc((hps, seq, B_HEAD_DIM), F32),
                        sc((hps, seq, B_HEAD_DIM), F32),
                        sc((hps, seq, B_HEAD_DIM), BF16),
                        sc((hps, seq, c), BF16), sc((hps, n_chunks, B_HEAD_DIM, c), BF16),
                        sc((hps, n_chunks, 2 * c), F32), sc((hps, n_chunks, B_HEAD_DIM), F32),
                        sc((hps, B_HEAD_DIM, B_HEAD_DIM), F32)],
        compiler_params=_params(2),
        name="deltanet",
    )(qkv_b, qkv_b, qkv_b, z_b, scal, st3, par, part,
      conv_w, conv_w, conv_w, gnorm_w.reshape(1, B_HEAD_DIM))


SUBLANES = 8
TOKEN_TILE_ROWS = D_MODEL // LANES
assert TOKEN_TILE_ROWS == SUBLANES


def _store_token_tiles(ref, x, n_tok):
    for j in range(TOKEN_TILE_ROWS):
        ref[pl.ds(j, n_tok, stride=TOKEN_TILE_ROWS), :] = x[:, j * LANES:(j + 1) * LANES]


def _layer_norm(h, g, b):
    mu = jnp.mean(h, axis=1, keepdims=True)
    hc = h - mu
    var = jnp.mean(hc * hc, axis=1, keepdims=True)
    return hc * lax.rsqrt(var + LN_EPS) * g + b


def _mix_route_kernel(ya_ref, yb_ref, x_ref, p_ref, woa_ref, wob_ref, g1_ref, b1_ref,
                      wr_ref, br_ref, wpg_ref, bpg_ref, wpp_ref,
                      x1_ref, res_ref, route_ref, gate_ref, cnt_ref, run_sc, wr2_sc, *, tm):
    i = pl.program_id(0)

    @pl.when(i == 0)
    def _():
        run_sc[...] = jnp.zeros(run_sc.shape, F32)
        w_hi, w_lo = _split2(wr_ref[...])
        wr2_sc[:, :LANES] = w_hi
        wr2_sc[:, LANES:] = w_lo

    mix = (jnp.dot(ya_ref[...].astype(BF16), woa_ref[...], preferred_element_type=F32)
           + jnp.dot(yb_ref[...].astype(BF16), wob_ref[...], preferred_element_type=F32))
    x1 = _layer_norm(DEEPNORM_ALPHA * x_ref[...] + mix, g1_ref[...], b1_ref[...])
    _store_token_tiles(x1_ref, x1, tm)
    x1b = x1.astype(BF16)

    x1l = (x1 - x1b.astype(F32)).astype(BF16)
    r_hi = jnp.dot(x1b, wr2_sc[...], preferred_element_type=F32)
    r_lo = jnp.dot(x1l, wr2_sc[:, :LANES], preferred_element_type=F32)
    logits = r_hi[:, :LANES] + (r_hi[:, LANES:] + r_lo) + br_ref[...]

    pgate = _sigmoid(jnp.dot(x1b, wpg_ref[...], preferred_element_type=F32) + bpg_ref[...])
    proj = jnp.dot(p_ref[...].astype(BF16), wpp_ref[...], preferred_element_type=F32)
    res_ref[...] = DEEPNORM_ALPHA * x1 + pgate * proj

    lane = lax.broadcasted_iota(jnp.int32, (tm, LANES), 1)
    cur = jnp.where(lane < N_EXPERTS, logits, -jnp.inf)
    vals, hots = [], []
    for _ in range(TOP_K):
        m = jnp.max(cur, axis=1, keepdims=True)
        idx = jnp.min(jnp.where(cur == m, lane, LANES), axis=1, keepdims=True)
        hot = lane == idx
        cur = jnp.where(hot, -jnp.inf, cur)
        vals.append(m)
        hots.append((hot, idx))
    exps = [jnp.exp(v - vals[0]) for v in vals]
    den = exps[0] + exps[1] + exps[2] + exps[3]

    member = jnp.zeros((tm, LANES), F32)
    for hot, _ in hots:
        member = member + jnp.where(hot, 1.0, 0.0)
    rr = lax.broadcasted_iota(jnp.int32, (tm, tm), 0)
    cc = lax.broadcasted_iota(jnp.int32, (tm, tm), 1)
    before = jnp.where(cc < rr, 1.0, 0.0).astype(BF16)
    prior = jnp.dot(before, member.astype(BF16), preferred_element_type=F32) + run_sc[0:1, :]
    route = jnp.zeros((tm, LANES), jnp.int32)
    gate = jnp.zeros((tm, LANES), F32)
    for kk, (hot, idx) in enumerate(hots):
        rank = jnp.sum(jnp.where(hot, prior, 0.0), axis=1, keepdims=True).astype(jnp.int32)
        route = jnp.where(lane == kk, idx, route)
        route = jnp.where(lane == TOP_K + kk, rank, route)
        gate = jnp.where(lane == kk, exps[kk] / den, gate)
    route_ref[...] = route
    gate_ref[...] = gate
    run_sc[...] = run_sc[...] + jnp.sum(member, axis=0, keepdims=True)
    cnt_ref[...] = run_sc[...]


def mix_route(ya, yb, x2d, p2d, w_out, ln1_g, ln1_b, w_router, b_router, w_ple_gate, b_ple_gate,
              w_ple_proj, tm=512):
    t, d = x2d.shape
    woa = w_out[:A_WIDTH].astype(BF16)
    wob = w_out[A_WIDTH:].astype(BF16)
    wr = jnp.pad(w_router, ((0, 0), (0, LANES - N_EXPERTS)))
    br = jnp.pad(b_router, (0, LANES - N_EXPERTS)).reshape(1, LANES)
    row = lambda w: pl.BlockSpec((tm, w), lambda i: (i, 0))
    full = lambda a: pl.BlockSpec(a.shape, lambda i: (0,) * a.ndim)
    tiles = pl.BlockSpec((tm * TOKEN_TILE_ROWS, LANES), lambda i: (i, 0))
    ops = [woa, wob, ln1_g.reshape(1, d), ln1_b.reshape(1, d), wr, br,
           w_ple_gate.astype(BF16), b_ple_gate.reshape(1, d), w_ple_proj.astype(BF16)]
    return pl.pallas_call(
        functools.partial(_mix_route_kernel, tm=tm),
        grid=(t // tm,),
        in_specs=[row(A_WIDTH), row(B_WIDTH), row(d), row(PLE_DIM)] + [full(a) for a in ops],
        out_specs=[tiles, row(d), row(LANES), row(LANES),
                   pl.BlockSpec((8, LANES), lambda i: (0, 0))],
        out_shape=[jax.ShapeDtypeStruct((t * TOKEN_TILE_ROWS, LANES), F32),
                   jax.ShapeDtypeStruct((t, d), F32),
                   jax.ShapeDtypeStruct((t, LANES), jnp.int32),
                   jax.ShapeDtypeStruct((t, LANES), F32),
                   jax.ShapeDtypeStruct((8, LANES), F32)],
        scratch_shapes=[pltpu.VMEM((8, LANES), F32), pltpu.VMEM((d, 2 * LANES), BF16)],
        compiler_params=_params(),
        name="mix_route",
    )(ya, yb, x2d, p2d, *ops)


MOE_TILE = 512
DISPATCH_TM = 2048
COMBINE_TM = 512
DMA_ROWS_PER_ITER = 8
DMA_PRIORITIES = 2


def _dispatch_kernel(pstart_ref, pend_ref, slot_ref, x_ref, xb_ref, zero_sc, sem, zsem, *, tm):
    i = pl.program_id(0)
    tr = TOKEN_TILE_ROWS
    tile_rows = MOE_TILE * tr

    def zero_copy(e):
        off = pl.multiple_of((pend_ref[e] - MOE_TILE) * tr, tile_rows)
        return pltpu.make_async_copy(zero_sc, xb_ref.at[pl.ds(off, tile_rows), :], zsem)

    @pl.when(i == 0)
    def _():
        zero_sc[...] = jnp.zeros(zero_sc.shape, zero_sc.dtype)
        for e in range(N_EXPERTS):
            @pl.when(pend_ref[e] > pstart_ref[e])
            def _():
                zero_copy(e).start()
        for e in range(N_EXPERTS):
            @pl.when(pend_ref[e] > pstart_ref[e])
            def _():
                zero_copy(e).wait()

        n_tiles = xb_ref.shape[0] // tile_rows
        first_unused = pend_ref[N_EXPERTS - 1] // MOE_TILE

        def tail_copy(j):
            off = pl.multiple_of(j * tile_rows, tile_rows)
            return pltpu.make_async_copy(zero_sc, xb_ref.at[pl.ds(off, tile_rows), :], zsem)

        def tail_start(j, c):
            tail_copy(j).start()
            return c

        def tail_wait(j, c):
            tail_copy(j).wait()
            return c

        lax.fori_loop(first_unused, n_tiles, tail_start, 0)
        lax.fori_loop(first_unused, n_tiles, tail_wait, 0)

    def token_copy(tok, slot):
        src = pl.multiple_of(tok * tr, tr)
        dst = pl.multiple_of(slot * tr, tr)
        return pltpu.make_async_copy(x_ref.at[pl.ds(src, tr), :], xb_ref.at[pl.ds(dst, tr), :], sem)

    def start(g, c):
        toks = [g * DMA_ROWS_PER_ITER + u for u in range(DMA_ROWS_PER_ITER)]
        slots = [[slot_ref[r * TOP_K + kk] for kk in range(TOP_K)] for r in toks]
        for r, row_slots in zip(toks, slots):
            for kk, slot in enumerate(row_slots):
                token_copy(r, slot).start(priority=kk % DMA_PRIORITIES)
        return c

    lax.fori_loop(0, tm // DMA_ROWS_PER_ITER, start, 0)
    for _ in range(TOP_K):
        pltpu.make_async_copy(x_ref, xb_ref.at[pl.ds(0, tm * tr), :], sem).wait()


def dispatch(x1_tiles, slot_flat, pad_start, pad_end, n_slots, tm=DISPATCH_TM):
    t = x1_tiles.shape[0] // TOKEN_TILE_ROWS
    return pl.pallas_call(
        functools.partial(_dispatch_kernel, tm=tm),
        grid_spec=pltpu.PrefetchScalarGridSpec(
            num_scalar_prefetch=2,
            grid=(t // tm,),
            in_specs=[pl.BlockSpec((tm * TOP_K,), lambda i, ps, pe: (i,), memory_space=pltpu.SMEM),
                      pl.BlockSpec((tm * TOKEN_TILE_ROWS, LANES), lambda i, ps, pe: (i, 0))],
            out_specs=pl.BlockSpec(memory_space=pl.ANY),
            scratch_shapes=[pltpu.VMEM((MOE_TILE * TOKEN_TILE_ROWS, LANES), F32),
                            pltpu.SemaphoreType.DMA(()), pltpu.SemaphoreType.DMA(())]),
        out_shape=jax.ShapeDtypeStruct((n_slots * TOKEN_TILE_ROWS, LANES), F32),
        compiler_params=_params(),
        name="dispatch",
    )(pad_start, pad_end, slot_flat, x1_tiles)


EXPERT_CAST_ROWS = 256
EXPERT_COL_CHUNK = 256


def _expert_kernel(be_ref, nu_ref, x_ref, wgu_ref, bgu_ref, wd_ref, bd_ref, y_ref, wgu_sc, wd_sc):
    i = pl.program_id(0)
    used = i < nu_ref[0]
    new_expert = jnp.logical_or(i == 0, be_ref[i] != be_ref[jnp.maximum(i - 1, 0)])

    @pl.when(jnp.logical_and(used, new_expert))
    def _():
        def cast(j, c):
            r = pl.ds(pl.multiple_of(j * EXPERT_CAST_ROWS, EXPERT_CAST_ROWS), EXPERT_CAST_ROWS)
            wgu_sc[r, :] = wgu_ref[0, r, :].astype(BF16)
            wd_sc[r, :] = wd_ref[0, r, :].astype(BF16)
            return c
        lax.fori_loop(0, D_MODEL // EXPERT_CAST_ROWS, cast, 0)

    @pl.when(used)
    def _():
        tr = TOKEN_TILE_ROWS
        nchunk = EXPERT_COL_CHUNK
        xc = [x_ref[pl.ds(j, MOE_TILE, stride=tr), :].astype(BF16) for j in range(tr)]
        xb = jnp.concatenate(xc, axis=1)
        acts = []
        for c0 in range(0, D_EXPERT, nchunk):
            if c0 == 0:
                half = D_MODEL // 2
                xs = (jnp.concatenate(xc[:tr // 2], axis=1), jnp.concatenate(xc[tr // 2:], axis=1))
                g = sum(jnp.dot(xs[h], wgu_sc[h * half:(h + 1) * half, c0:c0 + nchunk],
                                preferred_element_type=F32) for h in range(2))
                u = sum(jnp.dot(xs[h], wgu_sc[h * half:(h + 1) * half,
                                              D_EXPERT + c0:D_EXPERT + c0 + nchunk],
                                preferred_element_type=F32) for h in range(2))
            else:
                g = jnp.dot(xb, wgu_sc[:, c0:c0 + nchunk], preferred_element_type=F32)
                u = jnp.dot(xb, wgu_sc[:, D_EXPERT + c0:D_EXPERT + c0 + nchunk],
                            preferred_element_type=F32)
            g = g + bgu_ref[0, :, c0:c0 + nchunk]
            u = u + bgu_ref[0, :, D_EXPERT + c0:D_EXPERT + c0 + nchunk]
            g = jnp.minimum(g, SWIGLU_LIMIT)
            u = jnp.clip(u, -SWIGLU_LIMIT, SWIGLU_LIMIT)
            acts.append(((u + 1.0) * (g * _sigmoid(SWIGLU_ALPHA * g))).astype(BF16))
        act = jnp.concatenate(acts, axis=1)
        for n0 in range(0, D_MODEL, nchunk):
            yb = (jnp.dot(act, wd_sc[:, n0:n0 + nchunk], preferred_element_type=F32)
                  + bd_ref[0, :, n0:n0 + nchunk])
            for jj in range(nchunk // LANES):
                j = n0 // LANES + jj
                y_ref[pl.ds(j, MOE_TILE, stride=tr), :] = yb[:, jj * LANES:(jj + 1) * LANES]

    @pl.when(jnp.logical_not(used))
    def _():
        y_ref[...] = jnp.zeros(y_ref.shape, F32)


def experts(xb_tiles, block_e, n_used, w_gate_up, b_gate_up, w_down, b_down):
    tile_rows = MOE_TILE * TOKEN_TILE_ROWS
    n_blocks = xb_tiles.shape[0] // tile_rows
    d = D_MODEL
    clamp = lambda i, be, nu: jnp.minimum(i, nu[0] - 1)
    return pl.pallas_call(
        _expert_kernel,
        grid_spec=pltpu.PrefetchScalarGridSpec(
            num_scalar_prefetch=2,
            grid=(n_blocks,),
            in_specs=[pl.BlockSpec((tile_rows, LANES), lambda i, be, nu: (clamp(i, be, nu), 0)),
                      pl.BlockSpec((1, d, 2 * D_EXPERT), lambda i, be, nu: (be[i], 0, 0)),
                      pl.BlockSpec((1, 1, 2 * D_EXPERT), lambda i, be, nu: (be[i], 0, 0)),
                      pl.BlockSpec((1, D_EXPERT, d), lambda i, be, nu: (be[i], 0, 0)),
                      pl.BlockSpec((1, 1, d), lambda i, be, nu: (be[i], 0, 0))],
            out_specs=pl.BlockSpec((tile_rows, LANES), lambda i, be, nu: (i, 0)),
            scratch_shapes=[pltpu.VMEM((d, 2 * D_EXPERT), BF16), pltpu.VMEM((D_EXPERT, d), BF16)]),
        out_shape=jax.ShapeDtypeStruct(xb_tiles.shape, F32),
        compiler_params=_params(),
        name="experts",
    )(block_e, n_used, xb_tiles, w_gate_up, b_gate_up.reshape(N_EXPERTS, 1, -1),
      w_down, b_down.reshape(N_EXPERTS, 1, -1))


COMBINE_BUFFERS = 3


def _combine_kernel(slot_ref, slot1_ref, slot2_ref, gate_ref, res_ref, g2_ref, b2_ref,
                    yb_ref, o_ref, buf_sc, sem, *, tm):
    i = pl.program_id(0)
    n_steps = pl.num_programs(0)
    tr = TOKEN_TILE_ROWS
    group = DMA_ROWS_PER_ITER
    cur = lax.rem(i, COMBINE_BUFFERS)
    ahead = lax.rem(i + 2, COMBINE_BUFFERS)

    def token_copy(buf, kk, r, slot):
        src = pl.multiple_of(slot * tr, tr)
        dst = pl.multiple_of(r * tr, tr)
        return pltpu.make_async_copy(yb_ref.at[pl.ds(src, tr), :],
                                     buf_sc.at[buf, kk, pl.ds(dst, tr), :], sem.at[buf])

    def issue_group(sref, buf, g):
        toks = [g * group + u for u in range(group)]
        slots = [[sref[r * TOP_K + kk] for kk in range(TOP_K)] for r in toks]
        for r, row_slots in zip(toks, slots):
            for kk, slot in enumerate(row_slots):
                token_copy(buf, kk, r, slot).start(priority=kk % DMA_PRIORITIES)

    big = group * tr

    def compute_group(gb, issue=None):
        rows = pl.ds(pl.multiple_of(gb * big, big), big)
        gate = gate_ref[rows, :]
        chunks = []
        for j in range(tr):
            res = res_ref[rows, j * LANES:(j + 1) * LANES]
            ys = [buf_sc[cur, kk, pl.ds(gb * (big * tr) + j, big, stride=tr), :]
                  for kk in range(TOP_K)]
            if issue is not None:
                issue(gb * tr + j)
            hj = res
            for kk in range(TOP_K):
                hj = hj + gate[:, kk:kk + 1] * ys[kk]
            chunks.append(hj)
        mu = sum(jnp.sum(hj, axis=1, keepdims=True) for hj in chunks) * (1.0 / D_MODEL)
        cent = [hj - mu for hj in chunks]
        var = sum(jnp.sum(cj * cj, axis=1, keepdims=True) for cj in cent) * (1.0 / D_MODEL)
        inv = lax.rsqrt(var + LN_EPS)
        for j, cj in enumerate(cent):
            cols = slice(j * LANES, (j + 1) * LANES)
            o_ref[rows, cols] = cj * inv * g2_ref[:, cols] + b2_ref[:, cols]

    def issue_only(sref, buf):
        def body(g, c):
            issue_group(sref, buf, g)
            return c
        lax.fori_loop(0, tm // group, body, 0)

    @pl.when(i == 0)
    def _():
        issue_only(slot_ref, 0)

    @pl.when(jnp.logical_and(i == 0, n_steps > 1))
    def _():
        issue_only(slot1_ref, 1)

    for kk in range(TOP_K):
        pltpu.make_async_copy(yb_ref.at[pl.ds(0, tm * tr), :], buf_sc.at[cur, kk], sem.at[cur]).wait()

    @pl.when(i + 2 < n_steps)
    def _():
        def body(gb, c):
            compute_group(gb, issue=lambda g: issue_group(slot2_ref, ahead, g))
            return c
        lax.fori_loop(0, tm // big, body, 0)

    @pl.when(i + 2 >= n_steps)
    def _():
        def body(gb, c):
            compute_group(gb)
            return c
        lax.fori_loop(0, tm // big, body, 0)


def combine(y_tiles, slot_flat, gates, res, ln2_g, ln2_b, tm=COMBINE_TM):
    t, d = res.shape
    n_steps = t // tm
    slot_spec = lambda k: pl.BlockSpec((tm * TOP_K,), lambda i: (jnp.minimum(i + k, n_steps - 1),),
                                       memory_space=pltpu.SMEM)
    return pl.pallas_call(
        functools.partial(_combine_kernel, tm=tm),
        grid=(n_steps,),
        in_specs=[slot_spec(0), slot_spec(1), slot_spec(2),
                  pl.BlockSpec((tm, LANES), lambda i: (i, 0)),
                  pl.BlockSpec((tm, d), lambda i: (i, 0)),
                  pl.BlockSpec((1, d), lambda i: (0, 0)),
                  pl.BlockSpec((1, d), lambda i: (0, 0)),
                  pl.BlockSpec(memory_space=pl.ANY)],
        out_specs=pl.BlockSpec((tm, d), lambda i: (i, 0)),
        scratch_shapes=[pltpu.VMEM((COMBINE_BUFFERS, TOP_K, tm * TOKEN_TILE_ROWS, LANES), F32),
                        pltpu.SemaphoreType.DMA((COMBINE_BUFFERS,))],
        out_shape=jax.ShapeDtypeStruct((t, d), F32),
        compiler_params=_params(),
        name="combine",
    )(slot_flat, slot_flat, slot_flat, gates, res, ln2_g.reshape(1, d), ln2_b.reshape(1, d),
      y_tiles)


def kernel(x, p, w_in, conv_w, a_log, dt_bias, gnorm_w, w_out, ln1_g, ln1_b, w_router, b_router,
           w_gate_up, b_gate_up, w_down, b_down, w_ple_gate, b_ple_gate, w_ple_proj, ln2_g, ln2_b):
    bsz, seq, d = x.shape
    t = bsz * seq
    x2d = x.reshape(t, d)
    qkv_a, qkv_b, z_b, scal, scal_t = in_proj(x2d, w_in[0])
    ya = dilated_attn(qkv_a, bsz, seq)
    yb = deltanet(qkv_b, z_b, scal, scal_t, conv_w[0], a_log[0], dt_bias[0], gnorm_w[0], bsz, seq)
    x1, res, route, gates, counts = mix_route(
        ya, yb, x2d, p[0].reshape(t, PLE_DIM), w_out[0], ln1_g[0], ln1_b[0], w_router[0],
        b_router[0], w_ple_gate[0], b_ple_gate[0], w_ple_proj[0])

    cnt = counts[0, :N_EXPERTS].astype(jnp.int32)
    padded = (cnt + MOE_TILE - 1) // MOE_TILE * MOE_TILE
    pad_end = jnp.cumsum(padded).astype(jnp.int32)
    pad_start = pad_end - padded
    n_blocks = (t * TOP_K) // MOE_TILE + N_EXPERTS
    n_used = (pad_end[-1:] // MOE_TILE).astype(jnp.int32)
    tile_start = jnp.arange(n_blocks, dtype=jnp.int32) * MOE_TILE
    block_e = jnp.minimum(jnp.sum(pad_end[None, :] <= tile_start[:, None], axis=1),
                          N_EXPERTS - 1).astype(jnp.int32)
    experts_iota = jnp.arange(N_EXPERTS, dtype=jnp.int32)
    group_off = jnp.sum(jnp.where(route[:, :TOP_K, None] == experts_iota, pad_start, 0), axis=-1)
    slot_flat = (group_off + route[:, TOP_K:2 * TOP_K]).reshape(-1).astype(jnp.int32)

    xb = dispatch(x1, slot_flat, pad_start, pad_end, n_blocks * MOE_TILE)
    yexp = experts(xb, block_e, n_used, w_gate_up[0], b_gate_up[0], w_down[0], b_down[0])
    out = combine(yexp, slot_flat, gates, res, ln2_g[0], ln2_b[0])
    return out.reshape(bsz, seq, d)
```

```python
import functools
import math

import jax
import jax.numpy as jnp
from jax import lax
from jax.experimental import pallas as pl
from jax.experimental.pallas import tpu as pltpu

LANES = 128
VMEM_LIMIT_BYTES = 56 * 1024 * 1024

D_MODEL = 1024
PLE_DIM = 256
A_HEADS = 8
A_HEAD_DIM = 64
A_WIDTH = A_HEADS * A_HEAD_DIM
DILATIONS = (16, 4, 1)
ATT_BLOCK = 128
B_HEADS = 4
B_HEAD_DIM = 128
B_WIDTH = B_HEADS * B_HEAD_DIM
CONV_WIDTH = 4
DELTA_CHUNK = 64
N_EXPERTS = 32
TOP_K = 4
D_EXPERT = D_MODEL
SWIGLU_LIMIT = 7.0
SWIGLU_ALPHA = 1.702
MOE_BLOCK = 128
LN_EPS = 1e-5
RMS_EPS = 1e-6
L2_EPS = 1e-6
DEPTH = 1
DEEPNORM_ALPHA = (2.0 * DEPTH) ** 0.25

F32 = jnp.float32
BF16 = jnp.bfloat16


def _params(n_parallel_axes=1):
    return pltpu.CompilerParams(
        dimension_semantics=("arbitrary",) * n_parallel_axes,
        vmem_limit_bytes=VMEM_LIMIT_BYTES)


def _in_proj_kernel(x_ref, wa_ref, wb_ref, wz_ref, ws_ref, wst_ref, a_ref, b_ref, z_ref, s_ref, st_ref):
    xb = x_ref[...].astype(BF16)
    a_ref[...] = jnp.dot(xb, wa_ref[...], preferred_element_type=F32)
    b_ref[...] = jnp.dot(xb, wb_ref[...], preferred_element_type=F32).astype(BF16)
    z_ref[...] = jnp.dot(xb, wz_ref[...], preferred_element_type=F32)
    s_ref[...] = jnp.dot(xb, ws_ref[...], preferred_element_type=F32)
    st_ref[...] = lax.dot_general(wst_ref[...], xb, (((1,), (1,)), ((), ())),
                                  preferred_element_type=F32)


def in_proj(x2d, w_in, tm=512):
    t, d = x2d.shape
    na, nb = 3 * A_WIDTH, 3 * B_WIDTH
    wa = w_in[:, :na].astype(BF16)
    wb = w_in[:, na:na + nb].astype(BF16)
    wz = w_in[:, na + nb:na + nb + B_WIDTH].astype(BF16)
    ws = jnp.pad(w_in[:, na + nb + B_WIDTH:], ((0, 0), (0, LANES - 2 * B_HEADS))).astype(BF16)
    wst = ws[:, :16].T
    full = lambda a: pl.BlockSpec(a.shape, lambda i: (0, 0))
    return pl.pallas_call(
        _in_proj_kernel,
        grid=(t // tm,),
        in_specs=[pl.BlockSpec((tm, d), lambda i: (i, 0)),
                  full(wa), full(wb), full(wz), full(ws), full(wst)],
        out_specs=[pl.BlockSpec((tm, na), lambda i: (i, 0)),
                   pl.BlockSpec((tm, nb), lambda i: (i, 0)),
                   pl.BlockSpec((tm, B_WIDTH), lambda i: (i, 0)),
                   pl.BlockSpec((tm, LANES), lambda i: (i, 0)),
                   pl.BlockSpec((16, tm), lambda i: (0, i))],
        out_shape=[jax.ShapeDtypeStruct((t, na), F32),
                   jax.ShapeDtypeStruct((t, nb), BF16),
                   jax.ShapeDtypeStruct((t, B_WIDTH), F32),
                   jax.ShapeDtypeStruct((t, LANES), F32),
                   jax.ShapeDtypeStruct((16, t), F32)],
        compiler_params=_params(),
        name="in_proj",
    )(x2d, wa, wb, wz, ws, wst)


ATTN_UNROLL_DENSE = 5
ATTN_UNROLL_SINGLE = 8


def _attn_kernel(q_ref, k_ref, v_ref, o_ref, m_sc, l_sc, acc_sc, *, seq):
    blk = ATT_BLOCK
    lane = lax.broadcasted_iota(jnp.int32, (blk, LANES), 1)
    head0 = lane < A_HEAD_DIM
    row = lax.broadcasted_iota(jnp.int32, (blk, blk), 0)
    col = lax.broadcasted_iota(jnp.int32, (blk, blk), 1)
    prev_ok = col >= row
    cur_ok = col <= row
    neg_inf = jnp.float32(-jnp.inf)
    scale = 1.0 / math.sqrt(A_HEAD_DIM)

    def blocks(starts, stride, has_prev, first_branch, last_branch):
        def rows(ref, s0):
            if stride == 1:
                return ref[pl.ds(s0, blk), :]
            return ref[pl.ds(s0, blk, stride=stride), :]

        def put(ref, s0, val):
            if stride == 1:
                ref[pl.ds(s0, blk), :] = val
            else:
                ref[pl.ds(s0, blk, stride=stride), :] = val

        hmask = (head0, jnp.logical_not(head0))
        items = []
        for start, hp in zip(starts, has_prev):
            q = rows(q_ref, start) * scale
            keys = rows(k_ref, start)
            vals = rows(v_ref, start)
            if hp:
                keys = jnp.concatenate([rows(k_ref, start - blk * stride), keys], axis=0)
                vals = jnp.concatenate([rows(v_ref, start - blk * stride), vals], axis=0)
            it = dict(start=start, hp=hp, keys=keys.astype(BF16), vals=vals,
                      qh=[jnp.where(hm, q, 0.0).astype(BF16) for hm in hmask])
            if not first_branch:
                it["m_old"] = rows(m_sc, start)
                it["l_old"] = rows(l_sc, start)
                it["acc_old"] = rows(acc_sc, start)
            items.append(it)

        for it in items:
            it["s"] = [lax.dot_general(qh, it["keys"], (((1,), (1,)), ((), ())),
                                       preferred_element_type=F32) for qh in it["qh"]]
        for it in items:
            ok = jnp.concatenate([prev_ok, cur_ok], axis=1) if it["hp"] else cur_ok
            it["m"], it["sum"], it["p"] = [], [], []
            for h in range(2):
                s = jnp.where(ok, it["s"][h], neg_inf)
                t = jnp.maximum(s[:, :blk], s[:, blk:]) if it["hp"] else s
                if not first_branch:
                    t = jnp.maximum(t, jnp.where(hmask[h], it["m_old"], neg_inf))
                m_h = jnp.max(t, axis=1, keepdims=True)
                p = jnp.exp(s - m_h)
                psum = p[:, :blk] + p[:, blk:] if it["hp"] else p
                it["m"].append(m_h)
                it["sum"].append(jnp.sum(psum, axis=1, keepdims=True))
                it["p"].append(p.astype(BF16))
        for it in items:
            v16 = it["vals"].astype(BF16)
            it["pv"] = [jnp.dot(it["p"][h], v16, preferred_element_type=F32) for h in range(2)]
        for it in items:
            m_new = jnp.where(head0, it["m"][0], it["m"][1])
            l_new = jnp.where(head0, it["sum"][0], it["sum"][1])
            acc_new = jnp.where(head0, it["pv"][0], it["pv"][1])
            if not first_branch:
                alpha = jnp.exp(it["m_old"] - m_new)
                l_new = l_new + alpha * it["l_old"]
                acc_new = acc_new + alpha * it["acc_old"]
            if last_branch:
                put(o_ref, it["start"], acc_new / l_new)
            else:
                put(m_sc, it["start"], m_new)
                put(l_sc, it["start"], l_new)
                put(acc_sc, it["start"], acc_new)

    n_br = len(DILATIONS)
    for bi, dil in enumerate(DILATIONS):
        first, last = bi == 0, bi == n_br - 1
        nblk = seq // dil // blk
        if dil == 1:
            blocks([0], 1, [False], first, last)
            per = ATTN_UNROLL_DENSE
            assert (nblk - 1) % per == 0

            def body(g, c, first=first, last=last, per=per):
                starts = [pl.multiple_of((1 + g * per + j) * blk, blk) for j in range(per)]
                blocks(starts, 1, [True] * per, first, last)
                return c
            lax.fori_loop(0, (nblk - 1) // per, body, 0)
        elif nblk > 1:
            def body(r, c, dil=dil, nblk=nblk, first=first, last=last):
                blocks([r + n * blk * dil for n in range(nblk)], dil,
                       [n > 0 for n in range(nblk)], first, last)
                return c
            lax.fori_loop(0, dil, body, 0)
        else:
            per = ATTN_UNROLL_SINGLE
            assert dil % per == 0

            def body(g, c, dil=dil, first=first, last=last, per=per):
                blocks([g * per + j for j in range(per)], dil, [False] * per, first, last)
                return c
            lax.fori_loop(0, dil // per, body, 0)


def dilated_attn(qkv_a, bsz, seq):
    t = bsz * seq
    n_pairs = A_WIDTH // LANES
    blk = lambda off: pl.BlockSpec((seq, LANES), lambda b, p: (b, off + p))
    return pl.pallas_call(
        functools.partial(_attn_kernel, seq=seq),
        grid=(bsz, n_pairs),
        in_specs=[blk(0), blk(n_pairs), blk(2 * n_pairs)],
        out_specs=pl.BlockSpec((seq, LANES), lambda b, p: (b, p)),
        out_shape=jax.ShapeDtypeStruct((t, A_WIDTH), F32),
        scratch_shapes=[pltpu.VMEM((seq, LANES), F32)] * 3,
        compiler_params=_params(2),
        name="dilated_attn",
    )(qkv_a, qkv_a, qkv_a)


DELTA_HEADS_PER_STEP = 4
DELTA_P1_CHUNKS = 4
CONV_TILE = 256
CONV_HISTORY_ROWS = 16


def _dot16(a, b):
    return jnp.dot(a, b, preferred_element_type=F32)


def _split2(x):
    hi = x.astype(BF16)
    return hi, (x - hi.astype(F32)).astype(BF16)


def _split3(x):
    hi = x.astype(BF16)
    r = x - hi.astype(F32)
    mid = r.astype(BF16)
    return hi, mid, (r - mid.astype(F32)).astype(BF16)


def _dot_sel(m01, x):
    hi, mid, lo = _split3(x)
    return _dot16(m01, hi) + (_dot16(m01, mid) + _dot16(m01, lo))


def _dot_sel_r(x, m01):
    hi, mid, lo = _split3(x)
    return _dot16(hi, m01) + (_dot16(mid, m01) + _dot16(lo, m01))


def _stack_parts(x, lo_half, want_l=True, want_r=True):
    xh = x.astype(BF16)
    xh_f = xh.astype(F32)
    xl_f = x - xh_f
    left = right = None
    if want_l:
        mix = jnp.where(lo_half, xh_f, xl_f).astype(BF16)
        left = jnp.concatenate([mix, mix], axis=1)
    if want_r:
        xl = xl_f.astype(BF16)
        right = jnp.concatenate([xh, xh, xl, xl], axis=0)
    return left, right


def _softplus(x):
    return jnp.maximum(x, 0.0) + jnp.log(1.0 + jnp.exp(-jnp.abs(x)))


def _sigmoid(x):
    return 1.0 / (1.0 + jnp.exp(-x))


def _delta_kernel(q_ref, k_ref, v_ref, z_ref, sc_ref, st_ref, par_ref, part_ref,
                  wq_ref, wk_ref, wv_ref, nw_ref, o_ref,
                  conv_sc, qn_sc, kn_sc, vn_sc, w_sc, attn_sc, kdt_sc, gcr_sc, egl_sc,
                  state_sc, *, seq, hps):
    c = DELTA_CHUNK
    dk = B_HEAD_DIM
    n_chunks = seq // c
    tile = DELTA_P1_CHUNKS * c
    group = pl.program_id(1)
    lane = lax.broadcasted_iota(jnp.int32, (tile, LANES), 1)
    r2 = lax.broadcasted_iota(jnp.int32, (tile, tile), 0)
    c2 = lax.broadcasted_iota(jnp.int32, (tile, tile), 1)
    same_chunk = (r2 // c) == (c2 // c)
    cum_mat = jnp.where(same_chunk & (c2 <= r2), 1.0, 0.0).astype(BF16)
    tot_mat = jnp.where(same_chunk, 1.0, 0.0).astype(BF16)
    ri = lax.broadcasted_iota(jnp.int32, (c, 2 * c), 0)
    li = lax.broadcasted_iota(jnp.int32, (c, 2 * c), 1)
    lo_half = li < c
    ci = jnp.where(lo_half, li, li - c)
    causal = ci <= ri
    strict = ci < ri
    eye = jnp.where(ci == ri, 1.0, 0.0).astype(F32)
    upper = jnp.where(ri <= ci, 1.0, 0.0).astype(BF16)

    hist = CONV_HISTORY_ROWS
    for hh in range(hps):
        cols = slice(hh * dk, (hh + 1) * dk)
        streams = ((q_ref, wq_ref, qn_sc, "q"), (k_ref, wk_ref, kn_sc, "k"), (v_ref, wv_ref, vn_sc, "v"))

        def conv_tile(si, xe, c0, hh=hh, cols=cols):
            _, w_ref, dst, kind = streams[si]
            w = w_ref[:, cols]
            conv_sc[si] = xe
            off = hist - (CONV_WIDTH - 1)
            y = w[0:1, :] * conv_sc[si, off:off + CONV_TILE, :]
            for j in range(1, CONV_WIDTH):
                y = y + w[j:j + 1, :] * conv_sc[si, off + j:off + j + CONV_TILE, :]
            y = y * _sigmoid(y)
            if kind != "v":
                y = y * lax.rsqrt(jnp.sum(y * y, axis=1, keepdims=True) + L2_EPS)
            if kind == "q":
                y = y * (B_HEAD_DIM ** -0.5)
            dst[hh, pl.ds(c0, CONV_TILE), :] = y

        for si, (src, _, _, _) in enumerate(streams):
            conv_tile(si, jnp.concatenate([jnp.zeros((hist, dk), F32),
                                           src[0:CONV_TILE, cols].astype(F32)], axis=0), 0)

        def conv_body(ti, carry, cols=cols, conv_tile=conv_tile):
            c0 = pl.multiple_of(ti * CONV_TILE, CONV_TILE)
            for si, (src, _, _, _) in enumerate(streams):
                conv_tile(si, src[pl.ds(c0 - hist, hist + CONV_TILE), cols].astype(F32), c0)
            return carry

        lax.fori_loop(1, seq // CONV_TILE, conv_body, 0)

    for hh in range(hps):
        h = group * hps + hh
        a_row = st_ref[B_HEADS + h]
        g_row = -jnp.exp(part_ref[0, h]) * _softplus(a_row + part_ref[1, h])
        gcr_sc[hh] = _dot_sel_r(g_row, upper)

    alog_l = par_ref[0:1, :]
    dtb_l = par_ref[1:2, :]

    def aligned(x, m):
        return x if isinstance(x, int) else pl.multiple_of(x, m)

    def phase1(i):
        r0 = aligned(i * tile, tile)
        rows = pl.ds(r0, tile)
        s = sc_ref[rows, :]
        beta_all = _sigmoid(s)
        g_all = -jnp.exp(alog_l) * _softplus(s + dtb_l)
        gc_all = _dot_sel(cum_mat, g_all)
        gl_all = _dot_sel(tot_mat, g_all)
        chains = []
        for hh in range(hps):
            h = group * hps + hh
            pick = lambda a, l: jnp.sum(jnp.where(lane == l, a, 0.0), axis=1, keepdims=True)
            beta = pick(beta_all, h)
            gc = pick(gc_all, B_HEADS + h)
            gl = pick(gl_all, B_HEADS + h)
            egc = jnp.exp(gc)
            q = qn_sc[hh, rows, :]
            k = kn_sc[hh, rows, :]
            v = vn_sc[hh, rows, :]
            kbeta = k * beta
            vbeta = v * beta
            qn_sc[hh, rows, :] = q * egc
            kdec = k * jnp.exp(gl - gc)
            kb16 = k.astype(BF16)
            for cc in range(DELTA_P1_CHUNKS):
                sl = slice(cc * c, (cc + 1) * c)
                n = DELTA_P1_CHUNKS * i + cc
                kdt_sc[hh, n] = kdec[sl].T.astype(BF16)
                egl_sc[hh, pl.ds(n, 1), :] = jnp.broadcast_to(jnp.exp(gl[cc * c:cc * c + 1]), (1, dk))
                diff = gc[sl] - gcr_sc[hh, pl.ds(n, 1), :]
                e = jnp.exp(jnp.where(causal, diff, 0.0))
                k2 = jnp.concatenate([kb16[sl], kb16[sl]], axis=0)
                lhs = jnp.concatenate([kbeta[sl], q[sl]], axis=0).astype(BF16)
                rhs = jnp.concatenate([vbeta[sl], kbeta[sl] * egc[sl]], axis=1)
                chains.append(dict(hh=hh, row=r0 + cc * c, e=e, k2=k2, lhs=lhs, rhs=rhs))
        yield

        for ch in chains:
            kq = lax.dot_general(ch["lhs"], ch["k2"], (((1,), (1,)), ((), ())),
                                 preferred_element_type=F32)
            ch["a"] = jnp.where(strict, kq[:c] * ch["e"], 0.0)
            attn_sc[ch["hh"], pl.ds(ch["row"], c), :] = (
                jnp.where(causal, kq[c:] * ch["e"], 0.0)[:, :c].astype(BF16))
        yield
        for ch in chains:
            ch["inv"] = eye - ch["a"]
            p_l, p_r = _stack_parts(ch["a"], lo_half)
            ch["p"] = _dot16(p_l, p_r)
        yield
        for _ in range(4):
            for ch in chains:
                p_l, p_r = _stack_parts(ch["p"], lo_half)
                inv_l, _ = _stack_parts(ch["inv"], lo_half, want_r=False)
                res = _dot16(jnp.concatenate([p_l, inv_l], axis=0), p_r)
                ch["p"] = res[:c]
                ch["inv"] = ch["inv"] + res[c:]
            yield
        for ch in chains:
            inv_l, _ = _stack_parts(ch["inv"], lo_half, want_r=False)
            _, p_r = _stack_parts(ch["p"], lo_half, want_l=False)
            ch["inv"] = ch["inv"] + _dot16(inv_l, p_r)
        yield
        for ch in chains:
            inv_l, _ = _stack_parts(ch["inv"], lo_half, want_r=False)
            _, rhs_r = _stack_parts(ch["rhs"], None, want_l=False)
            sol = _dot16(inv_l, rhs_r)
            vn_sc[ch["hh"], pl.ds(ch["row"], c), :] = sol[:, :dk]
            w_sc[ch["hh"], pl.ds(ch["row"], c), :] = sol[:, dk:].astype(BF16)

    nw = nw_ref[...]

    def phase2(i):
        for cc in range(DELTA_P1_CHUNKS):
            n = DELTA_P1_CHUNKS * i + cc
            r0 = aligned(n * c, c)
            rows = pl.ds(r0, c)
            heads = range(hps)
            st = [state_sc[hh] for hh in heads]
            st16 = [s_.astype(BF16) for s_ in st]
            lhs = [jnp.concatenate([w_sc[hh, rows, :], qn_sc[hh, rows, :].astype(BF16)], axis=0)
                   for hh in heads]
            ws = [_dot16(lhs[hh], st16[hh]) for hh in heads]
            yield
            vn16 = [(vn_sc[hh, rows, :] - ws[hh][:c]).astype(BF16) for hh in heads]
            av = [_dot16(attn_sc[hh, rows, :], vn16[hh]) for hh in heads]
            kv = [_dot16(kdt_sc[hh, n], vn16[hh]) for hh in heads]
            for hh in heads:
                state_sc[hh] = st[hh] * egl_sc[hh, pl.ds(n, 1), :] + kv[hh]
                out = ws[hh][c:] + av[hh]
                o = out * lax.rsqrt(jnp.mean(out * out, axis=1, keepdims=True) + RMS_EPS) * nw
                z = z_ref[rows, hh * dk:(hh + 1) * dk]
                o_ref[rows, hh * dk:(hh + 1) * dk] = (o * (z * _sigmoid(z))).astype(o_ref.dtype)
            yield

    def interleave(*gens):
        live = list(gens)
        while live:
            for g in list(live):
                try:
                    next(g)
                except StopIteration:
                    live.remove(g)

    n_tiles = n_chunks // DELTA_P1_CHUNKS
    state_sc[...] = jnp.zeros(state_sc.shape, F32)
    interleave(phase1(0))

    def pipelined(i, carry):
        interleave(phase1(i), phase2(i - 1))
        return carry

    lax.fori_loop(1, n_tiles, pipelined, 0)
    interleave(phase2(n_tiles - 1))


def deltanet(qkv_b, z_b, scal, scal_t, conv_w, a_log, dt_bias, gnorm_w, bsz, seq):
    t = bsz * seq
    hps = DELTA_HEADS_PER_STEP
    n_groups = B_HEADS // hps
    wg = hps * B_HEAD_DIM
    c = DELTA_CHUNK
    n_chunks = seq // c
    st3 = scal_t.reshape(16, t // c, c)
    par = jnp.zeros((8, LANES), F32)
    par = par.at[0, B_HEADS:2 * B_HEADS].set(a_log).at[1, B_HEADS:2 * B_HEADS].set(dt_bias)
    part = jnp.stack([a_log, dt_bias]).astype(F32)
    blk = lambda off: pl.BlockSpec((seq, wg), lambda b, g: (b, off + g))
    wblk = lambda off: pl.BlockSpec((CONV_WIDTH, wg), lambda b, g: (0, off + g))
    sc = lambda shape, dt: pltpu.VMEM(shape, dt)
    return pl.pallas_call(
        functools.partial(_delta_kernel, seq=seq, hps=hps),
        grid=(bsz, n_groups),
        in_specs=[blk(0), blk(n_groups), blk(2 * n_groups), blk(0),
                  pl.BlockSpec((seq, LANES), lambda b, g: (b, 0)),
                  pl.BlockSpec((16, n_chunks, c), lambda b, g: (0, b, 0)),
                  pl.BlockSpec((8, LANES), lambda b, g: (0, 0)),
                  pl.BlockSpec(memory_space=pltpu.SMEM),
                  wblk(0), wblk(n_groups), wblk(2 * n_groups),
                  pl.BlockSpec((1, B_HEAD_DIM), lambda b, g: (0, 0))],
        out_specs=pl.BlockSpec((seq, wg), lambda b, g: (b, g)),
        out_shape=jax.ShapeDtypeStruct((t, B_WIDTH), BF16),
        scratch_shapes=[sc((3, CONV_HISTORY_ROWS + CONV_TILE, B_HEAD_DIM), F32),
                        sc((hps, seq, B_HEAD_DIM), F32), sc((hps, seq, B_HEAD_DIM), F32),
                        sc((hps, seq, B_HEAD_DIM), F32),
                        sc((hps, seq, B_HEAD_DIM), BF16),
                        sc((hps, seq, c), BF16), sc((hps, n_chunks, B_HEAD_DIM, c), BF16),
                        sc((hps, n_chunks, 2 * c), F32), sc((hps, n_chunks, B_HEAD_DIM), F32),
                        sc((hps, B_HEAD_DIM, B_HEAD_DIM), F32)],
        compiler_params=_params(2),
        name="deltanet",
    )(qkv_b, qkv_b, qkv_b, z_b, scal, st3, par, part,
      conv_w, conv_w, conv_w, gnorm_w.reshape(1, B_HEAD_DIM))


SUBLANES = 8
TOKEN_TILE_ROWS = D_MODEL // LANES
assert TOKEN_TILE_ROWS == SUBLANES


def _store_token_tiles(ref, x, n_tok):
    for j in range(TOKEN_TILE_ROWS):
        ref[pl.ds(j, n_tok, stride=TOKEN_TILE_ROWS), :] = x[:, j * LANES:(j + 1) * LANES]


def _layer_norm(h, g, b):
    mu = jnp.mean(h, axis=1, keepdims=True)
    hc = h - mu
    var = jnp.mean(hc * hc, axis=1, keepdims=True)
    return hc * lax.rsqrt(var + LN_EPS) * g + b


def _mix_route_kernel(ya_ref, yb_ref, x_ref, p_ref, woa_ref, wob_ref, g1_ref, b1_ref,
                      wr_ref, br_ref, wpg_ref, bpg_ref, wpp_ref,
                      x1_ref, res_ref, route_ref, gate_ref, cnt_ref, run_sc, wr2_sc, *, tm):
    i = pl.program_id(0)

    @pl.when(i == 0)
    def _():
        run_sc[...] = jnp.zeros(run_sc.shape, F32)
        w_hi, w_lo = _split2(wr_ref[...])
        wr2_sc[:, :LANES] = w_hi
        wr2_sc[:, LANES:] = w_lo

    mix = (jnp.dot(ya_ref[...].astype(BF16), woa_ref[...], preferred_element_type=F32)
           + jnp.dot(yb_ref[...].astype(BF16), wob_ref[...], preferred_element_type=F32))
    x1 = _layer_norm(DEEPNORM_ALPHA * x_ref[...] + mix, g1_ref[...], b1_ref[...])
    _store_token_tiles(x1_ref, x1, tm)
    x1b = x1.astype(BF16)

    x1l = (x1 - x1b.astype(F32)).astype(BF16)
    r_hi = jnp.dot(x1b, wr2_sc[...], preferred_element_type=F32)
    r_lo = jnp.dot(x1l, wr2_sc[:, :LANES], preferred_element_type=F32)
    logits = r_hi[:, :LANES] + (r_hi[:, LANES:] + r_lo) + br_ref[...]

    pgate = _sigmoid(jnp.dot(x1b, wpg_ref[...], preferred_element_type=F32) + bpg_ref[...])
    proj = jnp.dot(p_ref[...].astype(BF16), wpp_ref[...], preferred_element_type=F32)
    res_ref[...] = DEEPNORM_ALPHA * x1 + pgate * proj

    lane = lax.broadcasted_iota(jnp.int32, (tm, LANES), 1)
    cur = jnp.where(lane < N_EXPERTS, logits, -jnp.inf)
    vals, hots = [], []
    for _ in range(TOP_K):
        m = jnp.max(cur, axis=1, keepdims=True)
        idx = jnp.min(jnp.where(cur == m, lane, LANES), axis=1, keepdims=True)
        hot = lane == idx
        cur = jnp.where(hot, -jnp.inf, cur)
        vals.append(m)
        hots.append((hot, idx))
    exps = [jnp.exp(v - vals[0]) for v in vals]
    den = exps[0] + exps[1] + exps[2] + exps[3]

    member = jnp.zeros((tm, LANES), F32)
    for hot, _ in hots:
        member = member + jnp.where(hot, 1.0, 0.0)
    rr = lax.broadcasted_iota(jnp.int32, (tm, tm), 0)
    cc = lax.broadcasted_iota(jnp.int32, (tm, tm), 1)
    before = jnp.where(cc < rr, 1.0, 0.0).astype(BF16)
    prior = jnp.dot(before, member.astype(BF16), preferred_element_type=F32) + run_sc[0:1, :]
    route = jnp.zeros((tm, LANES), jnp.int32)
    gate = jnp.zeros((tm, LANES), F32)
    for kk, (hot, idx) in enumerate(hots):
        rank = jnp.sum(jnp.where(hot, prior, 0.0), axis=1, keepdims=True).astype(jnp.int32)
        route = jnp.where(lane == kk, idx, route)
        route = jnp.where(lane == TOP_K + kk, rank, route)
        gate = jnp.where(lane == kk, exps[kk] / den, gate)
    route_ref[...] = route
    gate_ref[...] = gate
    run_sc[...] = run_sc[...] + jnp.sum(member, axis=0, keepdims=True)
    cnt_ref[...] = run_sc[...]


def mix_route(ya, yb, x2d, p2d, w_out, ln1_g, ln1_b, w_router, b_router, w_ple_gate, b_ple_gate,
              w_ple_proj, tm=512):
    t, d = x2d.shape
    woa = w_out[:A_WIDTH].astype(BF16)
    wob = w_out[A_WIDTH:].astype(BF16)
    wr = jnp.pad(w_router, ((0, 0), (0, LANES - N_EXPERTS)))
    br = jnp.pad(b_router, (0, LANES - N_EXPERTS)).reshape(1, LANES)
    row = lambda w: pl.BlockSpec((tm, w), lambda i: (i, 0))
    full = lambda a: pl.BlockSpec(a.shape, lambda i: (0,) * a.ndim)
    tiles = pl.BlockSpec((tm * TOKEN_TILE_ROWS, LANES), lambda i: (i, 0))
    ops = [woa, wob, ln1_g.reshape(1, d), ln1_b.reshape(1, d), wr, br,
           w_ple_gate.astype(BF16), b_ple_gate.reshape(1, d), w_ple_proj.astype(BF16)]
    return pl.pallas_call(
        functools.partial(_mix_route_kernel, tm=tm),
        grid=(t // tm,),
        in_specs=[row(A_WIDTH), row(B_WIDTH), row(d), row(PLE_DIM)] + [full(a) for a in ops],
        out_specs=[tiles, row(d), row(LANES), row(LANES),
                   pl.BlockSpec((8, LANES), lambda i: (0, 0))],
        out_shape=[jax.ShapeDtypeStruct((t * TOKEN_TILE_ROWS, LANES), F32),
                   jax.ShapeDtypeStruct((t, d), F32),
                   jax.ShapeDtypeStruct((t, LANES), jnp.int32),
                   jax.ShapeDtypeStruct((t, LANES), F32),
                   jax.ShapeDtypeStruct((8, LANES), F32)],
        scratch_shapes=[pltpu.VMEM((8, LANES), F32), pltpu.VMEM((d, 2 * LANES), BF16)],
        compiler_params=_params(),
        name="mix_route",
    )(ya, yb, x2d, p2d, *ops)


MOE_TILE = 1024
DISPATCH_TM = 2048
COMBINE_TM = 512
DMA_ROWS_PER_ITER = 8
DMA_PRIORITIES = 2


def _dispatch_kernel(pstart_ref, pend_ref, slot_ref, x_ref, xb_ref, zero_sc, sem, zsem, *, tm):
    i = pl.program_id(0)
    tr = TOKEN_TILE_ROWS
    tile_rows = MOE_TILE * tr

    def zero_copy(e):
        off = pl.multiple_of((pend_ref[e] - MOE_TILE) * tr, tile_rows)
        return pltpu.make_async_copy(zero_sc, xb_ref.at[pl.ds(off, tile_rows), :], zsem)

    @pl.when(i == 0)
    def _():
        zero_sc[...] = jnp.zeros(zero_sc.shape, zero_sc.dtype)
        for e in range(N_EXPERTS):
            @pl.when(pend_ref[e] > pstart_ref[e])
            def _():
                zero_copy(e).start()
        for e in range(N_EXPERTS):
            @pl.when(pend_ref[e] > pstart_ref[e])
            def _():
                zero_copy(e).wait()

        n_tiles = xb_ref.shape[0] // tile_rows
        first_unused = pend_ref[N_EXPERTS - 1] // MOE_TILE

        def tail_copy(j):
            off = pl.multiple_of(j * tile_rows, tile_rows)
            return pltpu.make_async_copy(zero_sc, xb_ref.at[pl.ds(off, tile_rows), :], zsem)

        def tail_start(j, c):
            tail_copy(j).start()
            return c

        def tail_wait(j, c):
            tail_copy(j).wait()
            return c

        lax.fori_loop(first_unused, n_tiles, tail_start, 0)
        lax.fori_loop(first_unused, n_tiles, tail_wait, 0)

    def token_copy(tok, slot):
        src = pl.multiple_of(tok * tr, tr)
        dst = pl.multiple_of(slot * tr, tr)
        return pltpu.make_async_copy(x_ref.at[pl.ds(src, tr), :], xb_ref.at[pl.ds(dst, tr), :], sem)

    def start(g, c):
        toks = [g * DMA_ROWS_PER_ITER + u for u in range(DMA_ROWS_PER_ITER)]
        slots = [[slot_ref[r * TOP_K + kk] for kk in range(TOP_K)] for r in toks]
        for r, row_slots in zip(toks, slots):
            for kk, slot in enumerate(row_slots):
                token_copy(r, slot).start(priority=kk % DMA_PRIORITIES)
        return c

    lax.fori_loop(0, tm // DMA_ROWS_PER_ITER, start, 0)
    for _ in range(TOP_K):
        pltpu.make_async_copy(x_ref, xb_ref.at[pl.ds(0, tm * tr), :], sem).wait()


def dispatch(x1_tiles, slot_flat, pad_start, pad_end, n_slots, tm=DISPATCH_TM):
    t = x1_tiles.shape[0] // TOKEN_TILE_ROWS
    return pl.pallas_call(
        functools.partial(_dispatch_kernel, tm=tm),
        grid_spec=pltpu.PrefetchScalarGridSpec(
            num_scalar_prefetch=2,
            grid=(t // tm,),
            in_specs=[pl.BlockSpec((tm * TOP_K,), lambda i, ps, pe: (i,), memory_space=pltpu.SMEM),
                      pl.BlockSpec((tm * TOKEN_TILE_ROWS, LANES), lambda i, ps, pe: (i, 0))],
            out_specs=pl.BlockSpec(memory_space=pl.ANY),
            scratch_shapes=[pltpu.VMEM((MOE_TILE * TOKEN_TILE_ROWS, LANES), F32),
                            pltpu.SemaphoreType.DMA(()), pltpu.SemaphoreType.DMA(())]),
        out_shape=jax.ShapeDtypeStruct((n_slots * TOKEN_TILE_ROWS, LANES), F32),
        compiler_params=_params(),
        name="dispatch",
    )(pad_start, pad_end, slot_flat, x1_tiles)


EXPERT_CAST_ROWS = 256
EXPERT_COL_CHUNK = 256


def _expert_kernel(be_ref, nu_ref, x_ref, wgu_ref, bgu_ref, wd_ref, bd_ref, y_ref, wgu_sc, wd_sc):
    i = pl.program_id(0)
    used = i < nu_ref[0]
    new_expert = jnp.logical_or(i == 0, be_ref[i] != be_ref[jnp.maximum(i - 1, 0)])

    @pl.when(jnp.logical_and(used, new_expert))
    def _():
        def cast(j, c):
            r = pl.ds(pl.multiple_of(j * EXPERT_CAST_ROWS, EXPERT_CAST_ROWS), EXPERT_CAST_ROWS)
            wgu_sc[r, :] = wgu_ref[0, r, :].astype(BF16)
            wd_sc[r, :] = wd_ref[0, r, :].astype(BF16)
            return c
        lax.fori_loop(0, D_MODEL // EXPERT_CAST_ROWS, cast, 0)

    @pl.when(used)
    def _():
        tr = TOKEN_TILE_ROWS
        nchunk = EXPERT_COL_CHUNK
        xc = [x_ref[pl.ds(j, MOE_TILE, stride=tr), :].astype(BF16) for j in range(tr)]
        xb = jnp.concatenate(xc, axis=1)
        acts = []
        for c0 in range(0, D_EXPERT, nchunk):
            if c0 == 0:
                half = D_MODEL // 2
                xs = (jnp.concatenate(xc[:tr // 2], axis=1), jnp.concatenate(xc[tr // 2:], axis=1))
                g = sum(jnp.dot(xs[h], wgu_sc[h * half:(h + 1) * half, c0:c0 + nchunk],
                                preferred_element_type=F32) for h in range(2))
                u = sum(jnp.dot(xs[h], wgu_sc[h * half:(h + 1) * half,
                                              D_EXPERT + c0:D_EXPERT + c0 + nchunk],
                                preferred_element_type=F32) for h in range(2))
            else:
                g = jnp.dot(xb, wgu_sc[:, c0:c0 + nchunk], preferred_element_type=F32)
                u = jnp.dot(xb, wgu_sc[:, D_EXPERT + c0:D_EXPERT + c0 + nchunk],
                            preferred_element_type=F32)
            g = g + bgu_ref[0, :, c0:c0 + nchunk]
            u = u + bgu_ref[0, :, D_EXPERT + c0:D_EXPERT + c0 + nchunk]
            g = jnp.minimum(g, SWIGLU_LIMIT)
            u = jnp.clip(u, -SWIGLU_LIMIT, SWIGLU_LIMIT)
            acts.append(((u + 1.0) * (g * _sigmoid(SWIGLU_ALPHA * g))).astype(BF16))
        act = jnp.concatenate(acts, axis=1)
        for n0 in range(0, D_MODEL, nchunk):
            yb = (jnp.dot(act, wd_sc[:, n0:n0 + nchunk], preferred_element_type=F32)
                  + bd_ref[0, :, n0:n0 + nchunk])
            for jj in range(nchunk // LANES):
                j = n0 // LANES + jj
                y_ref[pl.ds(j, MOE_TILE, stride=tr), :] = yb[:, jj * LANES:(jj + 1) * LANES]

    @pl.when(jnp.logical_not(used))
    def _():
        y_ref[...] = jnp.zeros(y_ref.shape, F32)


def experts(xb_tiles, block_e, n_used, w_gate_up, b_gate_up, w_down, b_down):
    tile_rows = MOE_TILE * TOKEN_TILE_ROWS
    n_blocks = xb_tiles.shape[0] // tile_rows
    d = D_MODEL
    clamp = lambda i, be, nu: jnp.minimum(i, nu[0] - 1)
    return pl.pallas_call(
        _expert_kernel,
        grid_spec=pltpu.PrefetchScalarGridSpec(
            num_scalar_prefetch=2,
            grid=(n_blocks,),
            in_specs=[pl.BlockSpec((tile_rows, LANES), lambda i, be, nu: (clamp(i, be, nu), 0)),
                      pl.BlockSpec((1, d, 2 * D_EXPERT), lambda i, be, nu: (be[i], 0, 0)),
                      pl.BlockSpec((1, 1, 2 * D_EXPERT), lambda i, be, nu: (be[i], 0, 0)),
                      pl.BlockSpec((1, D_EXPERT, d), lambda i, be, nu: (be[i], 0, 0)),
                      pl.BlockSpec((1, 1, d), lambda i, be, nu: (be[i], 0, 0))],
            out_specs=pl.BlockSpec((tile_rows, LANES), lambda i, be, nu: (i, 0)),
            scratch_shapes=[pltpu.VMEM((d, 2 * D_EXPERT), BF16), pltpu.VMEM((D_EXPERT, d), BF16)]),
        out_shape=jax.ShapeDtypeStruct(xb_tiles.shape, F32),
        compiler_params=_params(),
        name="experts",
    )(block_e, n_used, xb_tiles, w_gate_up, b_gate_up.reshape(N_EXPERTS, 1, -1),
      w_down, b_down.reshape(N_EXPERTS, 1, -1))


COMBINE_BUFFERS = 3


def _combine_kernel(slot_ref, slot1_ref, slot2_ref, gate_ref, res_ref, g2_ref, b2_ref,
                    yb_ref, o_ref, buf_sc, sem, *, tm):
    i = pl.program_id(0)
    n_steps = pl.num_programs(0)
    tr = TOKEN_TILE_ROWS
    group = DMA_ROWS_PER_ITER
    cur = lax.rem(i, COMBINE_BUFFERS)
    ahead = lax.rem(i + 2, COMBINE_BUFFERS)

    def token_copy(buf, kk, r, slot):
        src = pl.multiple_of(slot * tr, tr)
        dst = pl.multiple_of(r * tr, tr)
        return pltpu.make_async_copy(yb_ref.at[pl.ds(src, tr), :],
                                     buf_sc.at[buf, kk, pl.ds(dst, tr), :], sem.at[buf])

    def issue_group(sref, buf, g):
        toks = [g * group + u for u in range(group)]
        slots = [[sref[r * TOP_K + kk] for kk in range(TOP_K)] for r in toks]
        for r, row_slots in zip(toks, slots):
            for kk, slot in enumerate(row_slots):
                token_copy(buf, kk, r, slot).start(priority=kk % DMA_PRIORITIES)

    big = group * tr

    def compute_group(gb, issue=None):
        rows = pl.ds(pl.multiple_of(gb * big, big), big)
        gate = gate_ref[rows, :]
        chunks = []
        for j in range(tr):
            res = res_ref[rows, j * LANES:(j + 1) * LANES]
            ys = [buf_sc[cur, kk, pl.ds(gb * (big * tr) + j, big, stride=tr), :]
                  for kk in range(TOP_K)]
            if issue is not None:
                issue(gb * tr + j)
            hj = res
            for kk in range(TOP_K):
                hj = hj + gate[:, kk:kk + 1] * ys[kk]
            chunks.append(hj)
        mu = sum(jnp.sum(hj, axis=1, keepdims=True) for hj in chunks) * (1.0 / D_MODEL)
        cent = [hj - mu for hj in chunks]
        var = sum(jnp.sum(cj * cj, axis=1, keepdims=True) for cj in cent) * (1.0 / D_MODEL)
        inv = lax.rsqrt(var + LN_EPS)
        for j, cj in enumerate(cent):
            cols = slice(j * LANES, (j + 1) * LANES)
            o_ref[rows, cols] = cj * inv * g2_ref[:, cols] + b2_ref[:, cols]

    def issue_only(sref, buf):
        def body(g, c):
            issue_group(sref, buf, g)
            return c
        lax.fori_loop(0, tm // group, body, 0)

    @pl.when(i == 0)
    def _():
        issue_only(slot_ref, 0)

    @pl.when(jnp.logical_and(i == 0, n_steps > 1))
    def _():
        issue_only(slot1_ref, 1)

    for kk in range(TOP_K):
        pltpu.make_async_copy(yb_ref.at[pl.ds(0, tm * tr), :], buf_sc.at[cur, kk], sem.at[cur]).wait()

    @pl.when(i + 2 < n_steps)
    def _():
        def body(gb, c):
            compute_group(gb, issue=lambda g: issue_group(slot2_ref, ahead, g))
            return c
        lax.fori_loop(0, tm // big, body, 0)

    @pl.when(i + 2 >= n_steps)
    def _():
        def body(gb, c):
            compute_group(gb)
            return c
        lax.fori_loop(0, tm // big, body, 0)


def combine(y_tiles, slot_flat, gates, res, ln2_g, ln2_b, tm=COMBINE_TM):
    t, d = res.shape
    n_steps = t // tm
    slot_spec = lambda k: pl.BlockSpec((tm * TOP_K,), lambda i: (jnp.minimum(i + k, n_steps - 1),),
                                       memory_space=pltpu.SMEM)
    return pl.pallas_call(
        functools.partial(_combine_kernel, tm=tm),
        grid=(n_steps,),
        in_specs=[slot_spec(0), slot_spec(1), slot_spec(2),
                  pl.BlockSpec((tm, LANES), lambda i: (i, 0)),
                  pl.BlockSpec((tm, d), lambda i: (i, 0)),
                  pl.BlockSpec((1, d), lambda i: (0, 0)),
                  pl.BlockSpec((1, d), lambda i: (0, 0)),
                  pl.BlockSpec(memory_space=pl.ANY)],
        out_specs=pl.BlockSpec((tm, d), lambda i: (i, 0)),
        scratch_shapes=[pltpu.VMEM((COMBINE_BUFFERS, TOP_K, tm * TOKEN_TILE_ROWS, LANES), F32),
                        pltpu.SemaphoreType.DMA((COMBINE_BUFFERS,))],
        out_shape=jax.ShapeDtypeStruct((t, d), F32),
        compiler_params=_params(),
        name="combine",
    )(slot_flat, slot_flat, slot_flat, gates, res, ln2_g.reshape(1, d), ln2_b.reshape(1, d),
      y_tiles)


def kernel(x, p, w_in, conv_w, a_log, dt_bias, gnorm_w, w_out, ln1_g, ln1_b, w_router, b_router,
           w_gate_up, b_gate_up, w_down, b_down, w_ple_gate, b_ple_gate, w_ple_proj, ln2_g, ln2_b):
    bsz, seq, d = x.shape
    t = bsz * seq
    x2d = x.reshape(t, d)
    qkv_a, qkv_b, z_b, scal, scal_t = in_proj(x2d, w_in[0])
    ya = dilated_attn(qkv_a, bsz, seq)
    yb = deltanet(qkv_b, z_b, scal, scal_t, conv_w[0], a_log[0], dt_bias[0], gnorm_w[0], bsz, seq)
    x1, res, route, gates, counts = mix_route(
        ya, yb, x2d, p[0].reshape(t, PLE_DIM), w_out[0], ln1_g[0], ln1_b[0], w_router[0],
        b_router[0], w_ple_gate[0], b_ple_gate[0], w_ple_proj[0])

    cnt = counts[0, :N_EXPERTS].astype(jnp.int32)
    padded = (cnt + MOE_TILE - 1) // MOE_TILE * MOE_TILE
    pad_end = jnp.cumsum(padded).astype(jnp.int32)
    pad_start = pad_end - padded
    n_blocks = (t * TOP_K) // MOE_TILE + N_EXPERTS
    n_used = (pad_end[-1:] // MOE_TILE).astype(jnp.int32)
    tile_start = jnp.arange(n_blocks, dtype=jnp.int32) * MOE_TILE
    block_e = jnp.minimum(jnp.sum(pad_end[None, :] <= tile_start[:, None], axis=1),
                          N_EXPERTS - 1).astype(jnp.int32)
    experts_iota = jnp.arange(N_EXPERTS, dtype=jnp.int32)
    group_off = jnp.sum(jnp.where(route[:, :TOP_K, None] == experts_iota, pad_start, 0), axis=-1)
    slot_flat = (group_off + route[:, TOP_K:2 * TOP_K]).reshape(-1).astype(jnp.int32)

    xb = dispatch(x1, slot_flat, pad_start, pad_end, n_blocks * MOE_TILE)
    yexp = experts(xb, block_e, n_used, w_gate_up[0], b_gate_up[0], w_down[0], b_down[0])
    out = combine(yexp, slot_flat, gates, res, ln2_g[0], ln2_b[0])
    return out.reshape(bsz, seq, d)
```
